```python
import numpy as np
import jax
import jax.numpy as jnp
from jax import lax

D_MODEL = 2048
BATCH = 2
SEQ = 4096
DEPTH = 1

GLA_WIDTH = D_MODEL // 2
NSA_WIDTH = D_MODEL - GLA_WIDTH
GLA_HEADS = 4
GLA_DV = GLA_WIDTH // GLA_HEADS
GLA_DK = GLA_DV // 2
GLA_RANK = 16
GLA_TAU = 16.0
GLA_CHUNK = 64
NSA_HEAD_DIM = 128
NSA_HEADS = NSA_WIDTH // NSA_HEAD_DIM
NSA_GROUPS = 2
NSA_HPG = NSA_HEADS // NSA_GROUPS
CMP_STRIDE = 16
CMP_BLOCK = 2 * CMP_STRIDE
CMP_HIDDEN = 256
SEL_BLOCK = 64
SEL_TOPK = 16
SEL_QBLOCK = 64
WINDOW = 512
WIN_QBLOCK = 128
ROPE_DIM = NSA_HEAD_DIM // 4
ROPE_THETA = 500000.0
N_EXPERTS = 32
TOP_K = 4
D_FF = D_MODEL
SWIGLU_LIMIT = 7.0
SWIGLU_ALPHA = 1.702
DN_ALPHA = (2.0 * DEPTH) ** 0.25
DN_BETA = (8.0 * DEPTH) ** -0.25
LN_EPS = 1e-5
NSA_KV = NSA_GROUPS * NSA_HEAD_DIM
IN_SPLITS = (GLA_HEADS * GLA_DK, GLA_HEADS * GLA_DK, GLA_WIDTH, GLA_WIDTH, GLA_RANK,
             NSA_WIDTH, NSA_KV, NSA_KV, NSA_KV, NSA_KV, NSA_KV, NSA_KV, NSA_HEADS * 3)
D_IN = sum(IN_SPLITS)

kernel_name = 'hybrid_gla_nsa_moe_deepnorm_adaln'


def layer_norm(x, g=None, b=None):
    xf = x.astype(jnp.float32)
    xc = xf - jnp.mean(xf, -1, keepdims=True)
    y = xc * lax.rsqrt(jnp.mean(xc * xc, -1, keepdims=True) + LN_EPS)
    if g is not None:
        y = y * g.astype(jnp.float32) + b.astype(jnp.float32)
    return y.astype(x.dtype)


def rms_norm(x, g):
    xf = x.astype(jnp.float32)
    return xf * lax.rsqrt(jnp.mean(xf * xf, -1, keepdims=True) + LN_EPS) * g.astype(jnp.float32)


def partial_rope(x, pos):
    half = ROPE_DIM // 2
    inv_freq = ROPE_THETA ** (-jnp.arange(half, dtype=jnp.float32) * (2.0 / ROPE_DIM))
    ang = pos.astype(jnp.float32)[:, None] * inv_freq
    cos = jnp.cos(ang)[:, None, :]
    sin = jnp.sin(ang)[:, None, :]
    xr = x[..., :ROPE_DIM].astype(jnp.float32)
    x1, x2 = xr[..., :half], xr[..., half:]
    rot = jnp.concatenate([x1 * cos - x2 * sin, x2 * cos + x1 * sin], -1)
    return jnp.concatenate([rot.astype(x.dtype), x[..., ROPE_DIM:]], -1)


def gla_mixer(q, k, v, r, g_lr, w_gk, b_gk, norm_g):
    B, T, _ = q.shape
    H, C = GLA_HEADS, GLA_CHUNK
    NC = T // C
    f32 = jnp.float32
    q = q.astype(f32) * (GLA_DK ** -0.5)
    log_a = jax.nn.log_sigmoid((g_lr @ w_gk + b_gk).astype(f32)) / GLA_TAU

    def chunks(a):
        return a.astype(f32).reshape(B, NC, C, H, -1).transpose(1, 0, 3, 2, 4)

    qc, kc, vc = chunks(q), chunks(k), chunks(v)
    bc = jnp.cumsum(chunks(log_a), axis=3)
    causal = jnp.tril(jnp.ones((C, C), bool))[None, None, :, :, None]

    def step(S, inp):
        qt, kt, vt, bt = inp
        o_inter = jnp.einsum('bhtd,bhde->bhte', qt * jnp.exp(bt), S)
        diff = bt[:, :, :, None, :] - bt[:, :, None, :, :]
        decay = jnp.exp(jnp.where(causal, diff, -jnp.inf))
        att = jnp.einsum('bhtd,bhsd,bhtsd->bhts', qt, kt, decay)
        o = o_inter + jnp.einsum('bhts,bhse->bhte', att, vt)
        b_last = bt[:, :, -1:, :]
        S = jnp.exp(b_last[:, :, 0, :, None]) * S + jnp.einsum('bhsd,bhse->bhde', kt * jnp.exp(b_last - bt), vt)
        return S, o

    S0 = jnp.zeros((B, H, GLA_DK, GLA_DV), f32)
    _, o = lax.scan(step, S0, (qc, kc, vc, bc))
    o = o.transpose(1, 0, 3, 2, 4).reshape(B, T, H, GLA_DV)
    return rms_norm(o, norm_g).reshape(B, T, GLA_WIDTH) * jax.nn.silu(r.astype(f32))


def compress(a, pe, w1, b1, w2, b2):
    B, T, G, D = a.shape
    a16 = a.reshape(B, T // CMP_STRIDE, CMP_STRIDE, G, D)
    blocks = jnp.concatenate([a16[:, :-1], a16[:, 1:]], axis=2) + pe[None, None, :, None, :]
    flat = blocks.transpose(0, 1, 3, 2, 4).reshape(B, T // CMP_STRIDE - 1, G, CMP_BLOCK * D)
    return jax.nn.silu(flat @ w1 + b1) @ w2 + b2


def nsa_mixer(q, k_cmp, v_cmp, k_slc, v_slc, k_win, v_win, gate_logits, pe_k, pe_v,
              w_ck1, b_ck1, w_ck2, b_ck2, w_cv1, b_cv1, w_cv2, b_cv2):
    B, T, _ = q.shape
    G, HPG, DH = NSA_GROUPS, NSA_HPG, NSA_HEAD_DIM
    f32 = jnp.float32
    scale = DH ** -0.5
    pos = jnp.arange(T)
    qg = partial_rope(q.reshape(B, T, NSA_HEADS, DH), pos).reshape(B, T, G, HPG, DH)

    def kv(a):
        return a.reshape(B, T, G, DH)

    n_cmp = T // CMP_STRIDE - 1
    cmp_end = jnp.arange(n_cmp) * CMP_STRIDE + (CMP_BLOCK - 1)
    kc = partial_rope(compress(kv(k_cmp), pe_k, w_ck1, b_ck1, w_ck2, b_ck2), cmp_end)
    vc = compress(kv(v_cmp), pe_v, w_cv1, b_cv1, w_cv2, b_cv2)
    s = jnp.einsum('btghd,bngd->bghtn', qg, kc).astype(f32) * scale
    s = jnp.where(cmp_end[None, :] <= pos[:, None], s, -jnp.inf)
    m = jnp.max(s, -1, keepdims=True)
    e = jnp.exp(s - jnp.where(jnp.isfinite(m), m, 0.0))
    p_cmp = e / jnp.maximum(jnp.sum(e, -1, keepdims=True), 1e-30)
    o_cmp = jnp.einsum('bghtn,bngd->btghd', p_cmp, vc.astype(f32))

    n_blk = T // SEL_BLOCK
    topk = min(SEL_TOPK, n_blk)
    c_start = np.arange(n_cmp) * CMP_STRIDE
    b_start = np.arange(n_blk) * SEL_BLOCK
    overlap = ((c_start[:, None] < b_start[None, :] + SEL_BLOCK)
               & (c_start[:, None] + CMP_BLOCK > b_start[None, :])).astype(np.float32)
    p_blk = jnp.einsum('bghtn,nj->btgj', p_cmp, jnp.asarray(overlap))
    cur = pos // SEL_BLOCK
    j = jnp.arange(n_blk)
    forced = (j[None] == 0) | (j[None] == cur[:, None]) | (j[None] == cur[:, None] - 1)
    allowed = j[None] <= cur[:, None]
    score = jnp.where(forced[None, :, None, :], jnp.inf,
                      jnp.where(allowed[None, :, None, :], p_blk, -jnp.inf))
    _, sel_idx = lax.top_k(score, topk)
    kb = partial_rope(kv(k_slc), pos).reshape(B, n_blk, SEL_BLOCK, G, DH).transpose(0, 3, 1, 2, 4)
    vb = kv(v_slc).reshape(B, n_blk, SEL_BLOCK, G, DH).transpose(0, 3, 1, 2, 4)
    nq = T // SEL_QBLOCK
    q_blocks = qg.reshape(B, nq, SEL_QBLOCK, G, HPG, DH).transpose(1, 0, 3, 2, 4, 5)
    i_blocks = sel_idx.reshape(B, nq, SEL_QBLOCK, G, topk).transpose(1, 0, 3, 2, 4)
    p_blocks = pos.reshape(nq, SEL_QBLOCK)
    bi = jnp.arange(B)[:, None, None, None]
    gi = jnp.arange(G)[None, :, None, None]

    def sel_attend(args):
        qb, ib, pb = args
        kg = kb[bi, gi, ib]
        vg = vb[bi, gi, ib]
        s = jnp.einsum('bgqhd,bgqkpd->bgqhkp', qb, kg).astype(f32) * scale
        key_pos = ib[..., None] * SEL_BLOCK + jnp.arange(SEL_BLOCK)
        s = jnp.where((key_pos <= pb[None, None, :, None, None])[:, :, :, None], s, -jnp.inf)
        p = jax.nn.softmax(s.reshape(s.shape[:4] + (-1,)), axis=-1).reshape(s.shape)
        return jnp.einsum('bgqhkp,bgqkpd->bqghd', p, vg.astype(f32))

    o_slc = lax.map(sel_attend, (q_blocks, i_blocks, p_blocks))
    o_slc = o_slc.transpose(1, 0, 2, 3, 4, 5).reshape(B, T, G, HPG, DH)

    nw = T // WIN_QBLOCK
    n_pad = WINDOW // WIN_QBLOCK

    def band(a):
        ap = jnp.pad(a, ((0, 0), (WINDOW, 0), (0, 0), (0, 0))).reshape(B, nw + n_pad, WIN_QBLOCK, G, DH)
        return jnp.concatenate([ap[:, i:i + nw] for i in range(n_pad + 1)], axis=2)

    kw = band(partial_rope(kv(k_win), pos))
    vw = band(kv(v_win))
    qw = qg.reshape(B, nw, WIN_QBLOCK, G, HPG, DH)
    s = jnp.einsum('bnqghd,bnkgd->bnghqk', qw, kw).astype(f32) * scale
    q_pos = pos.reshape(nw, WIN_QBLOCK)
    k_pos = jnp.arange(nw)[:, None] * WIN_QBLOCK - WINDOW + jnp.arange((n_pad + 1) * WIN_QBLOCK)[None, :]
    dist = q_pos[:, :, None] - k_pos[:, None, :]
    win_mask = (dist >= 0) & (dist < WINDOW) & (k_pos[:, None, :] >= 0)
    p = jax.nn.softmax(jnp.where(win_mask[None, :, None, None], s, -jnp.inf), axis=-1)
    o_win = jnp.einsum('bnghqk,bnkgd->bnqghd', p, vw.astype(f32)).reshape(B, T, G, HPG, DH)

    gates = jax.nn.sigmoid(gate_logits.astype(f32)).reshape(B, T, G, HPG, 3)
    o = gates[..., 0:1] * o_cmp + gates[..., 1:2] * o_slc + gates[..., 2:3] * o_win
    return o.reshape(B, T, NSA_WIDTH)


def clamped_swiglu(gate, up):
    gate = jnp.minimum(gate, SWIGLU_LIMIT)
    up = jnp.clip(up, -SWIGLU_LIMIT, SWIGLU_LIMIT)
    return gate * jax.nn.sigmoid(SWIGLU_ALPHA * gate) * (up + 1.0)


def moe_ffn(h, w_router, b_router, w_gate, b_gate, w_up, b_up, w_down, b_down):
    B, T, D = h.shape
    xt = h.reshape(B * T, D)
    logits = (xt @ w_router + b_router).astype(jnp.float32)
    top_val, top_idx = lax.top_k(logits, TOP_K)
    top_w = jax.nn.softmax(top_val, axis=-1)
    combine = jnp.einsum('nk,nke->ne', top_w, jax.nn.one_hot(top_idx, N_EXPERTS, dtype=jnp.float32))
    out = jnp.zeros((B * T, D), jnp.float32)
    for e in range(N_EXPERTS):
        act = clamped_swiglu(xt @ w_gate[e] + b_gate[e], xt @ w_up[e] + b_up[e])
        out = out + combine[:, e:e + 1] * (act @ w_down[e] + b_down[e]).astype(jnp.float32)
    return out.reshape(B, T, D)


def setup_inputs(seed: int = 0) -> dict:
    key = jax.random.key(seed)
    ks = jax.random.split(key, 31)

    def nrm(i, shape, scale):
        return jax.random.normal(ks[i], shape, jnp.float32) * scale

    L, D, E, F = DEPTH, D_MODEL, N_EXPERTS, D_FF
    cin = CMP_BLOCK * NSA_HEAD_DIM
    return {
        'x': nrm(0, (BATCH, SEQ, D), 1.0),
        'c': nrm(1, (BATCH, D), 1.0),
        'w_ada': nrm(2, (L, D, 6 * D), 0.1 * D ** -0.5),
        'b_ada': nrm(3, (L, 6 * D), 0.02),
        'w_in': nrm(4, (L, D, D_IN), D ** -0.5),
        'w_gk': nrm(5, (L, GLA_RANK, GLA_HEADS * GLA_DK), GLA_RANK ** -0.5),
        'b_gk': nrm(6, (L, GLA_HEADS * GLA_DK), 0.1),
        'gla_norm_g': 1.0 + nrm(7, (L, GLA_DV), 0.02),
        'pe_k': nrm(8, (L, CMP_BLOCK, NSA_HEAD_DIM), 0.1),
        'pe_v': nrm(9, (L, CMP_BLOCK, NSA_HEAD_DIM), 0.1),
        'w_ck1': nrm(10, (L, cin, CMP_HIDDEN), cin ** -0.5),
        'b_ck1': nrm(11, (L, CMP_HIDDEN), 0.02),
        'w_ck2': nrm(12, (L, CMP_HIDDEN, NSA_HEAD_DIM), CMP_HIDDEN ** -0.5),
        'b_ck2': nrm(13, (L, NSA_HEAD_DIM), 0.02),
        'w_cv1': nrm(14, (L, cin, CMP_HIDDEN), cin ** -0.5),
        'b_cv1': nrm(15, (L, CMP_HIDDEN), 0.02),
        'w_cv2': nrm(16, (L, CMP_HIDDEN, NSA_HEAD_DIM), CMP_HIDDEN ** -0.5),
        'b_cv2': nrm(17, (L, NSA_HEAD_DIM), 0.02),
        'w_o': nrm(18, (L, D, D), DN_BETA * D ** -0.5),
        'ln1_g': 1.0 + nrm(19, (L, D), 0.02),
        'ln1_b': nrm(20, (L, D), 0.02),
        'w_router': nrm(21, (L, D, E), D ** -0.5),
        'b_router': nrm(22, (L, E), 0.01),
        'w_gate': nrm(23, (L, E, D, F), D ** -0.5),
        'b_gate': nrm(24, (L, E, F), 0.02),
        'w_up': nrm(25, (L, E, D, F), D ** -0.5),
        'b_up': nrm(26, (L, E, F), 0.02),
        'w_down': nrm(27, (L, E, F, D), DN_BETA * F ** -0.5),
        'b_down': nrm(28, (L, E, D), 0.02),
        'ln2_g': 1.0 + nrm(29, (L, D), 0.02),
        'ln2_b': nrm(30, (L, D), 0.02),
    }


def reference(x, c, w_ada, b_ada, w_in, w_gk, b_gk, gla_norm_g, pe_k, pe_v,
              w_ck1, b_ck1, w_ck2, b_ck2, w_cv1, b_cv1, w_cv2, b_cv2, w_o, ln1_g, ln1_b,
              w_router, b_router, w_gate, b_gate, w_up, b_up, w_down, b_down, ln2_g, ln2_b):
    split_at = tuple(int(v) for v in np.cumsum(IN_SPLITS)[:-1])
    for l in range(DEPTH):
        mod = (jax.nn.silu(c) @ w_ada[l] + b_ada[l])[:, None, :]
        sh1, sc1, g1, sh2, sc2, g2 = jnp.split(mod, 6, axis=-1)

        h = layer_norm(x) * (1.0 + sc1) + sh1
        (gq, gk, gv, gr, glr, nq, kcm, vcm, ksl, vsl, kwn, vwn, ngt) = jnp.split(h @ w_in[l], split_at, axis=-1)
        y_gla = gla_mixer(gq, gk, gv, gr, glr, w_gk[l], b_gk[l], gla_norm_g[l])
        y_nsa = nsa_mixer(nq, kcm, vcm, ksl, vsl, kwn, vwn, ngt, pe_k[l], pe_v[l],
                          w_ck1[l], b_ck1[l], w_ck2[l], b_ck2[l], w_cv1[l], b_cv1[l], w_cv2[l], b_cv2[l])
        mix = jnp.concatenate([y_gla, y_nsa], -1).astype(x.dtype) @ w_o[l]
        x = layer_norm(DN_ALPHA * x + (1.0 + g1) * mix, ln1_g[l], ln1_b[l])

        h2 = layer_norm(x) * (1.0 + sc2) + sh2
        y_ffn = moe_ffn(h2, w_router[l], b_router[l], w_gate[l], b_gate[l],
                        w_up[l], b_up[l], w_down[l], b_down[l]).astype(x.dtype)
        x = layer_norm(DN_ALPHA * x + (1.0 + g2) * y_ffn, ln2_g[l], ln2_b[l])
    return x
```

```python
import functools

import numpy as np
import jax
import jax.numpy as jnp
from jax import lax
from jax.experimental import pallas as pl
from jax.experimental.pallas import tpu as pltpu

F32 = jnp.float32
BF16 = jnp.bfloat16

D_MODEL = 2048
BATCH = 2
SEQ = 4096
N_TOK = BATCH * SEQ

GLA_HEADS = 4
GLA_DK = 128
GLA_DV = 256
GLA_RANK = 16
GLA_TAU = 16.0
GLA_CHUNK = 64

NSA_DH = 128
NSA_HEADS = 8
NSA_GROUPS = 2
NSA_HPG = 4
CMP_STRIDE = 16
CMP_BLOCK = 32
CMP_HIDDEN = 256
N_CMP_PAD = SEQ // CMP_STRIDE
SEL_BLOCK = 64
N_BLK = SEQ // SEL_BLOCK
SEL_TOPK = 16
WINDOW = 512
ROPE_DIM = 32
ROPE_THETA = 500000.0

N_EXPERTS = 32
TOP_K = 4
D_FF = D_MODEL
SWIGLU_LIMIT = 7.0
SWIGLU_ALPHA = 1.702
DN_ALPHA = 2.0 ** 0.25
LN_EPS = 1e-5

COL_GQ, COL_GK, COL_GV, COL_GR, COL_NQ = 0, 512, 1024, 2048, 3072
COL_KV = 4096
COL_TAIL = 5632
D_IN_PAD = 5760
GATE_LANE0 = GLA_RANK

LANE = 128
NEG = -1e30
VMEM_LIMIT = 56 * 1024 * 1024

TM_IN, TN_IN = 1024, 640
TT_GLA = 512
TR_PREP = 512
TQ_NSA = 128
TK_SEL = 256
WIN_SPAN = WINDOW + TQ_NSA
TM_OUT = 256
TM_MOE, TN_MOE = 512, 512
P_ROWS = N_TOK * TOP_K + N_EXPERTS * TM_MOE
N_MTILES = P_ROWS // TM_MOE
TM_FIN = 256


def _dot(a, b):
    return jnp.dot(a, b, preferred_element_type=F32)


def _dot_nt(a, b):
    return lax.dot_general(a, b, (((1,), (1,)), ((), ())), preferred_element_type=F32)


def _dot_tn(a, b):
    return lax.dot_general(a, b, (((0,), (0,)), ((), ())), preferred_element_type=F32)


def _ln(x):
    xc = x - jnp.mean(x, -1, keepdims=True)
    return xc * lax.rsqrt(jnp.mean(xc * xc, -1, keepdims=True) + LN_EPS)


def _split_bf16(x):
    hi = x.astype(BF16)
    lo = (x - hi.astype(F32)).astype(BF16)
    return hi, lo


def _params(*sem):
    return pltpu.CompilerParams(dimension_semantics=sem, vmem_limit_bytes=VMEM_LIMIT)


def _adaln_kernel(c_ref, w_ref, b_ref, o_ref):
    c = c_ref[...]
    a = (c * jax.nn.sigmoid(c)).astype(BF16)
    o_ref[...] = _dot(a, w_ref[...].astype(BF16)) + b_ref[...]


def _adaln(c8, w, b):
    n = w.shape[1]
    tn = 1024
    return pl.pallas_call(
        _adaln_kernel,
        grid=(n // tn,),
        in_specs=[pl.BlockSpec((8, D_MODEL), lambda j: (0, 0)),
                  pl.BlockSpec((D_MODEL, tn), lambda j: (0, j)),
                  pl.BlockSpec((1, tn), lambda j: (0, j))],
        out_specs=pl.BlockSpec((8, tn), lambda j: (0, j)),
        out_shape=jax.ShapeDtypeStruct((8, n), F32),
        compiler_params=_params("arbitrary"),
        name="adaln",
    )(c8, w, b)


def _inproj_kernel(x_ref, sh_ref, sc_ref, w_ref, o_ref, h_sc):
    @pl.when(pl.program_id(1) == 0)
    def _():
        h = _ln(x_ref[...]) * (1.0 + sc_ref[...]) + sh_ref[...]
        h_sc[...] = h.astype(BF16)

    o_ref[...] = _dot(h_sc[...], w_ref[...])


def _inproj(xf, mod3, w_in_p):
    tiles_per_batch = SEQ // TM_IN
    return pl.pallas_call(
        _inproj_kernel,
        grid=(N_TOK // TM_IN, D_IN_PAD // TN_IN),
        in_specs=[pl.BlockSpec((TM_IN, D_MODEL), lambda i, j: (i, 0)),
                  pl.BlockSpec((None, 1, D_MODEL), lambda i, j: (i // tiles_per_batch, 0, 0)),
                  pl.BlockSpec((None, 1, D_MODEL), lambda i, j: (i // tiles_per_batch, 0, 1)),
                  pl.BlockSpec((D_MODEL, TN_IN), lambda i, j: (0, j))],
        out_specs=pl.BlockSpec((TM_IN, TN_IN), lambda i, j: (i, j)),
        out_shape=jax.ShapeDtypeStruct((N_TOK, D_IN_PAD), F32),
        scratch_shapes=[pltpu.VMEM((TM_IN, D_MODEL), BF16)],
        compiler_params=_params("parallel", "arbitrary"),
        name="inproj",
    )(xf, mod3, mod3, w_in_p)


GLA_HALVES = (32, 16, 8, 4, 2, 1)
N_LEVELS = len(GLA_HALVES)
ROW_EB = 2 * N_LEVELS
ROW_EL = 2 * N_LEVELS + 1
N_EVIEWS = 2 * N_LEVELS + 2


def _gla_constants():
    c = GLA_CHUNK
    t = np.arange(c)[:, None]
    r = np.arange(c)[None, :]
    mall = np.zeros((N_EVIEWS, c, c), np.float32)
    valid = np.zeros((N_EVIEWS, c, LANE), np.float32)
    masks = np.zeros((N_LEVELS + 1, c, c), np.float32)
    for li, n in enumerate(GLA_HALVES):
        same = (t // (2 * n)) == (r // (2 * n))
        t_up = (t % (2 * n)) >= n
        r_up = (r % (2 * n)) >= n
        mall[2 * li] = same & t_up & r_up & (r <= t)
        mall[2 * li + 1] = same & ~t_up & ~r_up & (r > t)
        valid[2 * li] = np.broadcast_to(t_up, (c, LANE))
        valid[2 * li + 1] = np.broadcast_to(~t_up, (c, LANE))
        masks[li] = same & t_up & ~r_up
    mall[ROW_EB] = r <= t
    mall[ROW_EL] = r > t
    valid[ROW_EB] = 1.0
    valid[ROW_EL] = 1.0
    masks[N_LEVELS] = np.eye(c)
    return (mall.reshape(N_EVIEWS * c, c), valid.reshape(N_EVIEWS * c, LANE), masks)


def _gla_kernel(q_ref, k_ref, v_ref, r_ref, glr_ref, wgk_ref, bgk_ref, g_ref, mall_ref, valid_ref,
                masks_ref, o_ref, st_sc):
    c = GLA_CHUNK

    @pl.when(pl.program_id(2) == 0)
    def _():
        st_sc[...] = jnp.zeros_like(st_sc)

    z = _dot(glr_ref[...].astype(BF16), wgk_ref[...].astype(BF16)) + bgk_ref[...]
    log_a = (jnp.minimum(z, 0.0) - jnp.log1p(jnp.exp(-jnp.abs(z)))) * (1.0 / GLA_TAU)
    mall = mall_ref[...]
    for ci in range(TT_GLA // c):
        rows = slice(ci * c, (ci + 1) * c)
        la_hi, la_lo = _split_bf16(log_a[rows])
        e = jnp.exp(_dot(mall, la_hi) + _dot(mall, la_lo)) * valid_ref[...]
        q = q_ref[rows, :] * (GLA_DK ** -0.5)
        k = k_ref[rows, :]
        vb = v_ref[rows, :].astype(BF16)
        att = masks_ref[N_LEVELS] * _dot_nt(q.astype(BF16), k.astype(BF16))
        for li in range(N_LEVELS):
            eq = e[(2 * li) * c:(2 * li + 1) * c]
            ek = e[(2 * li + 1) * c:(2 * li + 2) * c]
            att = att + masks_ref[li] * _dot_nt((q * eq).astype(BF16), (k * ek).astype(BF16))
        eb = e[ROW_EB * c:(ROW_EB + 1) * c]
        el = e[ROW_EL * c:(ROW_EL + 1) * c]
        st = st_sc[...]
        o = _dot_nt((q * eb).astype(BF16), st.astype(BF16)) + _dot(att.astype(BF16), vb)
        st_sc[...] = st * eb[c - 1:c, :] + _dot_tn(vb, (k * el).astype(BF16))
        o = o * lax.rsqrt(jnp.mean(o * o, -1, keepdims=True) + LN_EPS) * g_ref[...]
        r = r_ref[rows, :]
        o_ref[rows, :] = (o * (r * jax.nn.sigmoid(r))).astype(BF16)


def _gla(proj, w_gk_pad, b_gk, norm_g):
    nt = SEQ // TT_GLA
    mall, valid, masks = _gla_constants()
    row = lambda b, h, i: b * nt + i
    return pl.pallas_call(
        _gla_kernel,
        grid=(BATCH, GLA_HEADS, nt),
        in_specs=[pl.BlockSpec((TT_GLA, GLA_DK), lambda b, h, i: (row(b, h, i), COL_GQ // GLA_DK + h)),
                  pl.BlockSpec((TT_GLA, GLA_DK), lambda b, h, i: (row(b, h, i), COL_GK // GLA_DK + h)),
                  pl.BlockSpec((TT_GLA, GLA_DV), lambda b, h, i: (row(b, h, i), COL_GV // GLA_DV + h)),
                  pl.BlockSpec((TT_GLA, GLA_DV), lambda b, h, i: (row(b, h, i), COL_GR // GLA_DV + h)),
                  pl.BlockSpec((TT_GLA, LANE), lambda b, h, i: (row(b, h, i), COL_TAIL // LANE)),
                  pl.BlockSpec((LANE, GLA_DK), lambda b, h, i: (0, h)),
                  pl.BlockSpec((1, GLA_DK), lambda b, h, i: (0, h)),
                  pl.BlockSpec((1, GLA_DV), lambda b, h, i: (0, 0)),
                  pl.BlockSpec(mall.shape, lambda b, h, i: (0, 0)),
                  pl.BlockSpec(valid.shape, lambda b, h, i: (0, 0)),
                  pl.BlockSpec(masks.shape, lambda b, h, i: (0, 0, 0))],
        out_specs=pl.BlockSpec((TT_GLA, GLA_DV), lambda b, h, i: (row(b, h, i), h)),
        out_shape=jax.ShapeDtypeStruct((N_TOK, GLA_HEADS * GLA_DV), BF16),
        scratch_shapes=[pltpu.VMEM((GLA_DV, GLA_DK), F32)],
        compiler_params=_params("parallel", "parallel", "arbitrary"),
        name="gla",
    )(proj, proj, proj, proj, proj, w_gk_pad, b_gk, norm_g,
      jnp.asarray(mall, BF16), jnp.asarray(valid), jnp.asarray(masks))


def _rope_tables(pos):
    half = ROPE_DIM // 2
    inv_freq = ROPE_THETA ** (-jnp.arange(half, dtype=F32) * (2.0 / ROPE_DIM))
    ang = pos.astype(F32)[:, None] * inv_freq
    cos, sin = jnp.cos(ang), jnp.sin(ang)
    n = pos.shape[0]
    cosf = jnp.concatenate([cos, cos, jnp.ones((n, LANE - ROPE_DIM), F32)], -1)
    sinf = jnp.concatenate([-sin, sin, jnp.zeros((n, LANE - ROPE_DIM), F32)], -1)
    return cosf, sinf


def _rope(x, cosf, sinf):
    lane = lax.broadcasted_iota(jnp.int32, x.shape, 1)
    half = ROPE_DIM // 2
    swapped = jnp.where(lane < half, pltpu.roll(x, LANE - half, 1), pltpu.roll(x, half, 1))
    return x * cosf + swapped * sinf


def _prep_kernel(q_ref, ks_ref, vs_ref, kw_ref, vw_ref, cos_ref, sin_ref,
                 qo_ref, kso_ref, vso_ref, kwo_ref, vwo_ref):
    cosf = cos_ref[...]
    sinf = sin_ref[...]
    for hh in range(NSA_HEADS):
        cols = slice(hh * NSA_DH, (hh + 1) * NSA_DH)
        qo_ref[:, cols] = _rope(q_ref[:, cols], cosf, sinf).astype(BF16)
    for g in range(NSA_GROUPS):
        cols = slice(g * NSA_DH, (g + 1) * NSA_DH)
        kso_ref[g] = _rope(ks_ref[:, cols], cosf, sinf).astype(BF16)
        kwo_ref[g] = _rope(kw_ref[:, cols], cosf, sinf).astype(BF16)
        vso_ref[g] = vs_ref[:, cols].astype(BF16)
        vwo_ref[g] = vw_ref[:, cols].astype(BF16)


def _prep(proj, cosf, sinf):
    nt = SEQ // TR_PREP
    kvw = NSA_GROUPS * NSA_DH
    kv_in = lambda which: pl.BlockSpec((TR_PREP, kvw), lambda b, i: (b * nt + i, COL_KV // kvw + which))
    kv_out = pl.BlockSpec((None, NSA_GROUPS, TR_PREP, NSA_DH), lambda b, i: (b, 0, i, 0))
    kv_shape = jax.ShapeDtypeStruct((BATCH, NSA_GROUPS, SEQ, NSA_DH), BF16)
    qw = NSA_HEADS * NSA_DH
    return pl.pallas_call(
        _prep_kernel,
        grid=(BATCH, nt),
        in_specs=[pl.BlockSpec((TR_PREP, qw), lambda b, i: (b * nt + i, COL_NQ // qw)),
                  kv_in(2), kv_in(3), kv_in(4), kv_in(5),
                  pl.BlockSpec((TR_PREP, LANE), lambda b, i: (i, 0)),
                  pl.BlockSpec((TR_PREP, LANE), lambda b, i: (i, 0))],
        out_specs=[pl.BlockSpec((TR_PREP, qw), lambda b, i: (b * nt + i, 0)),
                   kv_out, kv_out, kv_out, kv_out],
        out_shape=[jax.ShapeDtypeStruct((N_TOK, qw), BF16), kv_shape, kv_shape, kv_shape, kv_shape],
        compiler_params=_params("parallel", "parallel"),
        name="nsa_prep",
    )(proj, proj, proj, proj, proj, cosf, sinf)


def _compress_kernel(a_ref, pe_ref, w1_ref, b1_ref, w2_ref, b2_ref, cos_ref, sin_ref, o_ref, bot_sc,
                     *, rope):
    n = N_CMP_PAD
    top = jnp.zeros((n, CMP_HIDDEN), F32)
    bot = jnp.zeros((n, CMP_HIDDEN), F32)
    for p in range(CMP_STRIDE):
        ap = a_ref[pl.ds(p, n, stride=CMP_STRIDE), :]
        w_top = w1_ref[p * NSA_DH:(p + 1) * NSA_DH, :].astype(BF16)
        w_bot = w1_ref[(CMP_STRIDE + p) * NSA_DH:(CMP_STRIDE + p + 1) * NSA_DH, :].astype(BF16)
        top = top + _dot((ap + pe_ref[p:p + 1, :]).astype(BF16), w_top)
        bot = bot + _dot((ap + pe_ref[CMP_STRIDE + p:CMP_STRIDE + p + 1, :]).astype(BF16), w_bot)
    bot_sc[0:n, :] = bot
    bot_sc[n:n + 8, :] = jnp.zeros((8, CMP_HIDDEN), F32)
    h = top + bot_sc[1:n + 1, :] + b1_ref[...]
    h = h * jax.nn.sigmoid(h)
    out = _dot(h.astype(BF16), w2_ref[...].astype(BF16)) + b2_ref[...]
    if rope:
        out = _rope(out, cos_ref[...], sin_ref[...])
    row = lax.broadcasted_iota(jnp.int32, out.shape, 0)
    o_ref[...] = jnp.where(row < n - 1, out, 0.0).astype(BF16)


def _compress(proj, which, pe, w1, b1, w2, b2, cosf, sinf, rope):
    col0 = (COL_KV + which * NSA_GROUPS * NSA_DH) // NSA_DH
    full = lambda shape: pl.BlockSpec(shape, lambda b, g: tuple(0 for _ in shape))
    return pl.pallas_call(
        functools.partial(_compress_kernel, rope=rope),
        grid=(BATCH, NSA_GROUPS),
        in_specs=[pl.BlockSpec((SEQ, NSA_DH), lambda b, g: (b, col0 + g)),
                  full(pe.shape), full(w1.shape), full(b1.shape), full(w2.shape), full(b2.shape),
                  full(cosf.shape), full(sinf.shape)],
        out_specs=pl.BlockSpec((None, None, N_CMP_PAD, NSA_DH), lambda b, g: (b, g, 0, 0)),
        out_shape=jax.ShapeDtypeStruct((BATCH, NSA_GROUPS, N_CMP_PAD, NSA_DH), BF16),
        scratch_shapes=[pltpu.VMEM((N_CMP_PAD + 8, CMP_HIDDEN), F32)],
        compiler_params=_params("parallel", "parallel"),
        name="nsa_compress",
    )(proj, pe, w1, b1, w2, b2, cosf, sinf)


def _nsa_constants():
    c_start = np.arange(N_CMP_PAD) * CMP_STRIDE
    b_start = np.arange(N_BLK) * SEL_BLOCK
    overlap_t = ((c_start[None, :] < b_start[:, None] + SEL_BLOCK)
                 & (c_start[None, :] + CMP_BLOCK > b_start[:, None])).astype(np.float32)
    overlap_t[:, N_CMP_PAD - 1] = 0.0
    expand = (np.arange(SEQ)[None, :] // SEL_BLOCK == np.arange(LANE)[:, None]).astype(np.float32)
    return overlap_t, expand


def _tile_heads(a):
    return jnp.concatenate([a] * NSA_HPG, axis=0)


def _nsa_kernel(q_ref, kc_ref, vc_ref, ks_ref, vs_ref, kw_ref, vw_ref, gate_ref, ovt_ref, expand_ref,
                o_ref, score_sc, m_sc, l_sc, acc_sc):
    tq = TQ_NSA
    rows = NSA_HPG * tq
    scale = NSA_DH ** -0.5
    g = pl.program_id(1)
    q0 = pl.program_id(2) * tq
    qs = jnp.concatenate([q_ref[:, hh * NSA_DH:(hh + 1) * NSA_DH] for hh in range(NSA_HPG)], axis=0)
    pos_t = q0 + lax.broadcasted_iota(jnp.int32, (tq, 1), 0)

    s = _dot_nt(qs, kc_ref[...]) * scale
    cmp_end = lax.broadcasted_iota(jnp.int32, (tq, N_CMP_PAD), 1) * CMP_STRIDE + (CMP_BLOCK - 1)
    s = s + _tile_heads(jnp.where(cmp_end <= pos_t, 0.0, NEG))
    m = jnp.max(s, -1, keepdims=True)
    e = jnp.where(s > 0.5 * NEG, jnp.exp(s - m), 0.0)
    p = e / jnp.maximum(jnp.sum(e, -1, keepdims=True), 1e-30)
    o_cmp = _dot(p.astype(BF16), vc_ref[...])
    p_sum = p[0:tq] + p[tq:2 * tq] + p[2 * tq:3 * tq] + p[3 * tq:4 * tq]
    ps_hi, ps_lo = _split_bf16(p_sum)
    ovt = ovt_ref[...]
    p_blk_t = _dot_nt(ovt, ps_hi) + _dot_nt(ovt, ps_lo)

    jj = lax.broadcasted_iota(jnp.int32, (N_BLK, tq), 0)
    cur = (q0 + lax.broadcasted_iota(jnp.int32, (N_BLK, tq), 1)) // SEL_BLOCK
    forced = (jj == 0) | (jj == cur) | (jj == cur - 1)
    allowed = jj <= cur
    score = jnp.where(forced, 3.0e38, jnp.where(allowed, p_blk_t, -1.0))
    score_sc[...] = score

    def rank_body(i, rank):
        row = score_sc[pl.ds(i, 1), :]
        first = jnp.where(jj > i, 1.0, 0.0)
        return rank + jnp.where(row > score, 1.0, jnp.where(row == score, first, 0.0))

    n_live = (q0 + tq - 1) // SEL_BLOCK + 1
    rank = lax.fori_loop(0, n_live, rank_body, jnp.zeros((N_BLK, tq), F32))
    sel_t = jnp.where(allowed, jnp.where(rank < SEL_TOPK, 1.0, 0.0), 0.0)
    sel = jnp.concatenate([sel_t, jnp.zeros((LANE - N_BLK, tq), F32)], axis=0).T.astype(BF16)

    m_sc[...] = jnp.full(m_sc.shape, NEG, F32)
    l_sc[...] = jnp.zeros(l_sc.shape, F32)
    acc_sc[...] = jnp.zeros(acc_sc.shape, F32)

    def sel_body(kt, carry):
        k0 = pl.multiple_of(kt * TK_SEL, TK_SEL)
        s = _dot_nt(qs, ks_ref[pl.ds(k0, TK_SEL), :]) * scale
        sel_x = _dot(sel, expand_ref[:, pl.ds(k0, TK_SEL)])
        kpos = k0 + lax.broadcasted_iota(jnp.int32, (tq, TK_SEL), 1)
        bias = jnp.where(kpos <= pos_t, jnp.where(sel_x > 0.5, 0.0, NEG), NEG)
        s = s + _tile_heads(bias)
        m_prev = m_sc[...]
        m_new = jnp.maximum(m_prev, jnp.max(s, -1, keepdims=True))
        alpha = jnp.exp(m_prev - m_new)
        p = jnp.exp(s - jnp.concatenate([m_new] * (TK_SEL // LANE), axis=1))
        l_sc[...] = alpha * l_sc[...] + jnp.sum(p, -1, keepdims=True)
        acc_sc[...] = alpha * acc_sc[...] + _dot(p.astype(BF16), vs_ref[pl.ds(k0, TK_SEL), :])
        m_sc[...] = m_new
        return carry

    lax.fori_loop(0, (q0 + tq - 1) // TK_SEL + 1, sel_body, 0)
    o_slc = acc_sc[...] / l_sc[...]

    start = pl.multiple_of(jnp.maximum(q0 - WINDOW, 0), LANE)
    s = _dot_nt(qs, kw_ref[pl.ds(start, WIN_SPAN), :]) * scale
    dist = pos_t - (start + lax.broadcasted_iota(jnp.int32, (tq, WIN_SPAN), 1))
    bias = jnp.where(dist >= 0, jnp.where(dist < WINDOW, 0.0, NEG), NEG)
    s = s + _tile_heads(bias)
    p = jnp.exp(s - jnp.max(s, -1, keepdims=True))
    o_win = _dot(p.astype(BF16), vw_ref[pl.ds(start, WIN_SPAN), :]) / jnp.sum(p, -1, keepdims=True)

    gates = jax.nn.sigmoid(gate_ref[...])

    def gate(hh, branch):
        lane0 = GATE_LANE0 + hh * 3 + branch
        lane1 = lane0 + NSA_HPG * 3
        return jnp.where(g == 0, gates[:, lane0:lane0 + 1], gates[:, lane1:lane1 + 1])

    for hh in range(NSA_HPG):
        r = slice(hh * tq, (hh + 1) * tq)
        o = gate(hh, 0) * o_cmp[r] + gate(hh, 1) * o_slc[r] + gate(hh, 2) * o_win[r]
        o_ref[:, hh * NSA_DH:(hh + 1) * NSA_DH] = o.astype(BF16)


def _nsa(q_r, kc, vc, ks, vs, kw, vw, proj):
    nq = SEQ // TQ_NSA
    gw = NSA_HPG * NSA_DH
    overlap_t, expand = _nsa_constants()
    rows = NSA_HPG * TQ_NSA
    cmp_spec = pl.BlockSpec((None, None, N_CMP_PAD, NSA_DH), lambda b, g, i: (b, g, 0, 0))
    kv_spec = pl.BlockSpec((None, None, SEQ, NSA_DH), lambda b, g, i: (b, g, 0, 0))
    return pl.pallas_call(
        _nsa_kernel,
        grid=(BATCH, NSA_GROUPS, nq),
        in_specs=[pl.BlockSpec((TQ_NSA, gw), lambda b, g, i: (b * nq + i, g)),
                  cmp_spec, cmp_spec, kv_spec, kv_spec, kv_spec, kv_spec,
                  pl.BlockSpec((TQ_NSA, LANE), lambda b, g, i: (b * nq + i, COL_TAIL // LANE)),
                  pl.BlockSpec(overlap_t.shape, lambda b, g, i: (0, 0)),
                  pl.BlockSpec(expand.shape, lambda b, g, i: (0, 0))],
        out_specs=pl.BlockSpec((TQ_NSA, gw), lambda b, g, i: (b * nq + i, g)),
        out_shape=jax.ShapeDtypeStruct((N_TOK, NSA_GROUPS * gw), BF16),
        scratch_shapes=[pltpu.VMEM((N_BLK, TQ_NSA), F32),
                        pltpu.VMEM((rows, LANE), F32),
                        pltpu.VMEM((rows, LANE), F32),
                        pltpu.VMEM((rows, NSA_DH), F32)],
        compiler_params=_params("parallel", "parallel", "arbitrary"),
        name="nsa_attn",
    )(q_r, kc, vc, ks, vs, kw, vw, proj, jnp.asarray(overlap_t, BF16), jnp.asarray(expand, BF16))


def _outproj_kernel(yg_ref, yn_ref, x_ref, wo_ref, g1_ref, sc2_ref, sh2_ref, ln1g_ref, ln1b_ref,
                    wr_hi_ref, wr_lo_ref, br_ref, x1_ref, h2_ref, idx_ref, wt_ref):
    half = D_MODEL // 2
    mix = _dot(yg_ref[...], wo_ref[0:half, :]) + _dot(yn_ref[...], wo_ref[half:D_MODEL, :])
    x1 = _ln(DN_ALPHA * x_ref[...] + (1.0 + g1_ref[...]) * mix) * ln1g_ref[...] + ln1b_ref[...]
    x1_ref[...] = x1
    h2 = _ln(x1) * (1.0 + sc2_ref[...]) + sh2_ref[...]
    h_hi, h_lo = _split_bf16(h2)
    h2_ref[...] = h_hi
    wr_hi = wr_hi_ref[...]
    logits = _dot(h_hi, wr_hi) + _dot(h_lo, wr_hi) + _dot(h_hi, wr_lo_ref[...]) + br_ref[...]

    lane = lax.broadcasted_iota(jnp.int32, logits.shape, 1)
    lane_f = lane.astype(F32)
    vals = logits
    idx_out = jnp.zeros(logits.shape, jnp.int32)
    exp_out = jnp.zeros(logits.shape, F32)
    denom = jnp.zeros((logits.shape[0], 1), F32)
    m0 = None
    for k in range(TOP_K):
        mk = jnp.max(vals, -1, keepdims=True)
        ik = jnp.min(jnp.where(vals == mk, lane_f, float(LANE)), -1, keepdims=True)
        if k == 0:
            m0 = mk
        ek = jnp.exp(mk - m0)
        denom = denom + ek
        idx_out = jnp.where(lane == k, ik.astype(jnp.int32), idx_out)
        exp_out = jnp.where(lane == k, ek, exp_out)
        vals = jnp.where(lane_f == ik, -3.0e38, vals)
    idx_ref[...] = idx_out
    wt_ref[...] = exp_out / denom


def _outproj(y_gla, y_nsa, xf, w_o, mod3, ln1_g, ln1_b, wr_hi, wr_lo, br):
    tiles_per_batch = SEQ // TM_OUT
    half = D_MODEL // 2
    mod_spec = lambda chunk: pl.BlockSpec((None, 1, D_MODEL), lambda i: (i // tiles_per_batch, 0, chunk))
    full = lambda shape: pl.BlockSpec(shape, lambda i: tuple(0 for _ in shape))
    row = lambda width: pl.BlockSpec((TM_OUT, width), lambda i: (i, 0))
    return pl.pallas_call(
        _outproj_kernel,
        grid=(N_TOK // TM_OUT,),
        in_specs=[row(half), row(half), row(D_MODEL), full(w_o.shape),
                  mod_spec(2), mod_spec(4), mod_spec(3),
                  full(ln1_g.shape), full(ln1_b.shape), full(wr_hi.shape), full(wr_lo.shape), full(br.shape)],
        out_specs=[row(D_MODEL), row(D_MODEL), row(LANE), row(LANE)],
        out_shape=[jax.ShapeDtypeStruct((N_TOK, D_MODEL), F32),
                   jax.ShapeDtypeStruct((N_TOK, D_MODEL), BF16),
                   jax.ShapeDtypeStruct((N_TOK, LANE), jnp.int32),
                   jax.ShapeDtypeStruct((N_TOK, LANE), F32)],
        compiler_params=_params("parallel"),
        name="outproj_router",
    )(y_gla, y_nsa, xf, w_o, mod3, mod3, mod3, ln1_g, ln1_b, wr_hi, wr_lo, br)


def _moe_up_kernel(eid_ref, nused_ref, x_ref, wg_ref, wu_ref, bg_ref, bu_ref, h_ref, wg_sc, wu_sc):
    t = pl.program_id(1)
    live = t < nused_ref[0]

    @pl.when(live)
    def _():
        @pl.when((t == 0) | (eid_ref[t] != eid_ref[jnp.maximum(t - 1, 0)]))
        def _():
            wg_sc[...] = wg_ref[...].astype(BF16)
            wu_sc[...] = wu_ref[...].astype(BF16)

        x = x_ref[...]
        gate = jnp.minimum(_dot(x, wg_sc[...]) + bg_ref[...], SWIGLU_LIMIT)
        up = jnp.clip(_dot(x, wu_sc[...]) + bu_ref[...], -SWIGLU_LIMIT, SWIGLU_LIMIT)
        h_ref[...] = (gate * jax.nn.sigmoid(SWIGLU_ALPHA * gate) * (up + 1.0)).astype(BF16)

    @pl.when(jnp.logical_not(live))
    def _():
        h_ref[...] = jnp.zeros(h_ref.shape, BF16)


def _moe_down_kernel(eid_ref, nused_ref, h_ref, wd_ref, bd_ref, y_ref, wd_sc):
    t = pl.program_id(1)
    live = t < nused_ref[0]

    @pl.when(live)
    def _():
        @pl.when((t == 0) | (eid_ref[t] != eid_ref[jnp.maximum(t - 1, 0)]))
        def _():
            wd_sc[...] = wd_ref[...].astype(BF16)

        y_ref[...] = _dot(h_ref[...], wd_sc[...]) + bd_ref[...]

    @pl.when(jnp.logical_not(live))
    def _():
        y_ref[...] = jnp.zeros(y_ref.shape, F32)


def _row_tile(t, nused_ref):
    return jnp.minimum(t, jnp.maximum(nused_ref[0] - 1, 0))


def _moe_up(tile_eid, n_used, xs, w_gate, w_up, b_gate, b_up):
    w_spec = pl.BlockSpec((None, D_MODEL, TN_MOE), lambda j, t, eid, nu: (eid[t], 0, j))
    b_spec = pl.BlockSpec((None, 1, TN_MOE), lambda j, t, eid, nu: (eid[t], 0, j))
    return pl.pallas_call(
        _moe_up_kernel,
        grid_spec=pltpu.PrefetchScalarGridSpec(
            num_scalar_prefetch=2,
            grid=(D_FF // TN_MOE, N_MTILES),
            in_specs=[pl.BlockSpec((TM_MOE, D_MODEL), lambda j, t, eid, nu: (_row_tile(t, nu), 0)),
                      w_spec, w_spec, b_spec, b_spec],
            out_specs=pl.BlockSpec((TM_MOE, TN_MOE), lambda j, t, eid, nu: (t, j)),
            scratch_shapes=[pltpu.VMEM((D_MODEL, TN_MOE), BF16), pltpu.VMEM((D_MODEL, TN_MOE), BF16)]),
        out_shape=jax.ShapeDtypeStruct((P_ROWS, D_FF), BF16),
        compiler_params=_params("arbitrary", "arbitrary"),
        name="moe_up",
    )(tile_eid, n_used, xs, w_gate, w_up, b_gate, b_up)


def _moe_down(tile_eid, n_used, h, w_down, b_down):
    return pl.pallas_call(
        _moe_down_kernel,
        grid_spec=pltpu.PrefetchScalarGridSpec(
            num_scalar_prefetch=2,
            grid=(D_MODEL // TN_MOE, N_MTILES),
            in_specs=[pl.BlockSpec((TM_MOE, D_FF), lambda j, t, eid, nu: (_row_tile(t, nu), 0)),
                      pl.BlockSpec((None, D_FF, TN_MOE), lambda j, t, eid, nu: (eid[t], 0, j)),
                      pl.BlockSpec((None, 1, TN_MOE), lambda j, t, eid, nu: (eid[t], 0, j))],
            out_specs=pl.BlockSpec((TM_MOE, TN_MOE), lambda j, t, eid, nu: (t, j)),
            scratch_shapes=[pltpu.VMEM((D_FF, TN_MOE), BF16)]),
        out_shape=jax.ShapeDtypeStruct((P_ROWS, D_MODEL), F32),
        compiler_params=_params("arbitrary", "arbitrary"),
        name="moe_down",
    )(tile_eid, n_used, h, w_down, b_down)


def _final_kernel(y_ref, x1_ref, wt_ref, g2_ref, ln2g_ref, ln2b_ref, o_ref):
    wt = wt_ref[...]
    y = wt[:, 0:1] * y_ref[:, 0:D_MODEL]
    for k in range(1, TOP_K):
        y = y + wt[:, k:k + 1] * y_ref[:, k * D_MODEL:(k + 1) * D_MODEL]
    o_ref[...] = _ln(DN_ALPHA * x1_ref[...] + (1.0 + g2_ref[...]) * y) * ln2g_ref[...] + ln2b_ref[...]


def _final(y_tok, x1, wt, mod3, ln2_g, ln2_b):
    tiles_per_batch = SEQ // TM_FIN
    full = lambda shape: pl.BlockSpec(shape, lambda i: tuple(0 for _ in shape))
    row = lambda width: pl.BlockSpec((TM_FIN, width), lambda i: (i, 0))
    return pl.pallas_call(
        _final_kernel,
        grid=(N_TOK // TM_FIN,),
        in_specs=[row(TOP_K * D_MODEL), row(D_MODEL), row(LANE),
                  pl.BlockSpec((None, 1, D_MODEL), lambda i: (i // tiles_per_batch, 0, 5)),
                  full(ln2_g.shape), full(ln2_b.shape)],
        out_specs=row(D_MODEL),
        out_shape=jax.ShapeDtypeStruct((N_TOK, D_MODEL), F32),
        compiler_params=_params("parallel"),
        name="combine_ln2",
    )(y_tok, x1, wt, mod3, ln2_g, ln2_b)


def _route(idx, n_tiles_cap):
    flat_e = idx.reshape(-1)
    onehot = (flat_e[:, None] == jnp.arange(N_EXPERTS, dtype=jnp.int32)[None, :]).astype(jnp.int32)
    before = jnp.cumsum(onehot, axis=0) - onehot
    rank = jnp.sum(before * onehot, axis=1)
    counts = jnp.sum(onehot, axis=0)
    padded = ((counts + TM_MOE - 1) // TM_MOE) * TM_MOE
    ends = jnp.cumsum(padded)
    dest = (ends - padded)[flat_e] + rank
    src_tok = jnp.zeros((P_ROWS,), jnp.int32).at[dest].set(
        jnp.arange(flat_e.shape[0], dtype=jnp.int32) // TOP_K)
    tile_start = jnp.arange(n_tiles_cap, dtype=jnp.int32) * TM_MOE
    tile_eid = jnp.minimum(jnp.searchsorted(ends, tile_start, side="right"), N_EXPERTS - 1).astype(jnp.int32)
    n_used = (ends[-1] // TM_MOE).astype(jnp.int32).reshape(1)
    return dest, src_tok, tile_eid, n_used


def kernel(x, c, w_ada, b_ada, w_in, w_gk, b_gk, gla_norm_g, pe_k, pe_v, w_ck1, b_ck1, w_ck2, b_ck2,
           w_cv1, b_cv1, w_cv2, b_cv2, w_o, ln1_g, ln1_b, w_router, b_router, w_gate, b_gate, w_up, b_up,
           w_down, b_down, ln2_g, ln2_b):
    l = 0
    xf = x.reshape(N_TOK, D_MODEL)
    row2 = lambda a: a.reshape(1, -1)

    c8 = jnp.pad(c, ((0, 8 - BATCH), (0, 0)))
    mod3 = _adaln(c8, w_ada[l], row2(b_ada[l]))[:BATCH].reshape(BATCH, 1, 6 * D_MODEL)

    w = w_in[l]
    glr0 = 3072
    nsa0 = glr0 + GLA_RANK
    ngt0 = nsa0 + 1024 + 6 * 256
    w_in_p = jnp.concatenate(
        [w[:, :glr0], w[:, nsa0:ngt0], w[:, glr0:nsa0], w[:, ngt0:],
         jnp.zeros((D_MODEL, D_IN_PAD - w.shape[1]), F32)], axis=1).astype(BF16)
    proj = _inproj(xf, mod3, w_in_p)

    w_gk_pad = jnp.pad(w_gk[l], ((0, LANE - GLA_RANK), (0, 0)))
    y_gla = _gla(proj, w_gk_pad, row2(b_gk[l]), row2(gla_norm_g[l]))

    cos_t, sin_t = _rope_tables(jnp.arange(SEQ))
    cmp_end = jnp.arange(N_CMP_PAD) * CMP_STRIDE + (CMP_BLOCK - 1)
    cos_c, sin_c = _rope_tables(cmp_end)
    q_r, ks, vs, kw, vw = _prep(proj, cos_t, sin_t)
    kc = _compress(proj, 0, pe_k[l], w_ck1[l], row2(b_ck1[l]), w_ck2[l], row2(b_ck2[l]), cos_c, sin_c, True)
    vc = _compress(proj, 1, pe_v[l], w_cv1[l], row2(b_cv1[l]), w_cv2[l], row2(b_cv2[l]), cos_c, sin_c, False)
    y_nsa = _nsa(q_r, kc, vc, ks, vs, kw, vw, proj)

    wr = jnp.pad(w_router[l], ((0, 0), (0, LANE - N_EXPERTS)))
    wr_hi, wr_lo = _split_bf16(wr)
    br = jnp.concatenate([b_router[l], jnp.full((LANE - N_EXPERTS,), NEG, F32)]).reshape(1, LANE)
    x1, h2, idx128, wt128 = _outproj(y_gla, y_nsa, xf, w_o[l].astype(BF16), mod3, row2(ln1_g[l]),
                                     row2(ln1_b[l]), wr_hi, wr_lo, br)

    dest, src_tok, tile_eid, n_used = _route(idx128[:, :TOP_K], N_MTILES)
    xs = jnp.take(h2, src_tok, axis=0)
    h = _moe_up(tile_eid, n_used, xs, w_gate[l], w_up[l],
                b_gate[l].reshape(N_EXPERTS, 1, D_FF), b_up[l].reshape(N_EXPERTS, 1, D_FF))
    y = _moe_down(tile_eid, n_used, h, w_down[l], b_down[l].reshape(N_EXPERTS, 1, D_MODEL))
    y_tok = jnp.take(y, dest, axis=0).reshape(N_TOK, TOP_K * D_MODEL)

    out = _final(y_tok, x1, wt128, mod3, row2(ln2_g[l]), row2(ln2_b[l]))
    return out.reshape(BATCH, SEQ, D_MODEL)
```

```python
import functools

import numpy as np
import jax
import jax.numpy as jnp
from jax import lax
from jax.experimental import pallas as pl
from jax.experimental.pallas import tpu as pltpu

F32 = jnp.float32
BF16 = jnp.bfloat16

D_MODEL = 2048
BATCH = 2
SEQ = 4096
N_TOK = BATCH * SEQ

GLA_HEADS = 4
GLA_DK = 128
GLA_DV = 256
GLA_RANK = 16
GLA_TAU = 16.0
GLA_CHUNK = 64

NSA_DH = 128
NSA_HEADS = 8
NSA_GROUPS = 2
NSA_HPG = 4
CMP_STRIDE = 16
CMP_BLOCK = 32
CMP_HIDDEN = 256
N_CMP_PAD = SEQ // CMP_STRIDE
SEL_BLOCK = 64
N_BLK = SEQ // SEL_BLOCK
SEL_TOPK = 16
WINDOW = 512
ROPE_DIM = 32
ROPE_THETA = 500000.0

N_EXPERTS = 32
TOP_K = 4
D_FF = D_MODEL
SWIGLU_LIMIT = 7.0
SWIGLU_ALPHA = 1.702
DN_ALPHA = 2.0 ** 0.25
LN_EPS = 1e-5

COL_GQ, COL_GK, COL_GV, COL_GR, COL_NQ = 0, 512, 1024, 2048, 3072
COL_KV = 4096
COL_TAIL = 5632
D_IN_PAD = 5760
GATE_LANE0 = GLA_RANK

LANE = 128
NEG = -1e30
VMEM_LIMIT = 56 * 1024 * 1024

TM_IN, TN_IN = 1024, 640
TT_GLA = 512
TR_PREP = 512
TQ_NSA = 128
TK_SEL = 256
WIN_SPAN = WINDOW + TQ_NSA
TM_OUT = 256
TM_MOE, TN_MOE = 512, 512
P_ROWS = N_TOK * TOP_K + N_EXPERTS * TM_MOE
N_MTILES = P_ROWS // TM_MOE
TM_FIN = 256


def _dot(a, b):
    return jnp.dot(a, b, preferred_element_type=F32)


def _dot_nt(a, b):
    return lax.dot_general(a, b, (((1,), (1,)), ((), ())), preferred_element_type=F32)


def _dot_tn(a, b):
    return lax.dot_general(a, b, (((0,), (0,)), ((), ())), preferred_element_type=F32)


def _ln(x):
    xc = x - jnp.mean(x, -1, keepdims=True)
    return xc * lax.rsqrt(jnp.mean(xc * xc, -1, keepdims=True) + LN_EPS)


def _split_bf16(x):
    hi = x.astype(BF16)
    lo = (x - hi.astype(F32)).astype(BF16)
    return hi, lo


def _params(*sem):
    return pltpu.CompilerParams(dimension_semantics=sem, vmem_limit_bytes=VMEM_LIMIT)


def _adaln_kernel(c_ref, w_ref, b_ref, o_ref):
    c = c_ref[...]
    a = (c * jax.nn.sigmoid(c)).astype(BF16)
    o_ref[...] = _dot(a, w_ref[...].astype(BF16)) + b_ref[...]


def _adaln(c8, w, b):
    n = w.shape[1]
    tn = 1024
    return pl.pallas_call(
        _adaln_kernel,
        grid=(n // tn,),
        in_specs=[pl.BlockSpec((8, D_MODEL), lambda j: (0, 0)),
                  pl.BlockSpec((D_MODEL, tn), lambda j: (0, j)),
                  pl.BlockSpec((1, tn), lambda j: (0, j))],
        out_specs=pl.BlockSpec((8, tn), lambda j: (0, j)),
        out_shape=jax.ShapeDtypeStruct((8, n), F32),
        compiler_params=_params("arbitrary"),
        name="adaln",
    )(c8, w, b)


def _inproj_kernel(x_ref, sh_ref, sc_ref, w_ref, o_ref, h_sc):
    @pl.when(pl.program_id(1) == 0)
    def _():
        h = _ln(x_ref[...]) * (1.0 + sc_ref[...]) + sh_ref[...]
        h_sc[...] = h.astype(BF16)

    o_ref[...] = _dot(h_sc[...], w_ref[...])


def _inproj(xf, mod3, w_in_p):
    tiles_per_batch = SEQ // TM_IN
    return pl.pallas_call(
        _inproj_kernel,
        grid=(N_TOK // TM_IN, D_IN_PAD // TN_IN),
        in_specs=[pl.BlockSpec((TM_IN, D_MODEL), lambda i, j: (i, 0)),
                  pl.BlockSpec((None, 1, D_MODEL), lambda i, j: (i // tiles_per_batch, 0, 0)),
                  pl.BlockSpec((None, 1, D_MODEL), lambda i, j: (i // tiles_per_batch, 0, 1)),
                  pl.BlockSpec((D_MODEL, TN_IN), lambda i, j: (0, j))],
        out_specs=pl.BlockSpec((TM_IN, TN_IN), lambda i, j: (i, j)),
        out_shape=jax.ShapeDtypeStruct((N_TOK, D_IN_PAD), F32),
        scratch_shapes=[pltpu.VMEM((TM_IN, D_MODEL), BF16)],
        compiler_params=_params("parallel", "arbitrary"),
        name="inproj",
    )(xf, mod3, mod3, w_in_p)


GLA_HALVES = (32, 16, 8, 4, 2, 1)
N_LEVELS = len(GLA_HALVES)
ROW_EB = 2 * N_LEVELS
ROW_EL = 2 * N_LEVELS + 1
N_EVIEWS = 2 * N_LEVELS + 2


def _gla_constants():
    c = GLA_CHUNK
    t = np.arange(c)[:, None]
    r = np.arange(c)[None, :]
    mall = np.zeros((N_EVIEWS, c, c), np.float32)
    valid = np.zeros((N_EVIEWS, c, LANE), np.float32)
    masks = np.zeros((N_LEVELS + 1, c, c), np.float32)
    for li, n in enumerate(GLA_HALVES):
        same = (t // (2 * n)) == (r // (2 * n))
        t_up = (t % (2 * n)) >= n
        r_up = (r % (2 * n)) >= n
        mall[2 * li] = same & t_up & r_up & (r <= t)
        mall[2 * li + 1] = same & ~t_up & ~r_up & (r > t)
        valid[2 * li] = np.broadcast_to(t_up, (c, LANE))
        valid[2 * li + 1] = np.broadcast_to(~t_up, (c, LANE))
        masks[li] = same & t_up & ~r_up
    mall[ROW_EB] = r <= t
    mall[ROW_EL] = r > t
    valid[ROW_EB] = 1.0
    valid[ROW_EL] = 1.0
    masks[N_LEVELS] = np.eye(c)
    return (mall.reshape(N_EVIEWS * c, c), valid.reshape(N_EVIEWS * c, LANE), masks)


def _gla_kernel(q_ref, k_ref, v_ref, r_ref, glr_ref, wgk_ref, bgk_ref, g_ref, mall_ref, valid_ref,
                masks_ref, o_ref, st_sc):
    c = GLA_CHUNK

    @pl.when(pl.program_id(2) == 0)
    def _():
        st_sc[...] = jnp.zeros_like(st_sc)

    z = _dot(glr_ref[...].astype(BF16), wgk_ref[...].astype(BF16)) + bgk_ref[...]
    log_a = (jnp.minimum(z, 0.0) - jnp.log1p(jnp.exp(-jnp.abs(z)))) * (1.0 / GLA_TAU)
    mall = mall_ref[...]
    for ci in range(TT_GLA // c):
        rows = slice(ci * c, (ci + 1) * c)
        la_hi, la_lo = _split_bf16(log_a[rows])
        e = jnp.exp(_dot(mall, la_hi) + _dot(mall, la_lo)) * valid_ref[...]
        q = q_ref[rows, :] * (GLA_DK ** -0.5)
        k = k_ref[rows, :]
        vb = v_ref[rows, :].astype(BF16)
        att = masks_ref[N_LEVELS] * _dot_nt(q.astype(BF16), k.astype(BF16))
        for li in range(N_LEVELS):
            eq = e[(2 * li) * c:(2 * li + 1) * c]
            ek = e[(2 * li + 1) * c:(2 * li + 2) * c]
            att = att + masks_ref[li] * _dot_nt((q * eq).astype(BF16), (k * ek).astype(BF16))
        eb = e[ROW_EB * c:(ROW_EB + 1) * c]
        el = e[ROW_EL * c:(ROW_EL + 1) * c]
        st = st_sc[...]
        o = _dot_nt((q * eb).astype(BF16), st.astype(BF16)) + _dot(att.astype(BF16), vb)
        st_sc[...] = st * eb[c - 1:c, :] + _dot_tn(vb, (k * el).astype(BF16))
        o = o * lax.rsqrt(jnp.mean(o * o, -1, keepdims=True) + LN_EPS) * g_ref[...]
        r = r_ref[rows, :]
        o_ref[rows, :] = (o * (r * jax.nn.sigmoid(r))).astype(BF16)


def _gla(proj, w_gk_pad, b_gk, norm_g):
    nt = SEQ // TT_GLA
    mall, valid, masks = _gla_constants()
    row = lambda b, h, i: b * nt + i
    return pl.pallas_call(
        _gla_kernel,
        grid=(BATCH, GLA_HEADS, nt),
        in_specs=[pl.BlockSpec((TT_GLA, GLA_DK), lambda b, h, i: (row(b, h, i), COL_GQ // GLA_DK + h)),
                  pl.BlockSpec((TT_GLA, GLA_DK), lambda b, h, i: (row(b, h, i), COL_GK // GLA_DK + h)),
                  pl.BlockSpec((TT_GLA, GLA_DV), lambda b, h, i: (row(b, h, i), COL_GV // GLA_DV + h)),
                  pl.BlockSpec((TT_GLA, GLA_DV), lambda b, h, i: (row(b, h, i), COL_GR // GLA_DV + h)),
                  pl.BlockSpec((TT_GLA, LANE), lambda b, h, i: (row(b, h, i), COL_TAIL // LANE)),
                  pl.BlockSpec((LANE, GLA_DK), lambda b, h, i: (0, h)),
                  pl.BlockSpec((1, GLA_DK), lambda b, h, i: (0, h)),
                  pl.BlockSpec((1, GLA_DV), lambda b, h, i: (0, 0)),
                  pl.BlockSpec(mall.shape, lambda b, h, i: (0, 0)),
                  pl.BlockSpec(valid.shape, lambda b, h, i: (0, 0)),
                  pl.BlockSpec(masks.shape, lambda b, h, i: (0, 0, 0))],
        out_specs=pl.BlockSpec((TT_GLA, GLA_DV), lambda b, h, i: (row(b, h, i), h)),
        out_shape=jax.ShapeDtypeStruct((N_TOK, GLA_HEADS * GLA_DV), BF16),
        scratch_shapes=[pltpu.VMEM((GLA_DV, GLA_DK), F32)],
        compiler_params=_params("parallel", "parallel", "arbitrary"),
        name="gla",
    )(proj, proj, proj, proj, proj, w_gk_pad, b_gk, norm_g,
      jnp.asarray(mall, BF16), jnp.asarray(valid), jnp.asarray(masks))


def _rope_tables(pos):
    half = ROPE_DIM // 2
    inv_freq = ROPE_THETA ** (-jnp.arange(half, dtype=F32) * (2.0 / ROPE_DIM))
    ang = pos.astype(F32)[:, None] * inv_freq
    cos, sin = jnp.cos(ang), jnp.sin(ang)
    n = pos.shape[0]
    cosf = jnp.concatenate([cos, cos, jnp.ones((n, LANE - ROPE_DIM), F32)], -1)
    sinf = jnp.concatenate([-sin, sin, jnp.zeros((n, LANE - ROPE_DIM), F32)], -1)
    return cosf, sinf


def _rope(x, cosf, sinf):
    lane = lax.broadcasted_iota(jnp.int32, x.shape, 1)
    half = ROPE_DIM // 2
    swapped = jnp.where(lane < half, pltpu.roll(x, LANE - half, 1), pltpu.roll(x, half, 1))
    return x * cosf + swapped * sinf


def _prep_kernel(q_ref, ks_ref, vs_ref, kw_ref, vw_ref, cos_ref, sin_ref,
                 qo_ref, kso_ref, vso_ref, kwo_ref, vwo_ref):
    cosf = cos_ref[...]
    sinf = sin_ref[...]
    for hh in range(NSA_HEADS):
        cols = slice(hh * NSA_DH, (hh + 1) * NSA_DH)
        qo_ref[:, cols] = _rope(q_ref[:, cols], cosf, sinf).astype(BF16)
    for g in range(NSA_GROUPS):
        cols = slice(g * NSA_DH, (g + 1) * NSA_DH)
        kso_ref[g] = _rope(ks_ref[:, cols], cosf, sinf).astype(BF16)
        kwo_ref[g] = _rope(kw_ref[:, cols], cosf, sinf).astype(BF16)
        vso_ref[g] = vs_ref[:, cols].astype(BF16)
        vwo_ref[g] = vw_ref[:, cols].astype(BF16)


def _prep(proj, cosf, sinf):
    nt = SEQ // TR_PREP
    kvw = NSA_GROUPS * NSA_DH
    kv_in = lambda which: pl.BlockSpec((TR_PREP, kvw), lambda b, i: (b * nt + i, COL_KV // kvw + which))
    kv_out = pl.BlockSpec((None, NSA_GROUPS, TR_PREP, NSA_DH), lambda b, i: (b, 0, i, 0))
    kv_shape = jax.ShapeDtypeStruct((BATCH, NSA_GROUPS, SEQ, NSA_DH), BF16)
    qw = NSA_HEADS * NSA_DH
    return pl.pallas_call(
        _prep_kernel,
        grid=(BATCH, nt),
        in_specs=[pl.BlockSpec((TR_PREP, qw), lambda b, i: (b * nt + i, COL_NQ // qw)),
                  kv_in(2), kv_in(3), kv_in(4), kv_in(5),
                  pl.BlockSpec((TR_PREP, LANE), lambda b, i: (i, 0)),
                  pl.BlockSpec((TR_PREP, LANE), lambda b, i: (i, 0))],
        out_specs=[pl.BlockSpec((TR_PREP, qw), lambda b, i: (b * nt + i, 0)),
                   kv_out, kv_out, kv_out, kv_out],
        out_shape=[jax.ShapeDtypeStruct((N_TOK, qw), BF16), kv_shape, kv_shape, kv_shape, kv_shape],
        compiler_params=_params("parallel", "parallel"),
        name="nsa_prep",
    )(proj, proj, proj, proj, proj, cosf, sinf)


def _compress_kernel(a_ref, pe_ref, w1_ref, b1_ref, w2_ref, b2_ref, cos_ref, sin_ref, o_ref, bot_sc,
                     *, rope):
    n = N_CMP_PAD
    top = jnp.zeros((n, CMP_HIDDEN), F32)
    bot = jnp.zeros((n, CMP_HIDDEN), F32)
    for p in range(CMP_STRIDE):
        ap = a_ref[pl.ds(p, n, stride=CMP_STRIDE), :]
        w_top = w1_ref[p * NSA_DH:(p + 1) * NSA_DH, :].astype(BF16)
        w_bot = w1_ref[(CMP_STRIDE + p) * NSA_DH:(CMP_STRIDE + p + 1) * NSA_DH, :].astype(BF16)
        top = top + _dot((ap + pe_ref[p:p + 1, :]).astype(BF16), w_top)
        bot = bot + _dot((ap + pe_ref[CMP_STRIDE + p:CMP_STRIDE + p + 1, :]).astype(BF16), w_bot)
    bot_sc[0:n, :] = bot
    bot_sc[n:n + 8, :] = jnp.zeros((8, CMP_HIDDEN), F32)
    h = top + bot_sc[1:n + 1, :] + b1_ref[...]
    h = h * jax.nn.sigmoid(h)
    out = _dot(h.astype(BF16), w2_ref[...].astype(BF16)) + b2_ref[...]
    if rope:
        out = _rope(out, cos_ref[...], sin_ref[...])
    row = lax.broadcasted_iota(jnp.int32, out.shape, 0)
    o_ref[...] = jnp.where(row < n - 1, out, 0.0).astype(BF16)


def _compress(proj, which, pe, w1, b1, w2, b2, cosf, sinf, rope):
    col0 = (COL_KV + which * NSA_GROUPS * NSA_DH) // NSA_DH
    full = lambda shape: pl.BlockSpec(shape, lambda b, g: tuple(0 for _ in shape))
    return pl.pallas_call(
        functools.partial(_compress_kernel, rope=rope),
        grid=(BATCH, NSA_GROUPS),
        in_specs=[pl.BlockSpec((SEQ, NSA_DH), lambda b, g: (b, col0 + g)),
                  full(pe.shape), full(w1.shape), full(b1.shape), full(w2.shape), full(b2.shape),
                  full(cosf.shape), full(sinf.shape)],
        out_specs=pl.BlockSpec((None, None, N_CMP_PAD, NSA_DH), lambda b, g: (b, g, 0, 0)),
        out_shape=jax.ShapeDtypeStruct((BATCH, NSA_GROUPS, N_CMP_PAD, NSA_DH), BF16),
        scratch_shapes=[pltpu.VMEM((N_CMP_PAD + 8, CMP_HIDDEN), F32)],
        compiler_params=_params("parallel", "parallel"),
        name="nsa_compress",
    )(proj, pe, w1, b1, w2, b2, cosf, sinf)


def _nsa_constants():
    c_start = np.arange(N_CMP_PAD) * CMP_STRIDE
    b_start = np.arange(N_BLK) * SEL_BLOCK
    overlap_t = ((c_start[None, :] < b_start[:, None] + SEL_BLOCK)
                 & (c_start[None, :] + CMP_BLOCK > b_start[:, None])).astype(np.float32)
    overlap_t[:, N_CMP_PAD - 1] = 0.0
    expand = (np.arange(SEQ)[None, :] // SEL_BLOCK == np.arange(LANE)[:, None]).astype(np.float32)
    return overlap_t, expand


def _tile_heads(a):
    return jnp.concatenate([a] * NSA_HPG, axis=0)


def _nsa_kernel(q_ref, kc_ref, vc_ref, ks_ref, vs_ref, kw_ref, vw_ref, gate_ref, ovt_ref, expand_ref,
                o_ref, score_sc, m_sc, l_sc, acc_sc):
    tq = TQ_NSA
    rows = NSA_HPG * tq
    scale = NSA_DH ** -0.5
    g = pl.program_id(1)
    q0 = pl.program_id(2) * tq
    qs = jnp.concatenate([q_ref[:, hh * NSA_DH:(hh + 1) * NSA_DH] for hh in range(NSA_HPG)], axis=0)
    pos_t = q0 + lax.broadcasted_iota(jnp.int32, (tq, 1), 0)

    s = _dot_nt(qs, kc_ref[...]) * scale
    cmp_end = lax.broadcasted_iota(jnp.int32, (tq, N_CMP_PAD), 1) * CMP_STRIDE + (CMP_BLOCK - 1)
    s = s + _tile_heads(jnp.where(cmp_end <= pos_t, 0.0, NEG))
    m = jnp.max(s, -1, keepdims=True)
    e = jnp.where(s > 0.5 * NEG, jnp.exp(s - m), 0.0)
    p = e / jnp.maximum(jnp.sum(e, -1, keepdims=True), 1e-30)
    o_cmp = _dot(p.astype(BF16), vc_ref[...])
    p_sum = p[0:tq] + p[tq:2 * tq] + p[2 * tq:3 * tq] + p[3 * tq:4 * tq]
    ps_hi, ps_lo = _split_bf16(p_sum)
    ovt = ovt_ref[...]
    p_blk_t = _dot_nt(ovt, ps_hi) + _dot_nt(ovt, ps_lo)

    jj = lax.broadcasted_iota(jnp.int32, (N_BLK, tq), 0)
    cur = (q0 + lax.broadcasted_iota(jnp.int32, (N_BLK, tq), 1)) // SEL_BLOCK
    forced = (jj == 0) | (jj == cur) | (jj == cur - 1)
    allowed = jj <= cur
    score = jnp.where(forced, 3.0e38, jnp.where(allowed, p_blk_t, -1.0))
    score_sc[...] = score

    def rank_body(i, rank):
        row = score_sc[pl.ds(i, 1), :]
        first = jnp.where(jj > i, 1.0, 0.0)
        return rank + jnp.where(row > score, 1.0, jnp.where(row == score, first, 0.0))

    n_live = (q0 + tq - 1) // SEL_BLOCK + 1
    rank = lax.fori_loop(0, n_live, rank_body, jnp.zeros((N_BLK, tq), F32))
    sel_t = jnp.where(allowed, jnp.where(rank < SEL_TOPK, 1.0, 0.0), 0.0)
    sel = jnp.concatenate([sel_t, jnp.zeros((LANE - N_BLK, tq), F32)], axis=0).T.astype(BF16)

    m_sc[...] = jnp.full(m_sc.shape, NEG, F32)
    l_sc[...] = jnp.zeros(l_sc.shape, F32)
    acc_sc[...] = jnp.zeros(acc_sc.shape, F32)

    def sel_body(kt, carry):
        k0 = pl.multiple_of(kt * TK_SEL, TK_SEL)
        s = _dot_nt(qs, ks_ref[pl.ds(k0, TK_SEL), :]) * scale
        sel_x = _dot(sel, expand_ref[:, pl.ds(k0, TK_SEL)])
        kpos = k0 + lax.broadcasted_iota(jnp.int32, (tq, TK_SEL), 1)
        bias = jnp.where(kpos <= pos_t, jnp.where(sel_x > 0.5, 0.0, NEG), NEG)
        s = s + _tile_heads(bias)
        m_prev = m_sc[...]
        m_new = jnp.maximum(m_prev, jnp.max(s, -1, keepdims=True))
        alpha = jnp.exp(m_prev - m_new)
        p = jnp.exp(s - jnp.concatenate([m_new] * (TK_SEL // LANE), axis=1))
        l_sc[...] = alpha * l_sc[...] + jnp.sum(p, -1, keepdims=True)
        acc_sc[...] = alpha * acc_sc[...] + _dot(p.astype(BF16), vs_ref[pl.ds(k0, TK_SEL), :])
        m_sc[...] = m_new
        return carry

    lax.fori_loop(0, (q0 + tq - 1) // TK_SEL + 1, sel_body, 0)
    o_slc = acc_sc[...] / l_sc[...]

    start = pl.multiple_of(jnp.maximum(q0 - WINDOW, 0), LANE)
    s = _dot_nt(qs, kw_ref[pl.ds(start, WIN_SPAN), :]) * scale
    dist = pos_t - (start + lax.broadcasted_iota(jnp.int32, (tq, WIN_SPAN), 1))
    bias = jnp.where(dist >= 0, jnp.where(dist < WINDOW, 0.0, NEG), NEG)
    s = s + _tile_heads(bias)
    p = jnp.exp(s - jnp.max(s, -1, keepdims=True))
    o_win = _dot(p.astype(BF16), vw_ref[pl.ds(start, WIN_SPAN), :]) / jnp.sum(p, -1, keepdims=True)

    gates = jax.nn.sigmoid(gate_ref[...])

    def gate(hh, branch):
        lane0 = GATE_LANE0 + hh * 3 + branch
        lane1 = lane0 + NSA_HPG * 3
        return jnp.where(g == 0, gates[:, lane0:lane0 + 1], gates[:, lane1:lane1 + 1])

    for hh in range(NSA_HPG):
        r = slice(hh * tq, (hh + 1) * tq)
        o = gate(hh, 0) * o_cmp[r] + gate(hh, 1) * o_slc[r] + gate(hh, 2) * o_win[r]
        o_ref[:, hh * NSA_DH:(hh + 1) * NSA_DH] = o.astype(BF16)


def _nsa(q_r, kc, vc, ks, vs, kw, vw, proj):
    nq = SEQ // TQ_NSA
    gw = NSA_HPG * NSA_DH
    overlap_t, expand = _nsa_constants()
    rows = NSA_HPG * TQ_NSA
    cmp_spec = pl.BlockSpec((None, None, N_CMP_PAD, NSA_DH), lambda b, g, i: (b, g, 0, 0))
    kv_spec = pl.BlockSpec((None, None, SEQ, NSA_DH), lambda b, g, i: (b, g, 0, 0))
    return pl.pallas_call(
        _nsa_kernel,
        grid=(BATCH, NSA_GROUPS, nq),
        in_specs=[pl.BlockSpec((TQ_NSA, gw), lambda b, g, i: (b * nq + i, g)),
                  cmp_spec, cmp_spec, kv_spec, kv_spec, kv_spec, kv_spec,
                  pl.BlockSpec((TQ_NSA, LANE), lambda b, g, i: (b * nq + i, COL_TAIL // LANE)),
                  pl.BlockSpec(overlap_t.shape, lambda b, g, i: (0, 0)),
                  pl.BlockSpec(expand.shape, lambda b, g, i: (0, 0))],
        out_specs=pl.BlockSpec((TQ_NSA, gw), lambda b, g, i: (b * nq + i, g)),
        out_shape=jax.ShapeDtypeStruct((N_TOK, NSA_GROUPS * gw), BF16),
        scratch_shapes=[pltpu.VMEM((N_BLK, TQ_NSA), F32),
                        pltpu.VMEM((rows, LANE), F32),
                        pltpu.VMEM((rows, LANE), F32),
                        pltpu.VMEM((rows, NSA_DH), F32)],
        compiler_params=_params("parallel", "parallel", "arbitrary"),
        name="nsa_attn",
    )(q_r, kc, vc, ks, vs, kw, vw, proj, jnp.asarray(overlap_t, BF16), jnp.asarray(expand, BF16))


def _outproj_kernel(yg_ref, yn_ref, x_ref, wo_ref, g1_ref, sc2_ref, sh2_ref, ln1g_ref, ln1b_ref,
                    wr_hi_ref, wr_lo_ref, br_ref, ltri_ref, x1_ref, h2_ref, idx_ref, wt_ref, rank_ref,
                    cnt_ref, base_sc):
    half = D_MODEL // 2

    @pl.when(pl.program_id(0) == 0)
    def _():
        base_sc[...] = jnp.zeros_like(base_sc)

    mix = _dot(yg_ref[...], wo_ref[0:half, :]) + _dot(yn_ref[...], wo_ref[half:D_MODEL, :])
    x1 = _ln(DN_ALPHA * x_ref[...] + (1.0 + g1_ref[...]) * mix) * ln1g_ref[...] + ln1b_ref[...]
    x1_ref[...] = x1
    h2 = _ln(x1) * (1.0 + sc2_ref[...]) + sh2_ref[...]
    h_hi, h_lo = _split_bf16(h2)
    h2_ref[...] = h_hi
    wr_hi = wr_hi_ref[...]
    logits = _dot(h_hi, wr_hi) + _dot(h_lo, wr_hi) + _dot(h_hi, wr_lo_ref[...]) + br_ref[...]

    lane = lax.broadcasted_iota(jnp.int32, logits.shape, 1)
    lane_f = lane.astype(F32)
    vals = logits
    idx_out = jnp.zeros(logits.shape, jnp.int32)
    exp_out = jnp.zeros(logits.shape, F32)
    denom = jnp.zeros((logits.shape[0], 1), F32)
    m0 = None
    onehots = []
    for k in range(TOP_K):
        mk = jnp.max(vals, -1, keepdims=True)
        ik = jnp.min(jnp.where(vals == mk, lane_f, float(LANE)), -1, keepdims=True)
        if k == 0:
            m0 = mk
        ek = jnp.exp(mk - m0)
        denom = denom + ek
        idx_out = jnp.where(lane == k, ik.astype(jnp.int32), idx_out)
        exp_out = jnp.where(lane == k, ek, exp_out)
        hit = lane_f == ik
        onehots.append(jnp.where(hit, 1.0, 0.0))
        vals = jnp.where(hit, -3.0e38, vals)
    idx_ref[...] = idx_out
    wt_ref[...] = exp_out / denom

    cnt = onehots[0] + onehots[1] + onehots[2] + onehots[3]
    base = base_sc[0:1, :]
    before = _dot(ltri_ref[...], cnt.astype(BF16)) + base
    rank_out = jnp.zeros(logits.shape, jnp.int32)
    for k in range(TOP_K):
        rk = jnp.sum(onehots[k] * before, -1, keepdims=True)
        rank_out = jnp.where(lane == k, rk.astype(jnp.int32), rank_out)
    rank_ref[...] = rank_out
    base_sc[...] = jnp.broadcast_to(base + jnp.sum(cnt, 0, keepdims=True), base_sc.shape)
    cnt_ref[...] = base_sc[...]


def _outproj(y_gla, y_nsa, xf, w_o, mod3, ln1_g, ln1_b, wr_hi, wr_lo, br):
    tiles_per_batch = SEQ // TM_OUT
    half = D_MODEL // 2
    mod_spec = lambda chunk: pl.BlockSpec((None, 1, D_MODEL), lambda i: (i // tiles_per_batch, 0, chunk))
    full = lambda shape: pl.BlockSpec(shape, lambda i: tuple(0 for _ in shape))
    row = lambda width: pl.BlockSpec((TM_OUT, width), lambda i: (i, 0))
    ltri = jnp.asarray(np.tril(np.ones((TM_OUT, TM_OUT), np.float32), -1), BF16)
    return pl.pallas_call(
        _outproj_kernel,
        grid=(N_TOK // TM_OUT,),
        in_specs=[row(half), row(half), row(D_MODEL), full(w_o.shape),
                  mod_spec(2), mod_spec(4), mod_spec(3),
                  full(ln1_g.shape), full(ln1_b.shape), full(wr_hi.shape), full(wr_lo.shape), full(br.shape),
                  full(ltri.shape)],
        out_specs=[row(D_MODEL), row(D_MODEL), row(LANE), row(LANE), row(LANE), full((8, LANE))],
        out_shape=[jax.ShapeDtypeStruct((N_TOK, D_MODEL), F32),
                   jax.ShapeDtypeStruct((N_TOK, D_MODEL), BF16),
                   jax.ShapeDtypeStruct((N_TOK, LANE), jnp.int32),
                   jax.ShapeDtypeStruct((N_TOK, LANE), F32),
                   jax.ShapeDtypeStruct((N_TOK, LANE), jnp.int32),
                   jax.ShapeDtypeStruct((8, LANE), F32)],
        scratch_shapes=[pltpu.VMEM((8, LANE), F32)],
        compiler_params=_params("arbitrary"),
        name="outproj_router",
    )(y_gla, y_nsa, xf, w_o, mod3, mod3, mod3, ln1_g, ln1_b, wr_hi, wr_lo, br, ltri)


def _moe_up_kernel(eid_ref, nused_ref, x_ref, wg_ref, wu_ref, bg_ref, bu_ref, h_ref, wg_sc, wu_sc):
    t = pl.program_id(1)
    live = t < nused_ref[0]

    @pl.when(live)
    def _():
        @pl.when((t == 0) | (eid_ref[t] != eid_ref[jnp.maximum(t - 1, 0)]))
        def _():
            wg_sc[...] = wg_ref[...].astype(BF16)
            wu_sc[...] = wu_ref[...].astype(BF16)

        x = x_ref[...]
        gate = jnp.minimum(_dot(x, wg_sc[...]) + bg_ref[...], SWIGLU_LIMIT)
        up = jnp.clip(_dot(x, wu_sc[...]) + bu_ref[...], -SWIGLU_LIMIT, SWIGLU_LIMIT)
        h_ref[...] = (gate * jax.nn.sigmoid(SWIGLU_ALPHA * gate) * (up + 1.0)).astype(BF16)

    @pl.when(jnp.logical_not(live))
    def _():
        h_ref[...] = jnp.zeros(h_ref.shape, BF16)


def _moe_down_kernel(eid_ref, nused_ref, h_ref, wd_ref, bd_ref, y_ref, wd_sc):
    t = pl.program_id(1)
    live = t < nused_ref[0]

    @pl.when(live)
    def _():
        @pl.when((t == 0) | (eid_ref[t] != eid_ref[jnp.maximum(t - 1, 0)]))
        def _():
            wd_sc[...] = wd_ref[...].astype(BF16)

        y_ref[...] = (_dot(h_ref[...], wd_sc[...]) + bd_ref[...]).astype(BF16)

    @pl.when(jnp.logical_not(live))
    def _():
        y_ref[...] = jnp.zeros(y_ref.shape, BF16)


def _row_tile(t, nused_ref):
    return jnp.minimum(t, jnp.maximum(nused_ref[0] - 1, 0))


def _moe_up(tile_eid, n_used, xs, w_gate, w_up, b_gate, b_up):
    w_spec = pl.BlockSpec((None, D_MODEL, TN_MOE), lambda j, t, eid, nu: (eid[t], 0, j))
    b_spec = pl.BlockSpec((None, 1, TN_MOE), lambda j, t, eid, nu: (eid[t], 0, j))
    return pl.pallas_call(
        _moe_up_kernel,
        grid_spec=pltpu.PrefetchScalarGridSpec(
            num_scalar_prefetch=2,
            grid=(D_FF // TN_MOE, N_MTILES),
            in_specs=[pl.BlockSpec((TM_MOE, D_MODEL), lambda j, t, eid, nu: (_row_tile(t, nu), 0)),
                      w_spec, w_spec, b_spec, b_spec],
            out_specs=pl.BlockSpec((TM_MOE, TN_MOE), lambda j, t, eid, nu: (t, j)),
            scratch_shapes=[pltpu.VMEM((D_MODEL, TN_MOE), BF16), pltpu.VMEM((D_MODEL, TN_MOE), BF16)]),
        out_shape=jax.ShapeDtypeStruct((P_ROWS, D_FF), BF16),
        compiler_params=_params("arbitrary", "arbitrary"),
        name="moe_up",
    )(tile_eid, n_used, xs, w_gate, w_up, b_gate, b_up)


def _moe_down(tile_eid, n_used, h, w_down, b_down):
    return pl.pallas_call(
        _moe_down_kernel,
        grid_spec=pltpu.PrefetchScalarGridSpec(
            num_scalar_prefetch=2,
            grid=(D_MODEL // TN_MOE, N_MTILES),
            in_specs=[pl.BlockSpec((TM_MOE, D_FF), lambda j, t, eid, nu: (_row_tile(t, nu), 0)),
                      pl.BlockSpec((None, D_FF, TN_MOE), lambda j, t, eid, nu: (eid[t], 0, j)),
                      pl.BlockSpec((None, 1, TN_MOE), lambda j, t, eid, nu: (eid[t], 0, j))],
            out_specs=pl.BlockSpec((TM_MOE, TN_MOE), lambda j, t, eid, nu: (t, j)),
            scratch_shapes=[pltpu.VMEM((D_FF, TN_MOE), BF16)]),
        out_shape=jax.ShapeDtypeStruct((P_ROWS, D_MODEL), BF16),
        compiler_params=_params("arbitrary", "arbitrary"),
        name="moe_down",
    )(tile_eid, n_used, h, w_down, b_down)


def _final_kernel(y0_ref, y1_ref, y2_ref, y3_ref, x1_ref, wt_ref, g2_ref, ln2g_ref, ln2b_ref, o_ref):
    wt = wt_ref[...]
    y = wt[:, 0:1] * y0_ref[...].astype(F32)
    for k, y_ref in enumerate((y1_ref, y2_ref, y3_ref), start=1):
        y = y + wt[:, k:k + 1] * y_ref[...].astype(F32)
    o_ref[...] = _ln(DN_ALPHA * x1_ref[...] + (1.0 + g2_ref[...]) * y) * ln2g_ref[...] + ln2b_ref[...]


def _final(y_tok, x1, wt, mod3, ln2_g, ln2_b):
    nt = N_TOK // TM_FIN
    tiles_per_batch = SEQ // TM_FIN
    full = lambda shape: pl.BlockSpec(shape, lambda i: tuple(0 for _ in shape))
    row = lambda width: pl.BlockSpec((TM_FIN, width), lambda i: (i, 0))
    y_spec = lambda k: pl.BlockSpec((TM_FIN, D_MODEL), lambda i: (k * nt + i, 0))
    return pl.pallas_call(
        _final_kernel,
        grid=(nt,),
        in_specs=[y_spec(0), y_spec(1), y_spec(2), y_spec(3), row(D_MODEL), row(LANE),
                  pl.BlockSpec((None, 1, D_MODEL), lambda i: (i // tiles_per_batch, 0, 5)),
                  full(ln2_g.shape), full(ln2_b.shape)],
        out_specs=row(D_MODEL),
        out_shape=jax.ShapeDtypeStruct((N_TOK, D_MODEL), F32),
        compiler_params=_params("parallel"),
        name="combine_ln2",
    )(y_tok, y_tok, y_tok, y_tok, x1, wt, mod3, ln2_g, ln2_b)


def _route(idx, rank, counts):
    experts = jnp.arange(N_EXPERTS, dtype=jnp.int32)
    padded = ((counts + TM_MOE - 1) // TM_MOE) * TM_MOE
    ends = jnp.sum(jnp.where(experts[None, :] <= experts[:, None], padded[None, :], 0), axis=1)
    starts = ends - padded
    dest = jnp.sum(jnp.where(idx[..., None] == experts, starts, 0), axis=-1) + rank
    tok = jnp.broadcast_to(jnp.arange(N_TOK, dtype=jnp.int32)[:, None], dest.shape)
    src_tok = jnp.zeros((P_ROWS,), jnp.int32).at[dest.reshape(-1)].set(tok.reshape(-1))
    tile_start = jnp.arange(N_MTILES, dtype=jnp.int32) * TM_MOE
    tile_eid = jnp.minimum(jnp.sum((ends[None, :] <= tile_start[:, None]).astype(jnp.int32), axis=1),
                           N_EXPERTS - 1)
    n_used = (ends[-1] // TM_MOE).astype(jnp.int32).reshape(1)
    return dest, src_tok, tile_eid, n_used


def kernel(x, c, w_ada, b_ada, w_in, w_gk, b_gk, gla_norm_g, pe_k, pe_v, w_ck1, b_ck1, w_ck2, b_ck2,
           w_cv1, b_cv1, w_cv2, b_cv2, w_o, ln1_g, ln1_b, w_router, b_router, w_gate, b_gate, w_up, b_up,
           w_down, b_down, ln2_g, ln2_b):
    l = 0
    xf = x.reshape(N_TOK, D_MODEL)
    row2 = lambda a: a.reshape(1, -1)

    c8 = jnp.pad(c, ((0, 8 - BATCH), (0, 0)))
    mod3 = _adaln(c8, w_ada[l], row2(b_ada[l]))[:BATCH].reshape(BATCH, 1, 6 * D_MODEL)

    w = w_in[l]
    glr0 = 3072
    nsa0 = glr0 + GLA_RANK
    ngt0 = nsa0 + 1024 + 6 * 256
    w_in_p = jnp.concatenate(
        [w[:, :glr0], w[:, nsa0:ngt0], w[:, glr0:nsa0], w[:, ngt0:],
         jnp.zeros((D_MODEL, D_IN_PAD - w.shape[1]), F32)], axis=1).astype(BF16)
    proj = _inproj(xf, mod3, w_in_p)

    w_gk_pad = jnp.pad(w_gk[l], ((0, LANE - GLA_RANK), (0, 0)))
    y_gla = _gla(proj, w_gk_pad, row2(b_gk[l]), row2(gla_norm_g[l]))

    cos_t, sin_t = _rope_tables(jnp.arange(SEQ))
    cmp_end = jnp.arange(N_CMP_PAD) * CMP_STRIDE + (CMP_BLOCK - 1)
    cos_c, sin_c = _rope_tables(cmp_end)
    q_r, ks, vs, kw, vw = _prep(proj, cos_t, sin_t)
    kc = _compress(proj, 0, pe_k[l], w_ck1[l], row2(b_ck1[l]), w_ck2[l], row2(b_ck2[l]), cos_c, sin_c, True)
    vc = _compress(proj, 1, pe_v[l], w_cv1[l], row2(b_cv1[l]), w_cv2[l], row2(b_cv2[l]), cos_c, sin_c, False)
    y_nsa = _nsa(q_r, kc, vc, ks, vs, kw, vw, proj)

    wr = jnp.pad(w_router[l], ((0, 0), (0, LANE - N_EXPERTS)))
    wr_hi, wr_lo = _split_bf16(wr)
    br = jnp.concatenate([b_router[l], jnp.full((LANE - N_EXPERTS,), NEG, F32)]).reshape(1, LANE)
    x1, h2, idx128, wt128, rank128, cnt8 = _outproj(y_gla, y_nsa, xf, w_o[l].astype(BF16), mod3,
                                                    row2(ln1_g[l]), row2(ln1_b[l]), wr_hi, wr_lo, br)

    counts = cnt8[0, :N_EXPERTS].astype(jnp.int32)
    dest, src_tok, tile_eid, n_used = _route(idx128[:, :TOP_K], rank128[:, :TOP_K], counts)
    xs = jnp.take(h2, src_tok, axis=0)
    h = _moe_up(tile_eid, n_used, xs, w_gate[l], w_up[l],
                b_gate[l].reshape(N_EXPERTS, 1, D_FF), b_up[l].reshape(N_EXPERTS, 1, D_FF))
    y = _moe_down(tile_eid, n_used, h, w_down[l], b_down[l].reshape(N_EXPERTS, 1, D_MODEL))
    y_tok = jnp.take(y, dest.T.reshape(-1), axis=0)

    out = _final(y_tok, x1, wt128, mod3, row2(ln2_g[l]), row2(ln2_b[l]))
    return out.reshape(BATCH, SEQ, D_MODEL)
```

```python
import functools

import numpy as np
import jax
import jax.numpy as jnp
from jax import lax
from jax.experimental import pallas as pl
from jax.experimental.pallas import tpu as pltpu

F32 = jnp.float32
BF16 = jnp.bfloat16

D_MODEL = 2048
BATCH = 2
SEQ = 4096
N_TOK = BATCH * SEQ

GLA_HEADS = 4
GLA_DK = 128
GLA_DV = 256
GLA_RANK = 16
GLA_TAU = 16.0
GLA_CHUNK = 64

NSA_DH = 128
NSA_HEADS = 8
NSA_GROUPS = 2
NSA_HPG = 4
CMP_STRIDE = 16
CMP_BLOCK = 32
CMP_HIDDEN = 256
N_CMP_PAD = SEQ // CMP_STRIDE
SEL_BLOCK = 64
N_BLK = SEQ // SEL_BLOCK
SEL_TOPK = 16
WINDOW = 512
ROPE_DIM = 32
ROPE_THETA = 500000.0

N_EXPERTS = 32
TOP_K = 4
D_FF = D_MODEL
SWIGLU_LIMIT = 7.0
SWIGLU_ALPHA = 1.702
DN_ALPHA = 2.0 ** 0.25
LN_EPS = 1e-5

COL_GQ, COL_GK, COL_GV, COL_GR, COL_NQ = 0, 512, 1024, 2048, 3072
COL_KV = 4096
COL_TAIL = 5632
D_IN_PAD = 5760
GATE_LANE0 = GLA_RANK

LANE = 128
NEG = -1e30
VMEM_LIMIT = 56 * 1024 * 1024

TM_IN, TN_IN = 1024, 640
TT_GLA = 512
TR_PREP = 512
TQ_NSA = 128
TK_SEL = 256
WIN_SPAN = WINDOW + TQ_NSA
TM_OUT = 256
TM_MOE, TN_MOE = 512, 512
P_ROWS = N_TOK * TOP_K + N_EXPERTS * TM_MOE
N_MTILES = P_ROWS // TM_MOE
TM_FIN = 256


def _dot(a, b):
    return jnp.dot(a, b, preferred_element_type=F32)


def _dot_nt(a, b):
    return lax.dot_general(a, b, (((1,), (1,)), ((), ())), preferred_element_type=F32)


def _dot_tn(a, b):
    return lax.dot_general(a, b, (((0,), (0,)), ((), ())), preferred_element_type=F32)


def _ln(x):
    xc = x - jnp.mean(x, -1, keepdims=True)
    return xc * lax.rsqrt(jnp.mean(xc * xc, -1, keepdims=True) + LN_EPS)


def _split_bf16(x):
    hi = x.astype(BF16)
    lo = (x - hi.astype(F32)).astype(BF16)
    return hi, lo


def _params(*sem):
    return pltpu.CompilerParams(dimension_semantics=sem, vmem_limit_bytes=VMEM_LIMIT)


def _adaln_kernel(c_ref, w_ref, b_ref, o_ref):
    c = c_ref[...]
    a = (c * jax.nn.sigmoid(c)).astype(BF16)
    o_ref[...] = _dot(a, w_ref[...].astype(BF16)) + b_ref[...]


def _adaln(c8, w, b):
    n = w.shape[1]
    tn = 1024
    return pl.pallas_call(
        _adaln_kernel,
        grid=(n // tn,),
        in_specs=[pl.BlockSpec((8, D_MODEL), lambda j: (0, 0)),
                  pl.BlockSpec((D_MODEL, tn), lambda j: (0, j)),
                  pl.BlockSpec((1, tn), lambda j: (0, j))],
        out_specs=pl.BlockSpec((8, tn), lambda j: (0, j)),
        out_shape=jax.ShapeDtypeStruct((8, n), F32),
        compiler_params=_params("arbitrary"),
        name="adaln",
    )(c8, w, b)


def _inproj_kernel(x_ref, sh_ref, sc_ref, w_ref, o_ref, h_sc):
    @pl.when(pl.program_id(1) == 0)
    def _():
        h = _ln(x_ref[...]) * (1.0 + sc_ref[...]) + sh_ref[...]
        h_sc[...] = h.astype(BF16)

    o_ref[...] = _dot(h_sc[...], w_ref[...])


def _inproj(xf, mod3, w_in_p):
    tiles_per_batch = SEQ // TM_IN
    return pl.pallas_call(
        _inproj_kernel,
        grid=(N_TOK // TM_IN, D_IN_PAD // TN_IN),
        in_specs=[pl.BlockSpec((TM_IN, D_MODEL), lambda i, j: (i, 0)),
                  pl.BlockSpec((None, 1, D_MODEL), lambda i, j: (i // tiles_per_batch, 0, 0)),
                  pl.BlockSpec((None, 1, D_MODEL), lambda i, j: (i // tiles_per_batch, 0, 1)),
                  pl.BlockSpec((D_MODEL, TN_IN), lambda i, j: (0, j))],
        out_specs=pl.BlockSpec((TM_IN, TN_IN), lambda i, j: (i, j)),
        out_shape=jax.ShapeDtypeStruct((N_TOK, D_IN_PAD), F32),
        scratch_shapes=[pltpu.VMEM((TM_IN, D_MODEL), BF16)],
        compiler_params=_params("parallel", "arbitrary"),
        name="inproj",
    )(xf, mod3, mod3, w_in_p)


GLA_HALVES = (32, 16, 8, 4, 2, 1)
N_LEVELS = len(GLA_HALVES)
ROW_EB = 2 * N_LEVELS
ROW_EL = 2 * N_LEVELS + 1
N_EVIEWS = 2 * N_LEVELS + 2


def _gla_constants():
    c = GLA_CHUNK
    t = np.arange(c)[:, None]
    r = np.arange(c)[None, :]
    mall = np.zeros((N_EVIEWS, c, c), np.float32)
    valid = np.zeros((N_EVIEWS, c, LANE), np.float32)
    masks = np.zeros((N_LEVELS + 1, c, c), np.float32)
    for li, n in enumerate(GLA_HALVES):
        same = (t // (2 * n)) == (r // (2 * n))
        t_up = (t % (2 * n)) >= n
        r_up = (r % (2 * n)) >= n
        mall[2 * li] = same & t_up & r_up & (r <= t)
        mall[2 * li + 1] = same & ~t_up & ~r_up & (r > t)
        valid[2 * li] = np.broadcast_to(t_up, (c, LANE))
        valid[2 * li + 1] = np.broadcast_to(~t_up, (c, LANE))
        masks[li] = same & t_up & ~r_up
    mall[ROW_EB] = r <= t
    mall[ROW_EL] = r > t
    valid[ROW_EB] = 1.0
    valid[ROW_EL] = 1.0
    masks[N_LEVELS] = np.eye(c)
    return (mall.reshape(N_EVIEWS * c, c), valid.reshape(N_EVIEWS * c, LANE), masks)


def _gla_kernel(q_ref, k_ref, v_ref, r_ref, glr_ref, wgk_ref, bgk_ref, g_ref, mall_ref, valid_ref,
                masks_ref, o_ref, st_sc):
    c = GLA_CHUNK

    @pl.when(pl.program_id(2) == 0)
    def _():
        st_sc[...] = jnp.zeros_like(st_sc)

    z = _dot(glr_ref[...].astype(BF16), wgk_ref[...].astype(BF16)) + bgk_ref[...]
    log_a = (jnp.minimum(z, 0.0) - jnp.log1p(jnp.exp(-jnp.abs(z)))) * (1.0 / GLA_TAU)
    mall = mall_ref[...]
    for ci in range(TT_GLA // c):
        rows = slice(ci * c, (ci + 1) * c)
        la_hi, la_lo = _split_bf16(log_a[rows])
        e = jnp.exp(_dot(mall, la_hi) + _dot(mall, la_lo)) * valid_ref[...]
        q = q_ref[rows, :] * (GLA_DK ** -0.5)
        k = k_ref[rows, :]
        vb = v_ref[rows, :].astype(BF16)
        att = masks_ref[N_LEVELS] * _dot_nt(q.astype(BF16), k.astype(BF16))
        for li in range(N_LEVELS):
            eq = e[(2 * li) * c:(2 * li + 1) * c]
            ek = e[(2 * li + 1) * c:(2 * li + 2) * c]
            att = att + masks_ref[li] * _dot_nt((q * eq).astype(BF16), (k * ek).astype(BF16))
        eb = e[ROW_EB * c:(ROW_EB + 1) * c]
        el = e[ROW_EL * c:(ROW_EL + 1) * c]
        st = st_sc[...]
        o = _dot_nt((q * eb).astype(BF16), st.astype(BF16)) + _dot(att.astype(BF16), vb)
        st_sc[...] = st * eb[c - 1:c, :] + _dot_tn(vb, (k * el).astype(BF16))
        o = o * lax.rsqrt(jnp.mean(o * o, -1, keepdims=True) + LN_EPS) * g_ref[...]
        r = r_ref[rows, :]
        o_ref[rows, :] = (o * (r * jax.nn.sigmoid(r))).astype(BF16)


def _gla(proj, w_gk_pad, b_gk, norm_g):
    nt = SEQ // TT_GLA
    mall, valid, masks = _gla_constants()
    row = lambda b, h, i: b * nt + i
    return pl.pallas_call(
        _gla_kernel,
        grid=(BATCH, GLA_HEADS, nt),
        in_specs=[pl.BlockSpec((TT_GLA, GLA_DK), lambda b, h, i: (row(b, h, i), COL_GQ // GLA_DK + h)),
                  pl.BlockSpec((TT_GLA, GLA_DK), lambda b, h, i: (row(b, h, i), COL_GK // GLA_DK + h)),
                  pl.BlockSpec((TT_GLA, GLA_DV), lambda b, h, i: (row(b, h, i), COL_GV // GLA_DV + h)),
                  pl.BlockSpec((TT_GLA, GLA_DV), lambda b, h, i: (row(b, h, i), COL_GR // GLA_DV + h)),
                  pl.BlockSpec((TT_GLA, LANE), lambda b, h, i: (row(b, h, i), COL_TAIL // LANE)),
                  pl.BlockSpec((LANE, GLA_DK), lambda b, h, i: (0, h)),
                  pl.BlockSpec((1, GLA_DK), lambda b, h, i: (0, h)),
                  pl.BlockSpec((1, GLA_DV), lambda b, h, i: (0, 0)),
                  pl.BlockSpec(mall.shape, lambda b, h, i: (0, 0)),
                  pl.BlockSpec(valid.shape, lambda b, h, i: (0, 0)),
                  pl.BlockSpec(masks.shape, lambda b, h, i: (0, 0, 0))],
        out_specs=pl.BlockSpec((TT_GLA, GLA_DV), lambda b, h, i: (row(b, h, i), h)),
        out_shape=jax.ShapeDtypeStruct((N_TOK, GLA_HEADS * GLA_DV), BF16),
        scratch_shapes=[pltpu.VMEM((GLA_DV, GLA_DK), F32)],
        compiler_params=_params("parallel", "parallel", "arbitrary"),
        name="gla",
    )(proj, proj, proj, proj, proj, w_gk_pad, b_gk, norm_g,
      jnp.asarray(mall, BF16), jnp.asarray(valid), jnp.asarray(masks))


def _rope_tables(pos):
    half = ROPE_DIM // 2
    inv_freq = ROPE_THETA ** (-jnp.arange(half, dtype=F32) * (2.0 / ROPE_DIM))
    ang = pos.astype(F32)[:, None] * inv_freq
    cos, sin = jnp.cos(ang), jnp.sin(ang)
    n = pos.shape[0]
    cosf = jnp.concatenate([cos, cos, jnp.ones((n, LANE - ROPE_DIM), F32)], -1)
    sinf = jnp.concatenate([-sin, sin, jnp.zeros((n, LANE - ROPE_DIM), F32)], -1)
    return cosf, sinf


def _rope(x, cosf, sinf):
    lane = lax.broadcasted_iota(jnp.int32, x.shape, 1)
    half = ROPE_DIM // 2
    swapped = jnp.where(lane < half, pltpu.roll(x, LANE - half, 1), pltpu.roll(x, half, 1))
    return x * cosf + swapped * sinf


def _prep_kernel(q_ref, ks_ref, vs_ref, kw_ref, vw_ref, cos_ref, sin_ref,
                 qo_ref, kso_ref, vso_ref, kwo_ref, vwo_ref):
    cosf = cos_ref[...]
    sinf = sin_ref[...]
    for hh in range(NSA_HEADS):
        cols = slice(hh * NSA_DH, (hh + 1) * NSA_DH)
        qo_ref[:, cols] = _rope(q_ref[:, cols], cosf, sinf).astype(BF16)
    for g in range(NSA_GROUPS):
        cols = slice(g * NSA_DH, (g + 1) * NSA_DH)
        kso_ref[g] = _rope(ks_ref[:, cols], cosf, sinf).astype(BF16)
        kwo_ref[g] = _rope(kw_ref[:, cols], cosf, sinf).astype(BF16)
        vso_ref[g] = vs_ref[:, cols].astype(BF16)
        vwo_ref[g] = vw_ref[:, cols].astype(BF16)


def _prep(proj, cosf, sinf):
    nt = SEQ // TR_PREP
    kvw = NSA_GROUPS * NSA_DH
    kv_in = lambda which: pl.BlockSpec((TR_PREP, kvw), lambda b, i: (b * nt + i, COL_KV // kvw + which))
    kv_out = pl.BlockSpec((None, NSA_GROUPS, TR_PREP, NSA_DH), lambda b, i: (b, 0, i, 0))
    kv_shape = jax.ShapeDtypeStruct((BATCH, NSA_GROUPS, SEQ, NSA_DH), BF16)
    qw = NSA_HEADS * NSA_DH
    return pl.pallas_call(
        _prep_kernel,
        grid=(BATCH, nt),
        in_specs=[pl.BlockSpec((TR_PREP, qw), lambda b, i: (b * nt + i, COL_NQ // qw)),
                  kv_in(2), kv_in(3), kv_in(4), kv_in(5),
                  pl.BlockSpec((TR_PREP, LANE), lambda b, i: (i, 0)),
                  pl.BlockSpec((TR_PREP, LANE), lambda b, i: (i, 0))],
        out_specs=[pl.BlockSpec((TR_PREP, qw), lambda b, i: (b * nt + i, 0)),
                   kv_out, kv_out, kv_out, kv_out],
        out_shape=[jax.ShapeDtypeStruct((N_TOK, qw), BF16), kv_shape, kv_shape, kv_shape, kv_shape],
        compiler_params=_params("parallel", "parallel"),
        name="nsa_prep",
    )(proj, proj, proj, proj, proj, cosf, sinf)


def _compress_kernel(a_ref, pe_ref, w1_ref, b1_ref, w2_ref, b2_ref, cos_ref, sin_ref, o_ref, bot_sc,
                     *, rope):
    n = N_CMP_PAD
    top = jnp.zeros((n, CMP_HIDDEN), F32)
    bot = jnp.zeros((n, CMP_HIDDEN), F32)
    for p in range(CMP_STRIDE):
        ap = a_ref[pl.ds(p, n, stride=CMP_STRIDE), :]
        w_top = w1_ref[p * NSA_DH:(p + 1) * NSA_DH, :].astype(BF16)
        w_bot = w1_ref[(CMP_STRIDE + p) * NSA_DH:(CMP_STRIDE + p + 1) * NSA_DH, :].astype(BF16)
        top = top + _dot((ap + pe_ref[p:p + 1, :]).astype(BF16), w_top)
        bot = bot + _dot((ap + pe_ref[CMP_STRIDE + p:CMP_STRIDE + p + 1, :]).astype(BF16), w_bot)
    bot_sc[0:n, :] = bot
    bot_sc[n:n + 8, :] = jnp.zeros((8, CMP_HIDDEN), F32)
    h = top + bot_sc[1:n + 1, :] + b1_ref[...]
    h = h * jax.nn.sigmoid(h)
    out = _dot(h.astype(BF16), w2_ref[...].astype(BF16)) + b2_ref[...]
    if rope:
        out = _rope(out, cos_ref[...], sin_ref[...])
    row = lax.broadcasted_iota(jnp.int32, out.shape, 0)
    o_ref[...] = jnp.where(row < n - 1, out, 0.0).astype(BF16)


def _compress(proj, which, pe, w1, b1, w2, b2, cosf, sinf, rope):
    col0 = (COL_KV + which * NSA_GROUPS * NSA_DH) // NSA_DH
    full = lambda shape: pl.BlockSpec(shape, lambda b, g: tuple(0 for _ in shape))
    return pl.pallas_call(
        functools.partial(_compress_kernel, rope=rope),
        grid=(BATCH, NSA_GROUPS),
        in_specs=[pl.BlockSpec((SEQ, NSA_DH), lambda b, g: (b, col0 + g)),
                  full(pe.shape), full(w1.shape), full(b1.shape), full(w2.shape), full(b2.shape),
                  full(cosf.shape), full(sinf.shape)],
        out_specs=pl.BlockSpec((None, None, N_CMP_PAD, NSA_DH), lambda b, g: (b, g, 0, 0)),
        out_shape=jax.ShapeDtypeStruct((BATCH, NSA_GROUPS, N_CMP_PAD, NSA_DH), BF16),
        scratch_shapes=[pltpu.VMEM((N_CMP_PAD + 8, CMP_HIDDEN), F32)],
        compiler_params=_params("parallel", "parallel"),
        name="nsa_compress",
    )(proj, pe, w1, b1, w2, b2, cosf, sinf)


def _nsa_constants():
    c_start = np.arange(N_CMP_PAD) * CMP_STRIDE
    b_start = np.arange(N_BLK) * SEL_BLOCK
    overlap_t = ((c_start[None, :] < b_start[:, None] + SEL_BLOCK)
                 & (c_start[None, :] + CMP_BLOCK > b_start[:, None])).astype(np.float32)
    overlap_t[:, N_CMP_PAD - 1] = 0.0
    expand = (np.arange(SEQ)[None, :] // SEL_BLOCK == np.arange(LANE)[:, None]).astype(np.float32)
    return overlap_t, expand


def _tile_heads(a):
    return jnp.concatenate([a] * NSA_HPG, axis=0)


def _nsa_kernel(q_ref, kc_ref, vc_ref, ks_ref, vs_ref, kw_ref, vw_ref, gate_ref, ovt_ref, expand_ref,
                o_ref, score_sc, m_sc, l_sc, acc_sc):
    tq = TQ_NSA
    rows = NSA_HPG * tq
    scale = NSA_DH ** -0.5
    g = pl.program_id(1)
    q0 = pl.program_id(2) * tq
    qs = jnp.concatenate([q_ref[:, hh * NSA_DH:(hh + 1) * NSA_DH] for hh in range(NSA_HPG)], axis=0)
    pos_t = q0 + lax.broadcasted_iota(jnp.int32, (tq, 1), 0)

    s = _dot_nt(qs, kc_ref[...]) * scale
    cmp_end = lax.broadcasted_iota(jnp.int32, (tq, N_CMP_PAD), 1) * CMP_STRIDE + (CMP_BLOCK - 1)
    s = s + _tile_heads(jnp.where(cmp_end <= pos_t, 0.0, NEG))
    m = jnp.max(s, -1, keepdims=True)
    e = jnp.where(s > 0.5 * NEG, jnp.exp(s - m), 0.0)
    p = e / jnp.maximum(jnp.sum(e, -1, keepdims=True), 1e-30)
    o_cmp = _dot(p.astype(BF16), vc_ref[...])
    p_sum = p[0:tq] + p[tq:2 * tq] + p[2 * tq:3 * tq] + p[3 * tq:4 * tq]
    ps_hi, ps_lo = _split_bf16(p_sum)
    ovt = ovt_ref[...]
    p_blk_t = _dot_nt(ovt, ps_hi) + _dot_nt(ovt, ps_lo)

    jj = lax.broadcasted_iota(jnp.int32, (N_BLK, tq), 0)
    cur = (q0 + lax.broadcasted_iota(jnp.int32, (N_BLK, tq), 1)) // SEL_BLOCK
    forced = (jj == 0) | (jj == cur) | (jj == cur - 1)
    allowed = jj <= cur
    score = jnp.where(forced, 3.0e38, jnp.where(allowed, p_blk_t, -1.0))
    score_sc[...] = score

    def rank_body(i, rank):
        row = score_sc[pl.ds(i, 1), :]
        first = jnp.where(jj > i, 1.0, 0.0)
        return rank + jnp.where(row > score, 1.0, jnp.where(row == score, first, 0.0))

    n_live = (q0 + tq - 1) // SEL_BLOCK + 1
    rank = lax.fori_loop(0, n_live, rank_body, jnp.zeros((N_BLK, tq), F32))
    sel_t = jnp.where(allowed, jnp.where(rank < SEL_TOPK, 1.0, 0.0), 0.0)
    sel = jnp.concatenate([sel_t, jnp.zeros((LANE - N_BLK, tq), F32)], axis=0).T.astype(BF16)

    m_sc[...] = jnp.full(m_sc.shape, NEG, F32)
    l_sc[...] = jnp.zeros(l_sc.shape, F32)
    acc_sc[...] = jnp.zeros(acc_sc.shape, F32)

    def sel_body(kt, carry):
        k0 = pl.multiple_of(kt * TK_SEL, TK_SEL)
        s = _dot_nt(qs, ks_ref[pl.ds(k0, TK_SEL), :]) * scale
        sel_x = _dot(sel, expand_ref[:, pl.ds(k0, TK_SEL)])
        kpos = k0 + lax.broadcasted_iota(jnp.int32, (tq, TK_SEL), 1)
        bias = jnp.where(kpos <= pos_t, jnp.where(sel_x > 0.5, 0.0, NEG), NEG)
        s = s + _tile_heads(bias)
        m_prev = m_sc[...]
        m_new = jnp.maximum(m_prev, jnp.max(s, -1, keepdims=True))
        alpha = jnp.exp(m_prev - m_new)
        p = jnp.exp(s - jnp.concatenate([m_new] * (TK_SEL // LANE), axis=1))
        l_sc[...] = alpha * l_sc[...] + jnp.sum(p, -1, keepdims=True)
        acc_sc[...] = alpha * acc_sc[...] + _dot(p.astype(BF16), vs_ref[pl.ds(k0, TK_SEL), :])
        m_sc[...] = m_new
        return carry

    lax.fori_loop(0, (q0 + tq - 1) // TK_SEL + 1, sel_body, 0)
    o_slc = acc_sc[...] / l_sc[...]

    start = pl.multiple_of(jnp.maximum(q0 - WINDOW, 0), LANE)
    s = _dot_nt(qs, kw_ref[pl.ds(start, WIN_SPAN), :]) * scale
    dist = pos_t - (start + lax.broadcasted_iota(jnp.int32, (tq, WIN_SPAN), 1))
    bias = jnp.where(dist >= 0, jnp.where(dist < WINDOW, 0.0, NEG), NEG)
    s = s + _tile_heads(bias)
    p = jnp.exp(s - jnp.max(s, -1, keepdims=True))
    o_win = _dot(p.astype(BF16), vw_ref[pl.ds(start, WIN_SPAN), :]) / jnp.sum(p, -1, keepdims=True)

    gates = jax.nn.sigmoid(gate_ref[...])

    def gate(hh, branch):
        lane0 = GATE_LANE0 + hh * 3 + branch
        lane1 = lane0 + NSA_HPG * 3
        return jnp.where(g == 0, gates[:, lane0:lane0 + 1], gates[:, lane1:lane1 + 1])

    for hh in range(NSA_HPG):
        r = slice(hh * tq, (hh + 1) * tq)
        o = gate(hh, 0) * o_cmp[r] + gate(hh, 1) * o_slc[r] + gate(hh, 2) * o_win[r]
        o_ref[:, hh * NSA_DH:(hh + 1) * NSA_DH] = o.astype(BF16)


def _nsa(q_r, kc, vc, ks, vs, kw, vw, proj):
    nq = SEQ // TQ_NSA
    gw = NSA_HPG * NSA_DH
    overlap_t, expand = _nsa_constants()
    rows = NSA_HPG * TQ_NSA
    cmp_spec = pl.BlockSpec((None, None, N_CMP_PAD, NSA_DH), lambda b, g, i: (b, g, 0, 0))
    kv_spec = pl.BlockSpec((None, None, SEQ, NSA_DH), lambda b, g, i: (b, g, 0, 0))
    return pl.pallas_call(
        _nsa_kernel,
        grid=(BATCH, NSA_GROUPS, nq),
        in_specs=[pl.BlockSpec((TQ_NSA, gw), lambda b, g, i: (b * nq + i, g)),
                  cmp_spec, cmp_spec, kv_spec, kv_spec, kv_spec, kv_spec,
                  pl.BlockSpec((TQ_NSA, LANE), lambda b, g, i: (b * nq + i, COL_TAIL // LANE)),
                  pl.BlockSpec(overlap_t.shape, lambda b, g, i: (0, 0)),
                  pl.BlockSpec(expand.shape, lambda b, g, i: (0, 0))],
        out_specs=pl.BlockSpec((TQ_NSA, gw), lambda b, g, i: (b * nq + i, g)),
        out_shape=jax.ShapeDtypeStruct((N_TOK, NSA_GROUPS * gw), BF16),
        scratch_shapes=[pltpu.VMEM((N_BLK, TQ_NSA), F32),
                        pltpu.VMEM((rows, LANE), F32),
                        pltpu.VMEM((rows, LANE), F32),
                        pltpu.VMEM((rows, NSA_DH), F32)],
        compiler_params=_params("parallel", "parallel", "arbitrary"),
        name="nsa_attn",
    )(q_r, kc, vc, ks, vs, kw, vw, proj, jnp.asarray(overlap_t, BF16), jnp.asarray(expand, BF16))


def _outproj_kernel(yg_ref, yn_ref, x_ref, wo_ref, g1_ref, sc2_ref, sh2_ref, ln1g_ref, ln1b_ref,
                    wr_hi_ref, wr_lo_ref, br_ref, ltri_ref, x1_ref, h2_ref, idx_ref, wt_ref, rank_ref,
                    cnt_ref, base_sc):
    half = D_MODEL // 2

    @pl.when(pl.program_id(0) == 0)
    def _():
        base_sc[...] = jnp.zeros_like(base_sc)

    mix = _dot(yg_ref[...], wo_ref[0:half, :]) + _dot(yn_ref[...], wo_ref[half:D_MODEL, :])
    x1 = _ln(DN_ALPHA * x_ref[...] + (1.0 + g1_ref[...]) * mix) * ln1g_ref[...] + ln1b_ref[...]
    x1_ref[...] = x1
    h2 = _ln(x1) * (1.0 + sc2_ref[...]) + sh2_ref[...]
    h_hi, h_lo = _split_bf16(h2)
    h2_ref[...] = h_hi
    wr_hi = wr_hi_ref[...]
    logits = _dot(h_hi, wr_hi) + _dot(h_lo, wr_hi) + _dot(h_hi, wr_lo_ref[...]) + br_ref[...]

    lane = lax.broadcasted_iota(jnp.int32, logits.shape, 1)
    lane_f = lane.astype(F32)
    vals = logits
    idx_out = jnp.zeros(logits.shape, jnp.int32)
    exp_out = jnp.zeros(logits.shape, F32)
    denom = jnp.zeros((logits.shape[0], 1), F32)
    m0 = None
    onehots = []
    for k in range(TOP_K):
        mk = jnp.max(vals, -1, keepdims=True)
        ik = jnp.min(jnp.where(vals == mk, lane_f, float(LANE)), -1, keepdims=True)
        if k == 0:
            m0 = mk
        ek = jnp.exp(mk - m0)
        denom = denom + ek
        idx_out = jnp.where(lane == k, ik.astype(jnp.int32), idx_out)
        exp_out = jnp.where(lane == k, ek, exp_out)
        hit = lane_f == ik
        onehots.append(jnp.where(hit, 1.0, 0.0))
        vals = jnp.where(hit, -3.0e38, vals)
    idx_ref[...] = idx_out
    wt_ref[...] = exp_out / denom

    cnt = onehots[0] + onehots[1] + onehots[2] + onehots[3]
    base = base_sc[0:1, :]
    before = _dot(ltri_ref[...], cnt.astype(BF16)) + base
    rank_out = jnp.zeros(logits.shape, jnp.int32)
    for k in range(TOP_K):
        rk = jnp.sum(onehots[k] * before, -1, keepdims=True)
        rank_out = jnp.where(lane == k, rk.astype(jnp.int32), rank_out)
    rank_ref[...] = rank_out
    base_sc[...] = jnp.broadcast_to(base + jnp.sum(cnt, 0, keepdims=True), base_sc.shape)
    cnt_ref[...] = base_sc[...]


def _outproj(y_gla, y_nsa, xf, w_o, mod3, ln1_g, ln1_b, wr_hi, wr_lo, br):
    tiles_per_batch = SEQ // TM_OUT
    half = D_MODEL // 2
    mod_spec = lambda chunk: pl.BlockSpec((None, 1, D_MODEL), lambda i: (i // tiles_per_batch, 0, chunk))
    full = lambda shape: pl.BlockSpec(shape, lambda i: tuple(0 for _ in shape))
    row = lambda width: pl.BlockSpec((TM_OUT, width), lambda i: (i, 0))
    ltri = jnp.asarray(np.tril(np.ones((TM_OUT, TM_OUT), np.float32), -1), BF16)
    return pl.pallas_call(
        _outproj_kernel,
        grid=(N_TOK // TM_OUT,),
        in_specs=[row(half), row(half), row(D_MODEL), full(w_o.shape),
                  mod_spec(2), mod_spec(4), mod_spec(3),
                  full(ln1_g.shape), full(ln1_b.shape), full(wr_hi.shape), full(wr_lo.shape), full(br.shape),
                  full(ltri.shape)],
        out_specs=[row(D_MODEL), row(D_MODEL), row(LANE), row(LANE), row(LANE), full((8, LANE))],
        out_shape=[jax.ShapeDtypeStruct((N_TOK, D_MODEL), F32),
                   jax.ShapeDtypeStruct((N_TOK, D_MODEL), BF16),
                   jax.ShapeDtypeStruct((N_TOK, LANE), jnp.int32),
                   jax.ShapeDtypeStruct((N_TOK, LANE), F32),
                   jax.ShapeDtypeStruct((N_TOK, LANE), jnp.int32),
                   jax.ShapeDtypeStruct((8, LANE), F32)],
        scratch_shapes=[pltpu.VMEM((8, LANE), F32)],
        compiler_params=_params("arbitrary"),
        name="outproj_router",
    )(y_gla, y_nsa, xf, w_o, mod3, mod3, mod3, ln1_g, ln1_b, wr_hi, wr_lo, br, ltri)


def _moe_up_kernel(eid_ref, nused_ref, x_ref, wg_ref, wu_ref, bg_ref, bu_ref, h_ref, wg_sc, wu_sc):
    t = pl.program_id(1)
    live = t < nused_ref[0]

    @pl.when(live)
    def _():
        @pl.when((t == 0) | (eid_ref[t] != eid_ref[jnp.maximum(t - 1, 0)]))
        def _():
            wg_sc[...] = wg_ref[...].astype(BF16)
            wu_sc[...] = wu_ref[...].astype(BF16)

        x = x_ref[...]
        gate = jnp.minimum(_dot(x, wg_sc[...]) + bg_ref[...], SWIGLU_LIMIT)
        up = jnp.clip(_dot(x, wu_sc[...]) + bu_ref[...], -SWIGLU_LIMIT, SWIGLU_LIMIT)
        h_ref[...] = (gate * jax.nn.sigmoid(SWIGLU_ALPHA * gate) * (up + 1.0)).astype(BF16)

    @pl.when(jnp.logical_not(live))
    def _():
        h_ref[...] = jnp.zeros(h_ref.shape, BF16)


def _moe_down_kernel(eid_ref, nused_ref, h_ref, wd_ref, bd_ref, y_ref, wd_sc):
    t = pl.program_id(1)
    live = t < nused_ref[0]

    @pl.when(live)
    def _():
        @pl.when((t == 0) | (eid_ref[t] != eid_ref[jnp.maximum(t - 1, 0)]))
        def _():
            wd_sc[...] = wd_ref[...].astype(BF16)

        y_ref[...] = _dot(h_ref[...], wd_sc[...]) + bd_ref[...]

    @pl.when(jnp.logical_not(live))
    def _():
        y_ref[...] = jnp.zeros(y_ref.shape, F32)


def _row_tile(t, nused_ref):
    return jnp.minimum(t, jnp.maximum(nused_ref[0] - 1, 0))


def _moe_up(tile_eid, n_used, xs, w_gate, w_up, b_gate, b_up):
    w_spec = pl.BlockSpec((None, D_MODEL, TN_MOE), lambda j, t, eid, nu: (eid[t], 0, j))
    b_spec = pl.BlockSpec((None, 1, TN_MOE), lambda j, t, eid, nu: (eid[t], 0, j))
    return pl.pallas_call(
        _moe_up_kernel,
        grid_spec=pltpu.PrefetchScalarGridSpec(
            num_scalar_prefetch=2,
            grid=(D_FF // TN_MOE, N_MTILES),
            in_specs=[pl.BlockSpec((TM_MOE, D_MODEL), lambda j, t, eid, nu: (_row_tile(t, nu), 0)),
                      w_spec, w_spec, b_spec, b_spec],
            out_specs=pl.BlockSpec((TM_MOE, TN_MOE), lambda j, t, eid, nu: (t, j)),
            scratch_shapes=[pltpu.VMEM((D_MODEL, TN_MOE), BF16), pltpu.VMEM((D_MODEL, TN_MOE), BF16)]),
        out_shape=jax.ShapeDtypeStruct((P_ROWS, D_FF), BF16),
        compiler_params=_params("arbitrary", "arbitrary"),
        name="moe_up",
    )(tile_eid, n_used, xs, w_gate, w_up, b_gate, b_up)


def _moe_down(tile_eid, n_used, h, w_down, b_down):
    return pl.pallas_call(
        _moe_down_kernel,
        grid_spec=pltpu.PrefetchScalarGridSpec(
            num_scalar_prefetch=2,
            grid=(D_MODEL // TN_MOE, N_MTILES),
            in_specs=[pl.BlockSpec((TM_MOE, D_FF), lambda j, t, eid, nu: (_row_tile(t, nu), 0)),
                      pl.BlockSpec((None, D_FF, TN_MOE), lambda j, t, eid, nu: (eid[t], 0, j)),
                      pl.BlockSpec((None, 1, TN_MOE), lambda j, t, eid, nu: (eid[t], 0, j))],
            out_specs=pl.BlockSpec((TM_MOE, TN_MOE), lambda j, t, eid, nu: (t, j)),
            scratch_shapes=[pltpu.VMEM((D_FF, TN_MOE), BF16)]),
        out_shape=jax.ShapeDtypeStruct((P_ROWS, D_MODEL), F32),
        compiler_params=_params("arbitrary", "arbitrary"),
        name="moe_down",
    )(tile_eid, n_used, h, w_down, b_down)


def _final_kernel(dest_ref, y_hbm, x1_ref, wt_ref, g2_ref, ln2g_ref, ln2b_ref, o_ref, ybuf, sems):
    i = pl.program_id(0)
    n_tiles = pl.num_programs(0)

    def start_gather(tile, slot):
        base = tile * (TM_FIN * TOP_K)

        def body(r, carry):
            for k in range(TOP_K):
                d = dest_ref[base + r * TOP_K + k]
                pltpu.make_async_copy(y_hbm.at[pl.ds(d, 1), :], ybuf.at[slot, k, pl.ds(r, 1), :],
                                      sems.at[slot]).start()
            return carry

        lax.fori_loop(0, TM_FIN, body, 0)

    @pl.when(i == 0)
    def _():
        start_gather(0, 0)

    @pl.when(i + 1 < n_tiles)
    def _():
        start_gather(i + 1, (i + 1) % 2)

    slot = i % 2
    pltpu.make_async_copy(ybuf.at[slot], ybuf.at[slot], sems.at[slot]).wait()
    wt = wt_ref[...]
    y = wt[:, 0:1] * ybuf[slot, 0]
    for k in range(1, TOP_K):
        y = y + wt[:, k:k + 1] * ybuf[slot, k]
    o_ref[...] = _ln(DN_ALPHA * x1_ref[...] + (1.0 + g2_ref[...]) * y) * ln2g_ref[...] + ln2b_ref[...]


def _final(dest_flat, y, x1, wt, mod3, ln2_g, ln2_b):
    tiles_per_batch = SEQ // TM_FIN
    full = lambda shape: pl.BlockSpec(shape, lambda i, dest: tuple(0 for _ in shape))
    row = lambda width: pl.BlockSpec((TM_FIN, width), lambda i, dest: (i, 0))
    return pl.pallas_call(
        _final_kernel,
        grid_spec=pltpu.PrefetchScalarGridSpec(
            num_scalar_prefetch=1,
            grid=(N_TOK // TM_FIN,),
            in_specs=[pl.BlockSpec(memory_space=pl.ANY), row(D_MODEL), row(LANE),
                      pl.BlockSpec((None, 1, D_MODEL), lambda i, dest: (i // tiles_per_batch, 0, 5)),
                      full(ln2_g.shape), full(ln2_b.shape)],
            out_specs=row(D_MODEL),
            scratch_shapes=[pltpu.VMEM((2, TOP_K, TM_FIN, D_MODEL), F32),
                            pltpu.SemaphoreType.DMA((2,))]),
        out_shape=jax.ShapeDtypeStruct((N_TOK, D_MODEL), F32),
        compiler_params=_params("arbitrary"),
        name="combine_ln2",
    )(dest_flat, y, x1, wt, mod3, ln2_g, ln2_b)


def _route(idx, rank, counts):
    experts = jnp.arange(N_EXPERTS, dtype=jnp.int32)
    padded = ((counts + TM_MOE - 1) // TM_MOE) * TM_MOE
    ends = jnp.sum(jnp.where(experts[None, :] <= experts[:, None], padded[None, :], 0), axis=1)
    starts = ends - padded
    dest = jnp.sum(jnp.where(idx[..., None] == experts, starts, 0), axis=-1) + rank
    tok = jnp.broadcast_to(jnp.arange(N_TOK, dtype=jnp.int32)[:, None], dest.shape)
    src_tok = jnp.zeros((P_ROWS,), jnp.int32).at[dest.reshape(-1)].set(tok.reshape(-1))
    tile_start = jnp.arange(N_MTILES, dtype=jnp.int32) * TM_MOE
    tile_eid = jnp.minimum(jnp.sum((ends[None, :] <= tile_start[:, None]).astype(jnp.int32), axis=1),
                           N_EXPERTS - 1)
    n_used = (ends[-1] // TM_MOE).astype(jnp.int32).reshape(1)
    return dest, src_tok, tile_eid, n_used


def kernel(x, c, w_ada, b_ada, w_in, w_gk, b_gk, gla_norm_g, pe_k, pe_v, w_ck1, b_ck1, w_ck2, b_ck2,
           w_cv1, b_cv1, w_cv2, b_cv2, w_o, ln1_g, ln1_b, w_router, b_router, w_gate, b_gate, w_up, b_up,
           w_down, b_down, ln2_g, ln2_b):
    l = 0
    xf = x.reshape(N_TOK, D_MODEL)
    row2 = lambda a: a.reshape(1, -1)

    c8 = jnp.pad(c, ((0, 8 - BATCH), (0, 0)))
    mod3 = _adaln(c8, w_ada[l], row2(b_ada[l]))[:BATCH].reshape(BATCH, 1, 6 * D_MODEL)

    w = w_in[l]
    glr0 = 3072
    nsa0 = glr0 + GLA_RANK
    ngt0 = nsa0 + 1024 + 6 * 256
    w_in_p = jnp.concatenate(
        [w[:, :glr0], w[:, nsa0:ngt0], w[:, glr0:nsa0], w[:, ngt0:],
         jnp.zeros((D_MODEL, D_IN_PAD - w.shape[1]), F32)], axis=1).astype(BF16)
    proj = _inproj(xf, mod3, w_in_p)

    w_gk_pad = jnp.pad(w_gk[l], ((0, LANE - GLA_RANK), (0, 0)))
    y_gla = _gla(proj, w_gk_pad, row2(b_gk[l]), row2(gla_norm_g[l]))

    cos_t, sin_t = _rope_tables(jnp.arange(SEQ))
    cmp_end = jnp.arange(N_CMP_PAD) * CMP_STRIDE + (CMP_BLOCK - 1)
    cos_c, sin_c = _rope_tables(cmp_end)
    q_r, ks, vs, kw, vw = _prep(proj, cos_t, sin_t)
    kc = _compress(proj, 0, pe_k[l], w_ck1[l], row2(b_ck1[l]), w_ck2[l], row2(b_ck2[l]), cos_c, sin_c, True)
    vc = _compress(proj, 1, pe_v[l], w_cv1[l], row2(b_cv1[l]), w_cv2[l], row2(b_cv2[l]), cos_c, sin_c, False)
    y_nsa = _nsa(q_r, kc, vc, ks, vs, kw, vw, proj)

    wr = jnp.pad(w_router[l], ((0, 0), (0, LANE - N_EXPERTS)))
    wr_hi, wr_lo = _split_bf16(wr)
    br = jnp.concatenate([b_router[l], jnp.full((LANE - N_EXPERTS,), NEG, F32)]).reshape(1, LANE)
    x1, h2, idx128, wt128, rank128, cnt8 = _outproj(y_gla, y_nsa, xf, w_o[l].astype(BF16), mod3,
                                                    row2(ln1_g[l]), row2(ln1_b[l]), wr_hi, wr_lo, br)

    counts = cnt8[0, :N_EXPERTS].astype(jnp.int32)
    dest, src_tok, tile_eid, n_used = _route(idx128[:, :TOP_K], rank128[:, :TOP_K], counts)
    xs = jnp.take(h2, src_tok, axis=0)
    h = _moe_up(tile_eid, n_used, xs, w_gate[l], w_up[l],
                b_gate[l].reshape(N_EXPERTS, 1, D_FF), b_up[l].reshape(N_EXPERTS, 1, D_FF))
    y = _moe_down(tile_eid, n_used, h, w_down[l], b_down[l].reshape(N_EXPERTS, 1, D_MODEL))

    out = _final(dest.reshape(-1), y, x1, wt128, mod3, row2(ln2_g[l]), row2(ln2_b[l]))
    return out.reshape(BATCH, SEQ, D_MODEL)
```

```python
import functools

import numpy as np
import jax
import jax.numpy as jnp
from jax import lax
from jax.experimental import pallas as pl
from jax.experimental.pallas import tpu as pltpu

F32 = jnp.float32
BF16 = jnp.bfloat16

D_MODEL = 2048
BATCH = 2
SEQ = 4096
N_TOK = BATCH * SEQ

GLA_HEADS = 4
GLA_DK = 128
GLA_DV = 256
GLA_RANK = 16
GLA_TAU = 16.0
GLA_CHUNK = 64

NSA_DH = 128
NSA_HEADS = 8
NSA_GROUPS = 2
NSA_HPG = 4
CMP_STRIDE = 16
CMP_BLOCK = 32
CMP_HIDDEN = 256
N_CMP_PAD = SEQ // CMP_STRIDE
SEL_BLOCK = 64
N_BLK = SEQ // SEL_BLOCK
SEL_TOPK = 16
WINDOW = 512
ROPE_DIM = 32
ROPE_THETA = 500000.0

N_EXPERTS = 32
TOP_K = 4
D_FF = D_MODEL
SWIGLU_LIMIT = 7.0
SWIGLU_ALPHA = 1.702
DN_ALPHA = 2.0 ** 0.25
LN_EPS = 1e-5

COL_GQ, COL_GK, COL_GV, COL_GR, COL_NQ = 0, 512, 1024, 2048, 3072
COL_KV = 4096
COL_TAIL = 5632
D_IN_PAD = 5760
GATE_LANE0 = GLA_RANK

LANE = 128
NEG = -1e30
VMEM_LIMIT = 56 * 1024 * 1024

TM_IN, TN_IN = 1024, 640
TT_GLA = 512
TR_PREP = 512
TQ_NSA = 128
TK_SEL = 256
WIN_SPAN = WINDOW + TQ_NSA
TM_OUT = 256
TM_MOE, TN_MOE = 512, 512
P_ROWS = N_TOK * TOP_K + N_EXPERTS * TM_MOE
N_MTILES = P_ROWS // TM_MOE
TM_FIN = 256
DMA_BATCH = 8


def _dot(a, b):
    return jnp.dot(a, b, preferred_element_type=F32)


def _dot_nt(a, b):
    return lax.dot_general(a, b, (((1,), (1,)), ((), ())), preferred_element_type=F32)


def _dot_tn(a, b):
    return lax.dot_general(a, b, (((0,), (0,)), ((), ())), preferred_element_type=F32)


def _ln(x):
    xc = x - jnp.mean(x, -1, keepdims=True)
    return xc * lax.rsqrt(jnp.mean(xc * xc, -1, keepdims=True) + LN_EPS)


def _split_bf16(x):
    hi = x.astype(BF16)
    lo = (x - hi.astype(F32)).astype(BF16)
    return hi, lo


def _params(*sem):
    return pltpu.CompilerParams(dimension_semantics=sem, vmem_limit_bytes=VMEM_LIMIT)


def _adaln_kernel(c_ref, w_ref, b_ref, o_ref):
    c = c_ref[...]
    a = (c * jax.nn.sigmoid(c)).astype(BF16)
    o_ref[...] = _dot(a, w_ref[...].astype(BF16)) + b_ref[...]


def _adaln(c8, w, b):
    n = w.shape[1]
    tn = 1024
    return pl.pallas_call(
        _adaln_kernel,
        grid=(n // tn,),
        in_specs=[pl.BlockSpec((8, D_MODEL), lambda j: (0, 0)),
                  pl.BlockSpec((D_MODEL, tn), lambda j: (0, j)),
                  pl.BlockSpec((1, tn), lambda j: (0, j))],
        out_specs=pl.BlockSpec((8, tn), lambda j: (0, j)),
        out_shape=jax.ShapeDtypeStruct((8, n), F32),
        compiler_params=_params("arbitrary"),
        name="adaln",
    )(c8, w, b)


def _inproj_kernel(x_ref, sh_ref, sc_ref, w_ref, o_ref, h_sc):
    @pl.when(pl.program_id(1) == 0)
    def _():
        h = _ln(x_ref[...]) * (1.0 + sc_ref[...]) + sh_ref[...]
        h_sc[...] = h.astype(BF16)

    o_ref[...] = _dot(h_sc[...], w_ref[...])


def _inproj(xf, mod3, w_in_p):
    tiles_per_batch = SEQ // TM_IN
    return pl.pallas_call(
        _inproj_kernel,
        grid=(N_TOK // TM_IN, D_IN_PAD // TN_IN),
        in_specs=[pl.BlockSpec((TM_IN, D_MODEL), lambda i, j: (i, 0)),
                  pl.BlockSpec((None, 1, D_MODEL), lambda i, j: (i // tiles_per_batch, 0, 0)),
                  pl.BlockSpec((None, 1, D_MODEL), lambda i, j: (i // tiles_per_batch, 0, 1)),
                  pl.BlockSpec((D_MODEL, TN_IN), lambda i, j: (0, j))],
        out_specs=pl.BlockSpec((TM_IN, TN_IN), lambda i, j: (i, j)),
        out_shape=jax.ShapeDtypeStruct((N_TOK, D_IN_PAD), F32),
        scratch_shapes=[pltpu.VMEM((TM_IN, D_MODEL), BF16)],
        compiler_params=_params("parallel", "arbitrary"),
        name="inproj",
    )(xf, mod3, mod3, w_in_p)


GLA_HALVES = (32, 16, 8, 4, 2, 1)
N_LEVELS = len(GLA_HALVES)
ROW_EB = 2 * N_LEVELS
ROW_EL = 2 * N_LEVELS + 1
N_EVIEWS = 2 * N_LEVELS + 2


def _gla_constants():
    c = GLA_CHUNK
    t = np.arange(c)[:, None]
    r = np.arange(c)[None, :]
    mall = np.zeros((N_EVIEWS, c, c), np.float32)
    valid = np.zeros((N_EVIEWS, c, LANE), np.float32)
    masks = np.zeros((N_LEVELS + 1, c, c), np.float32)
    for li, n in enumerate(GLA_HALVES):
        same = (t // (2 * n)) == (r // (2 * n))
        t_up = (t % (2 * n)) >= n
        r_up = (r % (2 * n)) >= n
        mall[2 * li] = same & t_up & r_up & (r <= t)
        mall[2 * li + 1] = same & ~t_up & ~r_up & (r > t)
        valid[2 * li] = np.broadcast_to(t_up, (c, LANE))
        valid[2 * li + 1] = np.broadcast_to(~t_up, (c, LANE))
        masks[li] = same & t_up & ~r_up
    mall[ROW_EB] = r <= t
    mall[ROW_EL] = r > t
    valid[ROW_EB] = 1.0
    valid[ROW_EL] = 1.0
    masks[N_LEVELS] = np.eye(c)
    return (mall.reshape(N_EVIEWS * c, c), valid.reshape(N_EVIEWS * c, LANE), masks)


def _gla_kernel(q_ref, k_ref, v_ref, r_ref, glr_ref, wgk_ref, bgk_ref, g_ref, mall_ref, valid_ref,
                masks_ref, o_ref, st_sc):
    c = GLA_CHUNK

    @pl.when(pl.program_id(2) == 0)
    def _():
        st_sc[...] = jnp.zeros_like(st_sc)

    z = _dot(glr_ref[...].astype(BF16), wgk_ref[...].astype(BF16)) + bgk_ref[...]
    log_a = (jnp.minimum(z, 0.0) - jnp.log1p(jnp.exp(-jnp.abs(z)))) * (1.0 / GLA_TAU)
    mall = mall_ref[...]
    for ci in range(TT_GLA // c):
        rows = slice(ci * c, (ci + 1) * c)
        la_hi, la_lo = _split_bf16(log_a[rows])
        e = jnp.exp(_dot(mall, la_hi) + _dot(mall, la_lo)) * valid_ref[...]
        q = q_ref[rows, :] * (GLA_DK ** -0.5)
        k = k_ref[rows, :]
        vb = v_ref[rows, :].astype(BF16)
        att = masks_ref[N_LEVELS] * _dot_nt(q.astype(BF16), k.astype(BF16))
        for li in range(N_LEVELS):
            eq = e[(2 * li) * c:(2 * li + 1) * c]
            ek = e[(2 * li + 1) * c:(2 * li + 2) * c]
            att = att + masks_ref[li] * _dot_nt((q * eq).astype(BF16), (k * ek).astype(BF16))
        eb = e[ROW_EB * c:(ROW_EB + 1) * c]
        el = e[ROW_EL * c:(ROW_EL + 1) * c]
        st = st_sc[...]
        o = _dot_nt((q * eb).astype(BF16), st.astype(BF16)) + _dot(att.astype(BF16), vb)
        st_sc[...] = st * eb[c - 1:c, :] + _dot_tn(vb, (k * el).astype(BF16))
        o = o * lax.rsqrt(jnp.mean(o * o, -1, keepdims=True) + LN_EPS) * g_ref[...]
        r = r_ref[rows, :]
        o_ref[rows, :] = (o * (r * jax.nn.sigmoid(r))).astype(BF16)


def _gla(proj, w_gk_pad, b_gk, norm_g):
    nt = SEQ // TT_GLA
    mall, valid, masks = _gla_constants()
    row = lambda b, h, i: b * nt + i
    return pl.pallas_call(
        _gla_kernel,
        grid=(BATCH, GLA_HEADS, nt),
        in_specs=[pl.BlockSpec((TT_GLA, GLA_DK), lambda b, h, i: (row(b, h, i), COL_GQ // GLA_DK + h)),
                  pl.BlockSpec((TT_GLA, GLA_DK), lambda b, h, i: (row(b, h, i), COL_GK // GLA_DK + h)),
                  pl.BlockSpec((TT_GLA, GLA_DV), lambda b, h, i: (row(b, h, i), COL_GV // GLA_DV + h)),
                  pl.BlockSpec((TT_GLA, GLA_DV), lambda b, h, i: (row(b, h, i), COL_GR // GLA_DV + h)),
                  pl.BlockSpec((TT_GLA, LANE), lambda b, h, i: (row(b, h, i), COL_TAIL // LANE)),
                  pl.BlockSpec((LANE, GLA_DK), lambda b, h, i: (0, h)),
                  pl.BlockSpec((1, GLA_DK), lambda b, h, i: (0, h)),
                  pl.BlockSpec((1, GLA_DV), lambda b, h, i: (0, 0)),
                  pl.BlockSpec(mall.shape, lambda b, h, i: (0, 0)),
                  pl.BlockSpec(valid.shape, lambda b, h, i: (0, 0)),
                  pl.BlockSpec(masks.shape, lambda b, h, i: (0, 0, 0))],
        out_specs=pl.BlockSpec((TT_GLA, GLA_DV), lambda b, h, i: (row(b, h, i), h)),
        out_shape=jax.ShapeDtypeStruct((N_TOK, GLA_HEADS * GLA_DV), BF16),
        scratch_shapes=[pltpu.VMEM((GLA_DV, GLA_DK), F32)],
        compiler_params=_params("parallel", "parallel", "arbitrary"),
        name="gla",
    )(proj, proj, proj, proj, proj, w_gk_pad, b_gk, norm_g,
      jnp.asarray(mall, BF16), jnp.asarray(valid), jnp.asarray(masks))


def _rope_tables(pos):
    half = ROPE_DIM // 2
    inv_freq = ROPE_THETA ** (-jnp.arange(half, dtype=F32) * (2.0 / ROPE_DIM))
    ang = pos.astype(F32)[:, None] * inv_freq
    cos, sin = jnp.cos(ang), jnp.sin(ang)
    n = pos.shape[0]
    cosf = jnp.concatenate([cos, cos, jnp.ones((n, LANE - ROPE_DIM), F32)], -1)
    sinf = jnp.concatenate([-sin, sin, jnp.zeros((n, LANE - ROPE_DIM), F32)], -1)
    return cosf, sinf


def _rope(x, cosf, sinf):
    lane = lax.broadcasted_iota(jnp.int32, x.shape, 1)
    half = ROPE_DIM // 2
    swapped = jnp.where(lane < half, pltpu.roll(x, LANE - half, 1), pltpu.roll(x, half, 1))
    return x * cosf + swapped * sinf


def _prep_kernel(q_ref, ks_ref, vs_ref, kw_ref, vw_ref, cos_ref, sin_ref,
                 qo_ref, kso_ref, vso_ref, kwo_ref, vwo_ref):
    cosf = cos_ref[...]
    sinf = sin_ref[...]
    for hh in range(NSA_HEADS):
        cols = slice(hh * NSA_DH, (hh + 1) * NSA_DH)
        qo_ref[:, cols] = _rope(q_ref[:, cols], cosf, sinf).astype(BF16)
    for g in range(NSA_GROUPS):
        cols = slice(g * NSA_DH, (g + 1) * NSA_DH)
        kso_ref[g] = _rope(ks_ref[:, cols], cosf, sinf).astype(BF16)
        kwo_ref[g] = _rope(kw_ref[:, cols], cosf, sinf).astype(BF16)
        vso_ref[g] = vs_ref[:, cols].astype(BF16)
        vwo_ref[g] = vw_ref[:, cols].astype(BF16)


def _prep(proj, cosf, sinf):
    nt = SEQ // TR_PREP
    kvw = NSA_GROUPS * NSA_DH
    kv_in = lambda which: pl.BlockSpec((TR_PREP, kvw), lambda b, i: (b * nt + i, COL_KV // kvw + which))
    kv_out = pl.BlockSpec((None, NSA_GROUPS, TR_PREP, NSA_DH), lambda b, i: (b, 0, i, 0))
    kv_shape = jax.ShapeDtypeStruct((BATCH, NSA_GROUPS, SEQ, NSA_DH), BF16)
    qw = NSA_HEADS * NSA_DH
    return pl.pallas_call(
        _prep_kernel,
        grid=(BATCH, nt),
        in_specs=[pl.BlockSpec((TR_PREP, qw), lambda b, i: (b * nt + i, COL_NQ // qw)),
                  kv_in(2), kv_in(3), kv_in(4), kv_in(5),
                  pl.BlockSpec((TR_PREP, LANE), lambda b, i: (i, 0)),
                  pl.BlockSpec((TR_PREP, LANE), lambda b, i: (i, 0))],
        out_specs=[pl.BlockSpec((TR_PREP, qw), lambda b, i: (b * nt + i, 0)),
                   kv_out, kv_out, kv_out, kv_out],
        out_shape=[jax.ShapeDtypeStruct((N_TOK, qw), BF16), kv_shape, kv_shape, kv_shape, kv_shape],
        compiler_params=_params("parallel", "parallel"),
        name="nsa_prep",
    )(proj, proj, proj, proj, proj, cosf, sinf)


def _compress_kernel(a_ref, pe_ref, w1_ref, b1_ref, w2_ref, b2_ref, cos_ref, sin_ref, o_ref, bot_sc,
                     *, rope):
    n = N_CMP_PAD
    top = jnp.zeros((n, CMP_HIDDEN), F32)
    bot = jnp.zeros((n, CMP_HIDDEN), F32)
    for p in range(CMP_STRIDE):
        ap = a_ref[pl.ds(p, n, stride=CMP_STRIDE), :]
        w_top = w1_ref[p * NSA_DH:(p + 1) * NSA_DH, :].astype(BF16)
        w_bot = w1_ref[(CMP_STRIDE + p) * NSA_DH:(CMP_STRIDE + p + 1) * NSA_DH, :].astype(BF16)
        top = top + _dot((ap + pe_ref[p:p + 1, :]).astype(BF16), w_top)
        bot = bot + _dot((ap + pe_ref[CMP_STRIDE + p:CMP_STRIDE + p + 1, :]).astype(BF16), w_bot)
    bot_sc[0:n, :] = bot
    bot_sc[n:n + 8, :] = jnp.zeros((8, CMP_HIDDEN), F32)
    h = top + bot_sc[1:n + 1, :] + b1_ref[...]
    h = h * jax.nn.sigmoid(h)
    out = _dot(h.astype(BF16), w2_ref[...].astype(BF16)) + b2_ref[...]
    if rope:
        out = _rope(out, cos_ref[...], sin_ref[...])
    row = lax.broadcasted_iota(jnp.int32, out.shape, 0)
    o_ref[...] = jnp.where(row < n - 1, out, 0.0).astype(BF16)


def _compress(proj, which, pe, w1, b1, w2, b2, cosf, sinf, rope):
    col0 = (COL_KV + which * NSA_GROUPS * NSA_DH) // NSA_DH
    full = lambda shape: pl.BlockSpec(shape, lambda b, g: tuple(0 for _ in shape))
    return pl.pallas_call(
        functools.partial(_compress_kernel, rope=rope),
        grid=(BATCH, NSA_GROUPS),
        in_specs=[pl.BlockSpec((SEQ, NSA_DH), lambda b, g: (b, col0 + g)),
                  full(pe.shape), full(w1.shape), full(b1.shape), full(w2.shape), full(b2.shape),
                  full(cosf.shape), full(sinf.shape)],
        out_specs=pl.BlockSpec((None, None, N_CMP_PAD, NSA_DH), lambda b, g: (b, g, 0, 0)),
        out_shape=jax.ShapeDtypeStruct((BATCH, NSA_GROUPS, N_CMP_PAD, NSA_DH), BF16),
        scratch_shapes=[pltpu.VMEM((N_CMP_PAD + 8, CMP_HIDDEN), F32)],
        compiler_params=_params("parallel", "parallel"),
        name="nsa_compress",
    )(proj, pe, w1, b1, w2, b2, cosf, sinf)


def _nsa_constants():
    c_start = np.arange(N_CMP_PAD) * CMP_STRIDE
    b_start = np.arange(N_BLK) * SEL_BLOCK
    overlap_t = ((c_start[None, :] < b_start[:, None] + SEL_BLOCK)
                 & (c_start[None, :] + CMP_BLOCK > b_start[:, None])).astype(np.float32)
    overlap_t[:, N_CMP_PAD - 1] = 0.0
    expand = (np.arange(SEQ)[None, :] // SEL_BLOCK == np.arange(LANE)[:, None]).astype(np.float32)
    return overlap_t, expand


def _tile_heads(a):
    return jnp.concatenate([a] * NSA_HPG, axis=0)


def _nsa_kernel(q_ref, kc_ref, vc_ref, ks_ref, vs_ref, kw_ref, vw_ref, gate_ref, ovt_ref, expand_ref,
                o_ref, score_sc, m_sc, l_sc, acc_sc):
    tq = TQ_NSA
    rows = NSA_HPG * tq
    scale = NSA_DH ** -0.5
    g = pl.program_id(1)
    q0 = pl.program_id(2) * tq
    qs = jnp.concatenate([q_ref[:, hh * NSA_DH:(hh + 1) * NSA_DH] for hh in range(NSA_HPG)], axis=0)
    pos_t = q0 + lax.broadcasted_iota(jnp.int32, (tq, 1), 0)

    s = _dot_nt(qs, kc_ref[...]) * scale
    cmp_end = lax.broadcasted_iota(jnp.int32, (tq, N_CMP_PAD), 1) * CMP_STRIDE + (CMP_BLOCK - 1)
    s = s + _tile_heads(jnp.where(cmp_end <= pos_t, 0.0, NEG))
    m = jnp.max(s, -1, keepdims=True)
    e = jnp.where(s > 0.5 * NEG, jnp.exp(s - m), 0.0)
    p = e / jnp.maximum(jnp.sum(e, -1, keepdims=True), 1e-30)
    o_cmp = _dot(p.astype(BF16), vc_ref[...])
    p_sum = p[0:tq] + p[tq:2 * tq] + p[2 * tq:3 * tq] + p[3 * tq:4 * tq]
    ps_hi, ps_lo = _split_bf16(p_sum)
    ovt = ovt_ref[...]
    p_blk_t = _dot_nt(ovt, ps_hi) + _dot_nt(ovt, ps_lo)

    jj = lax.broadcasted_iota(jnp.int32, (N_BLK, tq), 0)
    cur = (q0 + lax.broadcasted_iota(jnp.int32, (N_BLK, tq), 1)) // SEL_BLOCK
    forced = (jj == 0) | (jj == cur) | (jj == cur - 1)
    allowed = jj <= cur
    score = jnp.where(forced, 3.0e38, jnp.where(allowed, p_blk_t, -1.0))
    score_sc[...] = score

    def rank_body(i, rank):
        row = score_sc[pl.ds(i, 1), :]
        first = jnp.where(jj > i, 1.0, 0.0)
        return rank + jnp.where(row > score, 1.0, jnp.where(row == score, first, 0.0))

    n_live = (q0 + tq - 1) // SEL_BLOCK + 1
    rank = lax.fori_loop(0, n_live, rank_body, jnp.zeros((N_BLK, tq), F32))
    sel_t = jnp.where(allowed, jnp.where(rank < SEL_TOPK, 1.0, 0.0), 0.0)
    sel = jnp.concatenate([sel_t, jnp.zeros((LANE - N_BLK, tq), F32)], axis=0).T.astype(BF16)

    m_sc[...] = jnp.full(m_sc.shape, NEG, F32)
    l_sc[...] = jnp.zeros(l_sc.shape, F32)
    acc_sc[...] = jnp.zeros(acc_sc.shape, F32)

    def sel_body(kt, carry):
        k0 = pl.multiple_of(kt * TK_SEL, TK_SEL)
        s = _dot_nt(qs, ks_ref[pl.ds(k0, TK_SEL), :]) * scale
        sel_x = _dot(sel, expand_ref[:, pl.ds(k0, TK_SEL)])
        kpos = k0 + lax.broadcasted_iota(jnp.int32, (tq, TK_SEL), 1)
        bias = jnp.where(kpos <= pos_t, jnp.where(sel_x > 0.5, 0.0, NEG), NEG)
        s = s + _tile_heads(bias)
        m_prev = m_sc[...]
        m_new = jnp.maximum(m_prev, jnp.max(s, -1, keepdims=True))
        alpha = jnp.exp(m_prev - m_new)
        p = jnp.exp(s - jnp.concatenate([m_new] * (TK_SEL // LANE), axis=1))
        l_sc[...] = alpha * l_sc[...] + jnp.sum(p, -1, keepdims=True)
        acc_sc[...] = alpha * acc_sc[...] + _dot(p.astype(BF16), vs_ref[pl.ds(k0, TK_SEL), :])
        m_sc[...] = m_new
        return carry

    lax.fori_loop(0, (q0 + tq - 1) // TK_SEL + 1, sel_body, 0)
    o_slc = acc_sc[...] / l_sc[...]

    start = pl.multiple_of(jnp.maximum(q0 - WINDOW, 0), LANE)
    s = _dot_nt(qs, kw_ref[pl.ds(start, WIN_SPAN), :]) * scale
    dist = pos_t - (start + lax.broadcasted_iota(jnp.int32, (tq, WIN_SPAN), 1))
    bias = jnp.where(dist >= 0, jnp.where(dist < WINDOW, 0.0, NEG), NEG)
    s = s + _tile_heads(bias)
    p = jnp.exp(s - jnp.max(s, -1, keepdims=True))
    o_win = _dot(p.astype(BF16), vw_ref[pl.ds(start, WIN_SPAN), :]) / jnp.sum(p, -1, keepdims=True)

    gates = jax.nn.sigmoid(gate_ref[...])

    def gate(hh, branch):
        lane0 = GATE_LANE0 + hh * 3 + branch
        lane1 = lane0 + NSA_HPG * 3
        return jnp.where(g == 0, gates[:, lane0:lane0 + 1], gates[:, lane1:lane1 + 1])

    for hh in range(NSA_HPG):
        r = slice(hh * tq, (hh + 1) * tq)
        o = gate(hh, 0) * o_cmp[r] + gate(hh, 1) * o_slc[r] + gate(hh, 2) * o_win[r]
        o_ref[:, hh * NSA_DH:(hh + 1) * NSA_DH] = o.astype(BF16)


def _nsa(q_r, kc, vc, ks, vs, kw, vw, proj):
    nq = SEQ // TQ_NSA
    gw = NSA_HPG * NSA_DH
    overlap_t, expand = _nsa_constants()
    rows = NSA_HPG * TQ_NSA
    cmp_spec = pl.BlockSpec((None, None, N_CMP_PAD, NSA_DH), lambda b, g, i: (b, g, 0, 0))
    kv_spec = pl.BlockSpec((None, None, SEQ, NSA_DH), lambda b, g, i: (b, g, 0, 0))
    return pl.pallas_call(
        _nsa_kernel,
        grid=(BATCH, NSA_GROUPS, nq),
        in_specs=[pl.BlockSpec((TQ_NSA, gw), lambda b, g, i: (b * nq + i, g)),
                  cmp_spec, cmp_spec, kv_spec, kv_spec, kv_spec, kv_spec,
                  pl.BlockSpec((TQ_NSA, LANE), lambda b, g, i: (b * nq + i, COL_TAIL // LANE)),
                  pl.BlockSpec(overlap_t.shape, lambda b, g, i: (0, 0)),
                  pl.BlockSpec(expand.shape, lambda b, g, i: (0, 0))],
        out_specs=pl.BlockSpec((TQ_NSA, gw), lambda b, g, i: (b * nq + i, g)),
        out_shape=jax.ShapeDtypeStruct((N_TOK, NSA_GROUPS * gw), BF16),
        scratch_shapes=[pltpu.VMEM((N_BLK, TQ_NSA), F32),
                        pltpu.VMEM((rows, LANE), F32),
                        pltpu.VMEM((rows, LANE), F32),
                        pltpu.VMEM((rows, NSA_DH), F32)],
        compiler_params=_params("parallel", "parallel", "arbitrary"),
        name="nsa_attn",
    )(q_r, kc, vc, ks, vs, kw, vw, proj, jnp.asarray(overlap_t, BF16), jnp.asarray(expand, BF16))


def _outproj_kernel(yg_ref, yn_ref, x_ref, wo_ref, g1_ref, sc2_ref, sh2_ref, ln1g_ref, ln1b_ref,
                    wr_hi_ref, wr_lo_ref, br_ref, ltri_ref, x1_ref, h2_ref, idx_ref, wt_ref, rank_ref,
                    cnt_ref, base_sc):
    half = D_MODEL // 2

    @pl.when(pl.program_id(0) == 0)
    def _():
        base_sc[...] = jnp.zeros_like(base_sc)

    mix = _dot(yg_ref[...], wo_ref[0:half, :]) + _dot(yn_ref[...], wo_ref[half:D_MODEL, :])
    x1 = _ln(DN_ALPHA * x_ref[...] + (1.0 + g1_ref[...]) * mix) * ln1g_ref[...] + ln1b_ref[...]
    x1_ref[...] = x1
    h2 = _ln(x1) * (1.0 + sc2_ref[...]) + sh2_ref[...]
    h_hi, h_lo = _split_bf16(h2)
    h2_ref[...] = h2
    wr_hi = wr_hi_ref[...]
    logits = _dot(h_hi, wr_hi) + _dot(h_lo, wr_hi) + _dot(h_hi, wr_lo_ref[...]) + br_ref[...]

    lane = lax.broadcasted_iota(jnp.int32, logits.shape, 1)
    lane_f = lane.astype(F32)
    vals = logits
    idx_out = jnp.zeros(logits.shape, jnp.int32)
    exp_out = jnp.zeros(logits.shape, F32)
    denom = jnp.zeros((logits.shape[0], 1), F32)
    m0 = None
    onehots = []
    for k in range(TOP_K):
        mk = jnp.max(vals, -1, keepdims=True)
        ik = jnp.min(jnp.where(vals == mk, lane_f, float(LANE)), -1, keepdims=True)
        if k == 0:
            m0 = mk
        ek = jnp.exp(mk - m0)
        denom = denom + ek
        idx_out = jnp.where(lane == k, ik.astype(jnp.int32), idx_out)
        exp_out = jnp.where(lane == k, ek, exp_out)
        hit = lane_f == ik
        onehots.append(jnp.where(hit, 1.0, 0.0))
        vals = jnp.where(hit, -3.0e38, vals)
    idx_ref[...] = idx_out
    wt_ref[...] = exp_out / denom

    cnt = onehots[0] + onehots[1] + onehots[2] + onehots[3]
    base = base_sc[0:1, :]
    before = _dot(ltri_ref[...], cnt.astype(BF16)) + base
    rank_out = jnp.zeros(logits.shape, jnp.int32)
    for k in range(TOP_K):
        rk = jnp.sum(onehots[k] * before, -1, keepdims=True)
        rank_out = jnp.where(lane == k, rk.astype(jnp.int32), rank_out)
    rank_ref[...] = rank_out
    base_sc[...] = jnp.broadcast_to(base + jnp.sum(cnt, 0, keepdims=True), base_sc.shape)
    cnt_ref[...] = base_sc[...]


def _outproj(y_gla, y_nsa, xf, w_o, mod3, ln1_g, ln1_b, wr_hi, wr_lo, br):
    tiles_per_batch = SEQ // TM_OUT
    half = D_MODEL // 2
    mod_spec = lambda chunk: pl.BlockSpec((None, 1, D_MODEL), lambda i: (i // tiles_per_batch, 0, chunk))
    full = lambda shape: pl.BlockSpec(shape, lambda i: tuple(0 for _ in shape))
    row = lambda width: pl.BlockSpec((TM_OUT, width), lambda i: (i, 0))
    ltri = jnp.asarray(np.tril(np.ones((TM_OUT, TM_OUT), np.float32), -1), BF16)
    return pl.pallas_call(
        _outproj_kernel,
        grid=(N_TOK // TM_OUT,),
        in_specs=[row(half), row(half), row(D_MODEL), full(w_o.shape),
                  mod_spec(2), mod_spec(4), mod_spec(3),
                  full(ln1_g.shape), full(ln1_b.shape), full(wr_hi.shape), full(wr_lo.shape), full(br.shape),
                  full(ltri.shape)],
        out_specs=[row(D_MODEL), row(D_MODEL), row(LANE), row(LANE), row(LANE), full((8, LANE))],
        out_shape=[jax.ShapeDtypeStruct((N_TOK, D_MODEL), F32),
                   jax.ShapeDtypeStruct((N_TOK, D_MODEL), F32),
                   jax.ShapeDtypeStruct((N_TOK, LANE), jnp.int32),
                   jax.ShapeDtypeStruct((N_TOK, LANE), F32),
                   jax.ShapeDtypeStruct((N_TOK, LANE), jnp.int32),
                   jax.ShapeDtypeStruct((8, LANE), F32)],
        scratch_shapes=[pltpu.VMEM((8, LANE), F32)],
        compiler_params=_params("arbitrary"),
        name="outproj_router",
    )(y_gla, y_nsa, xf, w_o, mod3, mod3, mod3, ln1_g, ln1_b, wr_hi, wr_lo, br, ltri)


def _dispatch_kernel(dest_ref, nused_ref, h2_hbm, xs_ref, src_sm, buf, sems):
    t = pl.program_id(0)
    n_used = nused_ref[0]

    @pl.when(t == 0)
    def _():
        def fill(r, carry):
            src_sm[r] = 0
            return carry

        lax.fori_loop(0, P_ROWS, fill, 0, unroll=8)

        def scatter(tok2, carry):
            a0 = tok2 * (2 * TOP_K)
            rows = [dest_ref[a0 + i] for i in range(2 * TOP_K)]
            for i, row in enumerate(rows):
                src_sm[row] = tok2 * 2 + i // TOP_K
            return carry

        lax.fori_loop(0, N_TOK // 2, scatter, 0)

    def start_gather(tile, slot):
        base = tile * TM_MOE

        def body(r8, carry):
            r0 = r8 * DMA_BATCH
            toks = [src_sm[base + r0 + i] for i in range(DMA_BATCH)]
            for i, tok in enumerate(toks):
                pltpu.make_async_copy(h2_hbm.at[pl.ds(tok, 1), :], buf.at[slot, pl.ds(r0 + i, 1), :],
                                      sems.at[slot]).start()
            return carry

        lax.fori_loop(0, TM_MOE // DMA_BATCH, body, 0)

    @pl.when(t == 0)
    def _():
        start_gather(0, 0)

    @pl.when(t + 1 < n_used)
    def _():
        start_gather(t + 1, (t + 1) % 2)

    @pl.when(t < n_used)
    def _():
        slot = t % 2
        pltpu.make_async_copy(buf.at[slot], buf.at[slot], sems.at[slot]).wait()
        xs_ref[...] = buf[slot].astype(BF16)

    @pl.when(t >= n_used)
    def _():
        xs_ref[...] = jnp.zeros(xs_ref.shape, BF16)


def _dispatch(dest_flat, n_used, h2):
    return pl.pallas_call(
        _dispatch_kernel,
        grid_spec=pltpu.PrefetchScalarGridSpec(
            num_scalar_prefetch=2,
            grid=(N_MTILES,),
            in_specs=[pl.BlockSpec(memory_space=pl.ANY)],
            out_specs=pl.BlockSpec((TM_MOE, D_MODEL), lambda t, dest, nu: (t, 0)),
            scratch_shapes=[pltpu.SMEM((P_ROWS,), jnp.int32),
                            pltpu.VMEM((2, TM_MOE, D_MODEL), F32),
                            pltpu.SemaphoreType.DMA((2,))]),
        out_shape=jax.ShapeDtypeStruct((P_ROWS, D_MODEL), BF16),
        compiler_params=_params("arbitrary"),
        name="moe_dispatch",
    )(dest_flat, n_used, h2)


def _moe_up_kernel(eid_ref, nused_ref, x_ref, wg_ref, wu_ref, bg_ref, bu_ref, h_ref, wg_sc, wu_sc):
    t = pl.program_id(1)
    live = t < nused_ref[0]

    @pl.when(live)
    def _():
        @pl.when((t == 0) | (eid_ref[t] != eid_ref[jnp.maximum(t - 1, 0)]))
        def _():
            wg_sc[...] = wg_ref[...].astype(BF16)
            wu_sc[...] = wu_ref[...].astype(BF16)

        x = x_ref[...]
        gate = jnp.minimum(_dot(x, wg_sc[...]) + bg_ref[...], SWIGLU_LIMIT)
        up = jnp.clip(_dot(x, wu_sc[...]) + bu_ref[...], -SWIGLU_LIMIT, SWIGLU_LIMIT)
        h_ref[...] = (gate * jax.nn.sigmoid(SWIGLU_ALPHA * gate) * (up + 1.0)).astype(BF16)

    @pl.when(jnp.logical_not(live))
    def _():
        h_ref[...] = jnp.zeros(h_ref.shape, BF16)


def _moe_down_kernel(eid_ref, nused_ref, h_ref, wd_ref, bd_ref, y_ref, wd_sc):
    t = pl.program_id(1)
    live = t < nused_ref[0]

    @pl.when(live)
    def _():
        @pl.when((t == 0) | (eid_ref[t] != eid_ref[jnp.maximum(t - 1, 0)]))
        def _():
            wd_sc[...] = wd_ref[...].astype(BF16)

        y_ref[...] = _dot(h_ref[...], wd_sc[...]) + bd_ref[...]

    @pl.when(jnp.logical_not(live))
    def _():
        y_ref[...] = jnp.zeros(y_ref.shape, F32)


def _row_tile(t, nused_ref):
    return jnp.minimum(t, jnp.maximum(nused_ref[0] - 1, 0))


def _moe_up(tile_eid, n_used, xs, w_gate, w_up, b_gate, b_up):
    w_spec = pl.BlockSpec((None, D_MODEL, TN_MOE), lambda j, t, eid, nu: (eid[t], 0, j))
    b_spec = pl.BlockSpec((None, 1, TN_MOE), lambda j, t, eid, nu: (eid[t], 0, j))
    return pl.pallas_call(
        _moe_up_kernel,
        grid_spec=pltpu.PrefetchScalarGridSpec(
            num_scalar_prefetch=2,
            grid=(D_FF // TN_MOE, N_MTILES),
            in_specs=[pl.BlockSpec((TM_MOE, D_MODEL), lambda j, t, eid, nu: (_row_tile(t, nu), 0)),
                      w_spec, w_spec, b_spec, b_spec],
            out_specs=pl.BlockSpec((TM_MOE, TN_MOE), lambda j, t, eid, nu: (t, j)),
            scratch_shapes=[pltpu.VMEM((D_MODEL, TN_MOE), BF16), pltpu.VMEM((D_MODEL, TN_MOE), BF16)]),
        out_shape=jax.ShapeDtypeStruct((P_ROWS, D_FF), BF16),
        compiler_params=_params("arbitrary", "arbitrary"),
        name="moe_up",
    )(tile_eid, n_used, xs, w_gate, w_up, b_gate, b_up)


def _moe_down(tile_eid, n_used, h, w_down, b_down):
    return pl.pallas_call(
        _moe_down_kernel,
        grid_spec=pltpu.PrefetchScalarGridSpec(
            num_scalar_prefetch=2,
            grid=(D_MODEL // TN_MOE, N_MTILES),
            in_specs=[pl.BlockSpec((TM_MOE, D_FF), lambda j, t, eid, nu: (_row_tile(t, nu), 0)),
                      pl.BlockSpec((None, D_FF, TN_MOE), lambda j, t, eid, nu: (eid[t], 0, j)),
                      pl.BlockSpec((None, 1, TN_MOE), lambda j, t, eid, nu: (eid[t], 0, j))],
            out_specs=pl.BlockSpec((TM_MOE, TN_MOE), lambda j, t, eid, nu: (t, j)),
            scratch_shapes=[pltpu.VMEM((D_FF, TN_MOE), BF16)]),
        out_shape=jax.ShapeDtypeStruct((P_ROWS, D_MODEL), F32),
        compiler_params=_params("arbitrary", "arbitrary"),
        name="moe_down",
    )(tile_eid, n_used, h, w_down, b_down)


def _final_kernel(dest_ref, y_hbm, x1_ref, wt_ref, g2_ref, ln2g_ref, ln2b_ref, o_ref, ybuf, sems):
    i = pl.program_id(0)
    n_tiles = pl.num_programs(0)

    def start_gather(tile, slot):
        base = tile * (TM_FIN * TOP_K)

        def body(r2, carry):
            a0 = base + r2 * DMA_BATCH
            rows = [dest_ref[a0 + i] for i in range(DMA_BATCH)]
            for i, d in enumerate(rows):
                r = r2 * (DMA_BATCH // TOP_K) + i // TOP_K
                pltpu.make_async_copy(y_hbm.at[pl.ds(d, 1), :], ybuf.at[slot, i % TOP_K, pl.ds(r, 1), :],
                                      sems.at[slot]).start()
            return carry

        lax.fori_loop(0, TM_FIN * TOP_K // DMA_BATCH, body, 0)

    @pl.when(i == 0)
    def _():
        start_gather(0, 0)

    @pl.when(i + 1 < n_tiles)
    def _():
        start_gather(i + 1, (i + 1) % 2)

    slot = i % 2
    pltpu.make_async_copy(ybuf.at[slot], ybuf.at[slot], sems.at[slot]).wait()
    wt = wt_ref[...]
    y = wt[:, 0:1] * ybuf[slot, 0]
    for k in range(1, TOP_K):
        y = y + wt[:, k:k + 1] * ybuf[slot, k]
    o_ref[...] = _ln(DN_ALPHA * x1_ref[...] + (1.0 + g2_ref[...]) * y) * ln2g_ref[...] + ln2b_ref[...]


def _final(dest_flat, y, x1, wt, mod3, ln2_g, ln2_b):
    tiles_per_batch = SEQ // TM_FIN
    full = lambda shape: pl.BlockSpec(shape, lambda i, dest: tuple(0 for _ in shape))
    row = lambda width: pl.BlockSpec((TM_FIN, width), lambda i, dest: (i, 0))
    return pl.pallas_call(
        _final_kernel,
        grid_spec=pltpu.PrefetchScalarGridSpec(
            num_scalar_prefetch=1,
            grid=(N_TOK // TM_FIN,),
            in_specs=[pl.BlockSpec(memory_space=pl.ANY), row(D_MODEL), row(LANE),
                      pl.BlockSpec((None, 1, D_MODEL), lambda i, dest: (i // tiles_per_batch, 0, 5)),
                      full(ln2_g.shape), full(ln2_b.shape)],
            out_specs=row(D_MODEL),
            scratch_shapes=[pltpu.VMEM((2, TOP_K, TM_FIN, D_MODEL), F32),
                            pltpu.SemaphoreType.DMA((2,))]),
        out_shape=jax.ShapeDtypeStruct((N_TOK, D_MODEL), F32),
        compiler_params=_params("arbitrary"),
        name="combine_ln2",
    )(dest_flat, y, x1, wt, mod3, ln2_g, ln2_b)


def _route(idx, rank, counts):
    experts = jnp.arange(N_EXPERTS, dtype=jnp.int32)
    padded = ((counts + TM_MOE - 1) // TM_MOE) * TM_MOE
    ends = jnp.sum(jnp.where(experts[None, :] <= experts[:, None], padded[None, :], 0), axis=1)
    starts = ends - padded
    dest = rank
    for e in range(N_EXPERTS):
        dest = dest + jnp.where(idx == e, starts[e], 0)
    dest = dest[:, :TOP_K]
    tile_start = jnp.arange(N_MTILES, dtype=jnp.int32) * TM_MOE
    tile_eid = jnp.minimum(jnp.sum((ends[None, :] <= tile_start[:, None]).astype(jnp.int32), axis=1),
                           N_EXPERTS - 1)
    n_used = (ends[-1] // TM_MOE).astype(jnp.int32).reshape(1)
    return dest.reshape(-1), tile_eid, n_used


def kernel(x, c, w_ada, b_ada, w_in, w_gk, b_gk, gla_norm_g, pe_k, pe_v, w_ck1, b_ck1, w_ck2, b_ck2,
           w_cv1, b_cv1, w_cv2, b_cv2, w_o, ln1_g, ln1_b, w_router, b_router, w_gate, b_gate, w_up, b_up,
           w_down, b_down, ln2_g, ln2_b):
    l = 0
    xf = x.reshape(N_TOK, D_MODEL)
    row2 = lambda a: a.reshape(1, -1)

    c8 = jnp.pad(c, ((0, 8 - BATCH), (0, 0)))
    mod3 = _adaln(c8, w_ada[l], row2(b_ada[l]))[:BATCH].reshape(BATCH, 1, 6 * D_MODEL)

    w = w_in[l]
    glr0 = 3072
    nsa0 = glr0 + GLA_RANK
    ngt0 = nsa0 + 1024 + 6 * 256
    w_in_p = jnp.concatenate(
        [w[:, :glr0], w[:, nsa0:ngt0], w[:, glr0:nsa0], w[:, ngt0:],
         jnp.zeros((D_MODEL, D_IN_PAD - w.shape[1]), F32)], axis=1).astype(BF16)
    proj = _inproj(xf, mod3, w_in_p)

    w_gk_pad = jnp.pad(w_gk[l], ((0, LANE - GLA_RANK), (0, 0)))
    y_gla = _gla(proj, w_gk_pad, row2(b_gk[l]), row2(gla_norm_g[l]))

    cos_t, sin_t = _rope_tables(jnp.arange(SEQ))
    cmp_end = jnp.arange(N_CMP_PAD) * CMP_STRIDE + (CMP_BLOCK - 1)
    cos_c, sin_c = _rope_tables(cmp_end)
    q_r, ks, vs, kw, vw = _prep(proj, cos_t, sin_t)
    kc = _compress(proj, 0, pe_k[l], w_ck1[l], row2(b_ck1[l]), w_ck2[l], row2(b_ck2[l]), cos_c, sin_c, True)
    vc = _compress(proj, 1, pe_v[l], w_cv1[l], row2(b_cv1[l]), w_cv2[l], row2(b_cv2[l]), cos_c, sin_c, False)
    y_nsa = _nsa(q_r, kc, vc, ks, vs, kw, vw, proj)

    wr = jnp.pad(w_router[l], ((0, 0), (0, LANE - N_EXPERTS)))
    wr_hi, wr_lo = _split_bf16(wr)
    br = jnp.concatenate([b_router[l], jnp.full((LANE - N_EXPERTS,), NEG, F32)]).reshape(1, LANE)
    x1, h2, idx128, wt128, rank128, cnt8 = _outproj(y_gla, y_nsa, xf, w_o[l].astype(BF16), mod3,
                                                    row2(ln1_g[l]), row2(ln1_b[l]), wr_hi, wr_lo, br)

    counts = cnt8[0, :N_EXPERTS].astype(jnp.int32)
    dest, tile_eid, n_used = _route(idx128, rank128, counts)
    xs = _dispatch(dest, n_used, h2)
    h = _moe_up(tile_eid, n_used, xs, w_gate[l], w_up[l],
                b_gate[l].reshape(N_EXPERTS, 1, D_FF), b_up[l].reshape(N_EXPERTS, 1, D_FF))
    y = _moe_down(tile_eid, n_used, h, w_down[l], b_down[l].reshape(N_EXPERTS, 1, D_MODEL))

    out = _final(dest, y, x1, wt128, mod3, row2(ln2_g[l]), row2(ln2_b[l]))
    return out.reshape(BATCH, SEQ, D_MODEL)
```

```python
import functools

import numpy as np
import jax
import jax.numpy as jnp
from jax import lax
from jax.experimental import pallas as pl
from jax.experimental.pallas import tpu as pltpu

F32 = jnp.float32
BF16 = jnp.bfloat16

D_MODEL = 2048
BATCH = 2
SEQ = 4096
N_TOK = BATCH * SEQ

GLA_HEADS = 4
GLA_DK = 128
GLA_DV = 256
GLA_RANK = 16
GLA_TAU = 16.0
GLA_CHUNK = 64

NSA_DH = 128
NSA_HEADS = 8
NSA_GROUPS = 2
NSA_HPG = 4
CMP_STRIDE = 16
CMP_BLOCK = 32
CMP_HIDDEN = 256
N_CMP_PAD = SEQ // CMP_STRIDE
SEL_BLOCK = 64
N_BLK = SEQ // SEL_BLOCK
SEL_TOPK = 16
WINDOW = 512
ROPE_DIM = 32
ROPE_THETA = 500000.0
Q_SCALE_LOG2 = NSA_DH ** -0.5 * 1.4426950408889634

N_EXPERTS = 32
TOP_K = 4
D_FF = D_MODEL
SWIGLU_LIMIT = 7.0
SWIGLU_ALPHA = 1.702
DN_ALPHA = 2.0 ** 0.25
LN_EPS = 1e-5

COL_GQ, COL_GK, COL_GV, COL_GR, COL_NQ = 0, 512, 1024, 2048, 3072
COL_KV = 4096
COL_TAIL = 5632
D_IN_PAD = 5760
GATE_LANE0 = GLA_RANK

LANE = 128
NEG = -1e30
VMEM_LIMIT = 56 * 1024 * 1024

TM_IN, TN_IN = 1024, 640
TT_GLA = 512
TR_PREP = 512
TQ_NSA = 256
TK_SEL = 512
WIN_SPAN = WINDOW + TQ_NSA
TM_OUT = 256
TM_MOE, TN_MOE = 256, 1024
P_ROWS = N_TOK * TOP_K + N_EXPERTS * TM_MOE
N_MTILES = P_ROWS // TM_MOE
TM_FIN = 256
DMA_BATCH = 8


def _dot(a, b):
    return jnp.dot(a, b, preferred_element_type=F32)


def _dot_nt(a, b):
    return lax.dot_general(a, b, (((1,), (1,)), ((), ())), preferred_element_type=F32)


def _dot_tn(a, b):
    return lax.dot_general(a, b, (((0,), (0,)), ((), ())), preferred_element_type=F32)


def _ln(x):
    xc = x - jnp.mean(x, -1, keepdims=True)
    return xc * lax.rsqrt(jnp.mean(xc * xc, -1, keepdims=True) + LN_EPS)


def _split_bf16(x):
    hi = x.astype(BF16)
    lo = (x - hi.astype(F32)).astype(BF16)
    return hi, lo


def _params(*sem):
    return pltpu.CompilerParams(dimension_semantics=sem, vmem_limit_bytes=VMEM_LIMIT)


def _adaln_kernel(c_ref, w_ref, b_ref, o_ref):
    c = c_ref[...]
    a = (c * jax.nn.sigmoid(c)).astype(BF16)
    o_ref[...] = _dot(a, w_ref[...].astype(BF16)) + b_ref[...]


def _adaln(c8, w, b):
    n = w.shape[1]
    tn = 1024
    return pl.pallas_call(
        _adaln_kernel,
        grid=(n // tn,),
        in_specs=[pl.BlockSpec((8, D_MODEL), lambda j: (0, 0)),
                  pl.BlockSpec((D_MODEL, tn), lambda j: (0, j)),
                  pl.BlockSpec((1, tn), lambda j: (0, j))],
        out_specs=pl.BlockSpec((8, tn), lambda j: (0, j)),
        out_shape=jax.ShapeDtypeStruct((8, n), F32),
        compiler_params=_params("arbitrary"),
        name="adaln",
    )(c8, w, b)


def _inproj_kernel(x_ref, sh_ref, sc_ref, w_ref, o_ref, h_sc):
    @pl.when(pl.program_id(1) == 0)
    def _():
        h = _ln(x_ref[...]) * (1.0 + sc_ref[...]) + sh_ref[...]
        h_sc[...] = h.astype(BF16)

    o_ref[...] = _dot(h_sc[...], w_ref[...])


def _inproj(xf, mod3, w_in_p):
    tiles_per_batch = SEQ // TM_IN
    return pl.pallas_call(
        _inproj_kernel,
        grid=(N_TOK // TM_IN, D_IN_PAD // TN_IN),
        in_specs=[pl.BlockSpec((TM_IN, D_MODEL), lambda i, j: (i, 0)),
                  pl.BlockSpec((None, 1, D_MODEL), lambda i, j: (i // tiles_per_batch, 0, 0)),
                  pl.BlockSpec((None, 1, D_MODEL), lambda i, j: (i // tiles_per_batch, 0, 1)),
                  pl.BlockSpec((D_MODEL, TN_IN), lambda i, j: (0, j))],
        out_specs=pl.BlockSpec((TM_IN, TN_IN), lambda i, j: (i, j)),
        out_shape=jax.ShapeDtypeStruct((N_TOK, D_IN_PAD), F32),
        scratch_shapes=[pltpu.VMEM((TM_IN, D_MODEL), BF16)],
        compiler_params=_params("parallel", "arbitrary"),
        name="inproj",
    )(xf, mod3, mod3, w_in_p)


GLA_HALVES = (32, 16, 8, 4, 2, 1)
N_LEVELS = len(GLA_HALVES)
ROW_EB = 2 * N_LEVELS
ROW_EL = 2 * N_LEVELS + 1
N_EVIEWS = 2 * N_LEVELS + 2


def _gla_constants():
    c = GLA_CHUNK
    t = np.arange(c)[:, None]
    r = np.arange(c)[None, :]
    mall = np.zeros((N_EVIEWS, c, c), np.float32)
    valid = np.zeros((N_EVIEWS, c, LANE), np.float32)
    masks = np.zeros((N_LEVELS + 1, c, c), np.float32)
    for li, n in enumerate(GLA_HALVES):
        same = (t // (2 * n)) == (r // (2 * n))
        t_up = (t % (2 * n)) >= n
        r_up = (r % (2 * n)) >= n
        mall[2 * li] = same & t_up & r_up & (r <= t)
        mall[2 * li + 1] = same & ~t_up & ~r_up & (r > t)
        valid[2 * li] = np.broadcast_to(t_up, (c, LANE))
        valid[2 * li + 1] = np.broadcast_to(~t_up, (c, LANE))
        masks[li] = same & t_up & ~r_up
    mall[ROW_EB] = r <= t
    mall[ROW_EL] = r > t
    valid[ROW_EB] = 1.0
    valid[ROW_EL] = 1.0
    masks[N_LEVELS] = np.eye(c)
    return (mall.reshape(N_EVIEWS * c, c), valid.reshape(N_EVIEWS * c, LANE), masks)


def _gla_kernel(q_ref, k_ref, v_ref, r_ref, glr_ref, wgk_ref, bgk_ref, g_ref, mall_ref, valid_ref,
                masks_ref, o_ref, st_sc):
    c = GLA_CHUNK

    @pl.when(pl.program_id(1) == 0)
    def _():
        st_sc[...] = jnp.zeros_like(st_sc)

    z = _dot(glr_ref[...].astype(BF16), wgk_ref[...].astype(BF16)) + bgk_ref[...]
    log_a = (jnp.minimum(z, 0.0) - jnp.log1p(jnp.exp(-jnp.abs(z)))) * (1.0 / GLA_TAU)
    mall = mall_ref[...]
    valid = valid_ref[...]
    for ci in range(TT_GLA // c):
        rows = slice(ci * c, (ci + 1) * c)
        la_hi, la_lo = _split_bf16(log_a[rows])
        e_all = jnp.exp(_dot(mall, la_hi) + _dot(mall, la_lo))
        for h in range(GLA_HEADS):
            kcols = slice(h * GLA_DK, (h + 1) * GLA_DK)
            vcols = slice(h * GLA_DV, (h + 1) * GLA_DV)
            e = e_all[:, kcols] * valid
            q = q_ref[rows, kcols] * (GLA_DK ** -0.5)
            k = k_ref[rows, kcols]
            vb = v_ref[rows, vcols].astype(BF16)
            att = masks_ref[N_LEVELS] * _dot_nt(q.astype(BF16), k.astype(BF16))
            for li in range(N_LEVELS):
                eq = e[(2 * li) * c:(2 * li + 1) * c]
                ek = e[(2 * li + 1) * c:(2 * li + 2) * c]
                att = att + masks_ref[li] * _dot_nt((q * eq).astype(BF16), (k * ek).astype(BF16))
            eb = e[ROW_EB * c:(ROW_EB + 1) * c]
            el = e[ROW_EL * c:(ROW_EL + 1) * c]
            st = st_sc[h]
            o = _dot_nt((q * eb).astype(BF16), st.astype(BF16)) + _dot(att.astype(BF16), vb)
            st_sc[h] = st * eb[c - 1:c, :] + _dot_tn(vb, (k * el).astype(BF16))
            o = o * lax.rsqrt(jnp.mean(o * o, -1, keepdims=True) + LN_EPS) * g_ref[...]
            r = r_ref[rows, vcols]
            o_ref[rows, vcols] = (o * (r * jax.nn.sigmoid(r))).astype(BF16)


def _gla(proj, w_gk_pad, b_gk, norm_g):
    nt = SEQ // TT_GLA
    mall, valid, masks = _gla_constants()
    kw = GLA_HEADS * GLA_DK
    vw = GLA_HEADS * GLA_DV
    full = lambda shape: pl.BlockSpec(shape, lambda b, i: tuple(0 for _ in shape))
    cols = lambda width, col0: pl.BlockSpec((TT_GLA, width), lambda b, i: (b * nt + i, col0 // width))
    return pl.pallas_call(
        _gla_kernel,
        grid=(BATCH, nt),
        in_specs=[cols(kw, COL_GQ), cols(kw, COL_GK), cols(vw, COL_GV), cols(vw, COL_GR),
                  cols(LANE, COL_TAIL), full(w_gk_pad.shape), full(b_gk.shape), full(norm_g.shape),
                  full(mall.shape), full(valid.shape), full(masks.shape)],
        out_specs=cols(vw, 0),
        out_shape=jax.ShapeDtypeStruct((N_TOK, vw), BF16),
        scratch_shapes=[pltpu.VMEM((GLA_HEADS, GLA_DV, GLA_DK), F32)],
        compiler_params=_params("parallel", "arbitrary"),
        name="gla",
    )(proj, proj, proj, proj, proj, w_gk_pad, b_gk, norm_g,
      jnp.asarray(mall, BF16), jnp.asarray(valid), jnp.asarray(masks))


def _rope_tables(pos):
    half = ROPE_DIM // 2
    inv_freq = ROPE_THETA ** (-jnp.arange(half, dtype=F32) * (2.0 / ROPE_DIM))
    ang = pos.astype(F32)[:, None] * inv_freq
    cos, sin = jnp.cos(ang), jnp.sin(ang)
    n = pos.shape[0]
    cosf = jnp.concatenate([cos, cos, jnp.ones((n, LANE - ROPE_DIM), F32)], -1)
    sinf = jnp.concatenate([-sin, sin, jnp.zeros((n, LANE - ROPE_DIM), F32)], -1)
    return cosf, sinf


def _rope(x, cosf, sinf):
    lane = lax.broadcasted_iota(jnp.int32, x.shape, 1)
    half = ROPE_DIM // 2
    swapped = jnp.where(lane < half, pltpu.roll(x, LANE - half, 1), pltpu.roll(x, half, 1))
    return x * cosf + swapped * sinf


def _prep_kernel(q_ref, ks_ref, vs_ref, kw_ref, vw_ref, cos_ref, sin_ref,
                 qo_ref, kso_ref, vso_ref, kwo_ref, vwo_ref):
    cosf = cos_ref[...]
    sinf = sin_ref[...]
    for hh in range(NSA_HEADS):
        cols = slice(hh * NSA_DH, (hh + 1) * NSA_DH)
        qo_ref[:, cols] = (_rope(q_ref[:, cols], cosf, sinf) * Q_SCALE_LOG2).astype(BF16)
    for g in range(NSA_GROUPS):
        cols = slice(g * NSA_DH, (g + 1) * NSA_DH)
        kso_ref[g] = _rope(ks_ref[:, cols], cosf, sinf).astype(BF16)
        kwo_ref[g] = _rope(kw_ref[:, cols], cosf, sinf).astype(BF16)
        vso_ref[g] = vs_ref[:, cols].astype(BF16)
        vwo_ref[g] = vw_ref[:, cols].astype(BF16)


def _prep(proj, cosf, sinf):
    nt = SEQ // TR_PREP
    kvw = NSA_GROUPS * NSA_DH
    kv_in = lambda which: pl.BlockSpec((TR_PREP, kvw), lambda b, i: (b * nt + i, COL_KV // kvw + which))
    kv_out = pl.BlockSpec((None, NSA_GROUPS, TR_PREP, NSA_DH), lambda b, i: (b, 0, i, 0))
    kv_shape = jax.ShapeDtypeStruct((BATCH, NSA_GROUPS, SEQ, NSA_DH), BF16)
    qw = NSA_HEADS * NSA_DH
    return pl.pallas_call(
        _prep_kernel,
        grid=(BATCH, nt),
        in_specs=[pl.BlockSpec((TR_PREP, qw), lambda b, i: (b * nt + i, COL_NQ // qw)),
                  kv_in(2), kv_in(3), kv_in(4), kv_in(5),
                  pl.BlockSpec((TR_PREP, LANE), lambda b, i: (i, 0)),
                  pl.BlockSpec((TR_PREP, LANE), lambda b, i: (i, 0))],
        out_specs=[pl.BlockSpec((TR_PREP, qw), lambda b, i: (b * nt + i, 0)),
                   kv_out, kv_out, kv_out, kv_out],
        out_shape=[jax.ShapeDtypeStruct((N_TOK, qw), BF16), kv_shape, kv_shape, kv_shape, kv_shape],
        compiler_params=_params("parallel", "parallel"),
        name="nsa_prep",
    )(proj, proj, proj, proj, proj, cosf, sinf)


def _compress_kernel(a_ref, pe_ref, w1_ref, b1_ref, w2_ref, b2_ref, cos_ref, sin_ref, o_ref, bot_sc,
                     *, rope):
    n = N_CMP_PAD
    top = jnp.zeros((n, CMP_HIDDEN), F32)
    bot = jnp.zeros((n, CMP_HIDDEN), F32)
    for p in range(CMP_STRIDE):
        ap = a_ref[pl.ds(p, n, stride=CMP_STRIDE), :]
        w_top = w1_ref[p * NSA_DH:(p + 1) * NSA_DH, :].astype(BF16)
        w_bot = w1_ref[(CMP_STRIDE + p) * NSA_DH:(CMP_STRIDE + p + 1) * NSA_DH, :].astype(BF16)
        top = top + _dot((ap + pe_ref[p:p + 1, :]).astype(BF16), w_top)
        bot = bot + _dot((ap + pe_ref[CMP_STRIDE + p:CMP_STRIDE + p + 1, :]).astype(BF16), w_bot)
    bot_sc[0:n, :] = bot
    bot_sc[n:n + 8, :] = jnp.zeros((8, CMP_HIDDEN), F32)
    h = top + bot_sc[1:n + 1, :] + b1_ref[...]
    h = h * jax.nn.sigmoid(h)
    out = _dot(h.astype(BF16), w2_ref[...].astype(BF16)) + b2_ref[...]
    if rope:
        out = _rope(out, cos_ref[...], sin_ref[...])
    row = lax.broadcasted_iota(jnp.int32, out.shape, 0)
    o_ref[...] = jnp.where(row < n - 1, out, 0.0).astype(BF16)


def _compress(proj, which, pe, w1, b1, w2, b2, cosf, sinf, rope):
    col0 = (COL_KV + which * NSA_GROUPS * NSA_DH) // NSA_DH
    full = lambda shape: pl.BlockSpec(shape, lambda b, g: tuple(0 for _ in shape))
    return pl.pallas_call(
        functools.partial(_compress_kernel, rope=rope),
        grid=(BATCH, NSA_GROUPS),
        in_specs=[pl.BlockSpec((SEQ, NSA_DH), lambda b, g: (b, col0 + g)),
                  full(pe.shape), full(w1.shape), full(b1.shape), full(w2.shape), full(b2.shape),
                  full(cosf.shape), full(sinf.shape)],
        out_specs=pl.BlockSpec((None, None, N_CMP_PAD, NSA_DH), lambda b, g: (b, g, 0, 0)),
        out_shape=jax.ShapeDtypeStruct((BATCH, NSA_GROUPS, N_CMP_PAD, NSA_DH), BF16),
        scratch_shapes=[pltpu.VMEM((N_CMP_PAD + 8, CMP_HIDDEN), F32)],
        compiler_params=_params("parallel", "parallel"),
        name="nsa_compress",
    )(proj, pe, w1, b1, w2, b2, cosf, sinf)


def _nsa_constants():
    c_start = np.arange(N_CMP_PAD) * CMP_STRIDE
    b_start = np.arange(N_BLK) * SEL_BLOCK
    overlap_t = ((c_start[None, :] < b_start[:, None] + SEL_BLOCK)
                 & (c_start[None, :] + CMP_BLOCK > b_start[:, None])).astype(np.float32)
    overlap_t[:, N_CMP_PAD - 1] = 0.0
    expand = (np.arange(SEQ)[None, :] // SEL_BLOCK == np.arange(LANE)[:, None]).astype(np.float32)
    return overlap_t, expand


def _tile_heads(a):
    return jnp.concatenate([a] * NSA_HPG, axis=0)


def _nsa_kernel(q_ref, kc_ref, vc_ref, ks_ref, vs_ref, kw_ref, vw_ref, gate_ref, ovt_ref, expand_ref,
                o_ref, score_sc, m_sc, acc_sc):
    tq = TQ_NSA
    g = pl.program_id(1)
    q0 = pl.program_id(2) * tq
    qs = jnp.concatenate([q_ref[:, hh * NSA_DH:(hh + 1) * NSA_DH] for hh in range(NSA_HPG)], axis=0)
    pos_t = q0 + lax.broadcasted_iota(jnp.int32, (tq, 1), 0)

    def with_ones(v):
        return jnp.concatenate([v, jnp.ones(v.shape, BF16)], axis=1)

    s = _dot_nt(qs, kc_ref[...])
    cmp_end = lax.broadcasted_iota(jnp.int32, (tq, N_CMP_PAD), 1) * CMP_STRIDE + (CMP_BLOCK - 1)
    s = s + _tile_heads(jnp.where(cmp_end <= pos_t, 0.0, NEG))
    m = jnp.max(s, -1, keepdims=True)
    e = jnp.where(s > 0.5 * NEG, jnp.exp2(s - m), 0.0)
    p = e / jnp.maximum(jnp.sum(e, -1, keepdims=True), 1e-30)
    o_cmp = _dot(p.astype(BF16), vc_ref[...])
    p_sum = p[0:tq] + p[tq:2 * tq] + p[2 * tq:3 * tq] + p[3 * tq:4 * tq]
    ps_hi, ps_lo = _split_bf16(p_sum)
    ovt = ovt_ref[...]
    p_blk_t = _dot_nt(ovt, ps_hi) + _dot_nt(ovt, ps_lo)

    jj = lax.broadcasted_iota(jnp.int32, (N_BLK, tq), 0)
    cur = (q0 + lax.broadcasted_iota(jnp.int32, (N_BLK, tq), 1)) // SEL_BLOCK
    forced = (jj == 0) | (jj == cur) | (jj == cur - 1)
    allowed = jj <= cur
    score = jnp.where(forced, 3.0e38, jnp.where(allowed, p_blk_t, -1.0))
    score_sc[...] = score

    def rank_body(i, rank):
        row = score_sc[pl.ds(i, 1), :]
        first = jnp.where(jj > i, 1.0, 0.0)
        return rank + jnp.where(row > score, 1.0, jnp.where(row == score, first, 0.0))

    n_live = (q0 + tq - 1) // SEL_BLOCK + 1
    rank = lax.fori_loop(0, n_live, rank_body, jnp.zeros((N_BLK, tq), F32))
    sel_t = jnp.where(allowed, jnp.where(rank < SEL_TOPK, 1.0, 0.0), 0.0)
    sel = jnp.concatenate([sel_t, jnp.zeros((LANE - N_BLK, tq), F32)], axis=0).T.astype(BF16)

    m_sc[...] = jnp.full(m_sc.shape, NEG, F32)
    acc_sc[...] = jnp.zeros(acc_sc.shape, F32)

    def sel_body(kt, carry):
        k0 = pl.multiple_of(kt * TK_SEL, TK_SEL)
        s = _dot_nt(qs, ks_ref[pl.ds(k0, TK_SEL), :])
        sel_x = _dot(sel, expand_ref[:, pl.ds(k0, TK_SEL)])
        kpos = k0 + lax.broadcasted_iota(jnp.int32, (tq, TK_SEL), 1)
        bias = jnp.where(kpos <= pos_t, jnp.where(sel_x > 0.5, 0.0, NEG), NEG)
        s = s + _tile_heads(bias)
        m_prev = m_sc[...]
        m_new = jnp.maximum(m_prev, jnp.max(s, -1, keepdims=True))
        alpha = jnp.exp2(m_prev - m_new)
        p = jnp.exp2(s - jnp.concatenate([m_new] * (TK_SEL // LANE), axis=1)).astype(BF16)
        acc_sc[...] = (jnp.concatenate([alpha, alpha], axis=1) * acc_sc[...]
                       + _dot(p, with_ones(vs_ref[pl.ds(k0, TK_SEL), :])))
        m_sc[...] = m_new
        return carry

    lax.fori_loop(0, (q0 + tq - 1) // TK_SEL + 1, sel_body, 0)
    o_slc = acc_sc[:, 0:NSA_DH] / acc_sc[:, NSA_DH:2 * NSA_DH]

    start = pl.multiple_of(jnp.maximum(q0 - WINDOW, 0), LANE)
    s = _dot_nt(qs, kw_ref[pl.ds(start, WIN_SPAN), :])
    dist = pos_t - (start + lax.broadcasted_iota(jnp.int32, (tq, WIN_SPAN), 1))
    bias = jnp.where(dist >= 0, jnp.where(dist < WINDOW, 0.0, NEG), NEG)
    s = s + _tile_heads(bias)
    p = jnp.exp2(s - jnp.max(s, -1, keepdims=True)).astype(BF16)
    ow = _dot(p, with_ones(vw_ref[pl.ds(start, WIN_SPAN), :]))
    o_win = ow[:, 0:NSA_DH] / ow[:, NSA_DH:2 * NSA_DH]

    gates = jax.nn.sigmoid(gate_ref[...])

    def gate(hh, branch):
        lane0 = GATE_LANE0 + hh * 3 + branch
        lane1 = lane0 + NSA_HPG * 3
        return jnp.where(g == 0, gates[:, lane0:lane0 + 1], gates[:, lane1:lane1 + 1])

    for hh in range(NSA_HPG):
        r = slice(hh * tq, (hh + 1) * tq)
        o = gate(hh, 0) * o_cmp[r] + gate(hh, 1) * o_slc[r] + gate(hh, 2) * o_win[r]
        o_ref[:, hh * NSA_DH:(hh + 1) * NSA_DH] = o.astype(BF16)


def _nsa(q_r, kc, vc, ks, vs, kw, vw, proj):
    nq = SEQ // TQ_NSA
    gw = NSA_HPG * NSA_DH
    overlap_t, expand = _nsa_constants()
    rows = NSA_HPG * TQ_NSA
    cmp_spec = pl.BlockSpec((None, None, N_CMP_PAD, NSA_DH), lambda b, g, i: (b, g, 0, 0))
    kv_spec = pl.BlockSpec((None, None, SEQ, NSA_DH), lambda b, g, i: (b, g, 0, 0))
    return pl.pallas_call(
        _nsa_kernel,
        grid=(BATCH, NSA_GROUPS, nq),
        in_specs=[pl.BlockSpec((TQ_NSA, gw), lambda b, g, i: (b * nq + i, g)),
                  cmp_spec, cmp_spec, kv_spec, kv_spec, kv_spec, kv_spec,
                  pl.BlockSpec((TQ_NSA, LANE), lambda b, g, i: (b * nq + i, COL_TAIL // LANE)),
                  pl.BlockSpec(overlap_t.shape, lambda b, g, i: (0, 0)),
                  pl.BlockSpec(expand.shape, lambda b, g, i: (0, 0))],
        out_specs=pl.BlockSpec((TQ_NSA, gw), lambda b, g, i: (b * nq + i, g)),
        out_shape=jax.ShapeDtypeStruct((N_TOK, NSA_GROUPS * gw), BF16),
        scratch_shapes=[pltpu.VMEM((N_BLK, TQ_NSA), F32),
                        pltpu.VMEM((rows, LANE), F32),
                        pltpu.VMEM((rows, 2 * NSA_DH), F32)],
        compiler_params=_params("parallel", "parallel", "arbitrary"),
        name="nsa_attn",
    )(q_r, kc, vc, ks, vs, kw, vw, proj, jnp.asarray(overlap_t, BF16), jnp.asarray(expand, BF16))


def _outproj_kernel(yg_ref, yn_ref, x_ref, wo_ref, g1_ref, sc2_ref, sh2_ref, ln1g_ref, ln1b_ref,
                    wr_hi_ref, wr_lo_ref, br_ref, ltri_ref, x1_ref, h2_ref, idx_ref, wt_ref, rank_ref,
                    cnt_ref, base_sc):
    half = D_MODEL // 2

    @pl.when(pl.program_id(0) == 0)
    def _():
        base_sc[...] = jnp.zeros_like(base_sc)

    mix = _dot(yg_ref[...], wo_ref[0:half, :]) + _dot(yn_ref[...], wo_ref[half:D_MODEL, :])
    x1 = _ln(DN_ALPHA * x_ref[...] + (1.0 + g1_ref[...]) * mix) * ln1g_ref[...] + ln1b_ref[...]
    x1_ref[...] = x1
    h2 = _ln(x1) * (1.0 + sc2_ref[...]) + sh2_ref[...]
    h_hi, h_lo = _split_bf16(h2)
    h2_ref[...] = h2
    wr_hi = wr_hi_ref[...]
    logits = _dot(h_hi, wr_hi) + _dot(h_lo, wr_hi) + _dot(h_hi, wr_lo_ref[...]) + br_ref[...]

    lane = lax.broadcasted_iota(jnp.int32, logits.shape, 1)
    lane_f = lane.astype(F32)
    vals = logits
    idx_out = jnp.zeros(logits.shape, jnp.int32)
    exp_out = jnp.zeros(logits.shape, F32)
    denom = jnp.zeros((logits.shape[0], 1), F32)
    m0 = None
    onehots = []
    for k in range(TOP_K):
        mk = jnp.max(vals, -1, keepdims=True)
        ik = jnp.min(jnp.where(vals == mk, lane_f, float(LANE)), -1, keepdims=True)
        if k == 0:
            m0 = mk
        ek = jnp.exp(mk - m0)
        denom = denom + ek
        idx_out = jnp.where(lane == k, ik.astype(jnp.int32), idx_out)
        exp_out = jnp.where(lane == k, ek, exp_out)
        hit = lane_f == ik
        onehots.append(jnp.where(hit, 1.0, 0.0))
        vals = jnp.where(hit, -3.0e38, vals)
    idx_ref[...] = idx_out
    wt_ref[...] = exp_out / denom

    cnt = onehots[0] + onehots[1] + onehots[2] + onehots[3]
    base = base_sc[0:1, :]
    before = _dot(ltri_ref[...], cnt.astype(BF16)) + base
    rank_out = jnp.zeros(logits.shape, jnp.int32)
    for k in range(TOP_K):
        rk = jnp.sum(onehots[k] * before, -1, keepdims=True)
        rank_out = jnp.where(lane == k, rk.astype(jnp.int32), rank_out)
    rank_ref[...] = rank_out
    base_sc[...] = jnp.broadcast_to(base + jnp.sum(cnt, 0, keepdims=True), base_sc.shape)
    cnt_ref[...] = base_sc[...]


def _outproj(y_gla, y_nsa, xf, w_o, mod3, ln1_g, ln1_b, wr_hi, wr_lo, br):
    tiles_per_batch = SEQ // TM_OUT
    half = D_MODEL // 2
    mod_spec = lambda chunk: pl.BlockSpec((None, 1, D_MODEL), lambda i: (i // tiles_per_batch, 0, chunk))
    full = lambda shape: pl.BlockSpec(shape, lambda i: tuple(0 for _ in shape))
    row = lambda width: pl.BlockSpec((TM_OUT, width), lambda i: (i, 0))
    ltri = jnp.asarray(np.tril(np.ones((TM_OUT, TM_OUT), np.float32), -1), BF16)
    return pl.pallas_call(
        _outproj_kernel,
        grid=(N_TOK // TM_OUT,),
        in_specs=[row(half), row(half), row(D_MODEL), full(w_o.shape),
                  mod_spec(2), mod_spec(4), mod_spec(3),
                  full(ln1_g.shape), full(ln1_b.shape), full(wr_hi.shape), full(wr_lo.shape), full(br.shape),
                  full(ltri.shape)],
        out_specs=[row(D_MODEL), row(D_MODEL), row(LANE), row(LANE), row(LANE), full((8, LANE))],
        out_shape=[jax.ShapeDtypeStruct((N_TOK, D_MODEL), F32),
                   jax.ShapeDtypeStruct((N_TOK, D_MODEL), F32),
                   jax.ShapeDtypeStruct((N_TOK, LANE), jnp.int32),
                   jax.ShapeDtypeStruct((N_TOK, LANE), F32),
                   jax.ShapeDtypeStruct((N_TOK, LANE), jnp.int32),
                   jax.ShapeDtypeStruct((8, LANE), F32)],
        scratch_shapes=[pltpu.VMEM((8, LANE), F32)],
        compiler_params=_params("arbitrary"),
        name="outproj_router",
    )(y_gla, y_nsa, xf, w_o, mod3, mod3, mod3, ln1_g, ln1_b, wr_hi, wr_lo, br, ltri)


def _dispatch_kernel(dest_ref, nused_ref, h2_hbm, xs_ref, src_sm, buf, sems):
    t = pl.program_id(0)
    n_used = nused_ref[0]

    @pl.when(t == 0)
    def _():
        def fill(r, carry):
            src_sm[r] = 0
            return carry

        lax.fori_loop(0, P_ROWS, fill, 0, unroll=8)

        def scatter(tok2, carry):
            a0 = tok2 * (2 * TOP_K)
            rows = [dest_ref[a0 + i] for i in range(2 * TOP_K)]
            for i, row in enumerate(rows):
                src_sm[row] = tok2 * 2 + i // TOP_K
            return carry

        lax.fori_loop(0, N_TOK // 2, scatter, 0)

    def start_gather(tile, slot):
        base = tile * TM_MOE

        def body(r8, carry):
            r0 = r8 * DMA_BATCH
            toks = [src_sm[base + r0 + i] for i in range(DMA_BATCH)]
            for i, tok in enumerate(toks):
                pltpu.make_async_copy(h2_hbm.at[pl.ds(tok, 1), :], buf.at[slot, pl.ds(r0 + i, 1), :],
                                      sems.at[slot]).start()
            return carry

        lax.fori_loop(0, TM_MOE // DMA_BATCH, body, 0)

    @pl.when(t == 0)
    def _():
        start_gather(0, 0)

    @pl.when(t + 1 < n_used)
    def _():
        start_gather(t + 1, (t + 1) % 2)

    @pl.when(t < n_used)
    def _():
        slot = t % 2
        pltpu.make_async_copy(buf.at[slot], buf.at[slot], sems.at[slot]).wait()
        xs_ref[...] = buf[slot].astype(BF16)

    @pl.when(t >= n_used)
    def _():
        xs_ref[...] = jnp.zeros(xs_ref.shape, BF16)


def _dispatch(dest_flat, n_used, h2):
    return pl.pallas_call(
        _dispatch_kernel,
        grid_spec=pltpu.PrefetchScalarGridSpec(
            num_scalar_prefetch=2,
            grid=(N_MTILES,),
            in_specs=[pl.BlockSpec(memory_space=pl.ANY)],
            out_specs=pl.BlockSpec((TM_MOE, D_MODEL), lambda t, dest, nu: (t, 0)),
            scratch_shapes=[pltpu.SMEM((P_ROWS,), jnp.int32),
                            pltpu.VMEM((2, TM_MOE, D_MODEL), F32),
                            pltpu.SemaphoreType.DMA((2,))]),
        out_shape=jax.ShapeDtypeStruct((P_ROWS, D_MODEL), BF16),
        compiler_params=_params("arbitrary"),
        name="moe_dispatch",
    )(dest_flat, n_used, h2)


def _moe_up_kernel(eid_ref, nused_ref, x_ref, wg_ref, wu_ref, bg_ref, bu_ref, h_ref, wg_sc, wu_sc):
    t = pl.program_id(1)
    live = t < nused_ref[0]

    @pl.when(live)
    def _():
        @pl.when((t == 0) | (eid_ref[t] != eid_ref[jnp.maximum(t - 1, 0)]))
        def _():
            wg_sc[...] = wg_ref[...].astype(BF16)
            wu_sc[...] = wu_ref[...].astype(BF16)

        x = x_ref[...]
        gate = jnp.minimum(_dot(x, wg_sc[...]) + bg_ref[...], SWIGLU_LIMIT)
        up = jnp.clip(_dot(x, wu_sc[...]) + bu_ref[...], -SWIGLU_LIMIT, SWIGLU_LIMIT)
        h_ref[...] = (gate * jax.nn.sigmoid(SWIGLU_ALPHA * gate) * (up + 1.0)).astype(BF16)

    @pl.when(jnp.logical_not(live))
    def _():
        h_ref[...] = jnp.zeros(h_ref.shape, BF16)


def _moe_down_kernel(eid_ref, nused_ref, h_ref, wd_ref, bd_ref, y_ref, wd_sc):
    t = pl.program_id(1)
    live = t < nused_ref[0]

    @pl.when(live)
    def _():
        @pl.when((t == 0) | (eid_ref[t] != eid_ref[jnp.maximum(t - 1, 0)]))
        def _():
            wd_sc[...] = wd_ref[...].astype(BF16)

        y_ref[...] = _dot(h_ref[...], wd_sc[...]) + bd_ref[...]

    @pl.when(jnp.logical_not(live))
    def _():
        y_ref[...] = jnp.zeros(y_ref.shape, F32)


def _row_tile(t, nused_ref):
    return jnp.minimum(t, jnp.maximum(nused_ref[0] - 1, 0))


def _moe_up(tile_eid, n_used, xs, w_gate, w_up, b_gate, b_up):
    w_spec = pl.BlockSpec((None, D_MODEL, TN_MOE), lambda j, t, eid, nu: (eid[t], 0, j))
    b_spec = pl.BlockSpec((None, 1, TN_MOE), lambda j, t, eid, nu: (eid[t], 0, j))
    return pl.pallas_call(
        _moe_up_kernel,
        grid_spec=pltpu.PrefetchScalarGridSpec(
            num_scalar_prefetch=2,
            grid=(D_FF // TN_MOE, N_MTILES),
            in_specs=[pl.BlockSpec((TM_MOE, D_MODEL), lambda j, t, eid, nu: (_row_tile(t, nu), 0)),
                      w_spec, w_spec, b_spec, b_spec],
            out_specs=pl.BlockSpec((TM_MOE, TN_MOE), lambda j, t, eid, nu: (t, j)),
            scratch_shapes=[pltpu.VMEM((D_MODEL, TN_MOE), BF16), pltpu.VMEM((D_MODEL, TN_MOE), BF16)]),
        out_shape=jax.ShapeDtypeStruct((P_ROWS, D_FF), BF16),
        compiler_params=_params("arbitrary", "arbitrary"),
        name="moe_up",
    )(tile_eid, n_used, xs, w_gate, w_up, b_gate, b_up)


def _moe_down(tile_eid, n_used, h, w_down, b_down):
    return pl.pallas_call(
        _moe_down_kernel,
        grid_spec=pltpu.PrefetchScalarGridSpec(
            num_scalar_prefetch=2,
            grid=(D_MODEL // TN_MOE, N_MTILES),
            in_specs=[pl.BlockSpec((TM_MOE, D_FF), lambda j, t, eid, nu: (_row_tile(t, nu), 0)),
                      pl.BlockSpec((None, D_FF, TN_MOE), lambda j, t, eid, nu: (eid[t], 0, j)),
                      pl.BlockSpec((None, 1, TN_MOE), lambda j, t, eid, nu: (eid[t], 0, j))],
            out_specs=pl.BlockSpec((TM_MOE, TN_MOE), lambda j, t, eid, nu: (t, j)),
            scratch_shapes=[pltpu.VMEM((D_FF, TN_MOE), BF16)]),
        out_shape=jax.ShapeDtypeStruct((P_ROWS, D_MODEL), F32),
        compiler_params=_params("arbitrary", "arbitrary"),
        name="moe_down",
    )(tile_eid, n_used, h, w_down, b_down)


def _final_kernel(dest_ref, y_hbm, x1_ref, wt_ref, g2_ref, ln2g_ref, ln2b_ref, o_ref, ybuf, sems):
    i = pl.program_id(0)
    n_tiles = pl.num_programs(0)

    def start_gather(tile, slot):
        base = tile * (TM_FIN * TOP_K)

        def body(r2, carry):
            a0 = base + r2 * DMA_BATCH
            rows = [dest_ref[a0 + i] for i in range(DMA_BATCH)]
            for i, d in enumerate(rows):
                r = r2 * (DMA_BATCH // TOP_K) + i // TOP_K
                pltpu.make_async_copy(y_hbm.at[pl.ds(d, 1), :], ybuf.at[slot, i % TOP_K, pl.ds(r, 1), :],
                                      sems.at[slot]).start()
            return carry

        lax.fori_loop(0, TM_FIN * TOP_K // DMA_BATCH, body, 0)

    @pl.when(i == 0)
    def _():
        start_gather(0, 0)

    @pl.when(i + 1 < n_tiles)
    def _():
        start_gather(i + 1, (i + 1) % 2)

    slot = i % 2
    pltpu.make_async_copy(ybuf.at[slot], ybuf.at[slot], sems.at[slot]).wait()
    wt = wt_ref[...]
    y = wt[:, 0:1] * ybuf[slot, 0]
    for k in range(1, TOP_K):
        y = y + wt[:, k:k + 1] * ybuf[slot, k]
    o_ref[...] = _ln(DN_ALPHA * x1_ref[...] + (1.0 + g2_ref[...]) * y) * ln2g_ref[...] + ln2b_ref[...]


def _final(dest_flat, y, x1, wt, mod3, ln2_g, ln2_b):
    tiles_per_batch = SEQ // TM_FIN
    full = lambda shape: pl.BlockSpec(shape, lambda i, dest: tuple(0 for _ in shape))
    row = lambda width: pl.BlockSpec((TM_FIN, width), lambda i, dest: (i, 0))
    return pl.pallas_call(
        _final_kernel,
        grid_spec=pltpu.PrefetchScalarGridSpec(
            num_scalar_prefetch=1,
            grid=(N_TOK // TM_FIN,),
            in_specs=[pl.BlockSpec(memory_space=pl.ANY), row(D_MODEL), row(LANE),
                      pl.BlockSpec((None, 1, D_MODEL), lambda i, dest: (i // tiles_per_batch, 0, 5)),
                      full(ln2_g.shape), full(ln2_b.shape)],
            out_specs=row(D_MODEL),
            scratch_shapes=[pltpu.VMEM((2, TOP_K, TM_FIN, D_MODEL), F32),
                            pltpu.SemaphoreType.DMA((2,))]),
        out_shape=jax.ShapeDtypeStruct((N_TOK, D_MODEL), F32),
        compiler_params=_params("arbitrary"),
        name="combine_ln2",
    )(dest_flat, y, x1, wt, mod3, ln2_g, ln2_b)


def _route(idx, rank, counts):
    experts = jnp.arange(N_EXPERTS, dtype=jnp.int32)
    padded = ((counts + TM_MOE - 1) // TM_MOE) * TM_MOE
    ends = jnp.sum(jnp.where(experts[None, :] <= experts[:, None], padded[None, :], 0), axis=1)
    starts = ends - padded
    dest = rank
    for e in range(N_EXPERTS):
        dest = dest + jnp.where(idx == e, starts[e], 0)
    dest = dest[:, :TOP_K]
    tile_start = jnp.arange(N_MTILES, dtype=jnp.int32) * TM_MOE
    tile_eid = jnp.minimum(jnp.sum((ends[None, :] <= tile_start[:, None]).astype(jnp.int32), axis=1),
                           N_EXPERTS - 1)
    n_used = (ends[-1] // TM_MOE).astype(jnp.int32).reshape(1)
    return dest.reshape(-1), tile_eid, n_used


def kernel(x, c, w_ada, b_ada, w_in, w_gk, b_gk, gla_norm_g, pe_k, pe_v, w_ck1, b_ck1, w_ck2, b_ck2,
           w_cv1, b_cv1, w_cv2, b_cv2, w_o, ln1_g, ln1_b, w_router, b_router, w_gate, b_gate, w_up, b_up,
           w_down, b_down, ln2_g, ln2_b):
    l = 0
    xf = x.reshape(N_TOK, D_MODEL)
    row2 = lambda a: a.reshape(1, -1)

    c8 = jnp.pad(c, ((0, 8 - BATCH), (0, 0)))
    mod3 = _adaln(c8, w_ada[l], row2(b_ada[l]))[:BATCH].reshape(BATCH, 1, 6 * D_MODEL)

    w = w_in[l]
    glr0 = 3072
    nsa0 = glr0 + GLA_RANK
    ngt0 = nsa0 + 1024 + 6 * 256
    w_in_p = jnp.concatenate(
        [w[:, :glr0], w[:, nsa0:ngt0], w[:, glr0:nsa0], w[:, ngt0:],
         jnp.zeros((D_MODEL, D_IN_PAD - w.shape[1]), F32)], axis=1).astype(BF16)
    proj = _inproj(xf, mod3, w_in_p)

    w_gk_pad = jnp.pad(w_gk[l], ((0, LANE - GLA_RANK), (0, 0)))
    y_gla = _gla(proj, w_gk_pad, row2(b_gk[l]), row2(gla_norm_g[l]))

    cos_t, sin_t = _rope_tables(jnp.arange(SEQ))
    cmp_end = jnp.arange(N_CMP_PAD) * CMP_STRIDE + (CMP_BLOCK - 1)
    cos_c, sin_c = _rope_tables(cmp_end)
    q_r, ks, vs, kw, vw = _prep(proj, cos_t, sin_t)
    kc = _compress(proj, 0, pe_k[l], w_ck1[l], row2(b_ck1[l]), w_ck2[l], row2(b_ck2[l]), cos_c, sin_c, True)
    vc = _compress(proj, 1, pe_v[l], w_cv1[l], row2(b_cv1[l]), w_cv2[l], row2(b_cv2[l]), cos_c, sin_c, False)
    y_nsa = _nsa(q_r, kc, vc, ks, vs, kw, vw, proj)

    wr = jnp.pad(w_router[l], ((0, 0), (0, LANE - N_EXPERTS)))
    wr_hi, wr_lo = _split_bf16(wr)
    br = jnp.concatenate([b_router[l], jnp.full((LANE - N_EXPERTS,), NEG, F32)]).reshape(1, LANE)
    x1, h2, idx128, wt128, rank128, cnt8 = _outproj(y_gla, y_nsa, xf, w_o[l].astype(BF16), mod3,
                                                    row2(ln1_g[l]), row2(ln1_b[l]), wr_hi, wr_lo, br)

    counts = cnt8[0, :N_EXPERTS].astype(jnp.int32)
    dest, tile_eid, n_used = _route(idx128, rank128, counts)
    xs = _dispatch(dest, n_used, h2)
    h = _moe_up(tile_eid, n_used, xs, w_gate[l], w_up[l],
                b_gate[l].reshape(N_EXPERTS, 1, D_FF), b_up[l].reshape(N_EXPERTS, 1, D_FF))
    y = _moe_down(tile_eid, n_used, h, w_down[l], b_down[l].reshape(N_EXPERTS, 1, D_MODEL))

    out = _final(dest, y, x1, wt128, mod3, row2(ln2_g[l]), row2(ln2_b[l]))
    return out.reshape(BATCH, SEQ, D_MODEL)
```

```python
import functools

import numpy as np
import jax
import jax.numpy as jnp
from jax import lax
from jax.experimental import pallas as pl
from jax.experimental.pallas import tpu as pltpu

F32 = jnp.float32
BF16 = jnp.bfloat16

D_MODEL = 2048
BATCH = 2
SEQ = 4096
N_TOK = BATCH * SEQ

GLA_HEADS = 4
GLA_DK = 128
GLA_DV = 256
GLA_RANK = 16
GLA_TAU = 16.0
GLA_CHUNK = 64

NSA_DH = 128
NSA_HEADS = 8
NSA_GROUPS = 2
NSA_HPG = 4
CMP_STRIDE = 16
CMP_BLOCK = 32
CMP_HIDDEN = 256
N_CMP_PAD = SEQ // CMP_STRIDE
SEL_BLOCK = 64
N_BLK = SEQ // SEL_BLOCK
SEL_TOPK = 16
WINDOW = 512
ROPE_DIM = 32
ROPE_THETA = 500000.0
Q_SCALE_LOG2 = NSA_DH ** -0.5 * 1.4426950408889634

N_EXPERTS = 32
TOP_K = 4
D_FF = D_MODEL
SWIGLU_LIMIT = 7.0
SWIGLU_ALPHA = 1.702
DN_ALPHA = 2.0 ** 0.25
LN_EPS = 1e-5

COL_GQ, COL_GK, COL_GV, COL_GR, COL_NQ = 0, 512, 1024, 2048, 3072
COL_KV = 4096
COL_TAIL = 5632
D_IN_PAD = 5760
GATE_LANE0 = GLA_RANK

LANE = 128
NEG = -1e30
VMEM_LIMIT = 56 * 1024 * 1024

TM_IN, TN_IN = 1024, 640
TT_GLA = 512
TR_PREP = 512
TQ_NSA = 256
TK_SEL = 512
WIN_SPAN = WINDOW + TQ_NSA
TM_OUT = 256
TM_MOE, TN_MOE = 256, 1024
P_ROWS = N_TOK * TOP_K + N_EXPERTS * TM_MOE
N_MTILES = P_ROWS // TM_MOE
TM_FIN = 256
DMA_BATCH = 8


def _dot(a, b):
    return jnp.dot(a, b, preferred_element_type=F32)


def _dot_nt(a, b):
    return lax.dot_general(a, b, (((1,), (1,)), ((), ())), preferred_element_type=F32)


def _dot_tn(a, b):
    return lax.dot_general(a, b, (((0,), (0,)), ((), ())), preferred_element_type=F32)


def _ln(x):
    xc = x - jnp.mean(x, -1, keepdims=True)
    return xc * lax.rsqrt(jnp.mean(xc * xc, -1, keepdims=True) + LN_EPS)


def _split_bf16(x):
    hi = x.astype(BF16)
    lo = (x - hi.astype(F32)).astype(BF16)
    return hi, lo


def _params(*sem):
    return pltpu.CompilerParams(dimension_semantics=sem, vmem_limit_bytes=VMEM_LIMIT)


def _adaln_kernel(c_ref, w_ref, b_ref, o_ref):
    c = c_ref[...]
    a = (c * jax.nn.sigmoid(c)).astype(BF16)
    o_ref[...] = _dot(a, w_ref[...].astype(BF16)) + b_ref[...]


def _adaln(c8, w, b):
    n = w.shape[1]
    tn = 1024
    return pl.pallas_call(
        _adaln_kernel,
        grid=(n // tn,),
        in_specs=[pl.BlockSpec((8, D_MODEL), lambda j: (0, 0)),
                  pl.BlockSpec((D_MODEL, tn), lambda j: (0, j)),
                  pl.BlockSpec((1, tn), lambda j: (0, j))],
        out_specs=pl.BlockSpec((8, tn), lambda j: (0, j)),
        out_shape=jax.ShapeDtypeStruct((8, n), F32),
        compiler_params=_params("arbitrary"),
        name="adaln",
    )(c8, w, b)


def _inproj_kernel(x_ref, sh_ref, sc_ref, w_ref, o_ref, h_sc):
    @pl.when(pl.program_id(1) == 0)
    def _():
        h = _ln(x_ref[...]) * (1.0 + sc_ref[...]) + sh_ref[...]
        h_sc[...] = h.astype(BF16)

    o_ref[...] = _dot(h_sc[...], w_ref[...])


def _inproj(xf, mod3, w_in_p):
    tiles_per_batch = SEQ // TM_IN
    return pl.pallas_call(
        _inproj_kernel,
        grid=(N_TOK // TM_IN, D_IN_PAD // TN_IN),
        in_specs=[pl.BlockSpec((TM_IN, D_MODEL), lambda i, j: (i, 0)),
                  pl.BlockSpec((None, 1, D_MODEL), lambda i, j: (i // tiles_per_batch, 0, 0)),
                  pl.BlockSpec((None, 1, D_MODEL), lambda i, j: (i // tiles_per_batch, 0, 1)),
                  pl.BlockSpec((D_MODEL, TN_IN), lambda i, j: (0, j))],
        out_specs=pl.BlockSpec((TM_IN, TN_IN), lambda i, j: (i, j)),
        out_shape=jax.ShapeDtypeStruct((N_TOK, D_IN_PAD), F32),
        scratch_shapes=[pltpu.VMEM((TM_IN, D_MODEL), BF16)],
        compiler_params=_params("parallel", "arbitrary"),
        name="inproj",
    )(xf, mod3, mod3, w_in_p)


GLA_HALVES = (32, 16, 8, 4, 2, 1)
N_LEVELS = len(GLA_HALVES)
ROW_EB = 2 * N_LEVELS
ROW_EL = 2 * N_LEVELS + 1
N_EVIEWS = 2 * N_LEVELS + 2


def _gla_constants():
    c = GLA_CHUNK
    t = np.arange(c)[:, None]
    r = np.arange(c)[None, :]
    mall = np.zeros((N_EVIEWS, c, c), np.float32)
    valid = np.zeros((N_EVIEWS, c, LANE), np.float32)
    masks = np.zeros((N_LEVELS + 1, c, c), np.float32)
    for li, n in enumerate(GLA_HALVES):
        same = (t // (2 * n)) == (r // (2 * n))
        t_up = (t % (2 * n)) >= n
        r_up = (r % (2 * n)) >= n
        mall[2 * li] = same & t_up & r_up & (r <= t)
        mall[2 * li + 1] = same & ~t_up & ~r_up & (r > t)
        valid[2 * li] = np.broadcast_to(t_up, (c, LANE))
        valid[2 * li + 1] = np.broadcast_to(~t_up, (c, LANE))
        masks[li] = same & t_up & ~r_up
    mall[ROW_EB] = r <= t
    mall[ROW_EL] = r > t
    valid[ROW_EB] = 1.0
    valid[ROW_EL] = 1.0
    masks[N_LEVELS] = np.eye(c)
    return (mall.reshape(N_EVIEWS * c, c), valid.reshape(N_EVIEWS * c, LANE), masks)


def _gla_kernel(q_ref, k_ref, v_ref, r_ref, glr_ref, wgk_ref, bgk_ref, g_ref, mall_ref, valid_ref,
                masks_ref, o_ref, st_sc):
    c = GLA_CHUNK

    @pl.when(pl.program_id(1) == 0)
    def _():
        st_sc[...] = jnp.zeros_like(st_sc)

    z = _dot(glr_ref[...].astype(BF16), wgk_ref[...].astype(BF16)) + bgk_ref[...]
    log_a = (jnp.minimum(z, 0.0) - jnp.log1p(jnp.exp(-jnp.abs(z)))) * (1.0 / GLA_TAU)
    mall = mall_ref[...]
    valid = valid_ref[...]
    for ci in range(TT_GLA // c):
        rows = slice(ci * c, (ci + 1) * c)
        la_hi, la_lo = _split_bf16(log_a[rows])
        e_all = jnp.exp(_dot(mall, la_hi) + _dot(mall, la_lo))
        for h in range(GLA_HEADS):
            kcols = slice(h * GLA_DK, (h + 1) * GLA_DK)
            vcols = slice(h * GLA_DV, (h + 1) * GLA_DV)
            e = e_all[:, kcols] * valid
            q = q_ref[rows, kcols] * (GLA_DK ** -0.5)
            k = k_ref[rows, kcols]
            vb = v_ref[rows, vcols].astype(BF16)
            att = masks_ref[N_LEVELS] * _dot_nt(q.astype(BF16), k.astype(BF16))
            for li in range(N_LEVELS):
                eq = e[(2 * li) * c:(2 * li + 1) * c]
                ek = e[(2 * li + 1) * c:(2 * li + 2) * c]
                att = att + masks_ref[li] * _dot_nt((q * eq).astype(BF16), (k * ek).astype(BF16))
            eb = e[ROW_EB * c:(ROW_EB + 1) * c]
            el = e[ROW_EL * c:(ROW_EL + 1) * c]
            st = st_sc[h]
            o = _dot_nt((q * eb).astype(BF16), st.astype(BF16)) + _dot(att.astype(BF16), vb)
            st_sc[h] = st * eb[c - 1:c, :] + _dot_tn(vb, (k * el).astype(BF16))
            o = o * lax.rsqrt(jnp.mean(o * o, -1, keepdims=True) + LN_EPS) * g_ref[...]
            r = r_ref[rows, vcols]
            o_ref[rows, vcols] = (o * (r * jax.nn.sigmoid(r))).astype(BF16)


def _gla(proj, w_gk_pad, b_gk, norm_g):
    nt = SEQ // TT_GLA
    mall, valid, masks = _gla_constants()
    kw = GLA_HEADS * GLA_DK
    vw = GLA_HEADS * GLA_DV
    full = lambda shape: pl.BlockSpec(shape, lambda b, i: tuple(0 for _ in shape))
    cols = lambda width, col0: pl.BlockSpec((TT_GLA, width), lambda b, i: (b * nt + i, col0 // width))
    return pl.pallas_call(
        _gla_kernel,
        grid=(BATCH, nt),
        in_specs=[cols(kw, COL_GQ), cols(kw, COL_GK), cols(vw, COL_GV), cols(vw, COL_GR),
                  cols(LANE, COL_TAIL), full(w_gk_pad.shape), full(b_gk.shape), full(norm_g.shape),
                  full(mall.shape), full(valid.shape), full(masks.shape)],
        out_specs=cols(vw, 0),
        out_shape=jax.ShapeDtypeStruct((N_TOK, vw), BF16),
        scratch_shapes=[pltpu.VMEM((GLA_HEADS, GLA_DV, GLA_DK), F32)],
        compiler_params=_params("parallel", "arbitrary"),
        name="gla",
    )(proj, proj, proj, proj, proj, w_gk_pad, b_gk, norm_g,
      jnp.asarray(mall, BF16), jnp.asarray(valid), jnp.asarray(masks))


def _rope_tables(pos):
    half = ROPE_DIM // 2
    inv_freq = ROPE_THETA ** (-jnp.arange(half, dtype=F32) * (2.0 / ROPE_DIM))
    ang = pos.astype(F32)[:, None] * inv_freq
    cos, sin = jnp.cos(ang), jnp.sin(ang)
    n = pos.shape[0]
    cosf = jnp.concatenate([cos, cos, jnp.ones((n, LANE - ROPE_DIM), F32)], -1)
    sinf = jnp.concatenate([-sin, sin, jnp.zeros((n, LANE - ROPE_DIM), F32)], -1)
    return cosf, sinf


def _rope(x, cosf, sinf):
    lane = lax.broadcasted_iota(jnp.int32, x.shape, 1)
    half = ROPE_DIM // 2
    swapped = jnp.where(lane < half, pltpu.roll(x, LANE - half, 1), pltpu.roll(x, half, 1))
    return x * cosf + swapped * sinf


def _prep_kernel(q_ref, ks_ref, vs_ref, kw_ref, vw_ref, cos_ref, sin_ref,
                 qo_ref, kso_ref, vso_ref, kwo_ref, vwo_ref):
    cosf = cos_ref[...]
    sinf = sin_ref[...]
    for hh in range(NSA_HEADS):
        cols = slice(hh * NSA_DH, (hh + 1) * NSA_DH)
        qo_ref[:, cols] = (_rope(q_ref[:, cols], cosf, sinf) * Q_SCALE_LOG2).astype(BF16)
    for g in range(NSA_GROUPS):
        cols = slice(g * NSA_DH, (g + 1) * NSA_DH)
        kso_ref[g] = _rope(ks_ref[:, cols], cosf, sinf).astype(BF16)
        kwo_ref[g] = _rope(kw_ref[:, cols], cosf, sinf).astype(BF16)
        vso_ref[g] = vs_ref[:, cols].astype(BF16)
        vwo_ref[g] = vw_ref[:, cols].astype(BF16)


def _prep(proj, cosf, sinf):
    nt = SEQ // TR_PREP
    kvw = NSA_GROUPS * NSA_DH
    kv_in = lambda which: pl.BlockSpec((TR_PREP, kvw), lambda b, i: (b * nt + i, COL_KV // kvw + which))
    kv_out = pl.BlockSpec((None, NSA_GROUPS, TR_PREP, NSA_DH), lambda b, i: (b, 0, i, 0))
    kv_shape = jax.ShapeDtypeStruct((BATCH, NSA_GROUPS, SEQ, NSA_DH), BF16)
    qw = NSA_HEADS * NSA_DH
    return pl.pallas_call(
        _prep_kernel,
        grid=(BATCH, nt),
        in_specs=[pl.BlockSpec((TR_PREP, qw), lambda b, i: (b * nt + i, COL_NQ // qw)),
                  kv_in(2), kv_in(3), kv_in(4), kv_in(5),
                  pl.BlockSpec((TR_PREP, LANE), lambda b, i: (i, 0)),
                  pl.BlockSpec((TR_PREP, LANE), lambda b, i: (i, 0))],
        out_specs=[pl.BlockSpec((TR_PREP, qw), lambda b, i: (b * nt + i, 0)),
                   kv_out, kv_out, kv_out, kv_out],
        out_shape=[jax.ShapeDtypeStruct((N_TOK, qw), BF16), kv_shape, kv_shape, kv_shape, kv_shape],
        compiler_params=_params("parallel", "parallel"),
        name="nsa_prep",
    )(proj, proj, proj, proj, proj, cosf, sinf)


def _compress_kernel(a_ref, pe_ref, w1_ref, b1_ref, w2_ref, b2_ref, cos_ref, sin_ref, o_ref, bot_sc,
                     *, rope):
    n = N_CMP_PAD
    top = jnp.zeros((n, CMP_HIDDEN), F32)
    bot = jnp.zeros((n, CMP_HIDDEN), F32)
    for p in range(CMP_STRIDE):
        ap = a_ref[pl.ds(p, n, stride=CMP_STRIDE), :]
        w_top = w1_ref[p * NSA_DH:(p + 1) * NSA_DH, :].astype(BF16)
        w_bot = w1_ref[(CMP_STRIDE + p) * NSA_DH:(CMP_STRIDE + p + 1) * NSA_DH, :].astype(BF16)
        top = top + _dot((ap + pe_ref[p:p + 1, :]).astype(BF16), w_top)
        bot = bot + _dot((ap + pe_ref[CMP_STRIDE + p:CMP_STRIDE + p + 1, :]).astype(BF16), w_bot)
    bot_sc[0:n, :] = bot
    bot_sc[n:n + 8, :] = jnp.zeros((8, CMP_HIDDEN), F32)
    h = top + bot_sc[1:n + 1, :] + b1_ref[...]
    h = h * jax.nn.sigmoid(h)
    out = _dot(h.astype(BF16), w2_ref[...].astype(BF16)) + b2_ref[...]
    if rope:
        out = _rope(out, cos_ref[...], sin_ref[...])
    row = lax.broadcasted_iota(jnp.int32, out.shape, 0)
    o_ref[...] = jnp.where(row < n - 1, out, 0.0).astype(BF16)


def _compress(proj, which, pe, w1, b1, w2, b2, cosf, sinf, rope):
    col0 = (COL_KV + which * NSA_GROUPS * NSA_DH) // NSA_DH
    full = lambda shape: pl.BlockSpec(shape, lambda b, g: tuple(0 for _ in shape))
    return pl.pallas_call(
        functools.partial(_compress_kernel, rope=rope),
        grid=(BATCH, NSA_GROUPS),
        in_specs=[pl.BlockSpec((SEQ, NSA_DH), lambda b, g: (b, col0 + g)),
                  full(pe.shape), full(w1.shape), full(b1.shape), full(w2.shape), full(b2.shape),
                  full(cosf.shape), full(sinf.shape)],
        out_specs=pl.BlockSpec((None, None, N_CMP_PAD, NSA_DH), lambda b, g: (b, g, 0, 0)),
        out_shape=jax.ShapeDtypeStruct((BATCH, NSA_GROUPS, N_CMP_PAD, NSA_DH), BF16),
        scratch_shapes=[pltpu.VMEM((N_CMP_PAD + 8, CMP_HIDDEN), F32)],
        compiler_params=_params("parallel", "parallel"),
        name="nsa_compress",
    )(proj, pe, w1, b1, w2, b2, cosf, sinf)


def _nsa_constants():
    c_start = np.arange(N_CMP_PAD) * CMP_STRIDE
    b_start = np.arange(N_BLK) * SEL_BLOCK
    overlap_t = ((c_start[None, :] < b_start[:, None] + SEL_BLOCK)
                 & (c_start[None, :] + CMP_BLOCK > b_start[:, None])).astype(np.float32)
    overlap_t[:, N_CMP_PAD - 1] = 0.0
    expand = (np.arange(SEQ)[None, :] // SEL_BLOCK == np.arange(LANE)[:, None]).astype(np.float32)
    return overlap_t, expand


def _tile_heads(a):
    return jnp.concatenate([a] * NSA_HPG, axis=0)


def _nsa_kernel(q_ref, kc_ref, vc_ref, ks_ref, vs_ref, kw_ref, vw_ref, gate_ref, ovt_ref, expand_ref,
                o_ref, score_sc, m_sc, acc_sc):
    tq = TQ_NSA
    g = pl.program_id(1)
    q0 = pl.program_id(2) * tq
    qs = jnp.concatenate([q_ref[:, hh * NSA_DH:(hh + 1) * NSA_DH] for hh in range(NSA_HPG)], axis=0)
    pos_t = q0 + lax.broadcasted_iota(jnp.int32, (tq, 1), 0)

    def with_ones(v):
        return jnp.concatenate([v, jnp.ones(v.shape, BF16)], axis=1)

    s = _dot_nt(qs, kc_ref[...])
    cmp_end = lax.broadcasted_iota(jnp.int32, (tq, N_CMP_PAD), 1) * CMP_STRIDE + (CMP_BLOCK - 1)
    s = s + _tile_heads(jnp.where(cmp_end <= pos_t, 0.0, NEG))
    m = jnp.max(s, -1, keepdims=True)
    e = jnp.where(s > 0.5 * NEG, jnp.exp2(s - m), 0.0)
    p = e / jnp.maximum(jnp.sum(e, -1, keepdims=True), 1e-30)
    o_cmp = _dot(p.astype(BF16), vc_ref[...])
    p_sum = p[0:tq] + p[tq:2 * tq] + p[2 * tq:3 * tq] + p[3 * tq:4 * tq]
    ps_hi, ps_lo = _split_bf16(p_sum)
    ovt = ovt_ref[...]
    p_blk_t = _dot_nt(ovt, ps_hi) + _dot_nt(ovt, ps_lo)

    jj = lax.broadcasted_iota(jnp.int32, (N_BLK, tq), 0)
    cur = (q0 + lax.broadcasted_iota(jnp.int32, (N_BLK, tq), 1)) // SEL_BLOCK
    forced = (jj == 0) | (jj == cur) | (jj == cur - 1)
    allowed = jj <= cur
    score = jnp.where(forced, 3.0e38, jnp.where(allowed, p_blk_t, -1.0))
    score_sc[...] = score

    def rank_body(i, rank):
        row = score_sc[pl.ds(i, 1), :]
        first = jnp.where(jj > i, 1.0, 0.0)
        return rank + jnp.where(row > score, 1.0, jnp.where(row == score, first, 0.0))

    n_live = (q0 + tq - 1) // SEL_BLOCK + 1
    rank = lax.fori_loop(0, n_live, rank_body, jnp.zeros((N_BLK, tq), F32))
    sel_t = jnp.where(allowed, jnp.where(rank < SEL_TOPK, 1.0, 0.0), 0.0)
    sel = jnp.concatenate([sel_t, jnp.zeros((LANE - N_BLK, tq), F32)], axis=0).T.astype(BF16)

    m_sc[...] = jnp.full(m_sc.shape, NEG, F32)
    acc_sc[...] = jnp.zeros(acc_sc.shape, F32)

    def sel_body(kt, carry):
        k0 = pl.multiple_of(kt * TK_SEL, TK_SEL)
        s = _dot_nt(qs, ks_ref[pl.ds(k0, TK_SEL), :])
        sel_x = _dot(sel, expand_ref[:, pl.ds(k0, TK_SEL)])
        kpos = k0 + lax.broadcasted_iota(jnp.int32, (tq, TK_SEL), 1)
        bias = jnp.where(kpos <= pos_t, jnp.where(sel_x > 0.5, 0.0, NEG), NEG)
        s = s + _tile_heads(bias)
        m_prev = m_sc[...]
        m_new = jnp.maximum(m_prev, jnp.max(s, -1, keepdims=True))
        alpha = jnp.exp2(m_prev - m_new)
        p = jnp.exp2(s - jnp.concatenate([m_new] * (TK_SEL // LANE), axis=1)).astype(BF16)
        acc_sc[...] = (jnp.concatenate([alpha, alpha], axis=1) * acc_sc[...]
                       + _dot(p, with_ones(vs_ref[pl.ds(k0, TK_SEL), :])))
        m_sc[...] = m_new
        return carry

    lax.fori_loop(0, (q0 + tq - 1) // TK_SEL + 1, sel_body, 0)
    o_slc = acc_sc[:, 0:NSA_DH] / acc_sc[:, NSA_DH:2 * NSA_DH]

    start = pl.multiple_of(jnp.maximum(q0 - WINDOW, 0), LANE)
    s = _dot_nt(qs, kw_ref[pl.ds(start, WIN_SPAN), :])
    dist = pos_t - (start + lax.broadcasted_iota(jnp.int32, (tq, WIN_SPAN), 1))
    bias = jnp.where(dist >= 0, jnp.where(dist < WINDOW, 0.0, NEG), NEG)
    s = s + _tile_heads(bias)
    p = jnp.exp2(s - jnp.max(s, -1, keepdims=True)).astype(BF16)
    ow = _dot(p, with_ones(vw_ref[pl.ds(start, WIN_SPAN), :]))
    o_win = ow[:, 0:NSA_DH] / ow[:, NSA_DH:2 * NSA_DH]

    gates = jax.nn.sigmoid(gate_ref[...])

    def gate(hh, branch):
        lane0 = GATE_LANE0 + hh * 3 + branch
        lane1 = lane0 + NSA_HPG * 3
        return jnp.where(g == 0, gates[:, lane0:lane0 + 1], gates[:, lane1:lane1 + 1])

    for hh in range(NSA_HPG):
        r = slice(hh * tq, (hh + 1) * tq)
        o = gate(hh, 0) * o_cmp[r] + gate(hh, 1) * o_slc[r] + gate(hh, 2) * o_win[r]
        o_ref[:, hh * NSA_DH:(hh + 1) * NSA_DH] = o.astype(BF16)


def _nsa(q_r, kc, vc, ks, vs, kw, vw, proj):
    nq = SEQ // TQ_NSA
    gw = NSA_HPG * NSA_DH
    overlap_t, expand = _nsa_constants()
    rows = NSA_HPG * TQ_NSA
    cmp_spec = pl.BlockSpec((None, None, N_CMP_PAD, NSA_DH), lambda b, g, i: (b, g, 0, 0))
    kv_spec = pl.BlockSpec((None, None, SEQ, NSA_DH), lambda b, g, i: (b, g, 0, 0))
    return pl.pallas_call(
        _nsa_kernel,
        grid=(BATCH, NSA_GROUPS, nq),
        in_specs=[pl.BlockSpec((TQ_NSA, gw), lambda b, g, i: (b * nq + i, g)),
                  cmp_spec, cmp_spec, kv_spec, kv_spec, kv_spec, kv_spec,
                  pl.BlockSpec((TQ_NSA, LANE), lambda b, g, i: (b * nq + i, COL_TAIL // LANE)),
                  pl.BlockSpec(overlap_t.shape, lambda b, g, i: (0, 0)),
                  pl.BlockSpec(expand.shape, lambda b, g, i: (0, 0))],
        out_specs=pl.BlockSpec((TQ_NSA, gw), lambda b, g, i: (b * nq + i, g)),
        out_shape=jax.ShapeDtypeStruct((N_TOK, NSA_GROUPS * gw), BF16),
        scratch_shapes=[pltpu.VMEM((N_BLK, TQ_NSA), F32),
                        pltpu.VMEM((rows, LANE), F32),
                        pltpu.VMEM((rows, 2 * NSA_DH), F32)],
        compiler_params=_params("parallel", "parallel", "arbitrary"),
        name="nsa_attn",
    )(q_r, kc, vc, ks, vs, kw, vw, proj, jnp.asarray(overlap_t, BF16), jnp.asarray(expand, BF16))


def _outproj_kernel(yg_ref, yn_ref, x_ref, wo_ref, g1_ref, sc2_ref, sh2_ref, ln1g_ref, ln1b_ref,
                    wr_hi_ref, wr_lo_ref, br_ref, ltri_ref, x1_ref, h2_ref, idx_ref, wt_ref, rank_ref,
                    cnt_ref, base_sc):
    half = D_MODEL // 2

    @pl.when(pl.program_id(0) == 0)
    def _():
        base_sc[...] = jnp.zeros_like(base_sc)

    mix = _dot(yg_ref[...], wo_ref[0:half, :]) + _dot(yn_ref[...], wo_ref[half:D_MODEL, :])
    x1 = _ln(DN_ALPHA * x_ref[...] + (1.0 + g1_ref[...]) * mix) * ln1g_ref[...] + ln1b_ref[...]
    x1_ref[...] = x1
    h2 = _ln(x1) * (1.0 + sc2_ref[...]) + sh2_ref[...]
    h_hi, h_lo = _split_bf16(h2)
    h2_ref[...] = h2
    wr_hi = wr_hi_ref[...]
    logits = _dot(h_hi, wr_hi) + _dot(h_lo, wr_hi) + _dot(h_hi, wr_lo_ref[...]) + br_ref[...]

    lane = lax.broadcasted_iota(jnp.int32, logits.shape, 1)
    lane_f = lane.astype(F32)
    vals = logits
    idx_out = jnp.zeros(logits.shape, jnp.int32)
    exp_out = jnp.zeros(logits.shape, F32)
    denom = jnp.zeros((logits.shape[0], 1), F32)
    m0 = None
    onehots = []
    for k in range(TOP_K):
        mk = jnp.max(vals, -1, keepdims=True)
        ik = jnp.min(jnp.where(vals == mk, lane_f, float(LANE)), -1, keepdims=True)
        if k == 0:
            m0 = mk
        ek = jnp.exp(mk - m0)
        denom = denom + ek
        idx_out = jnp.where(lane == k, ik.astype(jnp.int32), idx_out)
        exp_out = jnp.where(lane == k, ek, exp_out)
        hit = lane_f == ik
        onehots.append(jnp.where(hit, 1.0, 0.0))
        vals = jnp.where(hit, -3.0e38, vals)
    idx_ref[...] = idx_out
    wt_ref[...] = exp_out / denom

    cnt = onehots[0] + onehots[1] + onehots[2] + onehots[3]
    base = base_sc[0:1, :]
    before = _dot(ltri_ref[...], cnt.astype(BF16)) + base
    rank_out = jnp.zeros(logits.shape, jnp.int32)
    for k in range(TOP_K):
        rk = jnp.sum(onehots[k] * before, -1, keepdims=True)
        rank_out = jnp.where(lane == k, rk.astype(jnp.int32), rank_out)
    rank_ref[...] = rank_out
    base_sc[...] = jnp.broadcast_to(base + jnp.sum(cnt, 0, keepdims=True), base_sc.shape)
    cnt_ref[...] = base_sc[...]


def _outproj(y_gla, y_nsa, xf, w_o, mod3, ln1_g, ln1_b, wr_hi, wr_lo, br):
    tiles_per_batch = SEQ // TM_OUT
    half = D_MODEL // 2
    mod_spec = lambda chunk: pl.BlockSpec((None, 1, D_MODEL), lambda i: (i // tiles_per_batch, 0, chunk))
    full = lambda shape: pl.BlockSpec(shape, lambda i: tuple(0 for _ in shape))
    row = lambda width: pl.BlockSpec((TM_OUT, width), lambda i: (i, 0))
    ltri = jnp.asarray(np.tril(np.ones((TM_OUT, TM_OUT), np.float32), -1), BF16)
    return pl.pallas_call(
        _outproj_kernel,
        grid=(N_TOK // TM_OUT,),
        in_specs=[row(half), row(half), row(D_MODEL), full(w_o.shape),
                  mod_spec(2), mod_spec(4), mod_spec(3),
                  full(ln1_g.shape), full(ln1_b.shape), full(wr_hi.shape), full(wr_lo.shape), full(br.shape),
                  full(ltri.shape)],
        out_specs=[row(D_MODEL), row(D_MODEL), row(LANE), row(LANE), row(LANE), full((8, LANE))],
        out_shape=[jax.ShapeDtypeStruct((N_TOK, D_MODEL), F32),
                   jax.ShapeDtypeStruct((N_TOK, D_MODEL), F32),
                   jax.ShapeDtypeStruct((N_TOK, LANE), jnp.int32),
                   jax.ShapeDtypeStruct((N_TOK, LANE), F32),
                   jax.ShapeDtypeStruct((N_TOK, LANE), jnp.int32),
                   jax.ShapeDtypeStruct((8, LANE), F32)],
        scratch_shapes=[pltpu.VMEM((8, LANE), F32)],
        compiler_params=_params("arbitrary"),
        name="outproj_router",
    )(y_gla, y_nsa, xf, w_o, mod3, mod3, mod3, ln1_g, ln1_b, wr_hi, wr_lo, br, ltri)


def _dispatch_kernel(dest_ref, nused_ref, h2_hbm, xs_ref, src_sm, buf, sems):
    t = pl.program_id(0)
    n_used = nused_ref[0]

    @pl.when(t == 0)
    def _():
        def fill(r, carry):
            src_sm[r] = 0
            return carry

        lax.fori_loop(0, P_ROWS, fill, 0, unroll=8)

        def scatter(tok2, carry):
            a0 = tok2 * (2 * TOP_K)
            rows = [dest_ref[a0 + i] for i in range(2 * TOP_K)]
            for i, row in enumerate(rows):
                src_sm[row] = tok2 * 2 + i // TOP_K
            return carry

        lax.fori_loop(0, N_TOK // 2, scatter, 0)

    def start_gather(tile, slot):
        base = tile * TM_MOE

        def body(r8, carry):
            r0 = r8 * DMA_BATCH
            toks = [src_sm[base + r0 + i] for i in range(DMA_BATCH)]
            for i, tok in enumerate(toks):
                pltpu.make_async_copy(h2_hbm.at[pl.ds(tok, 1), :], buf.at[slot, pl.ds(r0 + i, 1), :],
                                      sems.at[slot]).start(priority=i % 2)
            return carry

        lax.fori_loop(0, TM_MOE // DMA_BATCH, body, 0)

    @pl.when(t == 0)
    def _():
        start_gather(0, 0)

    @pl.when(t + 1 < n_used)
    def _():
        start_gather(t + 1, (t + 1) % 2)

    @pl.when(t < n_used)
    def _():
        slot = t % 2
        pltpu.make_async_copy(buf.at[slot], buf.at[slot], sems.at[slot]).wait()
        xs_ref[...] = buf[slot].astype(BF16)

    @pl.when(t >= n_used)
    def _():
        xs_ref[...] = jnp.zeros(xs_ref.shape, BF16)


def _dispatch(dest_flat, n_used, h2):
    return pl.pallas_call(
        _dispatch_kernel,
        grid_spec=pltpu.PrefetchScalarGridSpec(
            num_scalar_prefetch=2,
            grid=(N_MTILES,),
            in_specs=[pl.BlockSpec(memory_space=pl.ANY)],
            out_specs=pl.BlockSpec((TM_MOE, D_MODEL), lambda t, dest, nu: (t, 0)),
            scratch_shapes=[pltpu.SMEM((P_ROWS,), jnp.int32),
                            pltpu.VMEM((2, TM_MOE, D_MODEL), F32),
                            pltpu.SemaphoreType.DMA((2,))]),
        out_shape=jax.ShapeDtypeStruct((P_ROWS, D_MODEL), BF16),
        compiler_params=_params("arbitrary"),
        name="moe_dispatch",
    )(dest_flat, n_used, h2)


class _ExpertWeights:
    def __init__(self, eid_ref, n_used, w_hbms, bufs, sems, cnt_sm):
        self.eid_ref, self.n_used = eid_ref, n_used
        self.w_hbms, self.bufs, self.sems, self.cnt_sm = w_hbms, bufs, sems, cnt_sm

    def _copies(self, expert, j, slot):
        cols = pl.ds(pl.multiple_of(j * TN_MOE, TN_MOE), TN_MOE)
        return [pltpu.make_async_copy(w.at[expert, :, cols], buf.at[slot], self.sems.at[i, slot])
                for i, (w, buf) in enumerate(zip(self.w_hbms, self.bufs))]

    def prologue(self):
        self.cnt_sm[0] = 0
        for cp in self._copies(self.eid_ref[0], 0, 0):
            cp.start()

    def acquire(self, j, t, n_passes):
        slot = self.cnt_sm[0] % 2
        for cp in self._copies(self.eid_ref[t], j, slot):
            cp.wait()
        expert = self.eid_ref[t]
        nxt = lax.while_loop(
            lambda u: (u < self.n_used) & (self.eid_ref[jnp.minimum(u, N_MTILES - 1)] == expert),
            lambda u: u + 1, t + 1)
        in_pass = nxt < self.n_used

        @pl.when(in_pass | (j + 1 < n_passes))
        def _():
            for cp in self._copies(self.eid_ref[jnp.where(in_pass, nxt, 0)], jnp.where(in_pass, j, j + 1),
                                   1 - slot):
                cp.start()

        self.cnt_sm[0] = self.cnt_sm[0] + 1
        return slot


def _moe_up_kernel(eid_ref, nused_ref, x_ref, wg_hbm, wu_hbm, bg_ref, bu_ref, h_ref,
                   wg_buf, wu_buf, wg_sc, wu_sc, sems, cnt_sm):
    j = pl.program_id(0)
    t = pl.program_id(1)
    n_used = nused_ref[0]
    weights = _ExpertWeights(eid_ref, n_used, (wg_hbm, wu_hbm), (wg_buf, wu_buf), sems, cnt_sm)

    @pl.when((j == 0) & (t == 0))
    def _():
        weights.prologue()

    @pl.when(t < n_used)
    def _():
        @pl.when((t == 0) | (eid_ref[t] != eid_ref[jnp.maximum(t - 1, 0)]))
        def _():
            slot = weights.acquire(j, t, pl.num_programs(0))
            wg_sc[...] = wg_buf[slot].astype(BF16)
            wu_sc[...] = wu_buf[slot].astype(BF16)

        x = x_ref[...]
        gate = jnp.minimum(_dot(x, wg_sc[...]) + bg_ref[...], SWIGLU_LIMIT)
        up = jnp.clip(_dot(x, wu_sc[...]) + bu_ref[...], -SWIGLU_LIMIT, SWIGLU_LIMIT)
        h_ref[...] = (gate * jax.nn.sigmoid(SWIGLU_ALPHA * gate) * (up + 1.0)).astype(BF16)

    @pl.when(t >= n_used)
    def _():
        h_ref[...] = jnp.zeros(h_ref.shape, BF16)


def _moe_down_kernel(eid_ref, nused_ref, h_ref, wd_hbm, bd_ref, y_ref, wd_buf, wd_sc, sems, cnt_sm):
    j = pl.program_id(0)
    t = pl.program_id(1)
    n_used = nused_ref[0]
    weights = _ExpertWeights(eid_ref, n_used, (wd_hbm,), (wd_buf,), sems, cnt_sm)

    @pl.when((j == 0) & (t == 0))
    def _():
        weights.prologue()

    @pl.when(t < n_used)
    def _():
        @pl.when((t == 0) | (eid_ref[t] != eid_ref[jnp.maximum(t - 1, 0)]))
        def _():
            slot = weights.acquire(j, t, pl.num_programs(0))
            wd_sc[...] = wd_buf[slot].astype(BF16)

        y_ref[...] = _dot(h_ref[...], wd_sc[...]) + bd_ref[...]

    @pl.when(t >= n_used)
    def _():
        y_ref[...] = jnp.zeros(y_ref.shape, F32)


def _row_tile(t, nused_ref):
    return jnp.minimum(t, jnp.maximum(nused_ref[0] - 1, 0))


def _moe_call(body, name, n_weights, k_dim, out_dim, out_dtype, tile_eid, n_used, rows, weights, biases):
    any_spec = pl.BlockSpec(memory_space=pl.ANY)
    b_spec = pl.BlockSpec((None, 1, TN_MOE), lambda j, t, eid, nu: (eid[t], 0, j))
    return pl.pallas_call(
        body,
        grid_spec=pltpu.PrefetchScalarGridSpec(
            num_scalar_prefetch=2,
            grid=(out_dim // TN_MOE, N_MTILES),
            in_specs=[pl.BlockSpec((TM_MOE, k_dim), lambda j, t, eid, nu: (_row_tile(t, nu), 0))]
            + [any_spec] * n_weights + [b_spec] * n_weights,
            out_specs=pl.BlockSpec((TM_MOE, TN_MOE), lambda j, t, eid, nu: (t, j)),
            scratch_shapes=[pltpu.VMEM((2, k_dim, TN_MOE), F32)] * n_weights
            + [pltpu.VMEM((k_dim, TN_MOE), BF16)] * n_weights
            + [pltpu.SemaphoreType.DMA((n_weights, 2)), pltpu.SMEM((1,), jnp.int32)]),
        out_shape=jax.ShapeDtypeStruct((P_ROWS, out_dim), out_dtype),
        compiler_params=_params("arbitrary", "arbitrary"),
        name=name,
    )(tile_eid, n_used, rows, *weights, *biases)


def _moe_up(tile_eid, n_used, xs, w_gate, w_up, b_gate, b_up):
    return _moe_call(_moe_up_kernel, "moe_up", 2, D_MODEL, D_FF, BF16, tile_eid, n_used, xs,
                     (w_gate, w_up), (b_gate, b_up))


def _moe_down(tile_eid, n_used, h, w_down, b_down):
    return _moe_call(_moe_down_kernel, "moe_down", 1, D_FF, D_MODEL, F32, tile_eid, n_used, h,
                     (w_down,), (b_down,))


def _final_kernel(dest_ref, y_hbm, x1_ref, wt_ref, g2_ref, ln2g_ref, ln2b_ref, o_ref, ybuf, sems):
    i = pl.program_id(0)
    n_tiles = pl.num_programs(0)

    def start_gather(tile, slot):
        base = tile * (TM_FIN * TOP_K)

        def body(r2, carry):
            a0 = base + r2 * DMA_BATCH
            rows = [dest_ref[a0 + i] for i in range(DMA_BATCH)]
            for i, d in enumerate(rows):
                r = r2 * (DMA_BATCH // TOP_K) + i // TOP_K
                pltpu.make_async_copy(y_hbm.at[pl.ds(d, 1), :], ybuf.at[slot, i % TOP_K, pl.ds(r, 1), :],
                                      sems.at[slot]).start(priority=i % 2)
            return carry

        lax.fori_loop(0, TM_FIN * TOP_K // DMA_BATCH, body, 0)

    @pl.when(i == 0)
    def _():
        start_gather(0, 0)

    @pl.when(i + 1 < n_tiles)
    def _():
        start_gather(i + 1, (i + 1) % 2)

    slot = i % 2
    pltpu.make_async_copy(ybuf.at[slot], ybuf.at[slot], sems.at[slot]).wait()
    wt = wt_ref[...]
    y = wt[:, 0:1] * ybuf[slot, 0]
    for k in range(1, TOP_K):
        y = y + wt[:, k:k + 1] * ybuf[slot, k]
    o_ref[...] = _ln(DN_ALPHA * x1_ref[...] + (1.0 + g2_ref[...]) * y) * ln2g_ref[...] + ln2b_ref[...]


def _final(dest_flat, y, x1, wt, mod3, ln2_g, ln2_b):
    tiles_per_batch = SEQ // TM_FIN
    full = lambda shape: pl.BlockSpec(shape, lambda i, dest: tuple(0 for _ in shape))
    row = lambda width: pl.BlockSpec((TM_FIN, width), lambda i, dest: (i, 0))
    return pl.pallas_call(
        _final_kernel,
        grid_spec=pltpu.PrefetchScalarGridSpec(
            num_scalar_prefetch=1,
            grid=(N_TOK // TM_FIN,),
            in_specs=[pl.BlockSpec(memory_space=pl.ANY), row(D_MODEL), row(LANE),
                      pl.BlockSpec((None, 1, D_MODEL), lambda i, dest: (i // tiles_per_batch, 0, 5)),
                      full(ln2_g.shape), full(ln2_b.shape)],
            out_specs=row(D_MODEL),
            scratch_shapes=[pltpu.VMEM((2, TOP_K, TM_FIN, D_MODEL), F32),
                            pltpu.SemaphoreType.DMA((2,))]),
        out_shape=jax.ShapeDtypeStruct((N_TOK, D_MODEL), F32),
        compiler_params=_params("arbitrary"),
        name="combine_ln2",
    )(dest_flat, y, x1, wt, mod3, ln2_g, ln2_b)


def _route(idx, rank, counts):
    experts = jnp.arange(N_EXPERTS, dtype=jnp.int32)
    padded = ((counts + TM_MOE - 1) // TM_MOE) * TM_MOE
    ends = jnp.sum(jnp.where(experts[None, :] <= experts[:, None], padded[None, :], 0), axis=1)
    starts = ends - padded
    dest = rank
    for e in range(N_EXPERTS):
        dest = dest + jnp.where(idx == e, starts[e], 0)
    dest = dest[:, :TOP_K]
    tile_start = jnp.arange(N_MTILES, dtype=jnp.int32) * TM_MOE
    tile_eid = jnp.minimum(jnp.sum((ends[None, :] <= tile_start[:, None]).astype(jnp.int32), axis=1),
                           N_EXPERTS - 1)
    n_used = (ends[-1] // TM_MOE).astype(jnp.int32).reshape(1)
    return dest.reshape(-1), tile_eid, n_used


def kernel(x, c, w_ada, b_ada, w_in, w_gk, b_gk, gla_norm_g, pe_k, pe_v, w_ck1, b_ck1, w_ck2, b_ck2,
           w_cv1, b_cv1, w_cv2, b_cv2, w_o, ln1_g, ln1_b, w_router, b_router, w_gate, b_gate, w_up, b_up,
           w_down, b_down, ln2_g, ln2_b):
    l = 0
    xf = x.reshape(N_TOK, D_MODEL)
    row2 = lambda a: a.reshape(1, -1)

    c8 = jnp.pad(c, ((0, 8 - BATCH), (0, 0)))
    mod3 = _adaln(c8, w_ada[l], row2(b_ada[l]))[:BATCH].reshape(BATCH, 1, 6 * D_MODEL)

    w = w_in[l]
    glr0 = 3072
    nsa0 = glr0 + GLA_RANK
    ngt0 = nsa0 + 1024 + 6 * 256
    w_in_p = jnp.concatenate(
        [w[:, :glr0], w[:, nsa0:ngt0], w[:, glr0:nsa0], w[:, ngt0:],
         jnp.zeros((D_MODEL, D_IN_PAD - w.shape[1]), F32)], axis=1).astype(BF16)
    proj = _inproj(xf, mod3, w_in_p)

    w_gk_pad = jnp.pad(w_gk[l], ((0, LANE - GLA_RANK), (0, 0)))
    y_gla = _gla(proj, w_gk_pad, row2(b_gk[l]), row2(gla_norm_g[l]))

    cos_t, sin_t = _rope_tables(jnp.arange(SEQ))
    cmp_end = jnp.arange(N_CMP_PAD) * CMP_STRIDE + (CMP_BLOCK - 1)
    cos_c, sin_c = _rope_tables(cmp_end)
    q_r, ks, vs, kw, vw = _prep(proj, cos_t, sin_t)
    kc = _compress(proj, 0, pe_k[l], w_ck1[l], row2(b_ck1[l]), w_ck2[l], row2(b_ck2[l]), cos_c, sin_c, True)
    vc = _compress(proj, 1, pe_v[l], w_cv1[l], row2(b_cv1[l]), w_cv2[l], row2(b_cv2[l]), cos_c, sin_c, False)
    y_nsa = _nsa(q_r, kc, vc, ks, vs, kw, vw, proj)

    wr = jnp.pad(w_router[l], ((0, 0), (0, LANE - N_EXPERTS)))
    wr_hi, wr_lo = _split_bf16(wr)
    br = jnp.concatenate([b_router[l], jnp.full((LANE - N_EXPERTS,), NEG, F32)]).reshape(1, LANE)
    x1, h2, idx128, wt128, rank128, cnt8 = _outproj(y_gla, y_nsa, xf, w_o[l].astype(BF16), mod3,
                                                    row2(ln1_g[l]), row2(ln1_b[l]), wr_hi, wr_lo, br)

    counts = cnt8[0, :N_EXPERTS].astype(jnp.int32)
    dest, tile_eid, n_used = _route(idx128, rank128, counts)
    xs = _dispatch(dest, n_used, h2)
    h = _moe_up(tile_eid, n_used, xs, w_gate[l], w_up[l],
                b_gate[l].reshape(N_EXPERTS, 1, D_FF), b_up[l].reshape(N_EXPERTS, 1, D_FF))
    y = _moe_down(tile_eid, n_used, h, w_down[l], b_down[l].reshape(N_EXPERTS, 1, D_MODEL))

    out = _final(dest, y, x1, wt128, mod3, row2(ln2_g[l]), row2(ln2_b[l]))
    return out.reshape(BATCH, SEQ, D_MODEL)
```

```python
import functools

import numpy as np
import jax
import jax.numpy as jnp
from jax import lax
from jax.experimental import pallas as pl
from jax.experimental.pallas import tpu as pltpu

F32 = jnp.float32
BF16 = jnp.bfloat16

D_MODEL = 2048
BATCH = 2
SEQ = 4096
N_TOK = BATCH * SEQ

GLA_HEADS = 4
GLA_DK = 128
GLA_DV = 256
GLA_RANK = 16
GLA_TAU = 16.0
GLA_CHUNK = 64

NSA_DH = 128
NSA_HEADS = 8
NSA_GROUPS = 2
NSA_HPG = 4
CMP_STRIDE = 16
CMP_BLOCK = 32
CMP_HIDDEN = 256
N_CMP_PAD = SEQ // CMP_STRIDE
SEL_BLOCK = 64
N_BLK = SEQ // SEL_BLOCK
SEL_TOPK = 16
WINDOW = 512
ROPE_DIM = 32
ROPE_THETA = 500000.0
Q_SCALE_LOG2 = NSA_DH ** -0.5 * 1.4426950408889634

N_EXPERTS = 32
TOP_K = 4
D_FF = D_MODEL
SWIGLU_LIMIT = 7.0
SWIGLU_ALPHA = 1.702
DN_ALPHA = 2.0 ** 0.25
LN_EPS = 1e-5

COL_GQ, COL_GK, COL_GV, COL_GR, COL_NQ = 0, 512, 1024, 2048, 3072
COL_KV = 4096
COL_TAIL = 5632
D_IN_PAD = 6144
GATE_LANE0 = GLA_RANK

LANE = 128
ROW_CHUNKS = D_MODEL // LANE
NEG = -1e30
VMEM_LIMIT = 56 * 1024 * 1024

TM_IN, TN_IN = 1024, 1024
TT_GLA = 512
TR_PREP = 512
TQ_NSA = 256
TK_SEL = 512
WIN_SPAN = WINDOW + TQ_NSA
TM_OUT = 256
TM_MOE, TN_MOE = 256, 1024
P_ROWS = N_TOK * TOP_K + N_EXPERTS * TM_MOE
N_MTILES = P_ROWS // TM_MOE
TM_FIN = 256
DMA_BATCH = 8


def _dot(a, b):
    return jnp.dot(a, b, preferred_element_type=F32)


def _dot_nt(a, b):
    return lax.dot_general(a, b, (((1,), (1,)), ((), ())), preferred_element_type=F32)


def _dot_tn(a, b):
    return lax.dot_general(a, b, (((0,), (0,)), ((), ())), preferred_element_type=F32)


def _ln(x):
    xc = x - jnp.mean(x, -1, keepdims=True)
    return xc * lax.rsqrt(jnp.mean(xc * xc, -1, keepdims=True) + LN_EPS)


def _split_bf16(x):
    hi = x.astype(BF16)
    lo = (x - hi.astype(F32)).astype(BF16)
    return hi, lo


def _params(*sem):
    return pltpu.CompilerParams(dimension_semantics=sem, vmem_limit_bytes=VMEM_LIMIT)


def _adaln_kernel(c_ref, w_ref, b_ref, o_ref):
    c = c_ref[...]
    a = (c * jax.nn.sigmoid(c)).astype(BF16)
    o_ref[...] = _dot(a, w_ref[...].astype(BF16)) + b_ref[...]


def _adaln(c8, w, b):
    n = w.shape[1]
    tn = 1024
    return pl.pallas_call(
        _adaln_kernel,
        grid=(n // tn,),
        in_specs=[pl.BlockSpec((8, D_MODEL), lambda j: (0, 0)),
                  pl.BlockSpec((D_MODEL, tn), lambda j: (0, j)),
                  pl.BlockSpec((1, tn), lambda j: (0, j))],
        out_specs=pl.BlockSpec((8, tn), lambda j: (0, j)),
        out_shape=jax.ShapeDtypeStruct((8, n), F32),
        compiler_params=_params("arbitrary"),
        name="adaln",
    )(c8, w, b)


def _inproj_kernel(x_ref, sh_ref, sc_ref, w_ref, o_ref, h_sc):
    @pl.when(pl.program_id(1) == 0)
    def _():
        h = _ln(x_ref[...]) * (1.0 + sc_ref[...]) + sh_ref[...]
        h_sc[...] = h.astype(BF16)

    o_ref[...] = _dot(h_sc[...], w_ref[...])


def _inproj(xf, mod3, w_in_p):
    tiles_per_batch = SEQ // TM_IN
    return pl.pallas_call(
        _inproj_kernel,
        grid=(N_TOK // TM_IN, D_IN_PAD // TN_IN),
        in_specs=[pl.BlockSpec((TM_IN, D_MODEL), lambda i, j: (i, 0)),
                  pl.BlockSpec((None, 1, D_MODEL), lambda i, j: (i // tiles_per_batch, 0, 0)),
                  pl.BlockSpec((None, 1, D_MODEL), lambda i, j: (i // tiles_per_batch, 0, 1)),
                  pl.BlockSpec((D_MODEL, TN_IN), lambda i, j: (0, j))],
        out_specs=pl.BlockSpec((TM_IN, TN_IN), lambda i, j: (i, j)),
        out_shape=jax.ShapeDtypeStruct((N_TOK, D_IN_PAD), F32),
        scratch_shapes=[pltpu.VMEM((TM_IN, D_MODEL), BF16)],
        compiler_params=_params("parallel", "arbitrary"),
        name="inproj",
    )(xf, mod3, mod3, w_in_p)


GLA_HALVES = (32, 16, 8, 4, 2, 1)
N_LEVELS = len(GLA_HALVES)
ROW_EB = 2 * N_LEVELS
ROW_EL = 2 * N_LEVELS + 1
N_EVIEWS = 2 * N_LEVELS + 2


def _gla_constants():
    c = GLA_CHUNK
    t = np.arange(c)[:, None]
    r = np.arange(c)[None, :]
    mall = np.zeros((N_EVIEWS, c, c), np.float32)
    valid = np.zeros((N_EVIEWS, c, LANE), np.float32)
    masks = np.zeros((N_LEVELS + 1, c, c), np.float32)
    for li, n in enumerate(GLA_HALVES):
        same = (t // (2 * n)) == (r // (2 * n))
        t_up = (t % (2 * n)) >= n
        r_up = (r % (2 * n)) >= n
        mall[2 * li] = same & t_up & r_up & (r <= t)
        mall[2 * li + 1] = same & ~t_up & ~r_up & (r > t)
        valid[2 * li] = np.broadcast_to(t_up, (c, LANE))
        valid[2 * li + 1] = np.broadcast_to(~t_up, (c, LANE))
        masks[li] = same & t_up & ~r_up
    mall[ROW_EB] = r <= t
    mall[ROW_EL] = r > t
    valid[ROW_EB] = 1.0
    valid[ROW_EL] = 1.0
    masks[N_LEVELS] = np.eye(c)
    return (mall.reshape(N_EVIEWS * c, c), valid.reshape(N_EVIEWS * c, LANE), masks)


def _gla_kernel(q_ref, k_ref, v_ref, r_ref, glr_ref, wgk_ref, bgk_ref, g_ref, mall_ref, valid_ref,
                masks_ref, o_ref, st_sc):
    c = GLA_CHUNK

    @pl.when(pl.program_id(1) == 0)
    def _():
        st_sc[...] = jnp.zeros_like(st_sc)

    z = _dot(glr_ref[...].astype(BF16), wgk_ref[...].astype(BF16)) + bgk_ref[...]
    log_a = (jnp.minimum(z, 0.0) - jnp.log1p(jnp.exp(-jnp.abs(z)))) * (1.0 / GLA_TAU)
    mall = mall_ref[...]
    valid = valid_ref[...]
    for ci in range(TT_GLA // c):
        rows = slice(ci * c, (ci + 1) * c)
        la_hi, la_lo = _split_bf16(log_a[rows])
        e_all = jnp.exp(_dot(mall, la_hi) + _dot(mall, la_lo))
        for h in range(GLA_HEADS):
            kcols = slice(h * GLA_DK, (h + 1) * GLA_DK)
            vcols = slice(h * GLA_DV, (h + 1) * GLA_DV)
            e = e_all[:, kcols] * valid
            q = q_ref[rows, kcols] * (GLA_DK ** -0.5)
            k = k_ref[rows, kcols]
            vb = v_ref[rows, vcols].astype(BF16)
            att = masks_ref[N_LEVELS] * _dot_nt(q.astype(BF16), k.astype(BF16))
            for li in range(N_LEVELS):
                eq = e[(2 * li) * c:(2 * li + 1) * c]
                ek = e[(2 * li + 1) * c:(2 * li + 2) * c]
                att = att + masks_ref[li] * _dot_nt((q * eq).astype(BF16), (k * ek).astype(BF16))
            eb = e[ROW_EB * c:(ROW_EB + 1) * c]
            el = e[ROW_EL * c:(ROW_EL + 1) * c]
            st = st_sc[h]
            o = _dot_nt((q * eb).astype(BF16), st.astype(BF16)) + _dot(att.astype(BF16), vb)
            st_sc[h] = st * eb[c - 1:c, :] + _dot_tn(vb, (k * el).astype(BF16))
            o = o * lax.rsqrt(jnp.mean(o * o, -1, keepdims=True) + LN_EPS) * g_ref[...]
            r = r_ref[rows, vcols]
            o_ref[rows, vcols] = (o * (r * jax.nn.sigmoid(r))).astype(BF16)


def _gla(proj, w_gk_pad, b_gk, norm_g):
    nt = SEQ // TT_GLA
    mall, valid, masks = _gla_constants()
    kw = GLA_HEADS * GLA_DK
    vw = GLA_HEADS * GLA_DV
    full = lambda shape: pl.BlockSpec(shape, lambda b, i: tuple(0 for _ in shape))
    cols = lambda width, col0: pl.BlockSpec((TT_GLA, width), lambda b, i: (b * nt + i, col0 // width))
    return pl.pallas_call(
        _gla_kernel,
        grid=(BATCH, nt),
        in_specs=[cols(kw, COL_GQ), cols(kw, COL_GK), cols(vw, COL_GV), cols(vw, COL_GR),
                  cols(LANE, COL_TAIL), full(w_gk_pad.shape), full(b_gk.shape), full(norm_g.shape),
                  full(mall.shape), full(valid.shape), full(masks.shape)],
        out_specs=cols(vw, 0),
        out_shape=jax.ShapeDtypeStruct((N_TOK, vw), BF16),
        scratch_shapes=[pltpu.VMEM((GLA_HEADS, GLA_DV, GLA_DK), F32)],
        compiler_params=_params("parallel", "arbitrary"),
        name="gla",
    )(proj, proj, proj, proj, proj, w_gk_pad, b_gk, norm_g,
      jnp.asarray(mall, BF16), jnp.asarray(valid), jnp.asarray(masks))


def _rope_tables(pos):
    half = ROPE_DIM // 2
    inv_freq = ROPE_THETA ** (-jnp.arange(half, dtype=F32) * (2.0 / ROPE_DIM))
    ang = pos.astype(F32)[:, None] * inv_freq
    cos, sin = jnp.cos(ang), jnp.sin(ang)
    n = pos.shape[0]
    cosf = jnp.concatenate([cos, cos, jnp.ones((n, LANE - ROPE_DIM), F32)], -1)
    sinf = jnp.concatenate([-sin, sin, jnp.zeros((n, LANE - ROPE_DIM), F32)], -1)
    return cosf, sinf


def _rope(x, cosf, sinf):
    lane = lax.broadcasted_iota(jnp.int32, x.shape, 1)
    half = ROPE_DIM // 2
    swapped = jnp.where(lane < half, pltpu.roll(x, LANE - half, 1), pltpu.roll(x, half, 1))
    return x * cosf + swapped * sinf


def _prep_kernel(q_ref, ks_ref, vs_ref, kw_ref, vw_ref, cos_ref, sin_ref,
                 qo_ref, kso_ref, vso_ref, kwo_ref, vwo_ref):
    cosf = cos_ref[...]
    sinf = sin_ref[...]
    for hh in range(NSA_HEADS):
        cols = slice(hh * NSA_DH, (hh + 1) * NSA_DH)
        qo_ref[:, cols] = (_rope(q_ref[:, cols], cosf, sinf) * Q_SCALE_LOG2).astype(BF16)
    for g in range(NSA_GROUPS):
        cols = slice(g * NSA_DH, (g + 1) * NSA_DH)
        kso_ref[g] = _rope(ks_ref[:, cols], cosf, sinf).astype(BF16)
        kwo_ref[g] = _rope(kw_ref[:, cols], cosf, sinf).astype(BF16)
        vso_ref[g] = vs_ref[:, cols].astype(BF16)
        vwo_ref[g] = vw_ref[:, cols].astype(BF16)


def _prep(proj, cosf, sinf):
    nt = SEQ // TR_PREP
    kvw = NSA_GROUPS * NSA_DH
    kv_in = lambda which: pl.BlockSpec((TR_PREP, kvw), lambda b, i: (b * nt + i, COL_KV // kvw + which))
    kv_out = pl.BlockSpec((None, NSA_GROUPS, TR_PREP, NSA_DH), lambda b, i: (b, 0, i, 0))
    kv_shape = jax.ShapeDtypeStruct((BATCH, NSA_GROUPS, SEQ, NSA_DH), BF16)
    qw = NSA_HEADS * NSA_DH
    return pl.pallas_call(
        _prep_kernel,
        grid=(BATCH, nt),
        in_specs=[pl.BlockSpec((TR_PREP, qw), lambda b, i: (b * nt + i, COL_NQ // qw)),
                  kv_in(2), kv_in(3), kv_in(4), kv_in(5),
                  pl.BlockSpec((TR_PREP, LANE), lambda b, i: (i, 0)),
                  pl.BlockSpec((TR_PREP, LANE), lambda b, i: (i, 0))],
        out_specs=[pl.BlockSpec((TR_PREP, qw), lambda b, i: (b * nt + i, 0)),
                   kv_out, kv_out, kv_out, kv_out],
        out_shape=[jax.ShapeDtypeStruct((N_TOK, qw), BF16), kv_shape, kv_shape, kv_shape, kv_shape],
        compiler_params=_params("parallel", "parallel"),
        name="nsa_prep",
    )(proj, proj, proj, proj, proj, cosf, sinf)


def _compress_kernel(a_ref, pe_ref, w1_ref, b1_ref, w2_ref, b2_ref, cos_ref, sin_ref, o_ref, bot_sc,
                     *, rope):
    n = N_CMP_PAD
    top = jnp.zeros((n, CMP_HIDDEN), F32)
    bot = jnp.zeros((n, CMP_HIDDEN), F32)
    for p in range(CMP_STRIDE):
        ap = a_ref[pl.ds(p, n, stride=CMP_STRIDE), :]
        w_top = w1_ref[p * NSA_DH:(p + 1) * NSA_DH, :].astype(BF16)
        w_bot = w1_ref[(CMP_STRIDE + p) * NSA_DH:(CMP_STRIDE + p + 1) * NSA_DH, :].astype(BF16)
        top = top + _dot((ap + pe_ref[p:p + 1, :]).astype(BF16), w_top)
        bot = bot + _dot((ap + pe_ref[CMP_STRIDE + p:CMP_STRIDE + p + 1, :]).astype(BF16), w_bot)
    bot_sc[0:n, :] = bot
    bot_sc[n:n + 8, :] = jnp.zeros((8, CMP_HIDDEN), F32)
    h = top + bot_sc[1:n + 1, :] + b1_ref[...]
    h = h * jax.nn.sigmoid(h)
    out = _dot(h.astype(BF16), w2_ref[...].astype(BF16)) + b2_ref[...]
    if rope:
        out = _rope(out, cos_ref[...], sin_ref[...])
    row = lax.broadcasted_iota(jnp.int32, out.shape, 0)
    o_ref[...] = jnp.where(row < n - 1, out, 0.0).astype(BF16)


def _compress(proj, which, pe, w1, b1, w2, b2, cosf, sinf, rope):
    col0 = (COL_KV + which * NSA_GROUPS * NSA_DH) // NSA_DH
    full = lambda shape: pl.BlockSpec(shape, lambda b, g: tuple(0 for _ in shape))
    return pl.pallas_call(
        functools.partial(_compress_kernel, rope=rope),
        grid=(BATCH, NSA_GROUPS),
        in_specs=[pl.BlockSpec((SEQ, NSA_DH), lambda b, g: (b, col0 + g)),
                  full(pe.shape), full(w1.shape), full(b1.shape), full(w2.shape), full(b2.shape),
                  full(cosf.shape), full(sinf.shape)],
        out_specs=pl.BlockSpec((None, None, N_CMP_PAD, NSA_DH), lambda b, g: (b, g, 0, 0)),
        out_shape=jax.ShapeDtypeStruct((BATCH, NSA_GROUPS, N_CMP_PAD, NSA_DH), BF16),
        scratch_shapes=[pltpu.VMEM((N_CMP_PAD + 8, CMP_HIDDEN), F32)],
        compiler_params=_params("parallel", "parallel"),
        name="nsa_compress",
    )(proj, pe, w1, b1, w2, b2, cosf, sinf)


def _nsa_constants():
    c_start = np.arange(N_CMP_PAD) * CMP_STRIDE
    b_start = np.arange(N_BLK) * SEL_BLOCK
    overlap_t = ((c_start[None, :] < b_start[:, None] + SEL_BLOCK)
                 & (c_start[None, :] + CMP_BLOCK > b_start[:, None])).astype(np.float32)
    overlap_t[:, N_CMP_PAD - 1] = 0.0
    expand = (np.arange(SEQ)[None, :] // SEL_BLOCK == np.arange(LANE)[:, None]).astype(np.float32)
    return overlap_t, expand


def _tile_heads(a):
    return jnp.concatenate([a] * NSA_HPG, axis=0)


def _nsa_kernel(q_ref, kc_ref, vc_ref, ks_ref, vs_ref, kw_ref, vw_ref, gate_ref, ovt_ref, expand_ref,
                o_ref, score_sc, m_sc, acc_sc):
    tq = TQ_NSA
    g = pl.program_id(1)
    q0 = pl.program_id(2) * tq
    qs = jnp.concatenate([q_ref[:, hh * NSA_DH:(hh + 1) * NSA_DH] for hh in range(NSA_HPG)], axis=0)
    pos_t = q0 + lax.broadcasted_iota(jnp.int32, (tq, 1), 0)

    def with_ones(v):
        return jnp.concatenate([v, jnp.ones(v.shape, BF16)], axis=1)

    s = _dot_nt(qs, kc_ref[...])
    cmp_end = lax.broadcasted_iota(jnp.int32, (tq, N_CMP_PAD), 1) * CMP_STRIDE + (CMP_BLOCK - 1)
    s = s + _tile_heads(jnp.where(cmp_end <= pos_t, 0.0, NEG))
    m = jnp.max(s, -1, keepdims=True)
    e = jnp.where(s > 0.5 * NEG, jnp.exp2(s - m), 0.0)
    p = e / jnp.maximum(jnp.sum(e, -1, keepdims=True), 1e-30)
    o_cmp = _dot(p.astype(BF16), vc_ref[...])
    p_sum = p[0:tq] + p[tq:2 * tq] + p[2 * tq:3 * tq] + p[3 * tq:4 * tq]
    ps_hi, ps_lo = _split_bf16(p_sum)
    ovt = ovt_ref[...]
    p_blk_t = _dot_nt(ovt, ps_hi) + _dot_nt(ovt, ps_lo)

    jj = lax.broadcasted_iota(jnp.int32, (N_BLK, tq), 0)
    cur = (q0 + lax.broadcasted_iota(jnp.int32, (N_BLK, tq), 1)) // SEL_BLOCK
    forced = (jj == 0) | (jj == cur) | (jj == cur - 1)
    allowed = jj <= cur
    score = jnp.where(forced, 3.0e38, jnp.where(allowed, p_blk_t, -1.0))
    score_sc[...] = score

    def rank_body(i, rank):
        row = score_sc[pl.ds(i, 1), :]
        first = jnp.where(jj > i, 1.0, 0.0)
        return rank + jnp.where(row > score, 1.0, jnp.where(row == score, first, 0.0))

    n_live = (q0 + tq - 1) // SEL_BLOCK + 1
    rank = lax.fori_loop(0, n_live, rank_body, jnp.zeros((N_BLK, tq), F32))
    sel_t = jnp.where(allowed, jnp.where(rank < SEL_TOPK, 1.0, 0.0), 0.0)
    sel = jnp.concatenate([sel_t, jnp.zeros((LANE - N_BLK, tq), F32)], axis=0).T.astype(BF16)

    m_sc[...] = jnp.full(m_sc.shape, NEG, F32)
    acc_sc[...] = jnp.zeros(acc_sc.shape, F32)

    def sel_body(kt, carry):
        k0 = pl.multiple_of(kt * TK_SEL, TK_SEL)
        s = _dot_nt(qs, ks_ref[pl.ds(k0, TK_SEL), :])
        sel_x = _dot(sel, expand_ref[:, pl.ds(k0, TK_SEL)])
        kpos = k0 + lax.broadcasted_iota(jnp.int32, (tq, TK_SEL), 1)
        bias = jnp.where(kpos <= pos_t, jnp.where(sel_x > 0.5, 0.0, NEG), NEG)
        s = s + _tile_heads(bias)
        m_prev = m_sc[...]
        m_new = jnp.maximum(m_prev, jnp.max(s, -1, keepdims=True))
        alpha = jnp.exp2(m_prev - m_new)
        p = jnp.exp2(s - jnp.concatenate([m_new] * (TK_SEL // LANE), axis=1)).astype(BF16)
        acc_sc[...] = (jnp.concatenate([alpha, alpha], axis=1) * acc_sc[...]
                       + _dot(p, with_ones(vs_ref[pl.ds(k0, TK_SEL), :])))
        m_sc[...] = m_new
        return carry

    lax.fori_loop(0, (q0 + tq - 1) // TK_SEL + 1, sel_body, 0)
    o_slc = acc_sc[:, 0:NSA_DH] / acc_sc[:, NSA_DH:2 * NSA_DH]

    start = pl.multiple_of(jnp.maximum(q0 - WINDOW, 0), LANE)
    s = _dot_nt(qs, kw_ref[pl.ds(start, WIN_SPAN), :])
    dist = pos_t - (start + lax.broadcasted_iota(jnp.int32, (tq, WIN_SPAN), 1))
    bias = jnp.where(dist >= 0, jnp.where(dist < WINDOW, 0.0, NEG), NEG)
    s = s + _tile_heads(bias)
    p = jnp.exp2(s - jnp.max(s, -1, keepdims=True)).astype(BF16)
    ow = _dot(p, with_ones(vw_ref[pl.ds(start, WIN_SPAN), :]))
    o_win = ow[:, 0:NSA_DH] / ow[:, NSA_DH:2 * NSA_DH]

    gates = jax.nn.sigmoid(gate_ref[...])

    def gate(hh, branch):
        lane0 = GATE_LANE0 + hh * 3 + branch
        lane1 = lane0 + NSA_HPG * 3
        return jnp.where(g == 0, gates[:, lane0:lane0 + 1], gates[:, lane1:lane1 + 1])

    for hh in range(NSA_HPG):
        r = slice(hh * tq, (hh + 1) * tq)
        o = gate(hh, 0) * o_cmp[r] + gate(hh, 1) * o_slc[r] + gate(hh, 2) * o_win[r]
        o_ref[:, hh * NSA_DH:(hh + 1) * NSA_DH] = o.astype(BF16)


def _nsa(q_r, kc, vc, ks, vs, kw, vw, proj):
    nq = SEQ // TQ_NSA
    gw = NSA_HPG * NSA_DH
    overlap_t, expand = _nsa_constants()
    rows = NSA_HPG * TQ_NSA
    cmp_spec = pl.BlockSpec((None, None, N_CMP_PAD, NSA_DH), lambda b, g, i: (b, g, 0, 0))
    kv_spec = pl.BlockSpec((None, None, SEQ, NSA_DH), lambda b, g, i: (b, g, 0, 0))
    return pl.pallas_call(
        _nsa_kernel,
        grid=(BATCH, NSA_GROUPS, nq),
        in_specs=[pl.BlockSpec((TQ_NSA, gw), lambda b, g, i: (b * nq + i, g)),
                  cmp_spec, cmp_spec, kv_spec, kv_spec, kv_spec, kv_spec,
                  pl.BlockSpec((TQ_NSA, LANE), lambda b, g, i: (b * nq + i, COL_TAIL // LANE)),
                  pl.BlockSpec(overlap_t.shape, lambda b, g, i: (0, 0)),
                  pl.BlockSpec(expand.shape, lambda b, g, i: (0, 0))],
        out_specs=pl.BlockSpec((TQ_NSA, gw), lambda b, g, i: (b * nq + i, g)),
        out_shape=jax.ShapeDtypeStruct((N_TOK, NSA_GROUPS * gw), BF16),
        scratch_shapes=[pltpu.VMEM((N_BLK, TQ_NSA), F32),
                        pltpu.VMEM((rows, LANE), F32),
                        pltpu.VMEM((rows, 2 * NSA_DH), F32)],
        compiler_params=_params("parallel", "parallel", "arbitrary"),
        name="nsa_attn",
    )(q_r, kc, vc, ks, vs, kw, vw, proj, jnp.asarray(overlap_t, BF16), jnp.asarray(expand, BF16))


def _outproj_kernel(yg_ref, yn_ref, x_ref, wo_ref, g1_ref, sc2_ref, sh2_ref, ln1g_ref, ln1b_ref,
                    wr_hi_ref, wr_lo_ref, br_ref, ltri_ref, x1_ref, h2_ref, idx_ref, wt_ref, rank_ref,
                    cnt_ref, base_sc):
    half = D_MODEL // 2

    @pl.when(pl.program_id(0) == 0)
    def _():
        base_sc[...] = jnp.zeros_like(base_sc)

    mix = _dot(yg_ref[...], wo_ref[0:half, :]) + _dot(yn_ref[...], wo_ref[half:D_MODEL, :])
    x1 = _ln(DN_ALPHA * x_ref[...] + (1.0 + g1_ref[...]) * mix) * ln1g_ref[...] + ln1b_ref[...]
    x1_ref[...] = x1
    h2 = _ln(x1) * (1.0 + sc2_ref[...]) + sh2_ref[...]
    h_hi, h_lo = _split_bf16(h2)
    for s in range(ROW_CHUNKS):
        h2_ref[pl.ds(s, TM_OUT, stride=ROW_CHUNKS), :] = h2[:, s * LANE:(s + 1) * LANE]
    wr_hi = wr_hi_ref[...]
    logits = _dot(h_hi, wr_hi) + _dot(h_lo, wr_hi) + _dot(h_hi, wr_lo_ref[...]) + br_ref[...]

    lane = lax.broadcasted_iota(jnp.int32, logits.shape, 1)
    lane_f = lane.astype(F32)
    vals = logits
    idx_out = jnp.zeros(logits.shape, jnp.int32)
    exp_out = jnp.zeros(logits.shape, F32)
    denom = jnp.zeros((logits.shape[0], 1), F32)
    m0 = None
    onehots = []
    for k in range(TOP_K):
        mk = jnp.max(vals, -1, keepdims=True)
        ik = jnp.min(jnp.where(vals == mk, lane_f, float(LANE)), -1, keepdims=True)
        if k == 0:
            m0 = mk
        ek = jnp.exp(mk - m0)
        denom = denom + ek
        idx_out = jnp.where(lane == k, ik.astype(jnp.int32), idx_out)
        exp_out = jnp.where(lane == k, ek, exp_out)
        hit = lane_f == ik
        onehots.append(jnp.where(hit, 1.0, 0.0))
        vals = jnp.where(hit, -3.0e38, vals)
    idx_ref[...] = idx_out
    wt_ref[...] = exp_out / denom

    cnt = onehots[0] + onehots[1] + onehots[2] + onehots[3]
    base = base_sc[0:1, :]
    before = _dot(ltri_ref[...], cnt.astype(BF16)) + base
    rank_out = jnp.zeros(logits.shape, jnp.int32)
    for k in range(TOP_K):
        rk = jnp.sum(onehots[k] * before, -1, keepdims=True)
        rank_out = jnp.where(lane == k, rk.astype(jnp.int32), rank_out)
    rank_ref[...] = rank_out
    base_sc[...] = jnp.broadcast_to(base + jnp.sum(cnt, 0, keepdims=True), base_sc.shape)
    cnt_ref[...] = base_sc[...]


def _outproj(y_gla, y_nsa, xf, w_o, mod3, ln1_g, ln1_b, wr_hi, wr_lo, br):
    tiles_per_batch = SEQ // TM_OUT
    half = D_MODEL // 2
    mod_spec = lambda chunk: pl.BlockSpec((None, 1, D_MODEL), lambda i: (i // tiles_per_batch, 0, chunk))
    full = lambda shape: pl.BlockSpec(shape, lambda i: tuple(0 for _ in shape))
    row = lambda width: pl.BlockSpec((TM_OUT, width), lambda i: (i, 0))
    ltri = jnp.asarray(np.tril(np.ones((TM_OUT, TM_OUT), np.float32), -1), BF16)
    return pl.pallas_call(
        _outproj_kernel,
        grid=(N_TOK // TM_OUT,),
        in_specs=[row(half), row(half), row(D_MODEL), full(w_o.shape),
                  mod_spec(2), mod_spec(4), mod_spec(3),
                  full(ln1_g.shape), full(ln1_b.shape), full(wr_hi.shape), full(wr_lo.shape), full(br.shape),
                  full(ltri.shape)],
        out_specs=[row(D_MODEL), pl.BlockSpec((TM_OUT * ROW_CHUNKS, LANE), lambda i: (i, 0)),
                   row(LANE), row(LANE), row(LANE), full((8, LANE))],
        out_shape=[jax.ShapeDtypeStruct((N_TOK, D_MODEL), F32),
                   jax.ShapeDtypeStruct((N_TOK * ROW_CHUNKS, LANE), F32),
                   jax.ShapeDtypeStruct((N_TOK, LANE), jnp.int32),
                   jax.ShapeDtypeStruct((N_TOK, LANE), F32),
                   jax.ShapeDtypeStruct((N_TOK, LANE), jnp.int32),
                   jax.ShapeDtypeStruct((8, LANE), F32)],
        scratch_shapes=[pltpu.VMEM((8, LANE), F32)],
        compiler_params=_params("arbitrary"),
        name="outproj_router",
    )(y_gla, y_nsa, xf, w_o, mod3, mod3, mod3, ln1_g, ln1_b, wr_hi, wr_lo, br, ltri)


def _dispatch_kernel(dest_ref, nused_ref, h2_hbm, xs_ref, src_sm, buf, sems):
    t = pl.program_id(0)
    n_used = nused_ref[0]

    @pl.when(t == 0)
    def _():
        def fill(r, carry):
            src_sm[r] = 0
            return carry

        lax.fori_loop(0, P_ROWS, fill, 0, unroll=8)

        def scatter(tok2, carry):
            a0 = tok2 * (2 * TOP_K)
            rows = [dest_ref[a0 + i] for i in range(2 * TOP_K)]
            for i, row in enumerate(rows):
                src_sm[row] = tok2 * 2 + i // TOP_K
            return carry

        lax.fori_loop(0, N_TOK // 2, scatter, 0)

    def start_gather(tile, slot):
        base = tile * TM_MOE

        def body(r8, carry):
            r0 = r8 * DMA_BATCH
            toks = [src_sm[base + r0 + i] for i in range(DMA_BATCH)]
            for i, tok in enumerate(toks):
                pltpu.make_async_copy(h2_hbm.at[pl.ds(pl.multiple_of(tok * ROW_CHUNKS, ROW_CHUNKS), ROW_CHUNKS), :],
                                      buf.at[slot, pl.ds(pl.multiple_of((r0 + i) * ROW_CHUNKS, ROW_CHUNKS),
                                                         ROW_CHUNKS), :],
                                      sems.at[slot]).start(priority=i % 2)
            return carry

        lax.fori_loop(0, TM_MOE // DMA_BATCH, body, 0)

    @pl.when(t == 0)
    def _():
        start_gather(0, 0)

    @pl.when(t + 1 < n_used)
    def _():
        start_gather(t + 1, (t + 1) % 2)

    @pl.when(t < n_used)
    def _():
        slot = t % 2
        pltpu.make_async_copy(buf.at[slot], buf.at[slot], sems.at[slot]).wait()
        for s in range(ROW_CHUNKS):
            xs_ref[:, s * LANE:(s + 1) * LANE] = buf[slot, pl.ds(s, TM_MOE, stride=ROW_CHUNKS), :].astype(BF16)

    @pl.when(t >= n_used)
    def _():
        xs_ref[...] = jnp.zeros(xs_ref.shape, BF16)


def _dispatch(dest_flat, n_used, h2):
    return pl.pallas_call(
        _dispatch_kernel,
        grid_spec=pltpu.PrefetchScalarGridSpec(
            num_scalar_prefetch=2,
            grid=(N_MTILES,),
            in_specs=[pl.BlockSpec(memory_space=pl.ANY)],
            out_specs=pl.BlockSpec((TM_MOE, D_MODEL), lambda t, dest, nu: (t, 0)),
            scratch_shapes=[pltpu.SMEM((P_ROWS,), jnp.int32),
                            pltpu.VMEM((2, TM_MOE * ROW_CHUNKS, LANE), F32),
                            pltpu.SemaphoreType.DMA((2,))]),
        out_shape=jax.ShapeDtypeStruct((P_ROWS, D_MODEL), BF16),
        compiler_params=_params("arbitrary"),
        name="moe_dispatch",
    )(dest_flat, n_used, h2)


class _ExpertWeights:
    def __init__(self, eid_ref, n_used, w_hbms, bufs, sems, cnt_sm):
        self.eid_ref, self.n_used = eid_ref, n_used
        self.w_hbms, self.bufs, self.sems, self.cnt_sm = w_hbms, bufs, sems, cnt_sm

    def _copies(self, expert, j, slot):
        cols = pl.ds(pl.multiple_of(j * TN_MOE, TN_MOE), TN_MOE)
        return [pltpu.make_async_copy(w.at[expert, :, cols], buf.at[slot], self.sems.at[i, slot])
                for i, (w, buf) in enumerate(zip(self.w_hbms, self.bufs))]

    def prologue(self):
        self.cnt_sm[0] = 0
        for cp in self._copies(self.eid_ref[0], 0, 0):
            cp.start()

    def acquire(self, j, t, n_passes):
        slot = self.cnt_sm[0] % 2
        for cp in self._copies(self.eid_ref[t], j, slot):
            cp.wait()
        expert = self.eid_ref[t]
        nxt = lax.while_loop(
            lambda u: (u < self.n_used) & (self.eid_ref[jnp.minimum(u, N_MTILES - 1)] == expert),
            lambda u: u + 1, t + 1)
        in_pass = nxt < self.n_used

        @pl.when(in_pass | (j + 1 < n_passes))
        def _():
            for cp in self._copies(self.eid_ref[jnp.where(in_pass, nxt, 0)], jnp.where(in_pass, j, j + 1),
                                   1 - slot):
                cp.start()

        self.cnt_sm[0] = self.cnt_sm[0] + 1
        return slot


def _moe_up_kernel(eid_ref, nused_ref, x_ref, wg_hbm, wu_hbm, bg_ref, bu_ref, h_ref,
                   wg_buf, wu_buf, wg_sc, wu_sc, sems, cnt_sm):
    j = pl.program_id(0)
    t = pl.program_id(1)
    n_used = nused_ref[0]
    weights = _ExpertWeights(eid_ref, n_used, (wg_hbm, wu_hbm), (wg_buf, wu_buf), sems, cnt_sm)

    @pl.when((j == 0) & (t == 0))
    def _():
        weights.prologue()

    @pl.when(t < n_used)
    def _():
        @pl.when((t == 0) | (eid_ref[t] != eid_ref[jnp.maximum(t - 1, 0)]))
        def _():
            slot = weights.acquire(j, t, pl.num_programs(0))
            wg_sc[...] = wg_buf[slot].astype(BF16)
            wu_sc[...] = wu_buf[slot].astype(BF16)

        x = x_ref[...]
        gate = jnp.minimum(_dot(x, wg_sc[...]) + bg_ref[...], SWIGLU_LIMIT)
        up = jnp.clip(_dot(x, wu_sc[...]) + bu_ref[...], -SWIGLU_LIMIT, SWIGLU_LIMIT)
        h_ref[...] = (gate * jax.nn.sigmoid(SWIGLU_ALPHA * gate) * (up + 1.0)).astype(BF16)

    @pl.when(t >= n_used)
    def _():
        h_ref[...] = jnp.zeros(h_ref.shape, BF16)


def _moe_down_kernel(eid_ref, nused_ref, h_ref, wd_hbm, bd_ref, y_ref, wd_buf, wd_sc, sems, cnt_sm):
    j = pl.program_id(0)
    t = pl.program_id(1)
    n_used = nused_ref[0]
    weights = _ExpertWeights(eid_ref, n_used, (wd_hbm,), (wd_buf,), sems, cnt_sm)

    @pl.when((j == 0) & (t == 0))
    def _():
        weights.prologue()

    @pl.when(t < n_used)
    def _():
        @pl.when((t == 0) | (eid_ref[t] != eid_ref[jnp.maximum(t - 1, 0)]))
        def _():
            slot = weights.acquire(j, t, pl.num_programs(0))
            wd_sc[...] = wd_buf[slot].astype(BF16)

        y_ref[...] = _dot(h_ref[...], wd_sc[...]) + bd_ref[...]

    @pl.when(t >= n_used)
    def _():
        y_ref[...] = jnp.zeros(y_ref.shape, F32)


def _row_tile(t, nused_ref):
    return jnp.minimum(t, jnp.maximum(nused_ref[0] - 1, 0))


def _moe_call(body, name, n_weights, k_dim, out_dim, out_dtype, tile_eid, n_used, rows, weights, biases):
    any_spec = pl.BlockSpec(memory_space=pl.ANY)
    b_spec = pl.BlockSpec((None, 1, TN_MOE), lambda j, t, eid, nu: (eid[t], 0, j))
    return pl.pallas_call(
        body,
        grid_spec=pltpu.PrefetchScalarGridSpec(
            num_scalar_prefetch=2,
            grid=(out_dim // TN_MOE, N_MTILES),
            in_specs=[pl.BlockSpec((TM_MOE, k_dim), lambda j, t, eid, nu: (_row_tile(t, nu), 0))]
            + [any_spec] * n_weights + [b_spec] * n_weights,
            out_specs=pl.BlockSpec((TM_MOE, TN_MOE), lambda j, t, eid, nu: (t, j)),
            scratch_shapes=[pltpu.VMEM((2, k_dim, TN_MOE), F32)] * n_weights
            + [pltpu.VMEM((k_dim, TN_MOE), BF16)] * n_weights
            + [pltpu.SemaphoreType.DMA((n_weights, 2)), pltpu.SMEM((1,), jnp.int32)]),
        out_shape=jax.ShapeDtypeStruct((P_ROWS, out_dim), out_dtype),
        compiler_params=_params("arbitrary", "arbitrary"),
        name=name,
    )(tile_eid, n_used, rows, *weights, *biases)


def _moe_up(tile_eid, n_used, xs, w_gate, w_up, b_gate, b_up):
    return _moe_call(_moe_up_kernel, "moe_up", 2, D_MODEL, D_FF, BF16, tile_eid, n_used, xs,
                     (w_gate, w_up), (b_gate, b_up))


def _moe_down(tile_eid, n_used, h, w_down, b_down):
    return _moe_call(_moe_down_kernel, "moe_down", 1, D_FF, D_MODEL, F32, tile_eid, n_used, h,
                     (w_down,), (b_down,))


def _final_kernel(dest_ref, y_hbm, x1_ref, wt_ref, g2_ref, ln2g_ref, ln2b_ref, o_ref, ybuf, sems):
    i = pl.program_id(0)
    n_tiles = pl.num_programs(0)

    def start_gather(tile, slot):
        base = tile * (TM_FIN * TOP_K)

        def body(r2, carry):
            a0 = base + r2 * DMA_BATCH
            rows = [dest_ref[a0 + i] for i in range(DMA_BATCH)]
            for i, d in enumerate(rows):
                r = r2 * (DMA_BATCH // TOP_K) + i // TOP_K
                pltpu.make_async_copy(y_hbm.at[pl.ds(d, 1), :], ybuf.at[slot, i % TOP_K, pl.ds(r, 1), :],
                                      sems.at[slot]).start(priority=i % 2)
            return carry

        lax.fori_loop(0, TM_FIN * TOP_K // DMA_BATCH, body, 0)

    @pl.when(i == 0)
    def _():
        start_gather(0, 0)

    @pl.when(i + 1 < n_tiles)
    def _():
        start_gather(i + 1, (i + 1) % 2)

    slot = i % 2
    pltpu.make_async_copy(ybuf.at[slot], ybuf.at[slot], sems.at[slot]).wait()
    wt = wt_ref[...]
    y = wt[:, 0:1] * ybuf[slot, 0]
    for k in range(1, TOP_K):
        y = y + wt[:, k:k + 1] * ybuf[slot, k]
    o_ref[...] = _ln(DN_ALPHA * x1_ref[...] + (1.0 + g2_ref[...]) * y) * ln2g_ref[...] + ln2b_ref[...]


def _final(dest_flat, y, x1, wt, mod3, ln2_g, ln2_b):
    tiles_per_batch = SEQ // TM_FIN
    full = lambda shape: pl.BlockSpec(shape, lambda i, dest: tuple(0 for _ in shape))
    row = lambda width: pl.BlockSpec((TM_FIN, width), lambda i, dest: (i, 0))
    return pl.pallas_call(
        _final_kernel,
        grid_spec=pltpu.PrefetchScalarGridSpec(
            num_scalar_prefetch=1,
            grid=(N_TOK // TM_FIN,),
            in_specs=[pl.BlockSpec(memory_space=pl.ANY), row(D_MODEL), row(LANE),
                      pl.BlockSpec((None, 1, D_MODEL), lambda i, dest: (i // tiles_per_batch, 0, 5)),
                      full(ln2_g.shape), full(ln2_b.shape)],
            out_specs=row(D_MODEL),
            scratch_shapes=[pltpu.VMEM((2, TOP_K, TM_FIN, D_MODEL), F32),
                            pltpu.SemaphoreType.DMA((2,))]),
        out_shape=jax.ShapeDtypeStruct((N_TOK, D_MODEL), F32),
        compiler_params=_params("arbitrary"),
        name="combine_ln2",
    )(dest_flat, y, x1, wt, mod3, ln2_g, ln2_b)


def _route(idx, rank, counts):
    experts = jnp.arange(N_EXPERTS, dtype=jnp.int32)
    padded = ((counts + TM_MOE - 1) // TM_MOE) * TM_MOE
    ends = jnp.sum(jnp.where(experts[None, :] <= experts[:, None], padded[None, :], 0), axis=1)
    starts = ends - padded
    dest = rank
    for e in range(N_EXPERTS):
        dest = dest + jnp.where(idx == e, starts[e], 0)
    dest = dest[:, :TOP_K]
    tile_start = jnp.arange(N_MTILES, dtype=jnp.int32) * TM_MOE
    tile_eid = jnp.minimum(jnp.sum((ends[None, :] <= tile_start[:, None]).astype(jnp.int32), axis=1),
                           N_EXPERTS - 1)
    n_used = (ends[-1] // TM_MOE).astype(jnp.int32).reshape(1)
    return dest.reshape(-1), tile_eid, n_used


def kernel(x, c, w_ada, b_ada, w_in, w_gk, b_gk, gla_norm_g, pe_k, pe_v, w_ck1, b_ck1, w_ck2, b_ck2,
           w_cv1, b_cv1, w_cv2, b_cv2, w_o, ln1_g, ln1_b, w_router, b_router, w_gate, b_gate, w_up, b_up,
           w_down, b_down, ln2_g, ln2_b):
    l = 0
    xf = x.reshape(N_TOK, D_MODEL)
    row2 = lambda a: a.reshape(1, -1)

    c8 = jnp.pad(c, ((0, 8 - BATCH), (0, 0)))
    mod3 = _adaln(c8, w_ada[l], row2(b_ada[l]))[:BATCH].reshape(BATCH, 1, 6 * D_MODEL)

    w = w_in[l]
    glr0 = 3072
    nsa0 = glr0 + GLA_RANK
    ngt0 = nsa0 + 1024 + 6 * 256
    w_in_p = jnp.concatenate(
        [w[:, :glr0], w[:, nsa0:ngt0], w[:, glr0:nsa0], w[:, ngt0:],
         jnp.zeros((D_MODEL, D_IN_PAD - w.shape[1]), F32)], axis=1).astype(BF16)
    proj = _inproj(xf, mod3, w_in_p)

    w_gk_pad = jnp.pad(w_gk[l], ((0, LANE - GLA_RANK), (0, 0)))
    y_gla = _gla(proj, w_gk_pad, row2(b_gk[l]), row2(gla_norm_g[l]))

    cos_t, sin_t = _rope_tables(jnp.arange(SEQ))
    cmp_end = jnp.arange(N_CMP_PAD) * CMP_STRIDE + (CMP_BLOCK - 1)
    cos_c, sin_c = _rope_tables(cmp_end)
    q_r, ks, vs, kw, vw = _prep(proj, cos_t, sin_t)
    kc = _compress(proj, 0, pe_k[l], w_ck1[l], row2(b_ck1[l]), w_ck2[l], row2(b_ck2[l]), cos_c, sin_c, True)
    vc = _compress(proj, 1, pe_v[l], w_cv1[l], row2(b_cv1[l]), w_cv2[l], row2(b_cv2[l]), cos_c, sin_c, False)
    y_nsa = _nsa(q_r, kc, vc, ks, vs, kw, vw, proj)

    wr = jnp.pad(w_router[l], ((0, 0), (0, LANE - N_EXPERTS)))
    wr_hi, wr_lo = _split_bf16(wr)
    br = jnp.concatenate([b_router[l], jnp.full((LANE - N_EXPERTS,), NEG, F32)]).reshape(1, LANE)
    x1, h2, idx128, wt128, rank128, cnt8 = _outproj(y_gla, y_nsa, xf, w_o[l].astype(BF16), mod3,
                                                    row2(ln1_g[l]), row2(ln1_b[l]), wr_hi, wr_lo, br)

    counts = cnt8[0, :N_EXPERTS].astype(jnp.int32)
    dest, tile_eid, n_used = _route(idx128, rank128, counts)
    xs = _dispatch(dest, n_used, h2)
    h = _moe_up(tile_eid, n_used, xs, w_gate[l], w_up[l],
                b_gate[l].reshape(N_EXPERTS, 1, D_FF), b_up[l].reshape(N_EXPERTS, 1, D_FF))
    y = _moe_down(tile_eid, n_used, h, w_down[l], b_down[l].reshape(N_EXPERTS, 1, D_MODEL))

    out = _final(dest, y, x1, wt128, mod3, row2(ln2_g[l]), row2(ln2_b[l]))
    return out.reshape(BATCH, SEQ, D_MODEL)
```

```python
import functools

import numpy as np
import jax
import jax.numpy as jnp
from jax import lax
from jax.experimental import pallas as pl
from jax.experimental.pallas import tpu as pltpu

F32 = jnp.float32
BF16 = jnp.bfloat16

D_MODEL = 2048
BATCH = 2
SEQ = 4096
N_TOK = BATCH * SEQ

GLA_HEADS = 4
GLA_DK = 128
GLA_DV = 256
GLA_RANK = 16
GLA_TAU = 16.0
GLA_CHUNK = 64

NSA_DH = 128
NSA_HEADS = 8
NSA_GROUPS = 2
NSA_HPG = 4
CMP_STRIDE = 16
CMP_BLOCK = 32
CMP_HIDDEN = 256
N_CMP_PAD = SEQ // CMP_STRIDE
SEL_BLOCK = 64
N_BLK = SEQ // SEL_BLOCK
SEL_TOPK = 16
WINDOW = 512
ROPE_DIM = 32
ROPE_THETA = 500000.0
Q_SCALE_LOG2 = NSA_DH ** -0.5 * 1.4426950408889634

N_EXPERTS = 32
TOP_K = 4
D_FF = D_MODEL
SWIGLU_LIMIT = 7.0
SWIGLU_ALPHA = 1.702
DN_ALPHA = 2.0 ** 0.25
LN_EPS = 1e-5

COL_GQ, COL_GK, COL_GV, COL_GR, COL_NQ = 0, 512, 1024, 2048, 3072
COL_KV = 4096
COL_TAIL = 5632
D_IN_PAD = 6144
GATE_LANE0 = GLA_RANK

LANE = 128
ROW_CHUNKS = D_MODEL // LANE
NEG = -1e30
VMEM_LIMIT = 56 * 1024 * 1024

TM_IN, TN_IN = 1024, 1024
TT_GLA = 512
TR_PREP = 512
TQ_NSA = 256
TK_SEL = 512
WIN_SPAN = WINDOW + TQ_NSA
ONES_ROWS = 16
TM_OUT = 256
TM_MOE, TN_MOE = 256, 1024
P_ROWS = N_TOK * TOP_K + N_EXPERTS * TM_MOE
N_MTILES = P_ROWS // TM_MOE
TM_FIN = 256
DMA_BATCH = 8


def _dot(a, b):
    return jnp.dot(a, b, preferred_element_type=F32)


def _dot_nt(a, b):
    return lax.dot_general(a, b, (((1,), (1,)), ((), ())), preferred_element_type=F32)


def _dot_tn(a, b):
    return lax.dot_general(a, b, (((0,), (0,)), ((), ())), preferred_element_type=F32)


def _ln(x):
    xc = x - jnp.mean(x, -1, keepdims=True)
    return xc * lax.rsqrt(jnp.mean(xc * xc, -1, keepdims=True) + LN_EPS)


def _split_bf16(x):
    hi = x.astype(BF16)
    lo = (x - hi.astype(F32)).astype(BF16)
    return hi, lo


def _params(*sem):
    return pltpu.CompilerParams(dimension_semantics=sem, vmem_limit_bytes=VMEM_LIMIT)


def _adaln_kernel(c_ref, w_ref, b_ref, o_ref):
    c = c_ref[...]
    a = (c * jax.nn.sigmoid(c)).astype(BF16)
    o_ref[...] = _dot(a, w_ref[...].astype(BF16)) + b_ref[...]


def _adaln(c8, w, b):
    n = w.shape[1]
    tn = 1024
    return pl.pallas_call(
        _adaln_kernel,
        grid=(n // tn,),
        in_specs=[pl.BlockSpec((8, D_MODEL), lambda j: (0, 0)),
                  pl.BlockSpec((D_MODEL, tn), lambda j: (0, j)),
                  pl.BlockSpec((1, tn), lambda j: (0, j))],
        out_specs=pl.BlockSpec((8, tn), lambda j: (0, j)),
        out_shape=jax.ShapeDtypeStruct((8, n), F32),
        compiler_params=_params("arbitrary"),
        name="adaln",
    )(c8, w, b)


def _inproj_kernel(x_ref, sh_ref, sc_ref, w_ref, o_ref, h_sc):
    @pl.when(pl.program_id(1) == 0)
    def _():
        h = _ln(x_ref[...]) * (1.0 + sc_ref[...]) + sh_ref[...]
        h_sc[...] = h.astype(BF16)

    o_ref[...] = _dot(h_sc[...], w_ref[...])


def _inproj(xf, mod3, w_in_p):
    tiles_per_batch = SEQ // TM_IN
    return pl.pallas_call(
        _inproj_kernel,
        grid=(N_TOK // TM_IN, D_IN_PAD // TN_IN),
        in_specs=[pl.BlockSpec((TM_IN, D_MODEL), lambda i, j: (i, 0)),
                  pl.BlockSpec((None, 1, D_MODEL), lambda i, j: (i // tiles_per_batch, 0, 0)),
                  pl.BlockSpec((None, 1, D_MODEL), lambda i, j: (i // tiles_per_batch, 0, 1)),
                  pl.BlockSpec((D_MODEL, TN_IN), lambda i, j: (0, j))],
        out_specs=pl.BlockSpec((TM_IN, TN_IN), lambda i, j: (i, j)),
        out_shape=jax.ShapeDtypeStruct((N_TOK, D_IN_PAD), F32),
        scratch_shapes=[pltpu.VMEM((TM_IN, D_MODEL), BF16)],
        compiler_params=_params("parallel", "arbitrary"),
        name="inproj",
    )(xf, mod3, mod3, w_in_p)


GLA_HALVES = (32, 16, 8, 4, 2, 1)
N_LEVELS = len(GLA_HALVES)
ROW_EB = 2 * N_LEVELS
ROW_EL = 2 * N_LEVELS + 1
N_EVIEWS = 2 * N_LEVELS + 2


def _gla_constants():
    c = GLA_CHUNK
    t = np.arange(c)[:, None]
    r = np.arange(c)[None, :]
    mall = np.zeros((N_EVIEWS, c, c), np.float32)
    valid = np.zeros((N_EVIEWS, c, LANE), np.float32)
    masks = np.zeros((N_LEVELS + 1, c, c), np.float32)
    for li, n in enumerate(GLA_HALVES):
        same = (t // (2 * n)) == (r // (2 * n))
        t_up = (t % (2 * n)) >= n
        r_up = (r % (2 * n)) >= n
        mall[2 * li] = same & t_up & r_up & (r <= t)
        mall[2 * li + 1] = same & ~t_up & ~r_up & (r > t)
        valid[2 * li] = np.broadcast_to(t_up, (c, LANE))
        valid[2 * li + 1] = np.broadcast_to(~t_up, (c, LANE))
        masks[li] = same & t_up & ~r_up
    mall[ROW_EB] = r <= t
    mall[ROW_EL] = r > t
    valid[ROW_EB] = 1.0
    valid[ROW_EL] = 1.0
    masks[N_LEVELS] = np.eye(c)
    return (mall.reshape(N_EVIEWS * c, c), valid.reshape(N_EVIEWS * c, LANE), masks)


def _gla_kernel(q_ref, k_ref, v_ref, r_ref, glr_ref, wgk_ref, bgk_ref, g_ref, mall_ref, valid_ref,
                masks_ref, o_ref, st_sc):
    c = GLA_CHUNK

    @pl.when(pl.program_id(1) == 0)
    def _():
        st_sc[...] = jnp.zeros_like(st_sc)

    z = _dot(glr_ref[...].astype(BF16), wgk_ref[...].astype(BF16)) + bgk_ref[...]
    log_a = (jnp.minimum(z, 0.0) - jnp.log1p(jnp.exp(-jnp.abs(z)))) * (1.0 / GLA_TAU)
    mall = mall_ref[...]
    valid = valid_ref[...]
    for ci in range(TT_GLA // c):
        rows = slice(ci * c, (ci + 1) * c)
        la_hi, la_lo = _split_bf16(log_a[rows])
        e_all = jnp.exp(_dot(mall, la_hi) + _dot(mall, la_lo))
        for h in range(GLA_HEADS):
            kcols = slice(h * GLA_DK, (h + 1) * GLA_DK)
            vcols = slice(h * GLA_DV, (h + 1) * GLA_DV)
            e = e_all[:, kcols] * valid
            q = q_ref[rows, kcols] * (GLA_DK ** -0.5)
            k = k_ref[rows, kcols]
            vb = v_ref[rows, vcols].astype(BF16)
            att = masks_ref[N_LEVELS] * _dot_nt(q.astype(BF16), k.astype(BF16))
            for li in range(N_LEVELS):
                eq = e[(2 * li) * c:(2 * li + 1) * c]
                ek = e[(2 * li + 1) * c:(2 * li + 2) * c]
                att = att + masks_ref[li] * _dot_nt((q * eq).astype(BF16), (k * ek).astype(BF16))
            eb = e[ROW_EB * c:(ROW_EB + 1) * c]
            el = e[ROW_EL * c:(ROW_EL + 1) * c]
            st = st_sc[h]
            o = _dot_nt((q * eb).astype(BF16), st.astype(BF16)) + _dot(att.astype(BF16), vb)
            st_sc[h] = st * eb[c - 1:c, :] + _dot_tn(vb, (k * el).astype(BF16))
            o = o * lax.rsqrt(jnp.mean(o * o, -1, keepdims=True) + LN_EPS) * g_ref[...]
            r = r_ref[rows, vcols]
            o_ref[rows, vcols] = (o * (r * jax.nn.sigmoid(r))).astype(BF16)


def _gla(proj, w_gk_pad, b_gk, norm_g):
    nt = SEQ // TT_GLA
    mall, valid, masks = _gla_constants()
    kw = GLA_HEADS * GLA_DK
    vw = GLA_HEADS * GLA_DV
    full = lambda shape: pl.BlockSpec(shape, lambda b, i: tuple(0 for _ in shape))
    cols = lambda width, col0: pl.BlockSpec((TT_GLA, width), lambda b, i: (b * nt + i, col0 // width))
    return pl.pallas_call(
        _gla_kernel,
        grid=(BATCH, nt),
        in_specs=[cols(kw, COL_GQ), cols(kw, COL_GK), cols(vw, COL_GV), cols(vw, COL_GR),
                  cols(LANE, COL_TAIL), full(w_gk_pad.shape), full(b_gk.shape), full(norm_g.shape),
                  full(mall.shape), full(valid.shape), full(masks.shape)],
        out_specs=cols(vw, 0),
        out_shape=jax.ShapeDtypeStruct((N_TOK, vw), BF16),
        scratch_shapes=[pltpu.VMEM((GLA_HEADS, GLA_DV, GLA_DK), F32)],
        compiler_params=_params("parallel", "arbitrary"),
        name="gla",
    )(proj, proj, proj, proj, proj, w_gk_pad, b_gk, norm_g,
      jnp.asarray(mall, BF16), jnp.asarray(valid), jnp.asarray(masks))


def _rope_tables(pos):
    half = ROPE_DIM // 2
    inv_freq = ROPE_THETA ** (-jnp.arange(half, dtype=F32) * (2.0 / ROPE_DIM))
    ang = pos.astype(F32)[:, None] * inv_freq
    cos, sin = jnp.cos(ang), jnp.sin(ang)
    n = pos.shape[0]
    cosf = jnp.concatenate([cos, cos, jnp.ones((n, LANE - ROPE_DIM), F32)], -1)
    sinf = jnp.concatenate([-sin, sin, jnp.zeros((n, LANE - ROPE_DIM), F32)], -1)
    return cosf, sinf


def _rope(x, cosf, sinf):
    lane = lax.broadcasted_iota(jnp.int32, x.shape, 1)
    half = ROPE_DIM // 2
    swapped = jnp.where(lane < half, pltpu.roll(x, LANE - half, 1), pltpu.roll(x, half, 1))
    return x * cosf + swapped * sinf


def _prep_kernel(q_ref, ks_ref, vs_ref, kw_ref, vw_ref, cos_ref, sin_ref,
                 qo_ref, kso_ref, vso_ref, kwo_ref, vwo_ref):
    cosf = cos_ref[...]
    sinf = sin_ref[...]
    for hh in range(NSA_HEADS):
        cols = slice(hh * NSA_DH, (hh + 1) * NSA_DH)
        qo_ref[:, cols] = (_rope(q_ref[:, cols], cosf, sinf) * Q_SCALE_LOG2).astype(BF16)
    for g in range(NSA_GROUPS):
        cols = slice(g * NSA_DH, (g + 1) * NSA_DH)
        kso_ref[g] = _rope(ks_ref[:, cols], cosf, sinf).astype(BF16)
        kwo_ref[g] = _rope(kw_ref[:, cols], cosf, sinf).astype(BF16)
        vso_ref[g] = vs_ref[:, cols].T.astype(BF16)
        vwo_ref[g] = vw_ref[:, cols].T.astype(BF16)


def _prep(proj, cosf, sinf):
    nt = SEQ // TR_PREP
    kvw = NSA_GROUPS * NSA_DH
    kv_in = lambda which: pl.BlockSpec((TR_PREP, kvw), lambda b, i: (b * nt + i, COL_KV // kvw + which))
    k_out = pl.BlockSpec((None, NSA_GROUPS, TR_PREP, NSA_DH), lambda b, i: (b, 0, i, 0))
    k_shape = jax.ShapeDtypeStruct((BATCH, NSA_GROUPS, SEQ, NSA_DH), BF16)
    vt_out = pl.BlockSpec((None, NSA_GROUPS, NSA_DH, TR_PREP), lambda b, i: (b, 0, 0, i))
    vt_shape = jax.ShapeDtypeStruct((BATCH, NSA_GROUPS, NSA_DH, SEQ), BF16)
    qw = NSA_HEADS * NSA_DH
    return pl.pallas_call(
        _prep_kernel,
        grid=(BATCH, nt),
        in_specs=[pl.BlockSpec((TR_PREP, qw), lambda b, i: (b * nt + i, COL_NQ // qw)),
                  kv_in(2), kv_in(3), kv_in(4), kv_in(5),
                  pl.BlockSpec((TR_PREP, LANE), lambda b, i: (i, 0)),
                  pl.BlockSpec((TR_PREP, LANE), lambda b, i: (i, 0))],
        out_specs=[pl.BlockSpec((TR_PREP, qw), lambda b, i: (b * nt + i, 0)),
                   k_out, vt_out, k_out, vt_out],
        out_shape=[jax.ShapeDtypeStruct((N_TOK, qw), BF16), k_shape, vt_shape, k_shape, vt_shape],
        compiler_params=_params("parallel", "parallel"),
        name="nsa_prep",
    )(proj, proj, proj, proj, proj, cosf, sinf)


def _compress_kernel(a_ref, pe_ref, w1_ref, b1_ref, w2_ref, b2_ref, cos_ref, sin_ref, o_ref, bot_sc,
                     *, rope):
    n = N_CMP_PAD
    top = jnp.zeros((n, CMP_HIDDEN), F32)
    bot = jnp.zeros((n, CMP_HIDDEN), F32)
    for p in range(CMP_STRIDE):
        ap = a_ref[pl.ds(p, n, stride=CMP_STRIDE), :]
        w_top = w1_ref[p * NSA_DH:(p + 1) * NSA_DH, :].astype(BF16)
        w_bot = w1_ref[(CMP_STRIDE + p) * NSA_DH:(CMP_STRIDE + p + 1) * NSA_DH, :].astype(BF16)
        top = top + _dot((ap + pe_ref[p:p + 1, :]).astype(BF16), w_top)
        bot = bot + _dot((ap + pe_ref[CMP_STRIDE + p:CMP_STRIDE + p + 1, :]).astype(BF16), w_bot)
    bot_sc[0:n, :] = bot
    bot_sc[n:n + 8, :] = jnp.zeros((8, CMP_HIDDEN), F32)
    h = top + bot_sc[1:n + 1, :] + b1_ref[...]
    h = h * jax.nn.sigmoid(h)
    out = _dot(h.astype(BF16), w2_ref[...].astype(BF16)) + b2_ref[...]
    if rope:
        out = _rope(out, cos_ref[...], sin_ref[...])
    row = lax.broadcasted_iota(jnp.int32, out.shape, 0)
    out = jnp.where(row < n - 1, out, 0.0)
    o_ref[...] = (out if rope else out.T).astype(BF16)


def _compress(proj, which, pe, w1, b1, w2, b2, cosf, sinf, rope):
    col0 = (COL_KV + which * NSA_GROUPS * NSA_DH) // NSA_DH
    out_dims = (N_CMP_PAD, NSA_DH) if rope else (NSA_DH, N_CMP_PAD)
    full = lambda shape: pl.BlockSpec(shape, lambda b, g: tuple(0 for _ in shape))
    return pl.pallas_call(
        functools.partial(_compress_kernel, rope=rope),
        grid=(BATCH, NSA_GROUPS),
        in_specs=[pl.BlockSpec((SEQ, NSA_DH), lambda b, g: (b, col0 + g)),
                  full(pe.shape), full(w1.shape), full(b1.shape), full(w2.shape), full(b2.shape),
                  full(cosf.shape), full(sinf.shape)],
        out_specs=pl.BlockSpec((None, None) + out_dims, lambda b, g: (b, g, 0, 0)),
        out_shape=jax.ShapeDtypeStruct((BATCH, NSA_GROUPS) + out_dims, BF16),
        scratch_shapes=[pltpu.VMEM((N_CMP_PAD + 8, CMP_HIDDEN), F32)],
        compiler_params=_params("parallel", "parallel"),
        name="nsa_compress",
    )(proj, pe, w1, b1, w2, b2, cosf, sinf)


def _nsa_constants():
    c_start = np.arange(N_CMP_PAD) * CMP_STRIDE
    b_start = np.arange(N_BLK) * SEL_BLOCK
    overlap_t = ((c_start[None, :] < b_start[:, None] + SEL_BLOCK)
                 & (c_start[None, :] + CMP_BLOCK > b_start[:, None])).astype(np.float32)
    overlap_t[:, N_CMP_PAD - 1] = 0.0
    expand_t = (np.arange(SEQ)[:, None] // SEL_BLOCK == np.arange(LANE)[None, :]).astype(np.float32)
    return overlap_t, expand_t


def _tile_heads(a):
    return jnp.concatenate([a] * NSA_HPG, axis=1)


def _nsa_kernel(q_ref, kc_ref, vc_ref, ks_ref, vs_ref, kw_ref, vw_ref, gate_ref, ovt_ref, expand_ref,
                o_ref, score_sc, m_sc, acc_sc):
    tq = TQ_NSA
    g = pl.program_id(1)
    q0 = pl.program_id(2) * tq
    qs = jnp.concatenate([q_ref[:, hh * NSA_DH:(hh + 1) * NSA_DH] for hh in range(NSA_HPG)], axis=0)
    pos_q = q0 + lax.broadcasted_iota(jnp.int32, (1, tq), 1)

    def with_ones(vt):
        return jnp.concatenate([vt, jnp.ones((ONES_ROWS, vt.shape[1]), BF16)], axis=0)

    def normalise(acc):
        return acc[0:NSA_DH] / acc[NSA_DH:NSA_DH + 1]

    s = _dot_nt(kc_ref[...], qs)
    cmp_end = lax.broadcasted_iota(jnp.int32, (N_CMP_PAD, tq), 0) * CMP_STRIDE + (CMP_BLOCK - 1)
    s = s + _tile_heads(jnp.where(cmp_end <= pos_q, 0.0, NEG))
    m = jnp.max(s, 0, keepdims=True)
    e = jnp.where(s > 0.5 * NEG, jnp.exp2(s - m), 0.0)
    p = e / jnp.maximum(jnp.sum(e, 0, keepdims=True), 1e-30)
    o_cmp = _dot(vc_ref[...], p.astype(BF16))
    p_sum = p[:, 0:tq] + p[:, tq:2 * tq] + p[:, 2 * tq:3 * tq] + p[:, 3 * tq:4 * tq]
    ps_hi, ps_lo = _split_bf16(p_sum)
    ovt = ovt_ref[...]
    p_blk_t = _dot(ovt, ps_hi) + _dot(ovt, ps_lo)

    jj = lax.broadcasted_iota(jnp.int32, (N_BLK, tq), 0)
    cur = (q0 + lax.broadcasted_iota(jnp.int32, (N_BLK, tq), 1)) // SEL_BLOCK
    forced = (jj == 0) | (jj == cur) | (jj == cur - 1)
    allowed = jj <= cur
    score = jnp.where(forced, 3.0e38, jnp.where(allowed, p_blk_t, -1.0))
    score_sc[...] = score

    def rank_body(i, rank):
        row = score_sc[pl.ds(i, 1), :]
        first = jnp.where(jj > i, 1.0, 0.0)
        return rank + jnp.where(row > score, 1.0, jnp.where(row == score, first, 0.0))

    n_live = (q0 + tq - 1) // SEL_BLOCK + 1
    rank = lax.fori_loop(0, n_live, rank_body, jnp.zeros((N_BLK, tq), F32))
    sel_t = jnp.where(allowed, jnp.where(rank < SEL_TOPK, 1.0, 0.0), 0.0)
    sel = jnp.concatenate([sel_t, jnp.zeros((LANE - N_BLK, tq), F32)], axis=0).astype(BF16)

    m_sc[...] = jnp.full(m_sc.shape, NEG, F32)
    acc_sc[...] = jnp.zeros(acc_sc.shape, F32)

    def sel_body(kt, carry):
        k0 = pl.multiple_of(kt * TK_SEL, TK_SEL)
        s = _dot_nt(ks_ref[pl.ds(k0, TK_SEL), :], qs)
        sel_x = _dot(expand_ref[pl.ds(k0, TK_SEL), :], sel)
        kpos = k0 + lax.broadcasted_iota(jnp.int32, (TK_SEL, tq), 0)
        bias = jnp.where(kpos <= pos_q, jnp.where(sel_x > 0.5, 0.0, NEG), NEG)
        s = s + _tile_heads(bias)
        m_prev = m_sc[...]
        m_new = jnp.maximum(m_prev, jnp.max(s, 0, keepdims=True))
        alpha = jnp.exp2(m_prev - m_new)
        p = jnp.exp2(s - m_new).astype(BF16)
        acc_sc[...] = alpha * acc_sc[...] + _dot(with_ones(vs_ref[:, pl.ds(k0, TK_SEL)]), p)
        m_sc[...] = m_new
        return carry

    lax.fori_loop(0, (q0 + tq - 1) // TK_SEL + 1, sel_body, 0)
    o_slc = normalise(acc_sc[...])

    start = pl.multiple_of(jnp.maximum(q0 - WINDOW, 0), LANE)
    s = _dot_nt(kw_ref[pl.ds(start, WIN_SPAN), :], qs)
    dist = pos_q - (start + lax.broadcasted_iota(jnp.int32, (WIN_SPAN, tq), 0))
    bias = jnp.where(dist >= 0, jnp.where(dist < WINDOW, 0.0, NEG), NEG)
    s = s + _tile_heads(bias)
    p = jnp.exp2(s - jnp.max(s, 0, keepdims=True)).astype(BF16)
    o_win = normalise(_dot(with_ones(vw_ref[:, pl.ds(start, WIN_SPAN)]), p))

    gates = jax.nn.sigmoid(gate_ref[...]).T

    def gate(hh, branch):
        lane0 = GATE_LANE0 + hh * 3 + branch
        lane1 = lane0 + NSA_HPG * 3
        return jnp.where(g == 0, gates[lane0:lane0 + 1, :], gates[lane1:lane1 + 1, :])

    for hh in range(NSA_HPG):
        c = slice(hh * tq, (hh + 1) * tq)
        o = gate(hh, 0) * o_cmp[:, c] + gate(hh, 1) * o_slc[:, c] + gate(hh, 2) * o_win[:, c]
        o_ref[:, hh * NSA_DH:(hh + 1) * NSA_DH] = o.T.astype(BF16)


def _nsa(q_r, kc, vc, ks, vs, kw, vw, proj):
    nq = SEQ // TQ_NSA
    gw = NSA_HPG * NSA_DH
    overlap_t, expand = _nsa_constants()
    cols = NSA_HPG * TQ_NSA
    per_group = lambda d0, d1: pl.BlockSpec((None, None, d0, d1), lambda b, g, i: (b, g, 0, 0))
    return pl.pallas_call(
        _nsa_kernel,
        grid=(BATCH, NSA_GROUPS, nq),
        in_specs=[pl.BlockSpec((TQ_NSA, gw), lambda b, g, i: (b * nq + i, g)),
                  per_group(N_CMP_PAD, NSA_DH), per_group(NSA_DH, N_CMP_PAD),
                  per_group(SEQ, NSA_DH), per_group(NSA_DH, SEQ),
                  per_group(SEQ, NSA_DH), per_group(NSA_DH, SEQ),
                  pl.BlockSpec((TQ_NSA, LANE), lambda b, g, i: (b * nq + i, COL_TAIL // LANE)),
                  pl.BlockSpec(overlap_t.shape, lambda b, g, i: (0, 0)),
                  pl.BlockSpec(expand.shape, lambda b, g, i: (0, 0))],
        out_specs=pl.BlockSpec((TQ_NSA, gw), lambda b, g, i: (b * nq + i, g)),
        out_shape=jax.ShapeDtypeStruct((N_TOK, NSA_GROUPS * gw), BF16),
        scratch_shapes=[pltpu.VMEM((N_BLK, TQ_NSA), F32),
                        pltpu.VMEM((1, cols), F32),
                        pltpu.VMEM((NSA_DH + ONES_ROWS, cols), F32)],
        compiler_params=_params("parallel", "parallel", "arbitrary"),
        name="nsa_attn",
    )(q_r, kc, vc, ks, vs, kw, vw, proj, jnp.asarray(overlap_t, BF16), jnp.asarray(expand, BF16))


def _outproj_kernel(yg_ref, yn_ref, x_ref, wo_ref, g1_ref, sc2_ref, sh2_ref, ln1g_ref, ln1b_ref,
                    wr_hi_ref, wr_lo_ref, br_ref, ltri_ref, x1_ref, h2_ref, idx_ref, wt_ref, rank_ref,
                    cnt_ref, base_sc):
    half = D_MODEL // 2

    @pl.when(pl.program_id(0) == 0)
    def _():
        base_sc[...] = jnp.zeros_like(base_sc)

    mix = _dot(yg_ref[...], wo_ref[0:half, :]) + _dot(yn_ref[...], wo_ref[half:D_MODEL, :])
    x1 = _ln(DN_ALPHA * x_ref[...] + (1.0 + g1_ref[...]) * mix) * ln1g_ref[...] + ln1b_ref[...]
    x1_ref[...] = x1
    h2 = _ln(x1) * (1.0 + sc2_ref[...]) + sh2_ref[...]
    h_hi, h_lo = _split_bf16(h2)
    for s in range(ROW_CHUNKS):
        h2_ref[pl.ds(s, TM_OUT, stride=ROW_CHUNKS), :] = h2[:, s * LANE:(s + 1) * LANE]
    wr_hi = wr_hi_ref[...]
    logits = _dot(h_hi, wr_hi) + _dot(h_lo, wr_hi) + _dot(h_hi, wr_lo_ref[...]) + br_ref[...]

    lane = lax.broadcasted_iota(jnp.int32, logits.shape, 1)
    lane_f = lane.astype(F32)
    vals = logits
    idx_out = jnp.zeros(logits.shape, jnp.int32)
    exp_out = jnp.zeros(logits.shape, F32)
    denom = jnp.zeros((logits.shape[0], 1), F32)
    m0 = None
    onehots = []
    for k in range(TOP_K):
        mk = jnp.max(vals, -1, keepdims=True)
        ik = jnp.min(jnp.where(vals == mk, lane_f, float(LANE)), -1, keepdims=True)
        if k == 0:
            m0 = mk
        ek = jnp.exp(mk - m0)
        denom = denom + ek
        idx_out = jnp.where(lane == k, ik.astype(jnp.int32), idx_out)
        exp_out = jnp.where(lane == k, ek, exp_out)
        hit = lane_f == ik
        onehots.append(jnp.where(hit, 1.0, 0.0))
        vals = jnp.where(hit, -3.0e38, vals)
    idx_ref[...] = idx_out
    wt_ref[...] = exp_out / denom

    cnt = onehots[0] + onehots[1] + onehots[2] + onehots[3]
    base = base_sc[0:1, :]
    before = _dot(ltri_ref[...], cnt.astype(BF16)) + base
    rank_out = jnp.zeros(logits.shape, jnp.int32)
    for k in range(TOP_K):
        rk = jnp.sum(onehots[k] * before, -1, keepdims=True)
        rank_out = jnp.where(lane == k, rk.astype(jnp.int32), rank_out)
    rank_ref[...] = rank_out
    base_sc[...] = jnp.broadcast_to(base + jnp.sum(cnt, 0, keepdims=True), base_sc.shape)
    cnt_ref[...] = base_sc[...]


def _outproj(y_gla, y_nsa, xf, w_o, mod3, ln1_g, ln1_b, wr_hi, wr_lo, br):
    tiles_per_batch = SEQ // TM_OUT
    half = D_MODEL // 2
    mod_spec = lambda chunk: pl.BlockSpec((None, 1, D_MODEL), lambda i: (i // tiles_per_batch, 0, chunk))
    full = lambda shape: pl.BlockSpec(shape, lambda i: tuple(0 for _ in shape))
    row = lambda width: pl.BlockSpec((TM_OUT, width), lambda i: (i, 0))
    ltri = jnp.asarray(np.tril(np.ones((TM_OUT, TM_OUT), np.float32), -1), BF16)
    return pl.pallas_call(
        _outproj_kernel,
        grid=(N_TOK // TM_OUT,),
        in_specs=[row(half), row(half), row(D_MODEL), full(w_o.shape),
                  mod_spec(2), mod_spec(4), mod_spec(3),
                  full(ln1_g.shape), full(ln1_b.shape), full(wr_hi.shape), full(wr_lo.shape), full(br.shape),
                  full(ltri.shape)],
        out_specs=[row(D_MODEL), pl.BlockSpec((TM_OUT * ROW_CHUNKS, LANE), lambda i: (i, 0)),
                   row(LANE), row(LANE), row(LANE), full((8, LANE))],
        out_shape=[jax.ShapeDtypeStruct((N_TOK, D_MODEL), F32),
                   jax.ShapeDtypeStruct((N_TOK * ROW_CHUNKS, LANE), F32),
                   jax.ShapeDtypeStruct((N_TOK, LANE), jnp.int32),
                   jax.ShapeDtypeStruct((N_TOK, LANE), F32),
                   jax.ShapeDtypeStruct((N_TOK, LANE), jnp.int32),
                   jax.ShapeDtypeStruct((8, LANE), F32)],
        scratch_shapes=[pltpu.VMEM((8, LANE), F32)],
        compiler_params=_params("arbitrary"),
        name="outproj_router",
    )(y_gla, y_nsa, xf, w_o, mod3, mod3, mod3, ln1_g, ln1_b, wr_hi, wr_lo, br, ltri)


def _invert_kernel(dest_ref, src_sm):
    def fill(r, carry):
        src_sm[r] = 0
        return carry

    lax.fori_loop(0, P_ROWS, fill, 0, unroll=8)

    def scatter(tok2, carry):
        a0 = tok2 * (2 * TOP_K)
        rows = [dest_ref[a0 + i] for i in range(2 * TOP_K)]
        for i, row in enumerate(rows):
            src_sm[row] = tok2 * 2 + i // TOP_K
        return carry

    lax.fori_loop(0, N_TOK // 2, scatter, 0)


def _invert(dest_flat):
    smem = pl.BlockSpec(memory_space=pltpu.SMEM)
    return pl.pallas_call(
        _invert_kernel,
        in_specs=[smem],
        out_specs=smem,
        out_shape=jax.ShapeDtypeStruct((P_ROWS,), jnp.int32),
        name="moe_invert",
    )(dest_flat)


def _dispatch_kernel(src_sm, nused_ref, h2_hbm, xs_ref, buf, sems):
    t = pl.program_id(0)
    n_used = nused_ref[0]

    def start_gather(tile, slot):
        base = tile * TM_MOE

        def body(r8, carry):
            r0 = r8 * DMA_BATCH
            toks = [src_sm[base + r0 + i] for i in range(DMA_BATCH)]
            for i, tok in enumerate(toks):
                pltpu.make_async_copy(h2_hbm.at[pl.ds(pl.multiple_of(tok * ROW_CHUNKS, ROW_CHUNKS), ROW_CHUNKS), :],
                                      buf.at[slot, pl.ds(pl.multiple_of((r0 + i) * ROW_CHUNKS, ROW_CHUNKS),
                                                         ROW_CHUNKS), :],
                                      sems.at[slot]).start(priority=i % 2)
            return carry

        lax.fori_loop(0, TM_MOE // DMA_BATCH, body, 0)

    @pl.when(t == 0)
    def _():
        start_gather(0, 0)

    @pl.when(t + 1 < n_used)
    def _():
        start_gather(t + 1, (t + 1) % 2)

    @pl.when(t < n_used)
    def _():
        slot = t % 2
        pltpu.make_async_copy(buf.at[slot], buf.at[slot], sems.at[slot]).wait()
        for s in range(ROW_CHUNKS):
            xs_ref[:, s * LANE:(s + 1) * LANE] = buf[slot, pl.ds(s, TM_MOE, stride=ROW_CHUNKS), :].astype(BF16)

    @pl.when(t >= n_used)
    def _():
        xs_ref[...] = jnp.zeros(xs_ref.shape, BF16)


def _dispatch(src_tok, n_used, h2):
    return pl.pallas_call(
        _dispatch_kernel,
        grid_spec=pltpu.PrefetchScalarGridSpec(
            num_scalar_prefetch=2,
            grid=(N_MTILES,),
            in_specs=[pl.BlockSpec(memory_space=pl.ANY)],
            out_specs=pl.BlockSpec((TM_MOE, D_MODEL), lambda t, src, nu: (t, 0)),
            scratch_shapes=[pltpu.VMEM((2, TM_MOE * ROW_CHUNKS, LANE), F32),
                            pltpu.SemaphoreType.DMA((2,))]),
        out_shape=jax.ShapeDtypeStruct((P_ROWS, D_MODEL), BF16),
        compiler_params=_params("arbitrary"),
        name="moe_dispatch",
    )(src_tok, n_used, h2)


class _ExpertWeights:
    def __init__(self, eid_ref, n_used, w_hbms, bufs, sems, cnt_sm):
        self.eid_ref, self.n_used = eid_ref, n_used
        self.w_hbms, self.bufs, self.sems, self.cnt_sm = w_hbms, bufs, sems, cnt_sm

    def _copies(self, expert, j, slot):
        cols = pl.ds(pl.multiple_of(j * TN_MOE, TN_MOE), TN_MOE)
        return [pltpu.make_async_copy(w.at[expert, :, cols], buf.at[slot], self.sems.at[i, slot])
                for i, (w, buf) in enumerate(zip(self.w_hbms, self.bufs))]

    def prologue(self):
        self.cnt_sm[0] = 0
        for cp in self._copies(self.eid_ref[0], 0, 0):
            cp.start()

    def acquire(self, j, t, n_passes):
        slot = self.cnt_sm[0] % 2
        for cp in self._copies(self.eid_ref[t], j, slot):
            cp.wait()
        expert = self.eid_ref[t]
        nxt = lax.while_loop(
            lambda u: (u < self.n_used) & (self.eid_ref[jnp.minimum(u, N_MTILES - 1)] == expert),
            lambda u: u + 1, t + 1)
        in_pass = nxt < self.n_used

        @pl.when(in_pass | (j + 1 < n_passes))
        def _():
            for cp in self._copies(self.eid_ref[jnp.where(in_pass, nxt, 0)], jnp.where(in_pass, j, j + 1),
                                   1 - slot):
                cp.start()

        self.cnt_sm[0] = self.cnt_sm[0] + 1
        return slot


def _moe_up_kernel(eid_ref, nused_ref, x_ref, wg_hbm, wu_hbm, bg_ref, bu_ref, h_ref,
                   wg_buf, wu_buf, wg_sc, wu_sc, sems, cnt_sm):
    j = pl.program_id(0)
    t = pl.program_id(1)
    n_used = nused_ref[0]
    weights = _ExpertWeights(eid_ref, n_used, (wg_hbm, wu_hbm), (wg_buf, wu_buf), sems, cnt_sm)

    @pl.when((j == 0) & (t == 0))
    def _():
        weights.prologue()

    @pl.when(t < n_used)
    def _():
        @pl.when((t == 0) | (eid_ref[t] != eid_ref[jnp.maximum(t - 1, 0)]))
        def _():
            slot = weights.acquire(j, t, pl.num_programs(0))
            wg_sc[...] = wg_buf[slot].astype(BF16)
            wu_sc[...] = wu_buf[slot].astype(BF16)

        x = x_ref[...]
        gate = jnp.minimum(_dot(x, wg_sc[...]) + bg_ref[...], SWIGLU_LIMIT)
        up = jnp.clip(_dot(x, wu_sc[...]) + bu_ref[...], -SWIGLU_LIMIT, SWIGLU_LIMIT)
        h_ref[...] = (gate * jax.nn.sigmoid(SWIGLU_ALPHA * gate) * (up + 1.0)).astype(BF16)

    @pl.when(t >= n_used)
    def _():
        h_ref[...] = jnp.zeros(h_ref.shape, BF16)


def _moe_down_kernel(eid_ref, nused_ref, h_ref, wd_hbm, bd_ref, y_ref, wd_buf, wd_sc, sems, cnt_sm):
    j = pl.program_id(0)
    t = pl.program_id(1)
    n_used = nused_ref[0]
    weights = _ExpertWeights(eid_ref, n_used, (wd_hbm,), (wd_buf,), sems, cnt_sm)

    @pl.when((j == 0) & (t == 0))
    def _():
        weights.prologue()

    @pl.when(t < n_used)
    def _():
        @pl.when((t == 0) | (eid_ref[t] != eid_ref[jnp.maximum(t - 1, 0)]))
        def _():
            slot = weights.acquire(j, t, pl.num_programs(0))
            wd_sc[...] = wd_buf[slot].astype(BF16)

        y_ref[...] = _dot(h_ref[...], wd_sc[...]) + bd_ref[...]

    @pl.when(t >= n_used)
    def _():
        y_ref[...] = jnp.zeros(y_ref.shape, F32)


def _row_tile(t, nused_ref):
    return jnp.minimum(t, jnp.maximum(nused_ref[0] - 1, 0))


def _moe_call(body, name, n_weights, k_dim, out_dim, out_dtype, tile_eid, n_used, rows, weights, biases):
    any_spec = pl.BlockSpec(memory_space=pl.ANY)
    b_spec = pl.BlockSpec((None, 1, TN_MOE), lambda j, t, eid, nu: (eid[t], 0, j))
    return pl.pallas_call(
        body,
        grid_spec=pltpu.PrefetchScalarGridSpec(
            num_scalar_prefetch=2,
            grid=(out_dim // TN_MOE, N_MTILES),
            in_specs=[pl.BlockSpec((TM_MOE, k_dim), lambda j, t, eid, nu: (_row_tile(t, nu), 0))]
            + [any_spec] * n_weights + [b_spec] * n_weights,
            out_specs=pl.BlockSpec((TM_MOE, TN_MOE), lambda j, t, eid, nu: (t, j)),
            scratch_shapes=[pltpu.VMEM((2, k_dim, TN_MOE), F32)] * n_weights
            + [pltpu.VMEM((k_dim, TN_MOE), BF16)] * n_weights
            + [pltpu.SemaphoreType.DMA((n_weights, 2)), pltpu.SMEM((1,), jnp.int32)]),
        out_shape=jax.ShapeDtypeStruct((P_ROWS, out_dim), out_dtype),
        compiler_params=_params("arbitrary", "arbitrary"),
        name=name,
    )(tile_eid, n_used, rows, *weights, *biases)


def _moe_up(tile_eid, n_used, xs, w_gate, w_up, b_gate, b_up):
    return _moe_call(_moe_up_kernel, "moe_up", 2, D_MODEL, D_FF, BF16, tile_eid, n_used, xs,
                     (w_gate, w_up), (b_gate, b_up))


def _moe_down(tile_eid, n_used, h, w_down, b_down):
    return _moe_call(_moe_down_kernel, "moe_down", 1, D_FF, D_MODEL, F32, tile_eid, n_used, h,
                     (w_down,), (b_down,))


def _final_kernel(dest_ref, y_hbm, x1_ref, wt_ref, g2_ref, ln2g_ref, ln2b_ref, o_ref, ybuf, sems):
    i = pl.program_id(0)
    n_tiles = pl.num_programs(0)

    def start_gather(tile, slot):
        base = tile * (TM_FIN * TOP_K)

        def body(r2, carry):
            a0 = base + r2 * DMA_BATCH
            rows = [dest_ref[a0 + i] for i in range(DMA_BATCH)]
            for i, d in enumerate(rows):
                r = r2 * (DMA_BATCH // TOP_K) + i // TOP_K
                pltpu.make_async_copy(y_hbm.at[pl.ds(d, 1), :], ybuf.at[slot, i % TOP_K, pl.ds(r, 1), :],
                                      sems.at[slot]).start(priority=i % 2)
            return carry

        lax.fori_loop(0, TM_FIN * TOP_K // DMA_BATCH, body, 0)

    @pl.when(i == 0)
    def _():
        start_gather(0, 0)

    @pl.when(i + 1 < n_tiles)
    def _():
        start_gather(i + 1, (i + 1) % 2)

    slot = i % 2
    pltpu.make_async_copy(ybuf.at[slot], ybuf.at[slot], sems.at[slot]).wait()
    wt = wt_ref[...]
    y = wt[:, 0:1] * ybuf[slot, 0]
    for k in range(1, TOP_K):
        y = y + wt[:, k:k + 1] * ybuf[slot, k]
    o_ref[...] = _ln(DN_ALPHA * x1_ref[...] + (1.0 + g2_ref[...]) * y) * ln2g_ref[...] + ln2b_ref[...]


def _final(dest_flat, y, x1, wt, mod3, ln2_g, ln2_b):
    tiles_per_batch = SEQ // TM_FIN
    full = lambda shape: pl.BlockSpec(shape, lambda i, dest: tuple(0 for _ in shape))
    row = lambda width: pl.BlockSpec((TM_FIN, width), lambda i, dest: (i, 0))
    return pl.pallas_call(
        _final_kernel,
        grid_spec=pltpu.PrefetchScalarGridSpec(
            num_scalar_prefetch=1,
            grid=(N_TOK // TM_FIN,),
            in_specs=[pl.BlockSpec(memory_space=pl.ANY), row(D_MODEL), row(LANE),
                      pl.BlockSpec((None, 1, D_MODEL), lambda i, dest: (i // tiles_per_batch, 0, 5)),
                      full(ln2_g.shape), full(ln2_b.shape)],
            out_specs=row(D_MODEL),
            scratch_shapes=[pltpu.VMEM((2, TOP_K, TM_FIN, D_MODEL), F32),
                            pltpu.SemaphoreType.DMA((2,))]),
        out_shape=jax.ShapeDtypeStruct((N_TOK, D_MODEL), F32),
        compiler_params=_params("arbitrary"),
        name="combine_ln2",
    )(dest_flat, y, x1, wt, mod3, ln2_g, ln2_b)


def _route(idx, rank, counts):
    experts = jnp.arange(N_EXPERTS, dtype=jnp.int32)
    padded = ((counts + TM_MOE - 1) // TM_MOE) * TM_MOE
    ends = jnp.sum(jnp.where(experts[None, :] <= experts[:, None], padded[None, :], 0), axis=1)
    starts = ends - padded
    dest = rank
    for e in range(N_EXPERTS):
        dest = dest + jnp.where(idx == e, starts[e], 0)
    dest = dest[:, :TOP_K]
    tile_start = jnp.arange(N_MTILES, dtype=jnp.int32) * TM_MOE
    tile_eid = jnp.minimum(jnp.sum((ends[None, :] <= tile_start[:, None]).astype(jnp.int32), axis=1),
                           N_EXPERTS - 1)
    n_used = (ends[-1] // TM_MOE).astype(jnp.int32).reshape(1)
    return dest.reshape(-1), tile_eid, n_used


def kernel(x, c, w_ada, b_ada, w_in, w_gk, b_gk, gla_norm_g, pe_k, pe_v, w_ck1, b_ck1, w_ck2, b_ck2,
           w_cv1, b_cv1, w_cv2, b_cv2, w_o, ln1_g, ln1_b, w_router, b_router, w_gate, b_gate, w_up, b_up,
           w_down, b_down, ln2_g, ln2_b):
    l = 0
    xf = x.reshape(N_TOK, D_MODEL)
    row2 = lambda a: a.reshape(1, -1)

    c8 = jnp.pad(c, ((0, 8 - BATCH), (0, 0)))
    mod3 = _adaln(c8, w_ada[l], row2(b_ada[l]))[:BATCH].reshape(BATCH, 1, 6 * D_MODEL)

    w = w_in[l]
    glr0 = 3072
    nsa0 = glr0 + GLA_RANK
    ngt0 = nsa0 + 1024 + 6 * 256
    w_in_p = jnp.concatenate(
        [w[:, :glr0], w[:, nsa0:ngt0], w[:, glr0:nsa0], w[:, ngt0:],
         jnp.zeros((D_MODEL, D_IN_PAD - w.shape[1]), F32)], axis=1).astype(BF16)
    proj = _inproj(xf, mod3, w_in_p)

    w_gk_pad = jnp.pad(w_gk[l], ((0, LANE - GLA_RANK), (0, 0)))
    y_gla = _gla(proj, w_gk_pad, row2(b_gk[l]), row2(gla_norm_g[l]))

    cos_t, sin_t = _rope_tables(jnp.arange(SEQ))
    cmp_end = jnp.arange(N_CMP_PAD) * CMP_STRIDE + (CMP_BLOCK - 1)
    cos_c, sin_c = _rope_tables(cmp_end)
    q_r, ks, vs, kw, vw = _prep(proj, cos_t, sin_t)
    kc = _compress(proj, 0, pe_k[l], w_ck1[l], row2(b_ck1[l]), w_ck2[l], row2(b_ck2[l]), cos_c, sin_c, True)
    vc = _compress(proj, 1, pe_v[l], w_cv1[l], row2(b_cv1[l]), w_cv2[l], row2(b_cv2[l]), cos_c, sin_c, False)
    y_nsa = _nsa(q_r, kc, vc, ks, vs, kw, vw, proj)

    wr = jnp.pad(w_router[l], ((0, 0), (0, LANE - N_EXPERTS)))
    wr_hi, wr_lo = _split_bf16(wr)
    br = jnp.concatenate([b_router[l], jnp.full((LANE - N_EXPERTS,), NEG, F32)]).reshape(1, LANE)
    x1, h2, idx128, wt128, rank128, cnt8 = _outproj(y_gla, y_nsa, xf, w_o[l].astype(BF16), mod3,
                                                    row2(ln1_g[l]), row2(ln1_b[l]), wr_hi, wr_lo, br)

    counts = cnt8[0, :N_EXPERTS].astype(jnp.int32)
    dest, tile_eid, n_used = _route(idx128, rank128, counts)
    xs = _dispatch(_invert(dest), n_used, h2)
    h = _moe_up(tile_eid, n_used, xs, w_gate[l], w_up[l],
                b_gate[l].reshape(N_EXPERTS, 1, D_FF), b_up[l].reshape(N_EXPERTS, 1, D_FF))
    y = _moe_down(tile_eid, n_used, h, w_down[l], b_down[l].reshape(N_EXPERTS, 1, D_MODEL))

    out = _final(dest, y, x1, wt128, mod3, row2(ln2_g[l]), row2(ln2_b[l]))
    return out.reshape(BATCH, SEQ, D_MODEL)
```

```python
import functools

import numpy as np
import jax
import jax.numpy as jnp
from jax import lax
from jax.experimental import pallas as pl
from jax.experimental.pallas import tpu as pltpu

F32 = jnp.float32
BF16 = jnp.bfloat16

D_MODEL = 2048
BATCH = 2
SEQ = 4096
N_TOK = BATCH * SEQ

GLA_HEADS = 4
GLA_DK = 128
GLA_DV = 256
GLA_RANK = 16
GLA_TAU = 16.0
GLA_CHUNK = 64

NSA_DH = 128
NSA_HEADS = 8
NSA_GROUPS = 2
NSA_HPG = 4
CMP_STRIDE = 16
CMP_BLOCK = 32
CMP_HIDDEN = 256
N_CMP_PAD = SEQ // CMP_STRIDE
SEL_BLOCK = 64
N_BLK = SEQ // SEL_BLOCK
SEL_TOPK = 16
WINDOW = 512
ROPE_DIM = 32
ROPE_THETA = 500000.0
Q_SCALE_LOG2 = NSA_DH ** -0.5 * 1.4426950408889634

N_EXPERTS = 32
TOP_K = 4
D_FF = D_MODEL
SWIGLU_LIMIT = 7.0
SWIGLU_ALPHA = 1.702
DN_ALPHA = 2.0 ** 0.25
LN_EPS = 1e-5

COL_GQ, COL_GK, COL_GV, COL_GR, COL_NQ = 0, 512, 1024, 2048, 3072
COL_KV = 4096
COL_TAIL = 5632
D_IN_PAD = 6144
GATE_LANE0 = GLA_RANK

LANE = 128
ROW_CHUNKS = D_MODEL // LANE
BUF_PITCH = ROW_CHUNKS + 8
NEG = -1e30
VMEM_LIMIT = 56 * 1024 * 1024

TM_IN, TN_IN = 1024, 1024
TT_GLA = 512
TR_PREP = 512
TQ_NSA = 256
TK_SEL = 512
WIN_SPAN = WINDOW + TQ_NSA
ONES_ROWS = 16
TM_OUT = 256
TM_MOE = 256
TN_UP, TN_DOWN = 1024, 2048
P_ROWS = N_TOK * TOP_K + N_EXPERTS * TM_MOE
N_MTILES = P_ROWS // TM_MOE
TM_FIN = 256
DMA_BATCH = 8


def _dot(a, b):
    return jnp.dot(a, b, preferred_element_type=F32)


def _dot_nt(a, b):
    return lax.dot_general(a, b, (((1,), (1,)), ((), ())), preferred_element_type=F32)


def _dot_tn(a, b):
    return lax.dot_general(a, b, (((0,), (0,)), ((), ())), preferred_element_type=F32)


def _ln(x):
    xc = x - jnp.mean(x, -1, keepdims=True)
    return xc * lax.rsqrt(jnp.mean(xc * xc, -1, keepdims=True) + LN_EPS)


def _split_bf16(x):
    hi = x.astype(BF16)
    lo = (x - hi.astype(F32)).astype(BF16)
    return hi, lo


def _params(*sem):
    return pltpu.CompilerParams(dimension_semantics=sem, vmem_limit_bytes=VMEM_LIMIT)


def _adaln_kernel(c_ref, w_ref, b_ref, o_ref):
    c = c_ref[...]
    a = (c * jax.nn.sigmoid(c)).astype(BF16)
    o_ref[...] = _dot(a, w_ref[...].astype(BF16)) + b_ref[...]


def _adaln(c8, w, b):
    n = w.shape[1]
    tn = 1024
    return pl.pallas_call(
        _adaln_kernel,
        grid=(n // tn,),
        in_specs=[pl.BlockSpec((8, D_MODEL), lambda j: (0, 0)),
                  pl.BlockSpec((D_MODEL, tn), lambda j: (0, j)),
                  pl.BlockSpec((1, tn), lambda j: (0, j))],
        out_specs=pl.BlockSpec((8, tn), lambda j: (0, j)),
        out_shape=jax.ShapeDtypeStruct((8, n), F32),
        compiler_params=_params("arbitrary"),
        name="adaln",
    )(c8, w, b)


def _inproj_kernel(x_ref, sh_ref, sc_ref, w_ref, o_ref, h_sc):
    @pl.when(pl.program_id(1) == 0)
    def _():
        h = _ln(x_ref[...]) * (1.0 + sc_ref[...]) + sh_ref[...]
        h_sc[...] = h.astype(BF16)

    o_ref[...] = _dot(h_sc[...], w_ref[...])


def _inproj(xf, mod3, w_in_p):
    tiles_per_batch = SEQ // TM_IN
    return pl.pallas_call(
        _inproj_kernel,
        grid=(N_TOK // TM_IN, D_IN_PAD // TN_IN),
        in_specs=[pl.BlockSpec((TM_IN, D_MODEL), lambda i, j: (i, 0)),
                  pl.BlockSpec((None, 1, D_MODEL), lambda i, j: (i // tiles_per_batch, 0, 0)),
                  pl.BlockSpec((None, 1, D_MODEL), lambda i, j: (i // tiles_per_batch, 0, 1)),
                  pl.BlockSpec((D_MODEL, TN_IN), lambda i, j: (0, j))],
        out_specs=pl.BlockSpec((TM_IN, TN_IN), lambda i, j: (i, j)),
        out_shape=jax.ShapeDtypeStruct((N_TOK, D_IN_PAD), F32),
        scratch_shapes=[pltpu.VMEM((TM_IN, D_MODEL), BF16)],
        compiler_params=_params("parallel", "arbitrary"),
        name="inproj",
    )(xf, mod3, mod3, w_in_p)


GLA_HALVES = (32, 16, 8, 4, 2, 1)
N_LEVELS = len(GLA_HALVES)
ROW_EB = 2 * N_LEVELS
ROW_EL = 2 * N_LEVELS + 1
N_EVIEWS = 2 * N_LEVELS + 2


def _gla_constants():
    c = GLA_CHUNK
    t = np.arange(c)[:, None]
    r = np.arange(c)[None, :]
    mall = np.zeros((N_EVIEWS, c, c), np.float32)
    valid = np.zeros((N_EVIEWS, c, LANE), np.float32)
    masks = np.zeros((N_LEVELS + 1, c, c), np.float32)
    for li, n in enumerate(GLA_HALVES):
        same = (t // (2 * n)) == (r // (2 * n))
        t_up = (t % (2 * n)) >= n
        r_up = (r % (2 * n)) >= n
        mall[2 * li] = same & t_up & r_up & (r <= t)
        mall[2 * li + 1] = same & ~t_up & ~r_up & (r > t)
        valid[2 * li] = np.broadcast_to(t_up, (c, LANE))
        valid[2 * li + 1] = np.broadcast_to(~t_up, (c, LANE))
        masks[li] = same & t_up & ~r_up
    mall[ROW_EB] = r <= t
    mall[ROW_EL] = r > t
    valid[ROW_EB] = 1.0
    valid[ROW_EL] = 1.0
    masks[N_LEVELS] = np.eye(c)
    return (mall.reshape(N_EVIEWS * c, c), valid.reshape(N_EVIEWS * c, LANE), masks)


def _gla_kernel(q_ref, k_ref, v_ref, r_ref, glr_ref, wgk_ref, bgk_ref, g_ref, mall_ref, valid_ref,
                masks_ref, o_ref, st_sc):
    c = GLA_CHUNK

    @pl.when(pl.program_id(1) == 0)
    def _():
        st_sc[...] = jnp.zeros_like(st_sc)

    z = _dot(glr_ref[...].astype(BF16), wgk_ref[...].astype(BF16)) + bgk_ref[...]
    log_a = (jnp.minimum(z, 0.0) - jnp.log1p(jnp.exp(-jnp.abs(z)))) * (1.0 / GLA_TAU)
    mall = mall_ref[...]
    valid = valid_ref[...]
    for ci in range(TT_GLA // c):
        rows = slice(ci * c, (ci + 1) * c)
        la_hi, la_lo = _split_bf16(log_a[rows])
        e_all = jnp.exp(_dot(mall, la_hi) + _dot(mall, la_lo))
        for h in range(GLA_HEADS):
            kcols = slice(h * GLA_DK, (h + 1) * GLA_DK)
            vcols = slice(h * GLA_DV, (h + 1) * GLA_DV)
            e = e_all[:, kcols] * valid
            q = q_ref[rows, kcols] * (GLA_DK ** -0.5)
            k = k_ref[rows, kcols]
            vb = v_ref[rows, vcols].astype(BF16)
            att = masks_ref[N_LEVELS] * _dot_nt(q.astype(BF16), k.astype(BF16))
            for li in range(N_LEVELS):
                eq = e[(2 * li) * c:(2 * li + 1) * c]
                ek = e[(2 * li + 1) * c:(2 * li + 2) * c]
                att = att + masks_ref[li] * _dot_nt((q * eq).astype(BF16), (k * ek).astype(BF16))
            eb = e[ROW_EB * c:(ROW_EB + 1) * c]
            el = e[ROW_EL * c:(ROW_EL + 1) * c]
            st = st_sc[h]
            o = _dot_nt((q * eb).astype(BF16), st.astype(BF16)) + _dot(att.astype(BF16), vb)
            st_sc[h] = st * eb[c - 1:c, :] + _dot_tn(vb, (k * el).astype(BF16))
            o = o * lax.rsqrt(jnp.mean(o * o, -1, keepdims=True) + LN_EPS) * g_ref[...]
            r = r_ref[rows, vcols]
            o_ref[rows, vcols] = (o * (r * jax.nn.sigmoid(r))).astype(BF16)


def _gla(proj, w_gk_pad, b_gk, norm_g):
    nt = SEQ // TT_GLA
    mall, valid, masks = _gla_constants()
    kw = GLA_HEADS * GLA_DK
    vw = GLA_HEADS * GLA_DV
    full = lambda shape: pl.BlockSpec(shape, lambda b, i: tuple(0 for _ in shape))
    cols = lambda width, col0: pl.BlockSpec((TT_GLA, width), lambda b, i: (b * nt + i, col0 // width))
    return pl.pallas_call(
        _gla_kernel,
        grid=(BATCH, nt),
        in_specs=[cols(kw, COL_GQ), cols(kw, COL_GK), cols(vw, COL_GV), cols(vw, COL_GR),
                  cols(LANE, COL_TAIL), full(w_gk_pad.shape), full(b_gk.shape), full(norm_g.shape),
                  full(mall.shape), full(valid.shape), full(masks.shape)],
        out_specs=cols(vw, 0),
        out_shape=jax.ShapeDtypeStruct((N_TOK, vw), BF16),
        scratch_shapes=[pltpu.VMEM((GLA_HEADS, GLA_DV, GLA_DK), F32)],
        compiler_params=_params("parallel", "arbitrary"),
        name="gla",
    )(proj, proj, proj, proj, proj, w_gk_pad, b_gk, norm_g,
      jnp.asarray(mall, BF16), jnp.asarray(valid), jnp.asarray(masks))


def _rope_tables(pos):
    half = ROPE_DIM // 2
    inv_freq = ROPE_THETA ** (-jnp.arange(half, dtype=F32) * (2.0 / ROPE_DIM))
    ang = pos.astype(F32)[:, None] * inv_freq
    cos, sin = jnp.cos(ang), jnp.sin(ang)
    n = pos.shape[0]
    cosf = jnp.concatenate([cos, cos, jnp.ones((n, LANE - ROPE_DIM), F32)], -1)
    sinf = jnp.concatenate([-sin, sin, jnp.zeros((n, LANE - ROPE_DIM), F32)], -1)
    return cosf, sinf


def _rope(x, cosf, sinf):
    lane = lax.broadcasted_iota(jnp.int32, x.shape, 1)
    half = ROPE_DIM // 2
    swapped = jnp.where(lane < half, pltpu.roll(x, LANE - half, 1), pltpu.roll(x, half, 1))
    return x * cosf + swapped * sinf


def _prep_kernel(q_ref, ks_ref, vs_ref, kw_ref, vw_ref, cos_ref, sin_ref,
                 qo_ref, kso_ref, vso_ref, kwo_ref, vwo_ref):
    cosf = cos_ref[...]
    sinf = sin_ref[...]
    for hh in range(NSA_HEADS):
        cols = slice(hh * NSA_DH, (hh + 1) * NSA_DH)
        qo_ref[:, cols] = (_rope(q_ref[:, cols], cosf, sinf) * Q_SCALE_LOG2).astype(BF16)
    for g in range(NSA_GROUPS):
        cols = slice(g * NSA_DH, (g + 1) * NSA_DH)
        kso_ref[g] = _rope(ks_ref[:, cols], cosf, sinf).astype(BF16)
        kwo_ref[g] = _rope(kw_ref[:, cols], cosf, sinf).astype(BF16)
        vso_ref[g] = vs_ref[:, cols].T.astype(BF16)
        vwo_ref[g] = vw_ref[:, cols].T.astype(BF16)


def _prep(proj, cosf, sinf):
    nt = SEQ // TR_PREP
    kvw = NSA_GROUPS * NSA_DH
    kv_in = lambda which: pl.BlockSpec((TR_PREP, kvw), lambda b, i: (b * nt + i, COL_KV // kvw + which))
    k_out = pl.BlockSpec((None, NSA_GROUPS, TR_PREP, NSA_DH), lambda b, i: (b, 0, i, 0))
    k_shape = jax.ShapeDtypeStruct((BATCH, NSA_GROUPS, SEQ, NSA_DH), BF16)
    vt_out = pl.BlockSpec((None, NSA_GROUPS, NSA_DH, TR_PREP), lambda b, i: (b, 0, 0, i))
    vt_shape = jax.ShapeDtypeStruct((BATCH, NSA_GROUPS, NSA_DH, SEQ), BF16)
    qw = NSA_HEADS * NSA_DH
    return pl.pallas_call(
        _prep_kernel,
        grid=(BATCH, nt),
        in_specs=[pl.BlockSpec((TR_PREP, qw), lambda b, i: (b * nt + i, COL_NQ // qw)),
                  kv_in(2), kv_in(3), kv_in(4), kv_in(5),
                  pl.BlockSpec((TR_PREP, LANE), lambda b, i: (i, 0)),
                  pl.BlockSpec((TR_PREP, LANE), lambda b, i: (i, 0))],
        out_specs=[pl.BlockSpec((TR_PREP, qw), lambda b, i: (b * nt + i, 0)),
                   k_out, vt_out, k_out, vt_out],
        out_shape=[jax.ShapeDtypeStruct((N_TOK, qw), BF16), k_shape, vt_shape, k_shape, vt_shape],
        compiler_params=_params("parallel", "parallel"),
        name="nsa_prep",
    )(proj, proj, proj, proj, proj, cosf, sinf)


def _compress_kernel(a_ref, pe_ref, w1_ref, b1_ref, w2_ref, b2_ref, cos_ref, sin_ref, o_ref, bot_sc,
                     *, rope):
    n = N_CMP_PAD
    top = jnp.zeros((n, CMP_HIDDEN), F32)
    bot = jnp.zeros((n, CMP_HIDDEN), F32)
    for p in range(CMP_STRIDE):
        ap = a_ref[pl.ds(p, n, stride=CMP_STRIDE), :]
        w_top = w1_ref[p * NSA_DH:(p + 1) * NSA_DH, :].astype(BF16)
        w_bot = w1_ref[(CMP_STRIDE + p) * NSA_DH:(CMP_STRIDE + p + 1) * NSA_DH, :].astype(BF16)
        top = top + _dot((ap + pe_ref[p:p + 1, :]).astype(BF16), w_top)
        bot = bot + _dot((ap + pe_ref[CMP_STRIDE + p:CMP_STRIDE + p + 1, :]).astype(BF16), w_bot)
    bot_sc[0:n, :] = bot
    bot_sc[n:n + 8, :] = jnp.zeros((8, CMP_HIDDEN), F32)
    h = top + bot_sc[1:n + 1, :] + b1_ref[...]
    h = h * jax.nn.sigmoid(h)
    out = _dot(h.astype(BF16), w2_ref[...].astype(BF16)) + b2_ref[...]
    if rope:
        out = _rope(out, cos_ref[...], sin_ref[...])
    row = lax.broadcasted_iota(jnp.int32, out.shape, 0)
    out = jnp.where(row < n - 1, out, 0.0)
    o_ref[...] = (out if rope else out.T).astype(BF16)


def _compress(proj, which, pe, w1, b1, w2, b2, cosf, sinf, rope):
    col0 = (COL_KV + which * NSA_GROUPS * NSA_DH) // NSA_DH
    out_dims = (N_CMP_PAD, NSA_DH) if rope else (NSA_DH, N_CMP_PAD)
    full = lambda shape: pl.BlockSpec(shape, lambda b, g: tuple(0 for _ in shape))
    return pl.pallas_call(
        functools.partial(_compress_kernel, rope=rope),
        grid=(BATCH, NSA_GROUPS),
        in_specs=[pl.BlockSpec((SEQ, NSA_DH), lambda b, g: (b, col0 + g)),
                  full(pe.shape), full(w1.shape), full(b1.shape), full(w2.shape), full(b2.shape),
                  full(cosf.shape), full(sinf.shape)],
        out_specs=pl.BlockSpec((None, None) + out_dims, lambda b, g: (b, g, 0, 0)),
        out_shape=jax.ShapeDtypeStruct((BATCH, NSA_GROUPS) + out_dims, BF16),
        scratch_shapes=[pltpu.VMEM((N_CMP_PAD + 8, CMP_HIDDEN), F32)],
        compiler_params=_params("parallel", "parallel"),
        name="nsa_compress",
    )(proj, pe, w1, b1, w2, b2, cosf, sinf)


def _nsa_constants():
    c_start = np.arange(N_CMP_PAD) * CMP_STRIDE
    b_start = np.arange(N_BLK) * SEL_BLOCK
    overlap_t = ((c_start[None, :] < b_start[:, None] + SEL_BLOCK)
                 & (c_start[None, :] + CMP_BLOCK > b_start[:, None])).astype(np.float32)
    overlap_t[:, N_CMP_PAD - 1] = 0.0
    expand_t = (np.arange(SEQ)[:, None] // SEL_BLOCK == np.arange(LANE)[None, :]).astype(np.float32)
    return overlap_t, expand_t


def _tile_heads(a):
    return jnp.concatenate([a] * NSA_HPG, axis=1)


def _nsa_kernel(q_ref, kc_ref, vc_ref, ks_ref, vs_ref, kw_ref, vw_ref, gate_ref, ovt_ref, expand_ref,
                o_ref, score_sc, m_sc, acc_sc):
    tq = TQ_NSA
    g = pl.program_id(1)
    q0 = pl.program_id(2) * tq
    qs = jnp.concatenate([q_ref[:, hh * NSA_DH:(hh + 1) * NSA_DH] for hh in range(NSA_HPG)], axis=0)
    pos_q = q0 + lax.broadcasted_iota(jnp.int32, (1, tq), 1)

    def with_ones(vt):
        return jnp.concatenate([vt, jnp.ones((ONES_ROWS, vt.shape[1]), BF16)], axis=0)

    def normalise(acc):
        return acc[0:NSA_DH] / acc[NSA_DH:NSA_DH + 1]

    s = _dot_nt(kc_ref[...], qs)
    cmp_end = lax.broadcasted_iota(jnp.int32, (N_CMP_PAD, tq), 0) * CMP_STRIDE + (CMP_BLOCK - 1)
    s = s + _tile_heads(jnp.where(cmp_end <= pos_q, 0.0, NEG))
    m = jnp.max(s, 0, keepdims=True)
    e = jnp.where(s > 0.5 * NEG, jnp.exp2(s - m), 0.0)
    p = e / jnp.maximum(jnp.sum(e, 0, keepdims=True), 1e-30)
    o_cmp = _dot(vc_ref[...], p.astype(BF16))
    p_sum = p[:, 0:tq] + p[:, tq:2 * tq] + p[:, 2 * tq:3 * tq] + p[:, 3 * tq:4 * tq]
    ps_hi, ps_lo = _split_bf16(p_sum)
    ovt = ovt_ref[...]
    p_blk_t = _dot(ovt, ps_hi) + _dot(ovt, ps_lo)

    jj = lax.broadcasted_iota(jnp.int32, (N_BLK, tq), 0)
    cur = (q0 + lax.broadcasted_iota(jnp.int32, (N_BLK, tq), 1)) // SEL_BLOCK
    forced = (jj == 0) | (jj == cur) | (jj == cur - 1)
    allowed = jj <= cur
    score = jnp.where(forced, 3.0e38, jnp.where(allowed, p_blk_t, -1.0))
    score_sc[...] = score

    def rank_body(i, rank):
        row = score_sc[pl.ds(i, 1), :]
        first = jnp.where(jj > i, 1.0, 0.0)
        return rank + jnp.where(row > score, 1.0, jnp.where(row == score, first, 0.0))

    n_live = (q0 + tq - 1) // SEL_BLOCK + 1
    rank = lax.fori_loop(0, n_live, rank_body, jnp.zeros((N_BLK, tq), F32))
    sel_t = jnp.where(allowed, jnp.where(rank < SEL_TOPK, 1.0, 0.0), 0.0)
    sel = jnp.concatenate([sel_t, jnp.zeros((LANE - N_BLK, tq), F32)], axis=0).astype(BF16)

    m_sc[...] = jnp.full(m_sc.shape, NEG, F32)
    acc_sc[...] = jnp.zeros(acc_sc.shape, F32)

    def sel_body(kt, carry):
        k0 = pl.multiple_of(kt * TK_SEL, TK_SEL)
        s = _dot_nt(ks_ref[pl.ds(k0, TK_SEL), :], qs)
        sel_x = _dot(expand_ref[pl.ds(k0, TK_SEL), :], sel)
        kpos = k0 + lax.broadcasted_iota(jnp.int32, (TK_SEL, tq), 0)
        bias = jnp.where(kpos <= pos_q, jnp.where(sel_x > 0.5, 0.0, NEG), NEG)
        s = s + _tile_heads(bias)
        m_prev = m_sc[...]
        m_new = jnp.maximum(m_prev, jnp.max(s, 0, keepdims=True))
        alpha = jnp.exp2(m_prev - m_new)
        p = jnp.exp2(s - m_new).astype(BF16)
        acc_sc[...] = alpha * acc_sc[...] + _dot(with_ones(vs_ref[:, pl.ds(k0, TK_SEL)]), p)
        m_sc[...] = m_new
        return carry

    lax.fori_loop(0, (q0 + tq - 1) // TK_SEL + 1, sel_body, 0)
    o_slc = normalise(acc_sc[...])

    start = pl.multiple_of(jnp.maximum(q0 - WINDOW, 0), LANE)
    s = _dot_nt(kw_ref[pl.ds(start, WIN_SPAN), :], qs)
    dist = pos_q - (start + lax.broadcasted_iota(jnp.int32, (WIN_SPAN, tq), 0))
    bias = jnp.where(dist >= 0, jnp.where(dist < WINDOW, 0.0, NEG), NEG)
    s = s + _tile_heads(bias)
    p = jnp.exp2(s - jnp.max(s, 0, keepdims=True)).astype(BF16)
    o_win = normalise(_dot(with_ones(vw_ref[:, pl.ds(start, WIN_SPAN)]), p))

    gates = jax.nn.sigmoid(gate_ref[...]).T

    def gate(hh, branch):
        lane0 = GATE_LANE0 + hh * 3 + branch
        lane1 = lane0 + NSA_HPG * 3
        return jnp.where(g == 0, gates[lane0:lane0 + 1, :], gates[lane1:lane1 + 1, :])

    for hh in range(NSA_HPG):
        c = slice(hh * tq, (hh + 1) * tq)
        o = gate(hh, 0) * o_cmp[:, c] + gate(hh, 1) * o_slc[:, c] + gate(hh, 2) * o_win[:, c]
        o_ref[:, hh * NSA_DH:(hh + 1) * NSA_DH] = o.T.astype(BF16)


def _nsa(q_r, kc, vc, ks, vs, kw, vw, proj):
    nq = SEQ // TQ_NSA
    gw = NSA_HPG * NSA_DH
    overlap_t, expand = _nsa_constants()
    cols = NSA_HPG * TQ_NSA
    per_group = lambda d0, d1: pl.BlockSpec((None, None, d0, d1), lambda b, g, i: (b, g, 0, 0))
    return pl.pallas_call(
        _nsa_kernel,
        grid=(BATCH, NSA_GROUPS, nq),
        in_specs=[pl.BlockSpec((TQ_NSA, gw), lambda b, g, i: (b * nq + i, g)),
                  per_group(N_CMP_PAD, NSA_DH), per_group(NSA_DH, N_CMP_PAD),
                  per_group(SEQ, NSA_DH), per_group(NSA_DH, SEQ),
                  per_group(SEQ, NSA_DH), per_group(NSA_DH, SEQ),
                  pl.BlockSpec((TQ_NSA, LANE), lambda b, g, i: (b * nq + i, COL_TAIL // LANE)),
                  pl.BlockSpec(overlap_t.shape, lambda b, g, i: (0, 0)),
                  pl.BlockSpec(expand.shape, lambda b, g, i: (0, 0))],
        out_specs=pl.BlockSpec((TQ_NSA, gw), lambda b, g, i: (b * nq + i, g)),
        out_shape=jax.ShapeDtypeStruct((N_TOK, NSA_GROUPS * gw), BF16),
        scratch_shapes=[pltpu.VMEM((N_BLK, TQ_NSA), F32),
                        pltpu.VMEM((1, cols), F32),
                        pltpu.VMEM((NSA_DH + ONES_ROWS, cols), F32)],
        compiler_params=_params("parallel", "parallel", "arbitrary"),
        name="nsa_attn",
    )(q_r, kc, vc, ks, vs, kw, vw, proj, jnp.asarray(overlap_t, BF16), jnp.asarray(expand, BF16))


def _outproj_kernel(yg_ref, yn_ref, x_ref, wo_ref, g1_ref, sc2_ref, sh2_ref, ln1g_ref, ln1b_ref,
                    wr_hi_ref, wr_lo_ref, br_ref, ltri_ref, x1_ref, h2_ref, idx_ref, wt_ref, rank_ref,
                    cnt_ref, base_sc):
    half = D_MODEL // 2

    @pl.when(pl.program_id(0) == 0)
    def _():
        base_sc[...] = jnp.zeros_like(base_sc)

    mix = _dot(yg_ref[...], wo_ref[0:half, :]) + _dot(yn_ref[...], wo_ref[half:D_MODEL, :])
    x1 = _ln(DN_ALPHA * x_ref[...] + (1.0 + g1_ref[...]) * mix) * ln1g_ref[...] + ln1b_ref[...]
    x1_ref[...] = x1
    h2 = _ln(x1) * (1.0 + sc2_ref[...]) + sh2_ref[...]
    h_hi, h_lo = _split_bf16(h2)
    for s in range(ROW_CHUNKS):
        h2_ref[pl.ds(s, TM_OUT, stride=ROW_CHUNKS), :] = h2[:, s * LANE:(s + 1) * LANE]
    wr_hi = wr_hi_ref[...]
    logits = _dot(h_hi, wr_hi) + _dot(h_lo, wr_hi) + _dot(h_hi, wr_lo_ref[...]) + br_ref[...]

    lane = lax.broadcasted_iota(jnp.int32, logits.shape, 1)
    lane_f = lane.astype(F32)
    vals = logits
    idx_out = jnp.zeros(logits.shape, jnp.int32)
    exp_out = jnp.zeros(logits.shape, F32)
    denom = jnp.zeros((logits.shape[0], 1), F32)
    m0 = None
    onehots = []
    for k in range(TOP_K):
        mk = jnp.max(vals, -1, keepdims=True)
        ik = jnp.min(jnp.where(vals == mk, lane_f, float(LANE)), -1, keepdims=True)
        if k == 0:
            m0 = mk
        ek = jnp.exp(mk - m0)
        denom = denom + ek
        idx_out = jnp.where(lane == k, ik.astype(jnp.int32), idx_out)
        exp_out = jnp.where(lane == k, ek, exp_out)
        hit = lane_f == ik
        onehots.append(jnp.where(hit, 1.0, 0.0))
        vals = jnp.where(hit, -3.0e38, vals)
    idx_ref[...] = idx_out
    wt_ref[...] = exp_out / denom

    cnt = onehots[0] + onehots[1] + onehots[2] + onehots[3]
    base = base_sc[0:1, :]
    before = _dot(ltri_ref[...], cnt.astype(BF16)) + base
    rank_out = jnp.zeros(logits.shape, jnp.int32)
    for k in range(TOP_K):
        rk = jnp.sum(onehots[k] * before, -1, keepdims=True)
        rank_out = jnp.where(lane == k, rk.astype(jnp.int32), rank_out)
    rank_ref[...] = rank_out
    base_sc[...] = jnp.broadcast_to(base + jnp.sum(cnt, 0, keepdims=True), base_sc.shape)
    cnt_ref[...] = base_sc[...]


def _outproj(y_gla, y_nsa, xf, w_o, mod3, ln1_g, ln1_b, wr_hi, wr_lo, br):
    tiles_per_batch = SEQ // TM_OUT
    half = D_MODEL // 2
    mod_spec = lambda chunk: pl.BlockSpec((None, 1, D_MODEL), lambda i: (i // tiles_per_batch, 0, chunk))
    full = lambda shape: pl.BlockSpec(shape, lambda i: tuple(0 for _ in shape))
    row = lambda width: pl.BlockSpec((TM_OUT, width), lambda i: (i, 0))
    ltri = jnp.asarray(np.tril(np.ones((TM_OUT, TM_OUT), np.float32), -1), BF16)
    return pl.pallas_call(
        _outproj_kernel,
        grid=(N_TOK // TM_OUT,),
        in_specs=[row(half), row(half), row(D_MODEL), full(w_o.shape),
                  mod_spec(2), mod_spec(4), mod_spec(3),
                  full(ln1_g.shape), full(ln1_b.shape), full(wr_hi.shape), full(wr_lo.shape), full(br.shape),
                  full(ltri.shape)],
        out_specs=[row(D_MODEL), pl.BlockSpec((TM_OUT * ROW_CHUNKS, LANE), lambda i: (i, 0)),
                   row(LANE), row(LANE), row(LANE), full((8, LANE))],
        out_shape=[jax.ShapeDtypeStruct((N_TOK, D_MODEL), F32),
                   jax.ShapeDtypeStruct((N_TOK * ROW_CHUNKS, LANE), F32),
                   jax.ShapeDtypeStruct((N_TOK, LANE), jnp.int32),
                   jax.ShapeDtypeStruct((N_TOK, LANE), F32),
                   jax.ShapeDtypeStruct((N_TOK, LANE), jnp.int32),
                   jax.ShapeDtypeStruct((8, LANE), F32)],
        scratch_shapes=[pltpu.VMEM((8, LANE), F32)],
        compiler_params=_params("arbitrary"),
        name="outproj_router",
    )(y_gla, y_nsa, xf, w_o, mod3, mod3, mod3, ln1_g, ln1_b, wr_hi, wr_lo, br, ltri)


def _invert_kernel(dest_ref, src_sm):
    def fill(r, carry):
        src_sm[r] = 0
        return carry

    lax.fori_loop(0, P_ROWS, fill, 0, unroll=8)

    def scatter(tok2, carry):
        a0 = tok2 * (2 * TOP_K)
        rows = [dest_ref[a0 + i] for i in range(2 * TOP_K)]
        for i, row in enumerate(rows):
            src_sm[row] = tok2 * 2 + i // TOP_K
        return carry

    lax.fori_loop(0, N_TOK // 2, scatter, 0)


def _invert(dest_flat):
    smem = pl.BlockSpec(memory_space=pltpu.SMEM)
    return pl.pallas_call(
        _invert_kernel,
        in_specs=[smem],
        out_specs=smem,
        out_shape=jax.ShapeDtypeStruct((P_ROWS,), jnp.int32),
        name="moe_invert",
    )(dest_flat)


def _dispatch_kernel(src_sm, nused_ref, h2_hbm, xs_ref, buf, sems):
    t = pl.program_id(0)
    n_used = nused_ref[0]

    def start_gather(tile, slot):
        base = tile * TM_MOE

        def body(r8, carry):
            r0 = r8 * DMA_BATCH
            toks = [src_sm[base + r0 + i] for i in range(DMA_BATCH)]
            for i, tok in enumerate(toks):
                pltpu.make_async_copy(h2_hbm.at[pl.ds(pl.multiple_of(tok * ROW_CHUNKS, ROW_CHUNKS), ROW_CHUNKS), :],
                                      buf.at[slot, pl.ds(pl.multiple_of((r0 + i) * BUF_PITCH, 8), ROW_CHUNKS), :],
                                      sems.at[slot]).start(priority=i % 2)
            return carry

        lax.fori_loop(0, TM_MOE // DMA_BATCH, body, 0)

    @pl.when(t == 0)
    def _():
        start_gather(0, 0)

    @pl.when(t + 1 < n_used)
    def _():
        start_gather(t + 1, (t + 1) % 2)

    @pl.when(t < n_used)
    def _():
        slot = t % 2
        issued = buf.at[slot, pl.ds(0, TM_MOE * ROW_CHUNKS), :]
        pltpu.make_async_copy(issued, issued, sems.at[slot]).wait()
        for s in range(ROW_CHUNKS):
            xs_ref[:, s * LANE:(s + 1) * LANE] = buf[slot, pl.ds(s, TM_MOE, stride=BUF_PITCH), :].astype(BF16)

    @pl.when(t >= n_used)
    def _():
        xs_ref[...] = jnp.zeros(xs_ref.shape, BF16)


def _dispatch(src_tok, n_used, h2):
    return pl.pallas_call(
        _dispatch_kernel,
        grid_spec=pltpu.PrefetchScalarGridSpec(
            num_scalar_prefetch=2,
            grid=(N_MTILES,),
            in_specs=[pl.BlockSpec(memory_space=pl.ANY)],
            out_specs=pl.BlockSpec((TM_MOE, D_MODEL), lambda t, src, nu: (t, 0)),
            scratch_shapes=[pltpu.VMEM((2, TM_MOE * BUF_PITCH, LANE), F32),
                            pltpu.SemaphoreType.DMA((2,))]),
        out_shape=jax.ShapeDtypeStruct((P_ROWS, D_MODEL), BF16),
        compiler_params=_params("arbitrary"),
        name="moe_dispatch",
    )(src_tok, n_used, h2)


class _ExpertWeights:
    def __init__(self, eid_ref, n_used, w_hbms, bufs, sems, cnt_sm):
        self.eid_ref, self.n_used = eid_ref, n_used
        self.w_hbms, self.bufs, self.sems, self.cnt_sm = w_hbms, bufs, sems, cnt_sm

    def _copies(self, expert, j, slot):
        tn = self.bufs[0].shape[-1]
        cols = pl.ds(pl.multiple_of(j * tn, tn), tn)
        return [pltpu.make_async_copy(w.at[expert, :, cols], buf.at[slot], self.sems.at[i, slot])
                for i, (w, buf) in enumerate(zip(self.w_hbms, self.bufs))]

    def prologue(self):
        self.cnt_sm[0] = 0
        for cp in self._copies(self.eid_ref[0], 0, 0):
            cp.start()

    def acquire(self, j, t, n_passes):
        slot = self.cnt_sm[0] % 2
        for cp in self._copies(self.eid_ref[t], j, slot):
            cp.wait()
        expert = self.eid_ref[t]
        nxt = lax.while_loop(
            lambda u: (u < self.n_used) & (self.eid_ref[jnp.minimum(u, N_MTILES - 1)] == expert),
            lambda u: u + 1, t + 1)
        in_pass = nxt < self.n_used

        @pl.when(in_pass | (j + 1 < n_passes))
        def _():
            for cp in self._copies(self.eid_ref[jnp.where(in_pass, nxt, 0)], jnp.where(in_pass, j, j + 1),
                                   1 - slot):
                cp.start()

        self.cnt_sm[0] = self.cnt_sm[0] + 1
        return slot


def _moe_up_kernel(eid_ref, nused_ref, x_ref, wg_hbm, wu_hbm, bg_ref, bu_ref, h_ref,
                   wg_buf, wu_buf, wg_sc, wu_sc, sems, cnt_sm):
    j = pl.program_id(0)
    t = pl.program_id(1)
    n_used = nused_ref[0]
    weights = _ExpertWeights(eid_ref, n_used, (wg_hbm, wu_hbm), (wg_buf, wu_buf), sems, cnt_sm)

    @pl.when((j == 0) & (t == 0))
    def _():
        weights.prologue()

    @pl.when(t < n_used)
    def _():
        @pl.when((t == 0) | (eid_ref[t] != eid_ref[jnp.maximum(t - 1, 0)]))
        def _():
            slot = weights.acquire(j, t, pl.num_programs(0))
            wg_sc[...] = wg_buf[slot].astype(BF16)
            wu_sc[...] = wu_buf[slot].astype(BF16)

        x = x_ref[...]
        gate = jnp.minimum(_dot(x, wg_sc[...]) + bg_ref[...], SWIGLU_LIMIT)
        up = jnp.clip(_dot(x, wu_sc[...]) + bu_ref[...], -SWIGLU_LIMIT, SWIGLU_LIMIT)
        h_ref[...] = (gate * jax.nn.sigmoid(SWIGLU_ALPHA * gate) * (up + 1.0)).astype(BF16)

    @pl.when(t >= n_used)
    def _():
        h_ref[...] = jnp.zeros(h_ref.shape, BF16)


def _moe_down_kernel(eid_ref, nused_ref, h_ref, wd_hbm, bd_ref, y_ref, wd_buf, wd_sc, sems, cnt_sm):
    j = pl.program_id(0)
    t = pl.program_id(1)
    n_used = nused_ref[0]
    weights = _ExpertWeights(eid_ref, n_used, (wd_hbm,), (wd_buf,), sems, cnt_sm)

    @pl.when((j == 0) & (t == 0))
    def _():
        weights.prologue()

    @pl.when(t < n_used)
    def _():
        @pl.when((t == 0) | (eid_ref[t] != eid_ref[jnp.maximum(t - 1, 0)]))
        def _():
            slot = weights.acquire(j, t, pl.num_programs(0))
            wd_sc[...] = wd_buf[slot].astype(BF16)

        y_ref[...] = _dot(h_ref[...], wd_sc[...]) + bd_ref[...]

    @pl.when(t >= n_used)
    def _():
        y_ref[...] = jnp.zeros(y_ref.shape, F32)


def _row_tile(t, nused_ref):
    return jnp.minimum(t, jnp.maximum(nused_ref[0] - 1, 0))


def _moe_call(body, name, n_weights, k_dim, out_dim, tn, out_dtype, tile_eid, n_used, rows, weights, biases):
    any_spec = pl.BlockSpec(memory_space=pl.ANY)
    b_spec = pl.BlockSpec((None, 1, tn), lambda j, t, eid, nu: (eid[t], 0, j))
    return pl.pallas_call(
        body,
        grid_spec=pltpu.PrefetchScalarGridSpec(
            num_scalar_prefetch=2,
            grid=(out_dim // tn, N_MTILES),
            in_specs=[pl.BlockSpec((TM_MOE, k_dim), lambda j, t, eid, nu: (_row_tile(t, nu), 0))]
            + [any_spec] * n_weights + [b_spec] * n_weights,
            out_specs=pl.BlockSpec((TM_MOE, tn), lambda j, t, eid, nu: (t, j)),
            scratch_shapes=[pltpu.VMEM((2, k_dim, tn), F32)] * n_weights
            + [pltpu.VMEM((k_dim, tn), BF16)] * n_weights
            + [pltpu.SemaphoreType.DMA((n_weights, 2)), pltpu.SMEM((1,), jnp.int32)]),
        out_shape=jax.ShapeDtypeStruct((P_ROWS, out_dim), out_dtype),
        compiler_params=_params("arbitrary", "arbitrary"),
        name=name,
    )(tile_eid, n_used, rows, *weights, *biases)


def _moe_up(tile_eid, n_used, xs, w_gate, w_up, b_gate, b_up):
    return _moe_call(_moe_up_kernel, "moe_up", 2, D_MODEL, D_FF, TN_UP, BF16, tile_eid, n_used, xs,
                     (w_gate, w_up), (b_gate, b_up))


def _moe_down(tile_eid, n_used, h, w_down, b_down):
    return _moe_call(_moe_down_kernel, "moe_down", 1, D_FF, D_MODEL, TN_DOWN, F32, tile_eid, n_used, h,
                     (w_down,), (b_down,))


def _final_kernel(dest_ref, y_hbm, x1_ref, wt_ref, g2_ref, ln2g_ref, ln2b_ref, o_ref, ybuf, sems):
    i = pl.program_id(0)
    n_tiles = pl.num_programs(0)

    def start_gather(tile, slot):
        base = tile * (TM_FIN * TOP_K)

        def body(r2, carry):
            a0 = base + r2 * DMA_BATCH
            rows = [dest_ref[a0 + i] for i in range(DMA_BATCH)]
            for i, d in enumerate(rows):
                r = r2 * (DMA_BATCH // TOP_K) + i // TOP_K
                pltpu.make_async_copy(y_hbm.at[pl.ds(d, 1), :], ybuf.at[slot, i % TOP_K, pl.ds(r, 1), :],
                                      sems.at[slot]).start(priority=i % 2)
            return carry

        lax.fori_loop(0, TM_FIN * TOP_K // DMA_BATCH, body, 0)

    @pl.when(i == 0)
    def _():
        start_gather(0, 0)

    @pl.when(i + 1 < n_tiles)
    def _():
        start_gather(i + 1, (i + 1) % 2)

    slot = i % 2
    pltpu.make_async_copy(ybuf.at[slot], ybuf.at[slot], sems.at[slot]).wait()
    wt = wt_ref[...]
    y = wt[:, 0:1] * ybuf[slot, 0]
    for k in range(1, TOP_K):
        y = y + wt[:, k:k + 1] * ybuf[slot, k]
    o_ref[...] = _ln(DN_ALPHA * x1_ref[...] + (1.0 + g2_ref[...]) * y) * ln2g_ref[...] + ln2b_ref[...]


def _final(dest_flat, y, x1, wt, mod3, ln2_g, ln2_b):
    tiles_per_batch = SEQ // TM_FIN
    full = lambda shape: pl.BlockSpec(shape, lambda i, dest: tuple(0 for _ in shape))
    row = lambda width: pl.BlockSpec((TM_FIN, width), lambda i, dest: (i, 0))
    return pl.pallas_call(
        _final_kernel,
        grid_spec=pltpu.PrefetchScalarGridSpec(
            num_scalar_prefetch=1,
            grid=(N_TOK // TM_FIN,),
            in_specs=[pl.BlockSpec(memory_space=pl.ANY), row(D_MODEL), row(LANE),
                      pl.BlockSpec((None, 1, D_MODEL), lambda i, dest: (i // tiles_per_batch, 0, 5)),
                      full(ln2_g.shape), full(ln2_b.shape)],
            out_specs=row(D_MODEL),
            scratch_shapes=[pltpu.VMEM((2, TOP_K, TM_FIN, D_MODEL), F32),
                            pltpu.SemaphoreType.DMA((2,))]),
        out_shape=jax.ShapeDtypeStruct((N_TOK, D_MODEL), F32),
        compiler_params=_params("arbitrary"),
        name="combine_ln2",
    )(dest_flat, y, x1, wt, mod3, ln2_g, ln2_b)


def _route(idx, rank, counts):
    experts = jnp.arange(N_EXPERTS, dtype=jnp.int32)
    padded = ((counts + TM_MOE - 1) // TM_MOE) * TM_MOE
    ends = jnp.sum(jnp.where(experts[None, :] <= experts[:, None], padded[None, :], 0), axis=1)
    starts = ends - padded
    dest = rank
    for e in range(N_EXPERTS):
        dest = dest + jnp.where(idx == e, starts[e], 0)
    dest = dest[:, :TOP_K]
    tile_start = jnp.arange(N_MTILES, dtype=jnp.int32) * TM_MOE
    tile_eid = jnp.minimum(jnp.sum((ends[None, :] <= tile_start[:, None]).astype(jnp.int32), axis=1),
                           N_EXPERTS - 1)
    n_used = (ends[-1] // TM_MOE).astype(jnp.int32).reshape(1)
    return dest.reshape(-1), tile_eid, n_used


def kernel(x, c, w_ada, b_ada, w_in, w_gk, b_gk, gla_norm_g, pe_k, pe_v, w_ck1, b_ck1, w_ck2, b_ck2,
           w_cv1, b_cv1, w_cv2, b_cv2, w_o, ln1_g, ln1_b, w_router, b_router, w_gate, b_gate, w_up, b_up,
           w_down, b_down, ln2_g, ln2_b):
    l = 0
    xf = x.reshape(N_TOK, D_MODEL)
    row2 = lambda a: a.reshape(1, -1)

    c8 = jnp.pad(c, ((0, 8 - BATCH), (0, 0)))
    mod3 = _adaln(c8, w_ada[l], row2(b_ada[l]))[:BATCH].reshape(BATCH, 1, 6 * D_MODEL)

    w = w_in[l]
    glr0 = 3072
    nsa0 = glr0 + GLA_RANK
    ngt0 = nsa0 + 1024 + 6 * 256
    w_in_p = jnp.concatenate(
        [w[:, :glr0], w[:, nsa0:ngt0], w[:, glr0:nsa0], w[:, ngt0:],
         jnp.zeros((D_MODEL, D_IN_PAD - w.shape[1]), F32)], axis=1).astype(BF16)
    proj = _inproj(xf, mod3, w_in_p)

    w_gk_pad = jnp.pad(w_gk[l], ((0, LANE - GLA_RANK), (0, 0)))
    y_gla = _gla(proj, w_gk_pad, row2(b_gk[l]), row2(gla_norm_g[l]))

    cos_t, sin_t = _rope_tables(jnp.arange(SEQ))
    cmp_end = jnp.arange(N_CMP_PAD) * CMP_STRIDE + (CMP_BLOCK - 1)
    cos_c, sin_c = _rope_tables(cmp_end)
    q_r, ks, vs, kw, vw = _prep(proj, cos_t, sin_t)
    kc = _compress(proj, 0, pe_k[l], w_ck1[l], row2(b_ck1[l]), w_ck2[l], row2(b_ck2[l]), cos_c, sin_c, True)
    vc = _compress(proj, 1, pe_v[l], w_cv1[l], row2(b_cv1[l]), w_cv2[l], row2(b_cv2[l]), cos_c, sin_c, False)
    y_nsa = _nsa(q_r, kc, vc, ks, vs, kw, vw, proj)

    wr = jnp.pad(w_router[l], ((0, 0), (0, LANE - N_EXPERTS)))
    wr_hi, wr_lo = _split_bf16(wr)
    br = jnp.concatenate([b_router[l], jnp.full((LANE - N_EXPERTS,), NEG, F32)]).reshape(1, LANE)
    x1, h2, idx128, wt128, rank128, cnt8 = _outproj(y_gla, y_nsa, xf, w_o[l].astype(BF16), mod3,
                                                    row2(ln1_g[l]), row2(ln1_b[l]), wr_hi, wr_lo, br)

    counts = cnt8[0, :N_EXPERTS].astype(jnp.int32)
    dest, tile_eid, n_used = _route(idx128, rank128, counts)
    xs = _dispatch(_invert(dest), n_used, h2)
    h = _moe_up(tile_eid, n_used, xs, w_gate[l], w_up[l],
                b_gate[l].reshape(N_EXPERTS, 1, D_FF), b_up[l].reshape(N_EXPERTS, 1, D_FF))
    y = _moe_down(tile_eid, n_used, h, w_down[l], b_down[l].reshape(N_EXPERTS, 1, D_MODEL))

    out = _final(dest, y, x1, wt128, mod3, row2(ln2_g[l]), row2(ln2_b[l]))
    return out.reshape(BATCH, SEQ, D_MODEL)
```

```python
import functools

import numpy as np
import jax
import jax.numpy as jnp
from jax import lax
from jax.experimental import pallas as pl
from jax.experimental.pallas import tpu as pltpu

F32 = jnp.float32
BF16 = jnp.bfloat16

D_MODEL = 2048
BATCH = 2
SEQ = 4096
N_TOK = BATCH * SEQ

GLA_HEADS = 4
GLA_DK = 128
GLA_DV = 256
GLA_RANK = 16
GLA_TAU = 16.0
GLA_CHUNK = 64

NSA_DH = 128
NSA_HEADS = 8
NSA_GROUPS = 2
NSA_HPG = 4
CMP_STRIDE = 16
CMP_BLOCK = 32
CMP_HIDDEN = 256
N_CMP_PAD = SEQ // CMP_STRIDE
SEL_BLOCK = 64
N_BLK = SEQ // SEL_BLOCK
SEL_TOPK = 16
WINDOW = 512
ROPE_DIM = 32
ROPE_THETA = 500000.0
Q_SCALE_LOG2 = NSA_DH ** -0.5 * 1.4426950408889634

N_EXPERTS = 32
TOP_K = 4
D_FF = D_MODEL
SWIGLU_LIMIT = 7.0
SWIGLU_ALPHA = 1.702
DN_ALPHA = 2.0 ** 0.25
LN_EPS = 1e-5

COL_GQ, COL_GK, COL_GV, COL_GR, COL_NQ = 0, 512, 1024, 2048, 3072
COL_KV = 4096
COL_TAIL = 5632
D_IN_PAD = 6144
GATE_LANE0 = GLA_RANK

LANE = 128
ROW_CHUNKS = D_MODEL // LANE
BUF_PITCH = ROW_CHUNKS + 8
NEG = -1e30
VMEM_LIMIT = 56 * 1024 * 1024

TM_IN, TN_IN = 1024, 1024
TT_GLA = 512
TR_PREP = 512
TQ_NSA = 256
TK_SEL = 512
WIN_SPAN = WINDOW + TQ_NSA
ONES_ROWS = 16
TM_OUT = 256
OUT_CHAINS = 2
TM_MOE = 256
TN_UP, TN_DOWN = 1024, 2048
P_ROWS = N_TOK * TOP_K + N_EXPERTS * TM_MOE
N_MTILES = P_ROWS // TM_MOE
TM_FIN = 256
DMA_BATCH = 8


def _dot(a, b):
    return jnp.dot(a, b, preferred_element_type=F32)


def _dot_nt(a, b):
    return lax.dot_general(a, b, (((1,), (1,)), ((), ())), preferred_element_type=F32)


def _dot_tn(a, b):
    return lax.dot_general(a, b, (((0,), (0,)), ((), ())), preferred_element_type=F32)


def _ln(x):
    xc = x - jnp.mean(x, -1, keepdims=True)
    return xc * lax.rsqrt(jnp.mean(xc * xc, -1, keepdims=True) + LN_EPS)


def _split_bf16(x):
    hi = x.astype(BF16)
    lo = (x - hi.astype(F32)).astype(BF16)
    return hi, lo


def _params(*sem):
    return pltpu.CompilerParams(dimension_semantics=sem, vmem_limit_bytes=VMEM_LIMIT)


def _adaln_kernel(c_ref, w_ref, b_ref, o_ref):
    c = c_ref[...]
    a = (c * jax.nn.sigmoid(c)).astype(BF16)
    o_ref[...] = _dot(a, w_ref[...].astype(BF16)) + b_ref[...]


def _adaln(c8, w, b):
    n = w.shape[1]
    tn = 1024
    return pl.pallas_call(
        _adaln_kernel,
        grid=(n // tn,),
        in_specs=[pl.BlockSpec((8, D_MODEL), lambda j: (0, 0)),
                  pl.BlockSpec((D_MODEL, tn), lambda j: (0, j)),
                  pl.BlockSpec((1, tn), lambda j: (0, j))],
        out_specs=pl.BlockSpec((8, tn), lambda j: (0, j)),
        out_shape=jax.ShapeDtypeStruct((8, n), F32),
        compiler_params=_params("arbitrary"),
        name="adaln",
    )(c8, w, b)


def _inproj_kernel(x_ref, sh_ref, sc_ref, w_ref, o_ref, h_sc):
    @pl.when(pl.program_id(1) == 0)
    def _():
        h = _ln(x_ref[...]) * (1.0 + sc_ref[...]) + sh_ref[...]
        h_sc[...] = h.astype(BF16)

    o_ref[...] = _dot_nt(h_sc[...], w_ref[...])


def _inproj(xf, mod3, w_in_t):
    tiles_per_batch = SEQ // TM_IN
    return pl.pallas_call(
        _inproj_kernel,
        grid=(N_TOK // TM_IN, D_IN_PAD // TN_IN),
        in_specs=[pl.BlockSpec((TM_IN, D_MODEL), lambda i, j: (i, 0)),
                  pl.BlockSpec((None, 1, D_MODEL), lambda i, j: (i // tiles_per_batch, 0, 0)),
                  pl.BlockSpec((None, 1, D_MODEL), lambda i, j: (i // tiles_per_batch, 0, 1)),
                  pl.BlockSpec((TN_IN, D_MODEL), lambda i, j: (j, 0))],
        out_specs=pl.BlockSpec((TM_IN, TN_IN), lambda i, j: (i, j)),
        out_shape=jax.ShapeDtypeStruct((N_TOK, D_IN_PAD), F32),
        scratch_shapes=[pltpu.VMEM((TM_IN, D_MODEL), BF16)],
        compiler_params=_params("parallel", "arbitrary"),
        name="inproj",
    )(xf, mod3, mod3, w_in_t)


GLA_HALVES = (32, 16, 8, 4, 2, 1)
N_LEVELS = len(GLA_HALVES)
ROW_EB = 2 * N_LEVELS
ROW_EL = 2 * N_LEVELS + 1
N_EVIEWS = 2 * N_LEVELS + 2


def _gla_constants():
    c = GLA_CHUNK
    t = np.arange(c)[:, None]
    r = np.arange(c)[None, :]
    mall = np.zeros((N_EVIEWS, c, c), np.float32)
    valid = np.zeros((N_EVIEWS, c, LANE), np.float32)
    masks = np.zeros((N_LEVELS + 1, c, c), np.float32)
    for li, n in enumerate(GLA_HALVES):
        same = (t // (2 * n)) == (r // (2 * n))
        t_up = (t % (2 * n)) >= n
        r_up = (r % (2 * n)) >= n
        mall[2 * li] = same & t_up & r_up & (r <= t)
        mall[2 * li + 1] = same & ~t_up & ~r_up & (r > t)
        valid[2 * li] = np.broadcast_to(t_up, (c, LANE))
        valid[2 * li + 1] = np.broadcast_to(~t_up, (c, LANE))
        masks[li] = same & t_up & ~r_up
    mall[ROW_EB] = r <= t
    mall[ROW_EL] = r > t
    valid[ROW_EB] = 1.0
    valid[ROW_EL] = 1.0
    masks[N_LEVELS] = np.eye(c)
    return (mall.reshape(N_EVIEWS * c, c), valid.reshape(N_EVIEWS * c, LANE), masks)


def _gla_kernel(q_ref, k_ref, v_ref, r_ref, glr_ref, wgk_ref, bgk_ref, g_ref, mall_ref, valid_ref,
                masks_ref, o_ref, st_sc):
    c = GLA_CHUNK

    @pl.when(pl.program_id(1) == 0)
    def _():
        st_sc[...] = jnp.zeros_like(st_sc)

    z = _dot(glr_ref[...].astype(BF16), wgk_ref[...].astype(BF16)) + bgk_ref[...]
    log_a = (jnp.minimum(z, 0.0) - jnp.log1p(jnp.exp(-jnp.abs(z)))) * (1.0 / GLA_TAU)
    mall = mall_ref[...]
    valid = valid_ref[...]
    for ci in range(TT_GLA // c):
        rows = slice(ci * c, (ci + 1) * c)
        la_hi, la_lo = _split_bf16(log_a[rows])
        e_all = jnp.exp(_dot(mall, la_hi) + _dot(mall, la_lo))
        for h in range(GLA_HEADS):
            kcols = slice(h * GLA_DK, (h + 1) * GLA_DK)
            vcols = slice(h * GLA_DV, (h + 1) * GLA_DV)
            e = e_all[:, kcols] * valid
            q = q_ref[rows, kcols] * (GLA_DK ** -0.5)
            k = k_ref[rows, kcols]
            vb = v_ref[rows, vcols].astype(BF16)
            att = masks_ref[N_LEVELS] * _dot_nt(q.astype(BF16), k.astype(BF16))
            for li in range(N_LEVELS):
                eq = e[(2 * li) * c:(2 * li + 1) * c]
                ek = e[(2 * li + 1) * c:(2 * li + 2) * c]
                att = att + masks_ref[li] * _dot_nt((q * eq).astype(BF16), (k * ek).astype(BF16))
            eb = e[ROW_EB * c:(ROW_EB + 1) * c]
            el = e[ROW_EL * c:(ROW_EL + 1) * c]
            st = st_sc[h]
            o = _dot_nt((q * eb).astype(BF16), st.astype(BF16)) + _dot(att.astype(BF16), vb)
            st_sc[h] = st * eb[c - 1:c, :] + _dot_tn(vb, (k * el).astype(BF16))
            o = o * lax.rsqrt(jnp.mean(o * o, -1, keepdims=True) + LN_EPS) * g_ref[...]
            r = r_ref[rows, vcols]
            o_ref[rows, vcols] = (o * (r * jax.nn.sigmoid(r))).astype(BF16)


def _gla(proj, w_gk_pad, b_gk, norm_g):
    nt = SEQ // TT_GLA
    mall, valid, masks = _gla_constants()
    kw = GLA_HEADS * GLA_DK
    vw = GLA_HEADS * GLA_DV
    full = lambda shape: pl.BlockSpec(shape, lambda b, i: tuple(0 for _ in shape))
    cols = lambda width, col0: pl.BlockSpec((TT_GLA, width), lambda b, i: (b * nt + i, col0 // width))
    return pl.pallas_call(
        _gla_kernel,
        grid=(BATCH, nt),
        in_specs=[cols(kw, COL_GQ), cols(kw, COL_GK), cols(vw, COL_GV), cols(vw, COL_GR),
                  cols(LANE, COL_TAIL), full(w_gk_pad.shape), full(b_gk.shape), full(norm_g.shape),
                  full(mall.shape), full(valid.shape), full(masks.shape)],
        out_specs=cols(vw, 0),
        out_shape=jax.ShapeDtypeStruct((N_TOK, vw), BF16),
        scratch_shapes=[pltpu.VMEM((GLA_HEADS, GLA_DV, GLA_DK), F32)],
        compiler_params=_params("parallel", "arbitrary"),
        name="gla",
    )(proj, proj, proj, proj, proj, w_gk_pad, b_gk, norm_g,
      jnp.asarray(mall, BF16), jnp.asarray(valid), jnp.asarray(masks))


def _rope_tables(pos):
    half = ROPE_DIM // 2
    inv_freq = ROPE_THETA ** (-jnp.arange(half, dtype=F32) * (2.0 / ROPE_DIM))
    ang = pos.astype(F32)[:, None] * inv_freq
    cos, sin = jnp.cos(ang), jnp.sin(ang)
    n = pos.shape[0]
    cosf = jnp.concatenate([cos, cos, jnp.ones((n, LANE - ROPE_DIM), F32)], -1)
    sinf = jnp.concatenate([-sin, sin, jnp.zeros((n, LANE - ROPE_DIM), F32)], -1)
    return cosf, sinf


def _rope(x, cosf, sinf):
    lane = lax.broadcasted_iota(jnp.int32, x.shape, 1)
    half = ROPE_DIM // 2
    swapped = jnp.where(lane < half, pltpu.roll(x, LANE - half, 1), pltpu.roll(x, half, 1))
    return x * cosf + swapped * sinf


def _prep_kernel(q_ref, ks_ref, vs_ref, kw_ref, vw_ref, cos_ref, sin_ref,
                 qo_ref, kso_ref, vso_ref, kwo_ref, vwo_ref):
    cosf = cos_ref[...]
    sinf = sin_ref[...]
    for hh in range(NSA_HEADS):
        cols = slice(hh * NSA_DH, (hh + 1) * NSA_DH)
        qo_ref[:, cols] = (_rope(q_ref[:, cols], cosf, sinf) * Q_SCALE_LOG2).astype(BF16)
    for g in range(NSA_GROUPS):
        cols = slice(g * NSA_DH, (g + 1) * NSA_DH)
        kso_ref[g] = _rope(ks_ref[:, cols], cosf, sinf).astype(BF16)
        kwo_ref[g] = _rope(kw_ref[:, cols], cosf, sinf).astype(BF16)
        vso_ref[g] = vs_ref[:, cols].T.astype(BF16)
        vwo_ref[g] = vw_ref[:, cols].T.astype(BF16)


def _prep(proj, cosf, sinf):
    nt = SEQ // TR_PREP
    kvw = NSA_GROUPS * NSA_DH
    kv_in = lambda which: pl.BlockSpec((TR_PREP, kvw), lambda b, i: (b * nt + i, COL_KV // kvw + which))
    k_out = pl.BlockSpec((None, NSA_GROUPS, TR_PREP, NSA_DH), lambda b, i: (b, 0, i, 0))
    k_shape = jax.ShapeDtypeStruct((BATCH, NSA_GROUPS, SEQ, NSA_DH), BF16)
    vt_out = pl.BlockSpec((None, NSA_GROUPS, NSA_DH, TR_PREP), lambda b, i: (b, 0, 0, i))
    vt_shape = jax.ShapeDtypeStruct((BATCH, NSA_GROUPS, NSA_DH, SEQ), BF16)
    qw = NSA_HEADS * NSA_DH
    return pl.pallas_call(
        _prep_kernel,
        grid=(BATCH, nt),
        in_specs=[pl.BlockSpec((TR_PREP, qw), lambda b, i: (b * nt + i, COL_NQ // qw)),
                  kv_in(2), kv_in(3), kv_in(4), kv_in(5),
                  pl.BlockSpec((TR_PREP, LANE), lambda b, i: (i, 0)),
                  pl.BlockSpec((TR_PREP, LANE), lambda b, i: (i, 0))],
        out_specs=[pl.BlockSpec((TR_PREP, qw), lambda b, i: (b * nt + i, 0)),
                   k_out, vt_out, k_out, vt_out],
        out_shape=[jax.ShapeDtypeStruct((N_TOK, qw), BF16), k_shape, vt_shape, k_shape, vt_shape],
        compiler_params=_params("parallel", "parallel"),
        name="nsa_prep",
    )(proj, proj, proj, proj, proj, cosf, sinf)


def _compress_kernel(a_ref, pe_ref, w1_ref, b1_ref, w2_ref, b2_ref, cos_ref, sin_ref, o_ref, bot_sc,
                     *, rope):
    n = N_CMP_PAD
    top = jnp.zeros((n, CMP_HIDDEN), F32)
    bot = jnp.zeros((n, CMP_HIDDEN), F32)
    for p in range(CMP_STRIDE):
        ap = a_ref[pl.ds(p, n, stride=CMP_STRIDE), :]
        w_top = w1_ref[p * NSA_DH:(p + 1) * NSA_DH, :].astype(BF16)
        w_bot = w1_ref[(CMP_STRIDE + p) * NSA_DH:(CMP_STRIDE + p + 1) * NSA_DH, :].astype(BF16)
        top = top + _dot((ap + pe_ref[p:p + 1, :]).astype(BF16), w_top)
        bot = bot + _dot((ap + pe_ref[CMP_STRIDE + p:CMP_STRIDE + p + 1, :]).astype(BF16), w_bot)
    bot_sc[0:n, :] = bot
    bot_sc[n:n + 8, :] = jnp.zeros((8, CMP_HIDDEN), F32)
    h = top + bot_sc[1:n + 1, :] + b1_ref[...]
    h = h * jax.nn.sigmoid(h)
    out = _dot(h.astype(BF16), w2_ref[...].astype(BF16)) + b2_ref[...]
    if rope:
        out = _rope(out, cos_ref[...], sin_ref[...])
    row = lax.broadcasted_iota(jnp.int32, out.shape, 0)
    out = jnp.where(row < n - 1, out, 0.0)
    o_ref[...] = (out if rope else out.T).astype(BF16)


def _compress(proj, which, pe, w1, b1, w2, b2, cosf, sinf, rope):
    col0 = (COL_KV + which * NSA_GROUPS * NSA_DH) // NSA_DH
    out_dims = (N_CMP_PAD, NSA_DH) if rope else (NSA_DH, N_CMP_PAD)
    full = lambda shape: pl.BlockSpec(shape, lambda b, g: tuple(0 for _ in shape))
    return pl.pallas_call(
        functools.partial(_compress_kernel, rope=rope),
        grid=(BATCH, NSA_GROUPS),
        in_specs=[pl.BlockSpec((SEQ, NSA_DH), lambda b, g: (b, col0 + g)),
                  full(pe.shape), full(w1.shape), full(b1.shape), full(w2.shape), full(b2.shape),
                  full(cosf.shape), full(sinf.shape)],
        out_specs=pl.BlockSpec((None, None) + out_dims, lambda b, g: (b, g, 0, 0)),
        out_shape=jax.ShapeDtypeStruct((BATCH, NSA_GROUPS) + out_dims, BF16),
        scratch_shapes=[pltpu.VMEM((N_CMP_PAD + 8, CMP_HIDDEN), F32)],
        compiler_params=_params("parallel", "parallel"),
        name="nsa_compress",
    )(proj, pe, w1, b1, w2, b2, cosf, sinf)


def _nsa_constants():
    c_start = np.arange(N_CMP_PAD) * CMP_STRIDE
    b_start = np.arange(N_BLK) * SEL_BLOCK
    overlap_t = ((c_start[None, :] < b_start[:, None] + SEL_BLOCK)
                 & (c_start[None, :] + CMP_BLOCK > b_start[:, None])).astype(np.float32)
    overlap_t[:, N_CMP_PAD - 1] = 0.0
    expand_t = (np.arange(SEQ)[:, None] // SEL_BLOCK == np.arange(LANE)[None, :]).astype(np.float32)
    return overlap_t, expand_t


def _tile_heads(a):
    return jnp.concatenate([a] * NSA_HPG, axis=1)


def _nsa_kernel(q_ref, kc_ref, vc_ref, ks_ref, vs_ref, kw_ref, vw_ref, gate_ref, ovt_ref, expand_ref,
                o_ref, score_sc, m_sc, acc_sc):
    tq = TQ_NSA
    g = pl.program_id(1)
    q0 = pl.program_id(2) * tq
    qs = jnp.concatenate([q_ref[:, hh * NSA_DH:(hh + 1) * NSA_DH] for hh in range(NSA_HPG)], axis=0)
    pos_q = q0 + lax.broadcasted_iota(jnp.int32, (1, tq), 1)

    def with_ones(vt):
        return jnp.concatenate([vt, jnp.ones((ONES_ROWS, vt.shape[1]), BF16)], axis=0)

    def normalise(acc):
        return acc[0:NSA_DH] / acc[NSA_DH:NSA_DH + 1]

    s = _dot_nt(kc_ref[...], qs)
    cmp_end = lax.broadcasted_iota(jnp.int32, (N_CMP_PAD, tq), 0) * CMP_STRIDE + (CMP_BLOCK - 1)
    s = s + _tile_heads(jnp.where(cmp_end <= pos_q, 0.0, NEG))
    m = jnp.max(s, 0, keepdims=True)
    e = jnp.where(s > 0.5 * NEG, jnp.exp2(s - m), 0.0)
    p = e / jnp.maximum(jnp.sum(e, 0, keepdims=True), 1e-30)
    o_cmp = _dot(vc_ref[...], p.astype(BF16))
    p_sum = p[:, 0:tq] + p[:, tq:2 * tq] + p[:, 2 * tq:3 * tq] + p[:, 3 * tq:4 * tq]
    ps_hi, ps_lo = _split_bf16(p_sum)
    ovt = ovt_ref[...]
    p_blk_t = _dot(ovt, ps_hi) + _dot(ovt, ps_lo)

    jj = lax.broadcasted_iota(jnp.int32, (N_BLK, tq), 0)
    cur = (q0 + lax.broadcasted_iota(jnp.int32, (N_BLK, tq), 1)) // SEL_BLOCK
    forced = (jj == 0) | (jj == cur) | (jj == cur - 1)
    allowed = jj <= cur
    score = jnp.where(forced, 3.0e38, jnp.where(allowed, p_blk_t, -1.0))
    score_sc[...] = score

    def rank_body(i, rank):
        row = score_sc[pl.ds(i, 1), :]
        first = jnp.where(jj > i, 1.0, 0.0)
        return rank + jnp.where(row > score, 1.0, jnp.where(row == score, first, 0.0))

    n_live = (q0 + tq - 1) // SEL_BLOCK + 1
    rank = lax.fori_loop(0, n_live, rank_body, jnp.zeros((N_BLK, tq), F32))
    sel_t = jnp.where(allowed, jnp.where(rank < SEL_TOPK, 1.0, 0.0), 0.0)
    sel = jnp.concatenate([sel_t, jnp.zeros((LANE - N_BLK, tq), F32)], axis=0).astype(BF16)

    m_sc[...] = jnp.full(m_sc.shape, NEG, F32)
    acc_sc[...] = jnp.zeros(acc_sc.shape, F32)

    def sel_body(kt, carry):
        k0 = pl.multiple_of(kt * TK_SEL, TK_SEL)
        s = _dot_nt(ks_ref[pl.ds(k0, TK_SEL), :], qs)
        sel_x = _dot(expand_ref[pl.ds(k0, TK_SEL), :], sel)
        kpos = k0 + lax.broadcasted_iota(jnp.int32, (TK_SEL, tq), 0)
        bias = jnp.where(kpos <= pos_q, jnp.where(sel_x > 0.5, 0.0, NEG), NEG)
        s = s + _tile_heads(bias)
        m_prev = m_sc[...]
        m_new = jnp.maximum(m_prev, jnp.max(s, 0, keepdims=True))
        alpha = jnp.exp2(m_prev - m_new)
        p = jnp.exp2(s - m_new).astype(BF16)
        acc_sc[...] = alpha * acc_sc[...] + _dot(with_ones(vs_ref[:, pl.ds(k0, TK_SEL)]), p)
        m_sc[...] = m_new
        return carry

    lax.fori_loop(0, (q0 + tq - 1) // TK_SEL + 1, sel_body, 0)
    o_slc = normalise(acc_sc[...])

    start = pl.multiple_of(jnp.maximum(q0 - WINDOW, 0), LANE)
    s = _dot_nt(kw_ref[pl.ds(start, WIN_SPAN), :], qs)
    dist = pos_q - (start + lax.broadcasted_iota(jnp.int32, (WIN_SPAN, tq), 0))
    bias = jnp.where(dist >= 0, jnp.where(dist < WINDOW, 0.0, NEG), NEG)
    s = s + _tile_heads(bias)
    p = jnp.exp2(s - jnp.max(s, 0, keepdims=True)).astype(BF16)
    o_win = normalise(_dot(with_ones(vw_ref[:, pl.ds(start, WIN_SPAN)]), p))

    gates = jax.nn.sigmoid(gate_ref[...]).T

    def gate(hh, branch):
        lane0 = GATE_LANE0 + hh * 3 + branch
        lane1 = lane0 + NSA_HPG * 3
        return jnp.where(g == 0, gates[lane0:lane0 + 1, :], gates[lane1:lane1 + 1, :])

    for hh in range(NSA_HPG):
        c = slice(hh * tq, (hh + 1) * tq)
        o = gate(hh, 0) * o_cmp[:, c] + gate(hh, 1) * o_slc[:, c] + gate(hh, 2) * o_win[:, c]
        o_ref[:, hh * NSA_DH:(hh + 1) * NSA_DH] = o.T.astype(BF16)


def _nsa(q_r, kc, vc, ks, vs, kw, vw, proj):
    nq = SEQ // TQ_NSA
    gw = NSA_HPG * NSA_DH
    overlap_t, expand = _nsa_constants()
    cols = NSA_HPG * TQ_NSA
    per_group = lambda d0, d1: pl.BlockSpec((None, None, d0, d1), lambda b, g, i: (b, g, 0, 0))
    return pl.pallas_call(
        _nsa_kernel,
        grid=(BATCH, NSA_GROUPS, nq),
        in_specs=[pl.BlockSpec((TQ_NSA, gw), lambda b, g, i: (b * nq + i, g)),
                  per_group(N_CMP_PAD, NSA_DH), per_group(NSA_DH, N_CMP_PAD),
                  per_group(SEQ, NSA_DH), per_group(NSA_DH, SEQ),
                  per_group(SEQ, NSA_DH), per_group(NSA_DH, SEQ),
                  pl.BlockSpec((TQ_NSA, LANE), lambda b, g, i: (b * nq + i, COL_TAIL // LANE)),
                  pl.BlockSpec(overlap_t.shape, lambda b, g, i: (0, 0)),
                  pl.BlockSpec(expand.shape, lambda b, g, i: (0, 0))],
        out_specs=pl.BlockSpec((TQ_NSA, gw), lambda b, g, i: (b * nq + i, g)),
        out_shape=jax.ShapeDtypeStruct((N_TOK, NSA_GROUPS * gw), BF16),
        scratch_shapes=[pltpu.VMEM((N_BLK, TQ_NSA), F32),
                        pltpu.VMEM((1, cols), F32),
                        pltpu.VMEM((NSA_DH + ONES_ROWS, cols), F32)],
        compiler_params=_params("parallel", "parallel", "arbitrary"),
        name="nsa_attn",
    )(q_r, kc, vc, ks, vs, kw, vw, proj, jnp.asarray(overlap_t, BF16), jnp.asarray(expand, BF16))


def _outproj_kernel(yg_ref, yn_ref, x_ref, wo_ref, g1_ref, sc2_ref, sh2_ref, ln1g_ref, ln1b_ref,
                    wr_ref, br_ref, ltri_ref, x1_ref, h2_ref, idx_ref, wt_ref, rank_ref,
                    cnt_ref, base_sc):
    half = D_MODEL // 2

    @pl.when(pl.program_id(0) == 0)
    def _():
        base_sc[...] = jnp.zeros_like(base_sc)

    mix_all = _dot(yg_ref[...], wo_ref[0:half, :]) + _dot(yn_ref[...], wo_ref[half:D_MODEL, :])
    wr = wr_ref[...]
    sub = TM_OUT // OUT_CHAINS
    lane = lax.broadcasted_iota(jnp.int32, (TM_OUT, LANE), 1)
    lane_sub = lax.broadcasted_iota(jnp.int32, (sub, LANE), 1)
    lane_f = lane_sub.astype(F32)
    onehot_parts, h2_parts = [], []
    for ci in range(OUT_CHAINS):
        r = slice(ci * sub, (ci + 1) * sub)
        x1 = _ln(DN_ALPHA * x_ref[r, :] + (1.0 + g1_ref[...]) * mix_all[r]) * ln1g_ref[...] + ln1b_ref[...]
        x1_ref[r, :] = x1
        h2 = _ln(x1) * (1.0 + sc2_ref[...]) + sh2_ref[...]
        h_hi, h_lo = _split_bf16(h2)
        h2_parts.append(h2)
        lg = _dot(h_hi, wr) + _dot(h_lo, wr)
        logits = lg[:, 0:LANE] + lg[:, LANE:2 * LANE] + br_ref[...]

        vals = logits
        idx_out = jnp.zeros(logits.shape, jnp.int32)
        exp_out = jnp.zeros(logits.shape, F32)
        denom = jnp.zeros((sub, 1), F32)
        m0 = None
        onehots = []
        for k in range(TOP_K):
            mk = jnp.max(vals, -1, keepdims=True)
            ik = jnp.min(jnp.where(vals == mk, lane_f, float(LANE)), -1, keepdims=True)
            if k == 0:
                m0 = mk
            ek = jnp.exp(mk - m0)
            denom = denom + ek
            idx_out = jnp.where(lane_sub == k, ik.astype(jnp.int32), idx_out)
            exp_out = jnp.where(lane_sub == k, ek, exp_out)
            hit = lane_f == ik
            onehots.append(jnp.where(hit, 1.0, 0.0))
            vals = jnp.where(hit, -3.0e38, vals)
        idx_ref[r, :] = idx_out
        wt_ref[r, :] = exp_out / denom
        onehot_parts.append(onehots)
    onehots = [jnp.concatenate([part[k] for part in onehot_parts], axis=0) for k in range(TOP_K)]
    h2_all = jnp.concatenate(h2_parts, axis=0)
    for s in range(ROW_CHUNKS):
        h2_ref[pl.ds(s, TM_OUT, stride=ROW_CHUNKS), :] = h2_all[:, s * LANE:(s + 1) * LANE]

    cnt = onehots[0] + onehots[1] + onehots[2] + onehots[3]
    base = base_sc[0:1, :]
    before = _dot(ltri_ref[...], cnt.astype(BF16)) + base
    rank_out = jnp.zeros((TM_OUT, LANE), jnp.int32)
    for k in range(TOP_K):
        rk = jnp.sum(onehots[k] * before, -1, keepdims=True)
        rank_out = jnp.where(lane == k, rk.astype(jnp.int32), rank_out)
    rank_ref[...] = rank_out
    base_sc[...] = jnp.broadcast_to(base + jnp.sum(cnt, 0, keepdims=True), base_sc.shape)
    cnt_ref[...] = base_sc[...]


def _outproj(y_gla, y_nsa, xf, w_o, mod3, ln1_g, ln1_b, wr, br):
    tiles_per_batch = SEQ // TM_OUT
    half = D_MODEL // 2
    mod_spec = lambda chunk: pl.BlockSpec((None, 1, D_MODEL), lambda i: (i // tiles_per_batch, 0, chunk))
    full = lambda shape: pl.BlockSpec(shape, lambda i: tuple(0 for _ in shape))
    row = lambda width: pl.BlockSpec((TM_OUT, width), lambda i: (i, 0))
    ltri = jnp.asarray(np.tril(np.ones((TM_OUT, TM_OUT), np.float32), -1), BF16)
    return pl.pallas_call(
        _outproj_kernel,
        grid=(N_TOK // TM_OUT,),
        in_specs=[row(half), row(half), row(D_MODEL), full(w_o.shape),
                  mod_spec(2), mod_spec(4), mod_spec(3),
                  full(ln1_g.shape), full(ln1_b.shape), full(wr.shape), full(br.shape),
                  full(ltri.shape)],
        out_specs=[row(D_MODEL), pl.BlockSpec((TM_OUT * ROW_CHUNKS, LANE), lambda i: (i, 0)),
                   row(LANE), row(LANE), row(LANE), full((8, LANE))],
        out_shape=[jax.ShapeDtypeStruct((N_TOK, D_MODEL), F32),
                   jax.ShapeDtypeStruct((N_TOK * ROW_CHUNKS, LANE), F32),
                   jax.ShapeDtypeStruct((N_TOK, LANE), jnp.int32),
                   jax.ShapeDtypeStruct((N_TOK, LANE), F32),
                   jax.ShapeDtypeStruct((N_TOK, LANE), jnp.int32),
                   jax.ShapeDtypeStruct((8, LANE), F32)],
        scratch_shapes=[pltpu.VMEM((8, LANE), F32)],
        compiler_params=_params("arbitrary"),
        name="outproj_router",
    )(y_gla, y_nsa, xf, w_o, mod3, mod3, mod3, ln1_g, ln1_b, wr, br, ltri)


def _invert_kernel(dest_ref, src_sm):
    def fill(r, carry):
        src_sm[r] = 0
        return carry

    lax.fori_loop(0, P_ROWS, fill, 0, unroll=8)

    def scatter(tok2, carry):
        a0 = tok2 * (2 * TOP_K)
        rows = [dest_ref[a0 + i] for i in range(2 * TOP_K)]
        for i, row in enumerate(rows):
            src_sm[row] = tok2 * 2 + i // TOP_K
        return carry

    lax.fori_loop(0, N_TOK // 2, scatter, 0)


def _invert(dest_flat):
    smem = pl.BlockSpec(memory_space=pltpu.SMEM)
    return pl.pallas_call(
        _invert_kernel,
        in_specs=[smem],
        out_specs=smem,
        out_shape=jax.ShapeDtypeStruct((P_ROWS,), jnp.int32),
        name="moe_invert",
    )(dest_flat)


def _dispatch_kernel(src_sm, nused_ref, h2_hbm, xs_ref, buf, sems):
    t = pl.program_id(0)
    n_used = nused_ref[0]

    def start_gather(tile, slot):
        base = tile * TM_MOE

        def body(r8, carry):
            r0 = r8 * DMA_BATCH
            toks = [src_sm[base + r0 + i] for i in range(DMA_BATCH)]
            for i, tok in enumerate(toks):
                pltpu.make_async_copy(h2_hbm.at[pl.ds(pl.multiple_of(tok * ROW_CHUNKS, ROW_CHUNKS), ROW_CHUNKS), :],
                                      buf.at[slot, pl.ds(pl.multiple_of((r0 + i) * BUF_PITCH, 8), ROW_CHUNKS), :],
                                      sems.at[slot]).start(priority=i % 2)
            return carry

        lax.fori_loop(0, TM_MOE // DMA_BATCH, body, 0)

    @pl.when(t == 0)
    def _():
        start_gather(0, 0)

    @pl.when(t + 1 < n_used)
    def _():
        start_gather(t + 1, (t + 1) % 2)

    @pl.when(t < n_used)
    def _():
        slot = t % 2
        issued = buf.at[slot, pl.ds(0, TM_MOE * ROW_CHUNKS), :]
        pltpu.make_async_copy(issued, issued, sems.at[slot]).wait()
        for s in range(ROW_CHUNKS):
            xs_ref[:, s * LANE:(s + 1) * LANE] = buf[slot, pl.ds(s, TM_MOE, stride=BUF_PITCH), :].astype(BF16)

    @pl.when(t >= n_used)
    def _():
        xs_ref[...] = jnp.zeros(xs_ref.shape, BF16)


def _dispatch(src_tok, n_used, h2):
    return pl.pallas_call(
        _dispatch_kernel,
        grid_spec=pltpu.PrefetchScalarGridSpec(
            num_scalar_prefetch=2,
            grid=(N_MTILES,),
            in_specs=[pl.BlockSpec(memory_space=pl.ANY)],
            out_specs=pl.BlockSpec((TM_MOE, D_MODEL), lambda t, src, nu: (t, 0)),
            scratch_shapes=[pltpu.VMEM((2, TM_MOE * BUF_PITCH, LANE), F32),
                            pltpu.SemaphoreType.DMA((2,))]),
        out_shape=jax.ShapeDtypeStruct((P_ROWS, D_MODEL), BF16),
        compiler_params=_params("arbitrary"),
        name="moe_dispatch",
    )(src_tok, n_used, h2)


class _ExpertWeights:
    def __init__(self, eid_ref, n_used, w_hbms, bufs, sems, cnt_sm):
        self.eid_ref, self.n_used = eid_ref, n_used
        self.w_hbms, self.bufs, self.sems, self.cnt_sm = w_hbms, bufs, sems, cnt_sm

    def _copies(self, expert, j, slot):
        tn = self.bufs[0].shape[-1]
        cols = pl.ds(pl.multiple_of(j * tn, tn), tn)
        return [pltpu.make_async_copy(w.at[expert, :, cols], buf.at[slot], self.sems.at[i, slot])
                for i, (w, buf) in enumerate(zip(self.w_hbms, self.bufs))]

    def prologue(self):
        self.cnt_sm[0] = 0
        for cp in self._copies(self.eid_ref[0], 0, 0):
            cp.start()

    def acquire(self, j, t, n_passes):
        slot = self.cnt_sm[0] % 2
        for cp in self._copies(self.eid_ref[t], j, slot):
            cp.wait()
        expert = self.eid_ref[t]
        nxt = lax.while_loop(
            lambda u: (u < self.n_used) & (self.eid_ref[jnp.minimum(u, N_MTILES - 1)] == expert),
            lambda u: u + 1, t + 1)
        in_pass = nxt < self.n_used

        @pl.when(in_pass | (j + 1 < n_passes))
        def _():
            for cp in self._copies(self.eid_ref[jnp.where(in_pass, nxt, 0)], jnp.where(in_pass, j, j + 1),
                                   1 - slot):
                cp.start()

        self.cnt_sm[0] = self.cnt_sm[0] + 1
        return slot


def _moe_up_kernel(eid_ref, nused_ref, x_ref, wg_hbm, wu_hbm, bg_ref, bu_ref, h_ref,
                   wg_buf, wu_buf, wg_sc, wu_sc, sems, cnt_sm):
    j = pl.program_id(0)
    t = pl.program_id(1)
    n_used = nused_ref[0]
    weights = _ExpertWeights(eid_ref, n_used, (wg_hbm, wu_hbm), (wg_buf, wu_buf), sems, cnt_sm)

    @pl.when((j == 0) & (t == 0))
    def _():
        weights.prologue()

    @pl.when(t < n_used)
    def _():
        @pl.when((t == 0) | (eid_ref[t] != eid_ref[jnp.maximum(t - 1, 0)]))
        def _():
            slot = weights.acquire(j, t, pl.num_programs(0))
            wg_sc[...] = wg_buf[slot].astype(BF16)
            wu_sc[...] = wu_buf[slot].astype(BF16)

        x = x_ref[...]
        gate = jnp.minimum(_dot(x, wg_sc[...]) + bg_ref[...], SWIGLU_LIMIT)
        up = jnp.clip(_dot(x, wu_sc[...]) + bu_ref[...], -SWIGLU_LIMIT, SWIGLU_LIMIT)
        h_ref[...] = (gate * jax.nn.sigmoid(SWIGLU_ALPHA * gate) * (up + 1.0)).astype(BF16)

    @pl.when(t >= n_used)
    def _():
        h_ref[...] = jnp.zeros(h_ref.shape, BF16)


def _moe_down_kernel(eid_ref, nused_ref, h_ref, wd_hbm, bd_ref, y_ref, wd_buf, wd_sc, sems, cnt_sm):
    j = pl.program_id(0)
    t = pl.program_id(1)
    n_used = nused_ref[0]
    weights = _ExpertWeights(eid_ref, n_used, (wd_hbm,), (wd_buf,), sems, cnt_sm)

    @pl.when((j == 0) & (t == 0))
    def _():
        weights.prologue()

    @pl.when(t < n_used)
    def _():
        @pl.when((t == 0) | (eid_ref[t] != eid_ref[jnp.maximum(t - 1, 0)]))
        def _():
            slot = weights.acquire(j, t, pl.num_programs(0))
            wd_sc[...] = wd_buf[slot].astype(BF16)

        y_ref[...] = _dot(h_ref[...], wd_sc[...]) + bd_ref[...]

    @pl.when(t >= n_used)
    def _():
        y_ref[...] = jnp.zeros(y_ref.shape, F32)


def _row_tile(t, nused_ref):
    return jnp.minimum(t, jnp.maximum(nused_ref[0] - 1, 0))


def _moe_call(body, name, n_weights, k_dim, out_dim, tn, out_dtype, tile_eid, n_used, rows, weights, biases):
    any_spec = pl.BlockSpec(memory_space=pl.ANY)
    b_spec = pl.BlockSpec((None, 1, tn), lambda j, t, eid, nu: (eid[t], 0, j))
    return pl.pallas_call(
        body,
        grid_spec=pltpu.PrefetchScalarGridSpec(
            num_scalar_prefetch=2,
            grid=(out_dim // tn, N_MTILES),
            in_specs=[pl.BlockSpec((TM_MOE, k_dim), lambda j, t, eid, nu: (_row_tile(t, nu), 0))]
            + [any_spec] * n_weights + [b_spec] * n_weights,
            out_specs=pl.BlockSpec((TM_MOE, tn), lambda j, t, eid, nu: (t, j)),
            scratch_shapes=[pltpu.VMEM((2, k_dim, tn), F32)] * n_weights
            + [pltpu.VMEM((k_dim, tn), BF16)] * n_weights
            + [pltpu.SemaphoreType.DMA((n_weights, 2)), pltpu.SMEM((1,), jnp.int32)]),
        out_shape=jax.ShapeDtypeStruct((P_ROWS, out_dim), out_dtype),
        compiler_params=_params("arbitrary", "arbitrary"),
        name=name,
    )(tile_eid, n_used, rows, *weights, *biases)


def _moe_up(tile_eid, n_used, xs, w_gate, w_up, b_gate, b_up):
    return _moe_call(_moe_up_kernel, "moe_up", 2, D_MODEL, D_FF, TN_UP, BF16, tile_eid, n_used, xs,
                     (w_gate, w_up), (b_gate, b_up))


def _moe_down(tile_eid, n_used, h, w_down, b_down):
    return _moe_call(_moe_down_kernel, "moe_down", 1, D_FF, D_MODEL, TN_DOWN, F32, tile_eid, n_used, h,
                     (w_down,), (b_down,))


def _final_kernel(dest_ref, y_hbm, x1_ref, wt_ref, g2_ref, ln2g_ref, ln2b_ref, o_ref, ybuf, sems):
    i = pl.program_id(0)
    n_tiles = pl.num_programs(0)

    def start_gather(tile, slot):
        base = tile * (TM_FIN * TOP_K)

        def body(r2, carry):
            a0 = base + r2 * DMA_BATCH
            rows = [dest_ref[a0 + i] for i in range(DMA_BATCH)]
            for i, d in enumerate(rows):
                r = r2 * (DMA_BATCH // TOP_K) + i // TOP_K
                pltpu.make_async_copy(y_hbm.at[pl.ds(d, 1), :], ybuf.at[slot, i % TOP_K, pl.ds(r, 1), :],
                                      sems.at[slot]).start(priority=i % 2)
            return carry

        lax.fori_loop(0, TM_FIN * TOP_K // DMA_BATCH, body, 0)

    @pl.when(i == 0)
    def _():
        start_gather(0, 0)

    @pl.when(i + 1 < n_tiles)
    def _():
        start_gather(i + 1, (i + 1) % 2)

    slot = i % 2
    pltpu.make_async_copy(ybuf.at[slot], ybuf.at[slot], sems.at[slot]).wait()
    wt = wt_ref[...]
    y = wt[:, 0:1] * ybuf[slot, 0]
    for k in range(1, TOP_K):
        y = y + wt[:, k:k + 1] * ybuf[slot, k]
    o_ref[...] = _ln(DN_ALPHA * x1_ref[...] + (1.0 + g2_ref[...]) * y) * ln2g_ref[...] + ln2b_ref[...]


def _final(dest_flat, y, x1, wt, mod3, ln2_g, ln2_b):
    tiles_per_batch = SEQ // TM_FIN
    full = lambda shape: pl.BlockSpec(shape, lambda i, dest: tuple(0 for _ in shape))
    row = lambda width: pl.BlockSpec((TM_FIN, width), lambda i, dest: (i, 0))
    return pl.pallas_call(
        _final_kernel,
        grid_spec=pltpu.PrefetchScalarGridSpec(
            num_scalar_prefetch=1,
            grid=(N_TOK // TM_FIN,),
            in_specs=[pl.BlockSpec(memory_space=pl.ANY), row(D_MODEL), row(LANE),
                      pl.BlockSpec((None, 1, D_MODEL), lambda i, dest: (i // tiles_per_batch, 0, 5)),
                      full(ln2_g.shape), full(ln2_b.shape)],
            out_specs=row(D_MODEL),
            scratch_shapes=[pltpu.VMEM((2, TOP_K, TM_FIN, D_MODEL), F32),
                            pltpu.SemaphoreType.DMA((2,))]),
        out_shape=jax.ShapeDtypeStruct((N_TOK, D_MODEL), F32),
        compiler_params=_params("arbitrary"),
        name="combine_ln2",
    )(dest_flat, y, x1, wt, mod3, ln2_g, ln2_b)


def _route(idx, rank, counts):
    experts = jnp.arange(N_EXPERTS, dtype=jnp.int32)
    padded = ((counts + TM_MOE - 1) // TM_MOE) * TM_MOE
    ends = jnp.sum(jnp.where(experts[None, :] <= experts[:, None], padded[None, :], 0), axis=1)
    starts = ends - padded
    dest = rank
    for e in range(N_EXPERTS):
        dest = dest + jnp.where(idx == e, starts[e], 0)
    dest = dest[:, :TOP_K]
    tile_start = jnp.arange(N_MTILES, dtype=jnp.int32) * TM_MOE
    tile_eid = jnp.minimum(jnp.sum((ends[None, :] <= tile_start[:, None]).astype(jnp.int32), axis=1),
                           N_EXPERTS - 1)
    n_used = (ends[-1] // TM_MOE).astype(jnp.int32).reshape(1)
    return dest.reshape(-1), tile_eid, n_used


def kernel(x, c, w_ada, b_ada, w_in, w_gk, b_gk, gla_norm_g, pe_k, pe_v, w_ck1, b_ck1, w_ck2, b_ck2,
           w_cv1, b_cv1, w_cv2, b_cv2, w_o, ln1_g, ln1_b, w_router, b_router, w_gate, b_gate, w_up, b_up,
           w_down, b_down, ln2_g, ln2_b):
    l = 0
    xf = x.reshape(N_TOK, D_MODEL)
    row2 = lambda a: a.reshape(1, -1)

    c8 = jnp.pad(c, ((0, 8 - BATCH), (0, 0)))
    mod3 = _adaln(c8, w_ada[l], row2(b_ada[l]))[:BATCH].reshape(BATCH, 1, 6 * D_MODEL)

    wt = w_in[l].T
    glr0 = 3072
    nsa0 = glr0 + GLA_RANK
    ngt0 = nsa0 + 1024 + 6 * 256
    w_in_t = jnp.concatenate(
        [wt[:glr0], wt[nsa0:ngt0], wt[glr0:nsa0], wt[ngt0:],
         jnp.zeros((D_IN_PAD - wt.shape[0], D_MODEL), F32)], axis=0).astype(BF16)
    proj = _inproj(xf, mod3, w_in_t)

    w_gk_pad = jnp.pad(w_gk[l], ((0, LANE - GLA_RANK), (0, 0)))
    y_gla = _gla(proj, w_gk_pad, row2(b_gk[l]), row2(gla_norm_g[l]))

    cos_t, sin_t = _rope_tables(jnp.arange(SEQ))
    cmp_end = jnp.arange(N_CMP_PAD) * CMP_STRIDE + (CMP_BLOCK - 1)
    cos_c, sin_c = _rope_tables(cmp_end)
    q_r, ks, vs, kw, vw = _prep(proj, cos_t, sin_t)
    kc = _compress(proj, 0, pe_k[l], w_ck1[l], row2(b_ck1[l]), w_ck2[l], row2(b_ck2[l]), cos_c, sin_c, True)
    vc = _compress(proj, 1, pe_v[l], w_cv1[l], row2(b_cv1[l]), w_cv2[l], row2(b_cv2[l]), cos_c, sin_c, False)
    y_nsa = _nsa(q_r, kc, vc, ks, vs, kw, vw, proj)

    wr = jnp.pad(w_router[l], ((0, 0), (0, LANE - N_EXPERTS)))
    wr_hi, wr_lo = _split_bf16(wr)
    br = jnp.concatenate([b_router[l], jnp.full((LANE - N_EXPERTS,), NEG, F32)]).reshape(1, LANE)
    x1, h2, idx128, wt128, rank128, cnt8 = _outproj(y_gla, y_nsa, xf, w_o[l].astype(BF16), mod3,
                                                    row2(ln1_g[l]), row2(ln1_b[l]),
                                                    jnp.concatenate([wr_hi, wr_lo], axis=1), br)

    counts = cnt8[0, :N_EXPERTS].astype(jnp.int32)
    dest, tile_eid, n_used = _route(idx128, rank128, counts)
    xs = _dispatch(_invert(dest), n_used, h2)
    h = _moe_up(tile_eid, n_used, xs, w_gate[l], w_up[l],
                b_gate[l].reshape(N_EXPERTS, 1, D_FF), b_up[l].reshape(N_EXPERTS, 1, D_FF))
    y = _moe_down(tile_eid, n_used, h, w_down[l], b_down[l].reshape(N_EXPERTS, 1, D_MODEL))

    out = _final(dest, y, x1, wt128, mod3, row2(ln2_g[l]), row2(ln2_b[l]))
    return out.reshape(BATCH, SEQ, D_MODEL)
```

```python
import functools

import numpy as np
import jax
import jax.numpy as jnp
from jax import lax
from jax.experimental import pallas as pl
from jax.experimental.pallas import tpu as pltpu

F32 = jnp.float32
BF16 = jnp.bfloat16

D_MODEL = 2048
BATCH = 2
SEQ = 4096
N_TOK = BATCH * SEQ

GLA_HEADS = 4
GLA_DK = 128
GLA_DV = 256
GLA_RANK = 16
GLA_TAU = 16.0
GLA_CHUNK = 64

NSA_DH = 128
NSA_HEADS = 8
NSA_GROUPS = 2
NSA_HPG = 4
CMP_STRIDE = 16
CMP_BLOCK = 32
CMP_HIDDEN = 256
N_CMP_PAD = SEQ // CMP_STRIDE
SEL_BLOCK = 64
N_BLK = SEQ // SEL_BLOCK
SEL_TOPK = 16
WINDOW = 512
ROPE_DIM = 32
ROPE_THETA = 500000.0
Q_SCALE_LOG2 = NSA_DH ** -0.5 * 1.4426950408889634

N_EXPERTS = 32
TOP_K = 4
D_FF = D_MODEL
SWIGLU_LIMIT = 7.0
SWIGLU_ALPHA = 1.702
DN_ALPHA = 2.0 ** 0.25
LN_EPS = 1e-5

COL_GQ, COL_GK, COL_GV, COL_GR, COL_NQ = 0, 512, 1024, 2048, 3072
COL_KV = 4096
COL_TAIL = 5632
D_IN_PAD = 6144
GATE_LANE0 = GLA_RANK

LANE = 128
ROW_CHUNKS = D_MODEL // LANE
BUF_PITCH = ROW_CHUNKS + 8
NEG = -1e30
VMEM_LIMIT = 56 * 1024 * 1024

TM_IN, TN_IN = 1024, 1024
TT_GLA = 512
TR_PREP = 512
TQ_NSA = 256
TK_SEL = 512
WIN_SPAN = WINDOW + TQ_NSA
ONES_ROWS = 16
TM_OUT = 256
OUT_CHAINS = 2
TM_MOE = 256
TN_UP, TN_DOWN = 1024, 2048
P_ROWS = N_TOK * TOP_K + N_EXPERTS * TM_MOE
N_MTILES = P_ROWS // TM_MOE
TM_FIN = 256
DMA_BATCH = 8


def _dot(a, b):
    return jnp.dot(a, b, preferred_element_type=F32)


def _dot_nt(a, b):
    return lax.dot_general(a, b, (((1,), (1,)), ((), ())), preferred_element_type=F32)


def _dot_tn(a, b):
    return lax.dot_general(a, b, (((0,), (0,)), ((), ())), preferred_element_type=F32)


def _ln(x):
    xc = x - jnp.mean(x, -1, keepdims=True)
    return xc * lax.rsqrt(jnp.mean(xc * xc, -1, keepdims=True) + LN_EPS)


def _split_bf16(x):
    hi = x.astype(BF16)
    lo = (x - hi.astype(F32)).astype(BF16)
    return hi, lo


def _params(*sem):
    return pltpu.CompilerParams(dimension_semantics=sem, vmem_limit_bytes=VMEM_LIMIT)


def _adaln_kernel(c_ref, w_ref, b_ref, o_ref):
    c = c_ref[...]
    a = (c * jax.nn.sigmoid(c)).astype(BF16)
    o_ref[...] = _dot(a, w_ref[...].astype(BF16)) + b_ref[...]


def _adaln(c8, w, b):
    n = w.shape[1]
    tn = 1024
    return pl.pallas_call(
        _adaln_kernel,
        grid=(n // tn,),
        in_specs=[pl.BlockSpec((8, D_MODEL), lambda j: (0, 0)),
                  pl.BlockSpec((D_MODEL, tn), lambda j: (0, j)),
                  pl.BlockSpec((1, tn), lambda j: (0, j))],
        out_specs=pl.BlockSpec((8, tn), lambda j: (0, j)),
        out_shape=jax.ShapeDtypeStruct((8, n), F32),
        compiler_params=_params("arbitrary"),
        name="adaln",
    )(c8, w, b)


def _inproj_kernel(x_ref, sh_ref, sc_ref, w_ref, o_ref, h_sc):
    @pl.when(pl.program_id(1) == 0)
    def _():
        h = _ln(x_ref[...]) * (1.0 + sc_ref[...]) + sh_ref[...]
        h_sc[...] = h.astype(BF16)

    o_ref[...] = _dot_nt(h_sc[...], w_ref[...])


def _inproj(xf, mod3, w_in_t):
    tiles_per_batch = SEQ // TM_IN
    return pl.pallas_call(
        _inproj_kernel,
        grid=(N_TOK // TM_IN, D_IN_PAD // TN_IN),
        in_specs=[pl.BlockSpec((TM_IN, D_MODEL), lambda i, j: (i, 0)),
                  pl.BlockSpec((None, 1, D_MODEL), lambda i, j: (i // tiles_per_batch, 0, 0)),
                  pl.BlockSpec((None, 1, D_MODEL), lambda i, j: (i // tiles_per_batch, 0, 1)),
                  pl.BlockSpec((TN_IN, D_MODEL), lambda i, j: (j, 0))],
        out_specs=pl.BlockSpec((TM_IN, TN_IN), lambda i, j: (i, j)),
        out_shape=jax.ShapeDtypeStruct((N_TOK, D_IN_PAD), F32),
        scratch_shapes=[pltpu.VMEM((TM_IN, D_MODEL), BF16)],
        compiler_params=_params("parallel", "arbitrary"),
        name="inproj",
    )(xf, mod3, mod3, w_in_t)


GLA_HALVES = (32, 16, 8, 4, 2, 1)
N_LEVELS = len(GLA_HALVES)
ROW_EB = 2 * N_LEVELS
ROW_EL = 2 * N_LEVELS + 1
N_EVIEWS = 2 * N_LEVELS + 2


def _gla_constants():
    c = GLA_CHUNK
    t = np.arange(c)[:, None]
    r = np.arange(c)[None, :]
    mall = np.zeros((N_EVIEWS, c, c), np.float32)
    valid = np.zeros((N_EVIEWS, c, LANE), np.float32)
    masks = np.zeros((N_LEVELS + 1, c, c), np.float32)
    for li, n in enumerate(GLA_HALVES):
        same = (t // (2 * n)) == (r // (2 * n))
        t_up = (t % (2 * n)) >= n
        r_up = (r % (2 * n)) >= n
        mall[2 * li] = same & t_up & r_up & (r <= t)
        mall[2 * li + 1] = same & ~t_up & ~r_up & (r > t)
        valid[2 * li] = np.broadcast_to(t_up, (c, LANE))
        valid[2 * li + 1] = np.broadcast_to(~t_up, (c, LANE))
        masks[li] = same & t_up & ~r_up
    mall[ROW_EB] = r <= t
    mall[ROW_EL] = r > t
    valid[ROW_EB] = 1.0
    valid[ROW_EL] = 1.0
    masks[N_LEVELS] = np.eye(c)
    return (mall.reshape(N_EVIEWS * c, c), valid.reshape(N_EVIEWS * c, LANE), masks)


def _gla_kernel(q_ref, k_ref, v_ref, r_ref, glr_ref, wgk_ref, bgk_ref, g_ref, mall_ref, valid_ref,
                masks_ref, o_ref, st_sc):
    c = GLA_CHUNK

    @pl.when(pl.program_id(1) == 0)
    def _():
        st_sc[...] = jnp.zeros_like(st_sc)

    z = _dot(glr_ref[...].astype(BF16), wgk_ref[...].astype(BF16)) + bgk_ref[...]
    log_a = (jnp.minimum(z, 0.0) - jnp.log1p(jnp.exp(-jnp.abs(z)))) * (1.0 / GLA_TAU)
    mall = mall_ref[...]
    valid = valid_ref[...]
    for ci in range(TT_GLA // c):
        rows = slice(ci * c, (ci + 1) * c)
        la_hi, la_lo = _split_bf16(log_a[rows])
        e_all = jnp.exp(_dot(mall, la_hi) + _dot(mall, la_lo))
        for h in range(GLA_HEADS):
            kcols = slice(h * GLA_DK, (h + 1) * GLA_DK)
            vcols = slice(h * GLA_DV, (h + 1) * GLA_DV)
            e = e_all[:, kcols] * valid
            q = q_ref[rows, kcols] * (GLA_DK ** -0.5)
            k = k_ref[rows, kcols]
            vb = v_ref[rows, vcols].astype(BF16)
            att = masks_ref[N_LEVELS] * _dot_nt(q.astype(BF16), k.astype(BF16))
            for li in range(N_LEVELS):
                eq = e[(2 * li) * c:(2 * li + 1) * c]
                ek = e[(2 * li + 1) * c:(2 * li + 2) * c]
                att = att + masks_ref[li] * _dot_nt((q * eq).astype(BF16), (k * ek).astype(BF16))
            eb = e[ROW_EB * c:(ROW_EB + 1) * c]
            el = e[ROW_EL * c:(ROW_EL + 1) * c]
            st = st_sc[h]
            o = _dot_nt((q * eb).astype(BF16), st.astype(BF16)) + _dot(att.astype(BF16), vb)
            st_sc[h] = st * eb[c - 1:c, :] + _dot_tn(vb, (k * el).astype(BF16))
            o = o * lax.rsqrt(jnp.mean(o * o, -1, keepdims=True) + LN_EPS) * g_ref[...]
            r = r_ref[rows, vcols]
            o_ref[rows, vcols] = (o * (r * jax.nn.sigmoid(r))).astype(BF16)


def _gla(proj, w_gk_pad, b_gk, norm_g):
    nt = SEQ // TT_GLA
    mall, valid, masks = _gla_constants()
    kw = GLA_HEADS * GLA_DK
    vw = GLA_HEADS * GLA_DV
    full = lambda shape: pl.BlockSpec(shape, lambda b, i: tuple(0 for _ in shape))
    cols = lambda width, col0: pl.BlockSpec((TT_GLA, width), lambda b, i: (b * nt + i, col0 // width))
    return pl.pallas_call(
        _gla_kernel,
        grid=(BATCH, nt),
        in_specs=[cols(kw, COL_GQ), cols(kw, COL_GK), cols(vw, COL_GV), cols(vw, COL_GR),
                  cols(LANE, COL_TAIL), full(w_gk_pad.shape), full(b_gk.shape), full(norm_g.shape),
                  full(mall.shape), full(valid.shape), full(masks.shape)],
        out_specs=cols(vw, 0),
        out_shape=jax.ShapeDtypeStruct((N_TOK, vw), BF16),
        scratch_shapes=[pltpu.VMEM((GLA_HEADS, GLA_DV, GLA_DK), F32)],
        compiler_params=_params("parallel", "arbitrary"),
        name="gla",
    )(proj, proj, proj, proj, proj, w_gk_pad, b_gk, norm_g,
      jnp.asarray(mall, BF16), jnp.asarray(valid), jnp.asarray(masks))


def _rope_tables(pos):
    half = ROPE_DIM // 2
    inv_freq = ROPE_THETA ** (-jnp.arange(half, dtype=F32) * (2.0 / ROPE_DIM))
    ang = pos.astype(F32)[:, None] * inv_freq
    cos, sin = jnp.cos(ang), jnp.sin(ang)
    n = pos.shape[0]
    cosf = jnp.concatenate([cos, cos, jnp.ones((n, LANE - ROPE_DIM), F32)], -1)
    sinf = jnp.concatenate([-sin, sin, jnp.zeros((n, LANE - ROPE_DIM), F32)], -1)
    return cosf, sinf


def _rope(x, cosf, sinf):
    lane = lax.broadcasted_iota(jnp.int32, x.shape, 1)
    half = ROPE_DIM // 2
    swapped = jnp.where(lane < half, pltpu.roll(x, LANE - half, 1), pltpu.roll(x, half, 1))
    return x * cosf + swapped * sinf


def _prep_kernel(q_ref, ks_ref, vs_ref, kw_ref, vw_ref, cos_ref, sin_ref,
                 qo_ref, kso_ref, vso_ref, kwo_ref, vwo_ref):
    cosf = cos_ref[...]
    sinf = sin_ref[...]
    for hh in range(NSA_HEADS):
        cols = slice(hh * NSA_DH, (hh + 1) * NSA_DH)
        qo_ref[:, cols] = (_rope(q_ref[:, cols], cosf, sinf) * Q_SCALE_LOG2).astype(BF16)
    for g in range(NSA_GROUPS):
        cols = slice(g * NSA_DH, (g + 1) * NSA_DH)
        kso_ref[g] = _rope(ks_ref[:, cols], cosf, sinf).astype(BF16)
        kwo_ref[g] = _rope(kw_ref[:, cols], cosf, sinf).astype(BF16)
        vso_ref[g] = vs_ref[:, cols].T.astype(BF16)
        vwo_ref[g] = vw_ref[:, cols].T.astype(BF16)


def _prep(proj, cosf, sinf):
    nt = SEQ // TR_PREP
    kvw = NSA_GROUPS * NSA_DH
    kv_in = lambda which: pl.BlockSpec((TR_PREP, kvw), lambda b, i: (b * nt + i, COL_KV // kvw + which))
    k_out = pl.BlockSpec((None, NSA_GROUPS, TR_PREP, NSA_DH), lambda b, i: (b, 0, i, 0))
    k_shape = jax.ShapeDtypeStruct((BATCH, NSA_GROUPS, SEQ, NSA_DH), BF16)
    vt_out = pl.BlockSpec((None, NSA_GROUPS, NSA_DH, TR_PREP), lambda b, i: (b, 0, 0, i))
    vt_shape = jax.ShapeDtypeStruct((BATCH, NSA_GROUPS, NSA_DH, SEQ), BF16)
    qw = NSA_HEADS * NSA_DH
    return pl.pallas_call(
        _prep_kernel,
        grid=(BATCH, nt),
        in_specs=[pl.BlockSpec((TR_PREP, qw), lambda b, i: (b * nt + i, COL_NQ // qw)),
                  kv_in(2), kv_in(3), kv_in(4), kv_in(5),
                  pl.BlockSpec((TR_PREP, LANE), lambda b, i: (i, 0)),
                  pl.BlockSpec((TR_PREP, LANE), lambda b, i: (i, 0))],
        out_specs=[pl.BlockSpec((TR_PREP, qw), lambda b, i: (b * nt + i, 0)),
                   k_out, vt_out, k_out, vt_out],
        out_shape=[jax.ShapeDtypeStruct((N_TOK, qw), BF16), k_shape, vt_shape, k_shape, vt_shape],
        compiler_params=_params("parallel", "parallel"),
        name="nsa_prep",
    )(proj, proj, proj, proj, proj, cosf, sinf)


def _compress_kernel(a_ref, pe_ref, w1_ref, b1_ref, w2_ref, b2_ref, cos_ref, sin_ref, o_ref, bot_sc,
                     *, rope):
    n = N_CMP_PAD
    top = jnp.zeros((n, CMP_HIDDEN), F32)
    bot = jnp.zeros((n, CMP_HIDDEN), F32)
    for p in range(CMP_STRIDE):
        ap = a_ref[pl.ds(p, n, stride=CMP_STRIDE), :]
        w_top = w1_ref[p * NSA_DH:(p + 1) * NSA_DH, :].astype(BF16)
        w_bot = w1_ref[(CMP_STRIDE + p) * NSA_DH:(CMP_STRIDE + p + 1) * NSA_DH, :].astype(BF16)
        top = top + _dot((ap + pe_ref[p:p + 1, :]).astype(BF16), w_top)
        bot = bot + _dot((ap + pe_ref[CMP_STRIDE + p:CMP_STRIDE + p + 1, :]).astype(BF16), w_bot)
    bot_sc[0:n, :] = bot
    bot_sc[n:n + 8, :] = jnp.zeros((8, CMP_HIDDEN), F32)
    h = top + bot_sc[1:n + 1, :] + b1_ref[...]
    h = h * jax.nn.sigmoid(h)
    out = _dot(h.astype(BF16), w2_ref[...].astype(BF16)) + b2_ref[...]
    if rope:
        out = _rope(out, cos_ref[...], sin_ref[...])
    row = lax.broadcasted_iota(jnp.int32, out.shape, 0)
    out = jnp.where(row < n - 1, out, 0.0)
    o_ref[...] = (out if rope else out.T).astype(BF16)


def _compress(proj, which, pe, w1, b1, w2, b2, cosf, sinf, rope):
    col0 = (COL_KV + which * NSA_GROUPS * NSA_DH) // NSA_DH
    out_dims = (N_CMP_PAD, NSA_DH) if rope else (NSA_DH, N_CMP_PAD)
    full = lambda shape: pl.BlockSpec(shape, lambda b, g: tuple(0 for _ in shape))
    return pl.pallas_call(
        functools.partial(_compress_kernel, rope=rope),
        grid=(BATCH, NSA_GROUPS),
        in_specs=[pl.BlockSpec((SEQ, NSA_DH), lambda b, g: (b, col0 + g)),
                  full(pe.shape), full(w1.shape), full(b1.shape), full(w2.shape), full(b2.shape),
                  full(cosf.shape), full(sinf.shape)],
        out_specs=pl.BlockSpec((None, None) + out_dims, lambda b, g: (b, g, 0, 0)),
        out_shape=jax.ShapeDtypeStruct((BATCH, NSA_GROUPS) + out_dims, BF16),
        scratch_shapes=[pltpu.VMEM((N_CMP_PAD + 8, CMP_HIDDEN), F32)],
        compiler_params=_params("parallel", "parallel"),
        name="nsa_compress",
    )(proj, pe, w1, b1, w2, b2, cosf, sinf)


def _nsa_constants():
    c_start = np.arange(N_CMP_PAD) * CMP_STRIDE
    b_start = np.arange(N_BLK) * SEL_BLOCK
    overlap_t = ((c_start[None, :] < b_start[:, None] + SEL_BLOCK)
                 & (c_start[None, :] + CMP_BLOCK > b_start[:, None])).astype(np.float32)
    overlap_t[:, N_CMP_PAD - 1] = 0.0
    expand_t = (np.arange(SEQ)[:, None] // SEL_BLOCK == np.arange(LANE)[None, :]).astype(np.float32)
    return overlap_t, expand_t


def _tile_heads(a):
    return jnp.concatenate([a] * NSA_HPG, axis=1)


def _nsa_kernel(q_ref, kc_ref, vc_ref, ks_ref, vs_ref, kw_ref, vw_ref, gate_ref, ovt_ref, expand_ref,
                o_ref, score_sc, m_sc, acc_sc):
    tq = TQ_NSA
    g = pl.program_id(1)
    q0 = pl.program_id(2) * tq
    qs = jnp.concatenate([q_ref[:, hh * NSA_DH:(hh + 1) * NSA_DH] for hh in range(NSA_HPG)], axis=0)
    pos_q = q0 + lax.broadcasted_iota(jnp.int32, (1, tq), 1)

    def with_ones(vt):
        return jnp.concatenate([vt, jnp.ones((ONES_ROWS, vt.shape[1]), BF16)], axis=0)

    def normalise(acc):
        return acc[0:NSA_DH] / acc[NSA_DH:NSA_DH + 1]

    s = _dot_nt(kc_ref[...], qs)
    cmp_end = lax.broadcasted_iota(jnp.int32, (N_CMP_PAD, tq), 0) * CMP_STRIDE + (CMP_BLOCK - 1)
    s = s + _tile_heads(jnp.where(cmp_end <= pos_q, 0.0, NEG))
    m = jnp.max(s, 0, keepdims=True)
    e = jnp.where(s > 0.5 * NEG, jnp.exp2(s - m), 0.0)
    p = e / jnp.maximum(jnp.sum(e, 0, keepdims=True), 1e-30)
    o_cmp = _dot(vc_ref[...], p.astype(BF16))
    p_sum = p[:, 0:tq] + p[:, tq:2 * tq] + p[:, 2 * tq:3 * tq] + p[:, 3 * tq:4 * tq]
    ps_hi, ps_lo = _split_bf16(p_sum)
    ovt = ovt_ref[...]
    p_blk_t = _dot(ovt, ps_hi) + _dot(ovt, ps_lo)

    jj = lax.broadcasted_iota(jnp.int32, (N_BLK, tq), 0)
    cur = (q0 + lax.broadcasted_iota(jnp.int32, (N_BLK, tq), 1)) // SEL_BLOCK
    forced = (jj == 0) | (jj == cur) | (jj == cur - 1)
    allowed = jj <= cur
    score = jnp.where(forced, 3.0e38, jnp.where(allowed, p_blk_t, -1.0))
    score_sc[...] = score

    def rank_body(i, rank):
        row = score_sc[pl.ds(i, 1), :]
        first = jnp.where(jj > i, 1.0, 0.0)
        return rank + jnp.where(row > score, 1.0, jnp.where(row == score, first, 0.0))

    n_live = (q0 + tq - 1) // SEL_BLOCK + 1
    rank = lax.fori_loop(0, n_live, rank_body, jnp.zeros((N_BLK, tq), F32))
    sel_t = jnp.where(allowed, jnp.where(rank < SEL_TOPK, 1.0, 0.0), 0.0)
    sel = jnp.concatenate([sel_t, jnp.zeros((LANE - N_BLK, tq), F32)], axis=0).astype(BF16)

    m_sc[...] = jnp.full(m_sc.shape, NEG, F32)
    acc_sc[...] = jnp.zeros(acc_sc.shape, F32)

    def sel_body(kt, carry):
        k0 = pl.multiple_of(kt * TK_SEL, TK_SEL)
        s = _dot_nt(ks_ref[pl.ds(k0, TK_SEL), :], qs)
        sel_x = _dot(expand_ref[pl.ds(k0, TK_SEL), :], sel)
        kpos = k0 + lax.broadcasted_iota(jnp.int32, (TK_SEL, tq), 0)
        bias = jnp.where(kpos <= pos_q, jnp.where(sel_x > 0.5, 0.0, NEG), NEG)
        s = s + _tile_heads(bias)
        m_prev = m_sc[...]
        m_new = jnp.maximum(m_prev, jnp.max(s, 0, keepdims=True))
        alpha = jnp.exp2(m_prev - m_new)
        p = jnp.exp2(s - m_new).astype(BF16)
        acc_sc[...] = alpha * acc_sc[...] + _dot(with_ones(vs_ref[:, pl.ds(k0, TK_SEL)]), p)
        m_sc[...] = m_new
        return carry

    lax.fori_loop(0, (q0 + tq - 1) // TK_SEL + 1, sel_body, 0)
    o_slc = normalise(acc_sc[...])

    start = pl.multiple_of(jnp.maximum(q0 - WINDOW, 0), LANE)
    s = _dot_nt(kw_ref[pl.ds(start, WIN_SPAN), :], qs)
    dist = pos_q - (start + lax.broadcasted_iota(jnp.int32, (WIN_SPAN, tq), 0))
    bias = jnp.where(dist >= 0, jnp.where(dist < WINDOW, 0.0, NEG), NEG)
    s = s + _tile_heads(bias)
    p = jnp.exp2(s - jnp.max(s, 0, keepdims=True)).astype(BF16)
    o_win = normalise(_dot(with_ones(vw_ref[:, pl.ds(start, WIN_SPAN)]), p))

    gates = jax.nn.sigmoid(gate_ref[...]).T

    def gate(hh, branch):
        lane0 = GATE_LANE0 + hh * 3 + branch
        lane1 = lane0 + NSA_HPG * 3
        return jnp.where(g == 0, gates[lane0:lane0 + 1, :], gates[lane1:lane1 + 1, :])

    for hh in range(NSA_HPG):
        c = slice(hh * tq, (hh + 1) * tq)
        o = gate(hh, 0) * o_cmp[:, c] + gate(hh, 1) * o_slc[:, c] + gate(hh, 2) * o_win[:, c]
        o_ref[:, hh * NSA_DH:(hh + 1) * NSA_DH] = o.T.astype(BF16)


def _nsa(q_r, kc, vc, ks, vs, kw, vw, proj):
    nq = SEQ // TQ_NSA
    gw = NSA_HPG * NSA_DH
    overlap_t, expand = _nsa_constants()
    cols = NSA_HPG * TQ_NSA
    per_group = lambda d0, d1: pl.BlockSpec((None, None, d0, d1), lambda b, g, i: (b, g, 0, 0))
    return pl.pallas_call(
        _nsa_kernel,
        grid=(BATCH, NSA_GROUPS, nq),
        in_specs=[pl.BlockSpec((TQ_NSA, gw), lambda b, g, i: (b * nq + i, g)),
                  per_group(N_CMP_PAD, NSA_DH), per_group(NSA_DH, N_CMP_PAD),
                  per_group(SEQ, NSA_DH), per_group(NSA_DH, SEQ),
                  per_group(SEQ, NSA_DH), per_group(NSA_DH, SEQ),
                  pl.BlockSpec((TQ_NSA, LANE), lambda b, g, i: (b * nq + i, COL_TAIL // LANE)),
                  pl.BlockSpec(overlap_t.shape, lambda b, g, i: (0, 0)),
                  pl.BlockSpec(expand.shape, lambda b, g, i: (0, 0))],
        out_specs=pl.BlockSpec((TQ_NSA, gw), lambda b, g, i: (b * nq + i, g)),
        out_shape=jax.ShapeDtypeStruct((N_TOK, NSA_GROUPS * gw), BF16),
        scratch_shapes=[pltpu.VMEM((N_BLK, TQ_NSA), F32),
                        pltpu.VMEM((1, cols), F32),
                        pltpu.VMEM((NSA_DH + ONES_ROWS, cols), F32)],
        compiler_params=_params("parallel", "parallel", "arbitrary"),
        name="nsa_attn",
    )(q_r, kc, vc, ks, vs, kw, vw, proj, jnp.asarray(overlap_t, BF16), jnp.asarray(expand, BF16))


def _outproj_kernel(yg_ref, yn_ref, x_ref, wo_ref, g1_ref, sc2_ref, sh2_ref, ln1g_ref, ln1b_ref,
                    wr_ref, br_ref, ltri_ref, x1_ref, h2_ref, idx_ref, wt_ref, rank_ref,
                    cnt_ref, base_sc):
    half = D_MODEL // 2

    @pl.when(pl.program_id(0) == 0)
    def _():
        base_sc[...] = jnp.zeros_like(base_sc)

    mix_all = _dot(yg_ref[...], wo_ref[0:half, :]) + _dot(yn_ref[...], wo_ref[half:D_MODEL, :])
    wr = wr_ref[...]
    sub = TM_OUT // OUT_CHAINS
    lane = lax.broadcasted_iota(jnp.int32, (TM_OUT, LANE), 1)
    lane_sub = lax.broadcasted_iota(jnp.int32, (sub, LANE), 1)
    lane_f = lane_sub.astype(F32)
    onehot_parts, h2_parts = [], []
    for ci in range(OUT_CHAINS):
        r = slice(ci * sub, (ci + 1) * sub)
        x1 = _ln(DN_ALPHA * x_ref[r, :] + (1.0 + g1_ref[...]) * mix_all[r]) * ln1g_ref[...] + ln1b_ref[...]
        x1_ref[r, :] = x1
        h2 = _ln(x1) * (1.0 + sc2_ref[...]) + sh2_ref[...]
        h_hi, h_lo = _split_bf16(h2)
        h2_parts.append(h2)
        lg = _dot(h_hi, wr) + _dot(h_lo, wr)
        logits = lg[:, 0:LANE] + lg[:, LANE:2 * LANE] + br_ref[...]

        vals = logits
        idx_out = jnp.zeros(logits.shape, jnp.int32)
        exp_out = jnp.zeros(logits.shape, F32)
        denom = jnp.zeros((sub, 1), F32)
        m0 = None
        onehots = []
        for k in range(TOP_K):
            mk = jnp.max(vals, -1, keepdims=True)
            ik = jnp.min(jnp.where(vals == mk, lane_f, float(LANE)), -1, keepdims=True)
            if k == 0:
                m0 = mk
            ek = jnp.exp(mk - m0)
            denom = denom + ek
            idx_out = jnp.where(lane_sub == k, ik.astype(jnp.int32), idx_out)
            exp_out = jnp.where(lane_sub == k, ek, exp_out)
            hit = lane_f == ik
            onehots.append(jnp.where(hit, 1.0, 0.0))
            vals = jnp.where(hit, -3.0e38, vals)
        idx_ref[r, :] = idx_out
        wt_ref[r, :] = exp_out / denom
        onehot_parts.append(onehots)
    onehots = [jnp.concatenate([part[k] for part in onehot_parts], axis=0) for k in range(TOP_K)]
    h2_all = jnp.concatenate(h2_parts, axis=0)
    for s in range(ROW_CHUNKS):
        h2_ref[pl.ds(s, TM_OUT, stride=ROW_CHUNKS), :] = h2_all[:, s * LANE:(s + 1) * LANE]

    cnt = onehots[0] + onehots[1] + onehots[2] + onehots[3]
    base = base_sc[0:1, :]
    before = _dot(ltri_ref[...], cnt.astype(BF16)) + base
    rank_out = jnp.zeros((TM_OUT, LANE), jnp.int32)
    for k in range(TOP_K):
        rk = jnp.sum(onehots[k] * before, -1, keepdims=True)
        rank_out = jnp.where(lane == k, rk.astype(jnp.int32), rank_out)
    rank_ref[...] = rank_out
    base_sc[...] = jnp.broadcast_to(base + jnp.sum(cnt, 0, keepdims=True), base_sc.shape)
    cnt_ref[...] = base_sc[...]


def _outproj(y_gla, y_nsa, xf, w_o, mod3, ln1_g, ln1_b, wr, br):
    tiles_per_batch = SEQ // TM_OUT
    half = D_MODEL // 2
    mod_spec = lambda chunk: pl.BlockSpec((None, 1, D_MODEL), lambda i: (i // tiles_per_batch, 0, chunk))
    full = lambda shape: pl.BlockSpec(shape, lambda i: tuple(0 for _ in shape))
    row = lambda width: pl.BlockSpec((TM_OUT, width), lambda i: (i, 0))
    ltri = jnp.asarray(np.tril(np.ones((TM_OUT, TM_OUT), np.float32), -1), BF16)
    return pl.pallas_call(
        _outproj_kernel,
        grid=(N_TOK // TM_OUT,),
        in_specs=[row(half), row(half), row(D_MODEL), full(w_o.shape),
                  mod_spec(2), mod_spec(4), mod_spec(3),
                  full(ln1_g.shape), full(ln1_b.shape), full(wr.shape), full(br.shape),
                  full(ltri.shape)],
        out_specs=[row(D_MODEL), pl.BlockSpec((TM_OUT * ROW_CHUNKS, LANE), lambda i: (i, 0)),
                   row(LANE), row(LANE), row(LANE), full((8, LANE))],
        out_shape=[jax.ShapeDtypeStruct((N_TOK, D_MODEL), F32),
                   jax.ShapeDtypeStruct((N_TOK * ROW_CHUNKS, LANE), F32),
                   jax.ShapeDtypeStruct((N_TOK, LANE), jnp.int32),
                   jax.ShapeDtypeStruct((N_TOK, LANE), F32),
                   jax.ShapeDtypeStruct((N_TOK, LANE), jnp.int32),
                   jax.ShapeDtypeStruct((8, LANE), F32)],
        scratch_shapes=[pltpu.VMEM((8, LANE), F32)],
        compiler_params=_params("arbitrary"),
        name="outproj_router",
    )(y_gla, y_nsa, xf, w_o, mod3, mod3, mod3, ln1_g, ln1_b, wr, br, ltri)


def _invert_kernel(dest_ref, src_sm):
    def fill(r, carry):
        src_sm[r] = 0
        return carry

    lax.fori_loop(0, P_ROWS, fill, 0, unroll=8)

    def scatter(tok2, carry):
        a0 = tok2 * (2 * TOP_K)
        rows = [dest_ref[a0 + i] for i in range(2 * TOP_K)]
        for i, row in enumerate(rows):
            src_sm[row] = tok2 * 2 + i // TOP_K
        return carry

    lax.fori_loop(0, N_TOK // 2, scatter, 0)


def _invert(dest_flat):
    smem = pl.BlockSpec(memory_space=pltpu.SMEM)
    return pl.pallas_call(
        _invert_kernel,
        in_specs=[smem],
        out_specs=smem,
        out_shape=jax.ShapeDtypeStruct((P_ROWS,), jnp.int32),
        name="moe_invert",
    )(dest_flat)


def _dispatch_kernel(src_sm, nused_ref, nbatch_ref, h2_hbm, xs_ref, buf, sems):
    t = pl.program_id(0)
    n_used = nused_ref[0]

    def start_gather(tile, slot):
        base = tile * TM_MOE

        def body(r8, carry):
            r0 = r8 * DMA_BATCH
            toks = [src_sm[base + r0 + i] for i in range(DMA_BATCH)]
            for i, tok in enumerate(toks):
                pltpu.make_async_copy(h2_hbm.at[pl.ds(pl.multiple_of(tok * ROW_CHUNKS, ROW_CHUNKS), ROW_CHUNKS), :],
                                      buf.at[slot, pl.ds(pl.multiple_of((r0 + i) * BUF_PITCH, 8), ROW_CHUNKS), :],
                                      sems.at[slot]).start(priority=i % 2)
            return carry

        lax.fori_loop(0, nbatch_ref[tile], body, 0)

    @pl.when(t == 0)
    def _():
        buf[...] = jnp.zeros(buf.shape, F32)
        start_gather(0, 0)

    @pl.when(t + 1 < n_used)
    def _():
        start_gather(t + 1, (t + 1) % 2)

    @pl.when(t < n_used)
    def _():
        slot = t % 2
        batch = buf.at[slot, pl.ds(0, DMA_BATCH * ROW_CHUNKS), :]

        def wait_batch(b, carry):
            pltpu.make_async_copy(batch, batch, sems.at[slot]).wait()
            return carry

        lax.fori_loop(0, nbatch_ref[t], wait_batch, 0)
        for s in range(ROW_CHUNKS):
            xs_ref[:, s * LANE:(s + 1) * LANE] = buf[slot, pl.ds(s, TM_MOE, stride=BUF_PITCH), :].astype(BF16)

    @pl.when(t >= n_used)
    def _():
        xs_ref[...] = jnp.zeros(xs_ref.shape, BF16)


def _dispatch(src_tok, n_used, n_batches, h2):
    return pl.pallas_call(
        _dispatch_kernel,
        grid_spec=pltpu.PrefetchScalarGridSpec(
            num_scalar_prefetch=3,
            grid=(N_MTILES,),
            in_specs=[pl.BlockSpec(memory_space=pl.ANY)],
            out_specs=pl.BlockSpec((TM_MOE, D_MODEL), lambda t, src, nu, nb: (t, 0)),
            scratch_shapes=[pltpu.VMEM((2, TM_MOE * BUF_PITCH, LANE), F32),
                            pltpu.SemaphoreType.DMA((2,))]),
        out_shape=jax.ShapeDtypeStruct((P_ROWS, D_MODEL), BF16),
        compiler_params=_params("arbitrary"),
        name="moe_dispatch",
    )(src_tok, n_used, n_batches, h2)


class _ExpertWeights:
    def __init__(self, eid_ref, n_used, w_hbms, bufs, sems, cnt_sm):
        self.eid_ref, self.n_used = eid_ref, n_used
        self.w_hbms, self.bufs, self.sems, self.cnt_sm = w_hbms, bufs, sems, cnt_sm

    def _copies(self, expert, j, slot):
        tn = self.bufs[0].shape[-1]
        cols = pl.ds(pl.multiple_of(j * tn, tn), tn)
        return [pltpu.make_async_copy(w.at[expert, :, cols], buf.at[slot], self.sems.at[i, slot])
                for i, (w, buf) in enumerate(zip(self.w_hbms, self.bufs))]

    def prologue(self):
        self.cnt_sm[0] = 0
        for cp in self._copies(self.eid_ref[0], 0, 0):
            cp.start()

    def acquire(self, j, t, n_passes):
        slot = self.cnt_sm[0] % 2
        for cp in self._copies(self.eid_ref[t], j, slot):
            cp.wait()
        expert = self.eid_ref[t]
        nxt = lax.while_loop(
            lambda u: (u < self.n_used) & (self.eid_ref[jnp.minimum(u, N_MTILES - 1)] == expert),
            lambda u: u + 1, t + 1)
        in_pass = nxt < self.n_used

        @pl.when(in_pass | (j + 1 < n_passes))
        def _():
            for cp in self._copies(self.eid_ref[jnp.where(in_pass, nxt, 0)], jnp.where(in_pass, j, j + 1),
                                   1 - slot):
                cp.start()

        self.cnt_sm[0] = self.cnt_sm[0] + 1
        return slot


def _moe_up_kernel(eid_ref, nused_ref, x_ref, wg_hbm, wu_hbm, bg_ref, bu_ref, h_ref,
                   wg_buf, wu_buf, wg_sc, wu_sc, sems, cnt_sm):
    j = pl.program_id(0)
    t = pl.program_id(1)
    n_used = nused_ref[0]
    weights = _ExpertWeights(eid_ref, n_used, (wg_hbm, wu_hbm), (wg_buf, wu_buf), sems, cnt_sm)

    @pl.when((j == 0) & (t == 0))
    def _():
        weights.prologue()

    @pl.when(t < n_used)
    def _():
        @pl.when((t == 0) | (eid_ref[t] != eid_ref[jnp.maximum(t - 1, 0)]))
        def _():
            slot = weights.acquire(j, t, pl.num_programs(0))
            wg_sc[...] = wg_buf[slot].astype(BF16)
            wu_sc[...] = wu_buf[slot].astype(BF16)

        x = x_ref[...]
        gate = jnp.minimum(_dot(x, wg_sc[...]) + bg_ref[...], SWIGLU_LIMIT)
        up = jnp.clip(_dot(x, wu_sc[...]) + bu_ref[...], -SWIGLU_LIMIT, SWIGLU_LIMIT)
        h_ref[...] = (gate * jax.nn.sigmoid(SWIGLU_ALPHA * gate) * (up + 1.0)).astype(BF16)

    @pl.when(t >= n_used)
    def _():
        h_ref[...] = jnp.zeros(h_ref.shape, BF16)


def _moe_down_kernel(eid_ref, nused_ref, h_ref, wd_hbm, bd_ref, y_ref, wd_buf, wd_sc, sems, cnt_sm):
    j = pl.program_id(0)
    t = pl.program_id(1)
    n_used = nused_ref[0]
    weights = _ExpertWeights(eid_ref, n_used, (wd_hbm,), (wd_buf,), sems, cnt_sm)

    @pl.when((j == 0) & (t == 0))
    def _():
        weights.prologue()

    @pl.when(t < n_used)
    def _():
        @pl.when((t == 0) | (eid_ref[t] != eid_ref[jnp.maximum(t - 1, 0)]))
        def _():
            slot = weights.acquire(j, t, pl.num_programs(0))
            wd_sc[...] = wd_buf[slot].astype(BF16)

        y_ref[...] = _dot(h_ref[...], wd_sc[...]) + bd_ref[...]

    @pl.when(t >= n_used)
    def _():
        y_ref[...] = jnp.zeros(y_ref.shape, F32)


def _row_tile(t, nused_ref):
    return jnp.minimum(t, jnp.maximum(nused_ref[0] - 1, 0))


def _moe_call(body, name, n_weights, k_dim, out_dim, tn, out_dtype, tile_eid, n_used, rows, weights, biases):
    any_spec = pl.BlockSpec(memory_space=pl.ANY)
    b_spec = pl.BlockSpec((None, 1, tn), lambda j, t, eid, nu: (eid[t], 0, j))
    return pl.pallas_call(
        body,
        grid_spec=pltpu.PrefetchScalarGridSpec(
            num_scalar_prefetch=2,
            grid=(out_dim // tn, N_MTILES),
            in_specs=[pl.BlockSpec((TM_MOE, k_dim), lambda j, t, eid, nu: (_row_tile(t, nu), 0))]
            + [any_spec] * n_weights + [b_spec] * n_weights,
            out_specs=pl.BlockSpec((TM_MOE, tn), lambda j, t, eid, nu: (t, j)),
            scratch_shapes=[pltpu.VMEM((2, k_dim, tn), F32)] * n_weights
            + [pltpu.VMEM((k_dim, tn), BF16)] * n_weights
            + [pltpu.SemaphoreType.DMA((n_weights, 2)), pltpu.SMEM((1,), jnp.int32)]),
        out_shape=jax.ShapeDtypeStruct((P_ROWS, out_dim), out_dtype),
        compiler_params=_params("arbitrary", "arbitrary"),
        name=name,
    )(tile_eid, n_used, rows, *weights, *biases)


def _moe_up(tile_eid, n_used, xs, w_gate, w_up, b_gate, b_up):
    return _moe_call(_moe_up_kernel, "moe_up", 2, D_MODEL, D_FF, TN_UP, BF16, tile_eid, n_used, xs,
                     (w_gate, w_up), (b_gate, b_up))


def _moe_down(tile_eid, n_used, h, w_down, b_down):
    return _moe_call(_moe_down_kernel, "moe_down", 1, D_FF, D_MODEL, TN_DOWN, F32, tile_eid, n_used, h,
                     (w_down,), (b_down,))


def _final_kernel(dest_ref, y_hbm, x1_ref, wt_ref, g2_ref, ln2g_ref, ln2b_ref, o_ref, ybuf, sems):
    i = pl.program_id(0)
    n_tiles = pl.num_programs(0)

    def start_gather(tile, slot):
        base = tile * (TM_FIN * TOP_K)

        def body(r2, carry):
            a0 = base + r2 * DMA_BATCH
            rows = [dest_ref[a0 + i] for i in range(DMA_BATCH)]
            for i, d in enumerate(rows):
                r = r2 * (DMA_BATCH // TOP_K) + i // TOP_K
                pltpu.make_async_copy(y_hbm.at[pl.ds(d, 1), :], ybuf.at[slot, i % TOP_K, pl.ds(r, 1), :],
                                      sems.at[slot]).start(priority=i % 2)
            return carry

        lax.fori_loop(0, TM_FIN * TOP_K // DMA_BATCH, body, 0)

    @pl.when(i == 0)
    def _():
        start_gather(0, 0)

    @pl.when(i + 1 < n_tiles)
    def _():
        start_gather(i + 1, (i + 1) % 2)

    slot = i % 2
    pltpu.make_async_copy(ybuf.at[slot], ybuf.at[slot], sems.at[slot]).wait()
    wt = wt_ref[...]
    y = wt[:, 0:1] * ybuf[slot, 0]
    for k in range(1, TOP_K):
        y = y + wt[:, k:k + 1] * ybuf[slot, k]
    o_ref[...] = _ln(DN_ALPHA * x1_ref[...] + (1.0 + g2_ref[...]) * y) * ln2g_ref[...] + ln2b_ref[...]


def _final(dest_flat, y, x1, wt, mod3, ln2_g, ln2_b):
    tiles_per_batch = SEQ // TM_FIN
    full = lambda shape: pl.BlockSpec(shape, lambda i, dest: tuple(0 for _ in shape))
    row = lambda width: pl.BlockSpec((TM_FIN, width), lambda i, dest: (i, 0))
    return pl.pallas_call(
        _final_kernel,
        grid_spec=pltpu.PrefetchScalarGridSpec(
            num_scalar_prefetch=1,
            grid=(N_TOK // TM_FIN,),
            in_specs=[pl.BlockSpec(memory_space=pl.ANY), row(D_MODEL), row(LANE),
                      pl.BlockSpec((None, 1, D_MODEL), lambda i, dest: (i // tiles_per_batch, 0, 5)),
                      full(ln2_g.shape), full(ln2_b.shape)],
            out_specs=row(D_MODEL),
            scratch_shapes=[pltpu.VMEM((2, TOP_K, TM_FIN, D_MODEL), F32),
                            pltpu.SemaphoreType.DMA((2,))]),
        out_shape=jax.ShapeDtypeStruct((N_TOK, D_MODEL), F32),
        compiler_params=_params("arbitrary"),
        name="combine_ln2",
    )(dest_flat, y, x1, wt, mod3, ln2_g, ln2_b)


def _route(idx, rank, counts):
    experts = jnp.arange(N_EXPERTS, dtype=jnp.int32)
    padded = ((counts + TM_MOE - 1) // TM_MOE) * TM_MOE
    ends = jnp.sum(jnp.where(experts[None, :] <= experts[:, None], padded[None, :], 0), axis=1)
    starts = ends - padded
    dest = rank
    for e in range(N_EXPERTS):
        dest = dest + jnp.where(idx == e, starts[e], 0)
    dest = dest[:, :TOP_K]
    tile_start = jnp.arange(N_MTILES, dtype=jnp.int32) * TM_MOE
    tile_eid = jnp.minimum(jnp.sum((ends[None, :] <= tile_start[:, None]).astype(jnp.int32), axis=1),
                           N_EXPERTS - 1)
    n_used = (ends[-1] // TM_MOE).astype(jnp.int32).reshape(1)
    group_fill = jnp.sum(jnp.where(tile_eid[:, None] == experts[None, :], (starts + counts)[None, :], 0), axis=1)
    n_batches = (jnp.clip(group_fill - tile_start, 0, TM_MOE) + DMA_BATCH - 1) // DMA_BATCH
    return dest.reshape(-1), tile_eid, n_used, n_batches


def kernel(x, c, w_ada, b_ada, w_in, w_gk, b_gk, gla_norm_g, pe_k, pe_v, w_ck1, b_ck1, w_ck2, b_ck2,
           w_cv1, b_cv1, w_cv2, b_cv2, w_o, ln1_g, ln1_b, w_router, b_router, w_gate, b_gate, w_up, b_up,
           w_down, b_down, ln2_g, ln2_b):
    l = 0
    xf = x.reshape(N_TOK, D_MODEL)
    row2 = lambda a: a.reshape(1, -1)

    c8 = jnp.pad(c, ((0, 8 - BATCH), (0, 0)))
    mod3 = _adaln(c8, w_ada[l], row2(b_ada[l]))[:BATCH].reshape(BATCH, 1, 6 * D_MODEL)

    wt = w_in[l].T
    glr0 = 3072
    nsa0 = glr0 + GLA_RANK
    ngt0 = nsa0 + 1024 + 6 * 256
    w_in_t = jnp.concatenate(
        [wt[:glr0], wt[nsa0:ngt0], wt[glr0:nsa0], wt[ngt0:],
         jnp.zeros((D_IN_PAD - wt.shape[0], D_MODEL), F32)], axis=0).astype(BF16)
    proj = _inproj(xf, mod3, w_in_t)

    w_gk_pad = jnp.pad(w_gk[l], ((0, LANE - GLA_RANK), (0, 0)))
    y_gla = _gla(proj, w_gk_pad, row2(b_gk[l]), row2(gla_norm_g[l]))

    cos_t, sin_t = _rope_tables(jnp.arange(SEQ))
    cmp_end = jnp.arange(N_CMP_PAD) * CMP_STRIDE + (CMP_BLOCK - 1)
    cos_c, sin_c = _rope_tables(cmp_end)
    q_r, ks, vs, kw, vw = _prep(proj, cos_t, sin_t)
    kc = _compress(proj, 0, pe_k[l], w_ck1[l], row2(b_ck1[l]), w_ck2[l], row2(b_ck2[l]), cos_c, sin_c, True)
    vc = _compress(proj, 1, pe_v[l], w_cv1[l], row2(b_cv1[l]), w_cv2[l], row2(b_cv2[l]), cos_c, sin_c, False)
    y_nsa = _nsa(q_r, kc, vc, ks, vs, kw, vw, proj)

    wr = jnp.pad(w_router[l], ((0, 0), (0, LANE - N_EXPERTS)))
    wr_hi, wr_lo = _split_bf16(wr)
    br = jnp.concatenate([b_router[l], jnp.full((LANE - N_EXPERTS,), NEG, F32)]).reshape(1, LANE)
    x1, h2, idx128, wt128, rank128, cnt8 = _outproj(y_gla, y_nsa, xf, w_o[l].astype(BF16), mod3,
                                                    row2(ln1_g[l]), row2(ln1_b[l]),
                                                    jnp.concatenate([wr_hi, wr_lo], axis=1), br)

    counts = cnt8[0, :N_EXPERTS].astype(jnp.int32)
    dest, tile_eid, n_used, n_batches = _route(idx128, rank128, counts)
    xs = _dispatch(_invert(dest), n_used, n_batches, h2)
    h = _moe_up(tile_eid, n_used, xs, w_gate[l], w_up[l],
                b_gate[l].reshape(N_EXPERTS, 1, D_FF), b_up[l].reshape(N_EXPERTS, 1, D_FF))
    y = _moe_down(tile_eid, n_used, h, w_down[l], b_down[l].reshape(N_EXPERTS, 1, D_MODEL))

    out = _final(dest, y, x1, wt128, mod3, row2(ln2_g[l]), row2(ln2_b[l]))
    return out.reshape(BATCH, SEQ, D_MODEL)
```

```python
import functools

import numpy as np
import jax
import jax.numpy as jnp
from jax import lax
from jax.experimental import pallas as pl
from jax.experimental.pallas import tpu as pltpu

F32 = jnp.float32
BF16 = jnp.bfloat16

D_MODEL = 2048
BATCH = 2
SEQ = 4096
N_TOK = BATCH * SEQ

GLA_HEADS = 4
GLA_DK = 128
GLA_DV = 256
GLA_RANK = 16
GLA_TAU = 16.0
GLA_CHUNK = 64

NSA_DH = 128
NSA_HEADS = 8
NSA_GROUPS = 2
NSA_HPG = 4
CMP_STRIDE = 16
CMP_BLOCK = 32
CMP_HIDDEN = 256
N_CMP_PAD = SEQ // CMP_STRIDE
SEL_BLOCK = 64
N_BLK = SEQ // SEL_BLOCK
SEL_TOPK = 16
WINDOW = 512
ROPE_DIM = 32
ROPE_THETA = 500000.0
Q_SCALE_LOG2 = NSA_DH ** -0.5 * 1.4426950408889634

N_EXPERTS = 32
TOP_K = 4
D_FF = D_MODEL
SWIGLU_LIMIT = 7.0
SWIGLU_ALPHA = 1.702
DN_ALPHA = 2.0 ** 0.25
LN_EPS = 1e-5

COL_GQ, COL_GK, COL_GV, COL_GR, COL_NQ = 0, 512, 1024, 2048, 3072
COL_KV = 4096
COL_TAIL = 5632
D_IN_PAD = 6144
GATE_LANE0 = GLA_RANK

LANE = 128
ROW_CHUNKS = D_MODEL // LANE
BUF_PITCH = ROW_CHUNKS + 8
NEG = -1e30
VMEM_LIMIT = 56 * 1024 * 1024

TM_IN, TN_IN = 1024, 1024
TT_GLA = 512
TR_PREP = 512
TQ_NSA = 256
TK_SEL = 512
WIN_SPAN = WINDOW + TQ_NSA
ONES_ROWS = 16
TM_OUT = 256
OUT_CHAINS = 2
TM_MOE = 256
ROW_STEPS = (64, 128, 192, 256)
TN_UP, TN_DOWN = 1024, 2048
P_ROWS = N_TOK * TOP_K + N_EXPERTS * TM_MOE
N_MTILES = P_ROWS // TM_MOE
TM_FIN = 256
DMA_BATCH = 8


def _dot(a, b):
    return jnp.dot(a, b, preferred_element_type=F32)


def _dot_nt(a, b):
    return lax.dot_general(a, b, (((1,), (1,)), ((), ())), preferred_element_type=F32)


def _dot_tn(a, b):
    return lax.dot_general(a, b, (((0,), (0,)), ((), ())), preferred_element_type=F32)


def _ln(x):
    xc = x - jnp.mean(x, -1, keepdims=True)
    return xc * lax.rsqrt(jnp.mean(xc * xc, -1, keepdims=True) + LN_EPS)


def _split_bf16(x):
    hi = x.astype(BF16)
    lo = (x - hi.astype(F32)).astype(BF16)
    return hi, lo


def _params(*sem):
    return pltpu.CompilerParams(dimension_semantics=sem, vmem_limit_bytes=VMEM_LIMIT)


def _adaln_kernel(c_ref, w_ref, b_ref, o_ref):
    c = c_ref[...]
    a = (c * jax.nn.sigmoid(c)).astype(BF16)
    o_ref[...] = _dot(a, w_ref[...].astype(BF16)) + b_ref[...]


def _adaln(c8, w, b):
    n = w.shape[1]
    tn = 1024
    return pl.pallas_call(
        _adaln_kernel,
        grid=(n // tn,),
        in_specs=[pl.BlockSpec((8, D_MODEL), lambda j: (0, 0)),
                  pl.BlockSpec((D_MODEL, tn), lambda j: (0, j)),
                  pl.BlockSpec((1, tn), lambda j: (0, j))],
        out_specs=pl.BlockSpec((8, tn), lambda j: (0, j)),
        out_shape=jax.ShapeDtypeStruct((8, n), F32),
        compiler_params=_params("arbitrary"),
        name="adaln",
    )(c8, w, b)


def _inproj_kernel(x_ref, sh_ref, sc_ref, w_ref, o_ref, h_sc):
    @pl.when(pl.program_id(1) == 0)
    def _():
        h = _ln(x_ref[...]) * (1.0 + sc_ref[...]) + sh_ref[...]
        h_sc[...] = h.astype(BF16)

    o_ref[...] = _dot_nt(h_sc[...], w_ref[...])


def _inproj(xf, mod3, w_in_t):
    tiles_per_batch = SEQ // TM_IN
    return pl.pallas_call(
        _inproj_kernel,
        grid=(N_TOK // TM_IN, D_IN_PAD // TN_IN),
        in_specs=[pl.BlockSpec((TM_IN, D_MODEL), lambda i, j: (i, 0)),
                  pl.BlockSpec((None, 1, D_MODEL), lambda i, j: (i // tiles_per_batch, 0, 0)),
                  pl.BlockSpec((None, 1, D_MODEL), lambda i, j: (i // tiles_per_batch, 0, 1)),
                  pl.BlockSpec((TN_IN, D_MODEL), lambda i, j: (j, 0))],
        out_specs=pl.BlockSpec((TM_IN, TN_IN), lambda i, j: (i, j)),
        out_shape=jax.ShapeDtypeStruct((N_TOK, D_IN_PAD), F32),
        scratch_shapes=[pltpu.VMEM((TM_IN, D_MODEL), BF16)],
        compiler_params=_params("parallel", "arbitrary"),
        name="inproj",
    )(xf, mod3, mod3, w_in_t)


GLA_HALVES = (32, 16, 8, 4, 2, 1)
N_LEVELS = len(GLA_HALVES)
ROW_EB = 2 * N_LEVELS
ROW_EL = 2 * N_LEVELS + 1
N_EVIEWS = 2 * N_LEVELS + 2


def _gla_constants():
    c = GLA_CHUNK
    t = np.arange(c)[:, None]
    r = np.arange(c)[None, :]
    mall = np.zeros((N_EVIEWS, c, c), np.float32)
    valid = np.zeros((N_EVIEWS, c, LANE), np.float32)
    masks = np.zeros((N_LEVELS + 1, c, c), np.float32)
    for li, n in enumerate(GLA_HALVES):
        same = (t // (2 * n)) == (r // (2 * n))
        t_up = (t % (2 * n)) >= n
        r_up = (r % (2 * n)) >= n
        mall[2 * li] = same & t_up & r_up & (r <= t)
        mall[2 * li + 1] = same & ~t_up & ~r_up & (r > t)
        valid[2 * li] = np.broadcast_to(t_up, (c, LANE))
        valid[2 * li + 1] = np.broadcast_to(~t_up, (c, LANE))
        masks[li] = same & t_up & ~r_up
    mall[ROW_EB] = r <= t
    mall[ROW_EL] = r > t
    valid[ROW_EB] = 1.0
    valid[ROW_EL] = 1.0
    masks[N_LEVELS] = np.eye(c)
    return (mall.reshape(N_EVIEWS * c, c), valid.reshape(N_EVIEWS * c, LANE), masks)


def _gla_kernel(q_ref, k_ref, v_ref, r_ref, glr_ref, wgk_ref, bgk_ref, g_ref, mall_ref, valid_ref,
                masks_ref, o_ref, st_sc):
    c = GLA_CHUNK

    @pl.when(pl.program_id(1) == 0)
    def _():
        st_sc[...] = jnp.zeros_like(st_sc)

    z = _dot(glr_ref[...].astype(BF16), wgk_ref[...].astype(BF16)) + bgk_ref[...]
    log_a = (jnp.minimum(z, 0.0) - jnp.log1p(jnp.exp(-jnp.abs(z)))) * (1.0 / GLA_TAU)
    mall = mall_ref[...]
    valid = valid_ref[...]
    for ci in range(TT_GLA // c):
        rows = slice(ci * c, (ci + 1) * c)
        la_hi, la_lo = _split_bf16(log_a[rows])
        e_all = jnp.exp(_dot(mall, la_hi) + _dot(mall, la_lo))
        for h in range(GLA_HEADS):
            kcols = slice(h * GLA_DK, (h + 1) * GLA_DK)
            vcols = slice(h * GLA_DV, (h + 1) * GLA_DV)
            e = e_all[:, kcols] * valid
            q = q_ref[rows, kcols] * (GLA_DK ** -0.5)
            k = k_ref[rows, kcols]
            vb = v_ref[rows, vcols].astype(BF16)
            att = masks_ref[N_LEVELS] * _dot_nt(q.astype(BF16), k.astype(BF16))
            for li in range(N_LEVELS):
                eq = e[(2 * li) * c:(2 * li + 1) * c]
                ek = e[(2 * li + 1) * c:(2 * li + 2) * c]
                att = att + masks_ref[li] * _dot_nt((q * eq).astype(BF16), (k * ek).astype(BF16))
            eb = e[ROW_EB * c:(ROW_EB + 1) * c]
            el = e[ROW_EL * c:(ROW_EL + 1) * c]
            st = st_sc[h]
            o = _dot_nt((q * eb).astype(BF16), st.astype(BF16)) + _dot(att.astype(BF16), vb)
            st_sc[h] = st * eb[c - 1:c, :] + _dot_tn(vb, (k * el).astype(BF16))
            o = o * lax.rsqrt(jnp.mean(o * o, -1, keepdims=True) + LN_EPS) * g_ref[...]
            r = r_ref[rows, vcols]
            o_ref[rows, vcols] = (o * (r * jax.nn.sigmoid(r))).astype(BF16)


def _gla(proj, w_gk_pad, b_gk, norm_g):
    nt = SEQ // TT_GLA
    mall, valid, masks = _gla_constants()
    kw = GLA_HEADS * GLA_DK
    vw = GLA_HEADS * GLA_DV
    full = lambda shape: pl.BlockSpec(shape, lambda b, i: tuple(0 for _ in shape))
    cols = lambda width, col0: pl.BlockSpec((TT_GLA, width), lambda b, i: (b * nt + i, col0 // width))
    return pl.pallas_call(
        _gla_kernel,
        grid=(BATCH, nt),
        in_specs=[cols(kw, COL_GQ), cols(kw, COL_GK), cols(vw, COL_GV), cols(vw, COL_GR),
                  cols(LANE, COL_TAIL), full(w_gk_pad.shape), full(b_gk.shape), full(norm_g.shape),
                  full(mall.shape), full(valid.shape), full(masks.shape)],
        out_specs=cols(vw, 0),
        out_shape=jax.ShapeDtypeStruct((N_TOK, vw), BF16),
        scratch_shapes=[pltpu.VMEM((GLA_HEADS, GLA_DV, GLA_DK), F32)],
        compiler_params=_params("parallel", "arbitrary"),
        name="gla",
    )(proj, proj, proj, proj, proj, w_gk_pad, b_gk, norm_g,
      jnp.asarray(mall, BF16), jnp.asarray(valid), jnp.asarray(masks))


def _rope_tables(pos):
    half = ROPE_DIM // 2
    inv_freq = ROPE_THETA ** (-jnp.arange(half, dtype=F32) * (2.0 / ROPE_DIM))
    ang = pos.astype(F32)[:, None] * inv_freq
    cos, sin = jnp.cos(ang), jnp.sin(ang)
    n = pos.shape[0]
    cosf = jnp.concatenate([cos, cos, jnp.ones((n, LANE - ROPE_DIM), F32)], -1)
    sinf = jnp.concatenate([-sin, sin, jnp.zeros((n, LANE - ROPE_DIM), F32)], -1)
    return cosf, sinf


def _rope(x, cosf, sinf):
    lane = lax.broadcasted_iota(jnp.int32, x.shape, 1)
    half = ROPE_DIM // 2
    swapped = jnp.where(lane < half, pltpu.roll(x, LANE - half, 1), pltpu.roll(x, half, 1))
    return x * cosf + swapped * sinf


def _prep_kernel(q_ref, ks_ref, vs_ref, kw_ref, vw_ref, cos_ref, sin_ref,
                 qo_ref, kso_ref, vso_ref, kwo_ref, vwo_ref):
    cosf = cos_ref[...]
    sinf = sin_ref[...]
    for hh in range(NSA_HEADS):
        cols = slice(hh * NSA_DH, (hh + 1) * NSA_DH)
        qo_ref[:, cols] = (_rope(q_ref[:, cols], cosf, sinf) * Q_SCALE_LOG2).astype(BF16)
    for g in range(NSA_GROUPS):
        cols = slice(g * NSA_DH, (g + 1) * NSA_DH)
        kso_ref[g] = _rope(ks_ref[:, cols], cosf, sinf).astype(BF16)
        kwo_ref[g] = _rope(kw_ref[:, cols], cosf, sinf).astype(BF16)
        vso_ref[g] = vs_ref[:, cols].T.astype(BF16)
        vwo_ref[g] = vw_ref[:, cols].T.astype(BF16)


def _prep(proj, cosf, sinf):
    nt = SEQ // TR_PREP
    kvw = NSA_GROUPS * NSA_DH
    kv_in = lambda which: pl.BlockSpec((TR_PREP, kvw), lambda b, i: (b * nt + i, COL_KV // kvw + which))
    k_out = pl.BlockSpec((None, NSA_GROUPS, TR_PREP, NSA_DH), lambda b, i: (b, 0, i, 0))
    k_shape = jax.ShapeDtypeStruct((BATCH, NSA_GROUPS, SEQ, NSA_DH), BF16)
    vt_out = pl.BlockSpec((None, NSA_GROUPS, NSA_DH, TR_PREP), lambda b, i: (b, 0, 0, i))
    vt_shape = jax.ShapeDtypeStruct((BATCH, NSA_GROUPS, NSA_DH, SEQ), BF16)
    qw = NSA_HEADS * NSA_DH
    return pl.pallas_call(
        _prep_kernel,
        grid=(BATCH, nt),
        in_specs=[pl.BlockSpec((TR_PREP, qw), lambda b, i: (b * nt + i, COL_NQ // qw)),
                  kv_in(2), kv_in(3), kv_in(4), kv_in(5),
                  pl.BlockSpec((TR_PREP, LANE), lambda b, i: (i, 0)),
                  pl.BlockSpec((TR_PREP, LANE), lambda b, i: (i, 0))],
        out_specs=[pl.BlockSpec((TR_PREP, qw), lambda b, i: (b * nt + i, 0)),
                   k_out, vt_out, k_out, vt_out],
        out_shape=[jax.ShapeDtypeStruct((N_TOK, qw), BF16), k_shape, vt_shape, k_shape, vt_shape],
        compiler_params=_params("parallel", "parallel"),
        name="nsa_prep",
    )(proj, proj, proj, proj, proj, cosf, sinf)


def _compress_kernel(a_ref, pe_ref, w1_ref, b1_ref, w2_ref, b2_ref, cos_ref, sin_ref, o_ref, bot_sc,
                     *, rope):
    n = N_CMP_PAD
    top = jnp.zeros((n, CMP_HIDDEN), F32)
    bot = jnp.zeros((n, CMP_HIDDEN), F32)
    for p in range(CMP_STRIDE):
        ap = a_ref[pl.ds(p, n, stride=CMP_STRIDE), :]
        w_top = w1_ref[p * NSA_DH:(p + 1) * NSA_DH, :].astype(BF16)
        w_bot = w1_ref[(CMP_STRIDE + p) * NSA_DH:(CMP_STRIDE + p + 1) * NSA_DH, :].astype(BF16)
        top = top + _dot((ap + pe_ref[p:p + 1, :]).astype(BF16), w_top)
        bot = bot + _dot((ap + pe_ref[CMP_STRIDE + p:CMP_STRIDE + p + 1, :]).astype(BF16), w_bot)
    bot_sc[0:n, :] = bot
    bot_sc[n:n + 8, :] = jnp.zeros((8, CMP_HIDDEN), F32)
    h = top + bot_sc[1:n + 1, :] + b1_ref[...]
    h = h * jax.nn.sigmoid(h)
    out = _dot(h.astype(BF16), w2_ref[...].astype(BF16)) + b2_ref[...]
    if rope:
        out = _rope(out, cos_ref[...], sin_ref[...])
    row = lax.broadcasted_iota(jnp.int32, out.shape, 0)
    out = jnp.where(row < n - 1, out, 0.0)
    o_ref[...] = (out if rope else out.T).astype(BF16)


def _compress(proj, which, pe, w1, b1, w2, b2, cosf, sinf, rope):
    col0 = (COL_KV + which * NSA_GROUPS * NSA_DH) // NSA_DH
    out_dims = (N_CMP_PAD, NSA_DH) if rope else (NSA_DH, N_CMP_PAD)
    full = lambda shape: pl.BlockSpec(shape, lambda b, g: tuple(0 for _ in shape))
    return pl.pallas_call(
        functools.partial(_compress_kernel, rope=rope),
        grid=(BATCH, NSA_GROUPS),
        in_specs=[pl.BlockSpec((SEQ, NSA_DH), lambda b, g: (b, col0 + g)),
                  full(pe.shape), full(w1.shape), full(b1.shape), full(w2.shape), full(b2.shape),
                  full(cosf.shape), full(sinf.shape)],
        out_specs=pl.BlockSpec((None, None) + out_dims, lambda b, g: (b, g, 0, 0)),
        out_shape=jax.ShapeDtypeStruct((BATCH, NSA_GROUPS) + out_dims, BF16),
        scratch_shapes=[pltpu.VMEM((N_CMP_PAD + 8, CMP_HIDDEN), F32)],
        compiler_params=_params("parallel", "parallel"),
        name="nsa_compress",
    )(proj, pe, w1, b1, w2, b2, cosf, sinf)


def _nsa_constants():
    c_start = np.arange(N_CMP_PAD) * CMP_STRIDE
    b_start = np.arange(N_BLK) * SEL_BLOCK
    overlap_t = ((c_start[None, :] < b_start[:, None] + SEL_BLOCK)
                 & (c_start[None, :] + CMP_BLOCK > b_start[:, None])).astype(np.float32)
    overlap_t[:, N_CMP_PAD - 1] = 0.0
    expand_t = (np.arange(SEQ)[:, None] // SEL_BLOCK == np.arange(LANE)[None, :]).astype(np.float32)
    return overlap_t, expand_t


def _tile_heads(a):
    return jnp.concatenate([a] * NSA_HPG, axis=1)


def _nsa_kernel(q_ref, kc_ref, vc_ref, ks_ref, vs_ref, kw_ref, vw_ref, gate_ref, ovt_ref, expand_ref,
                o_ref, score_sc, m_sc, acc_sc):
    tq = TQ_NSA
    g = pl.program_id(1)
    q0 = pl.program_id(2) * tq
    qs = jnp.concatenate([q_ref[:, hh * NSA_DH:(hh + 1) * NSA_DH] for hh in range(NSA_HPG)], axis=0)
    pos_q = q0 + lax.broadcasted_iota(jnp.int32, (1, tq), 1)

    def with_ones(vt):
        return jnp.concatenate([vt, jnp.ones((ONES_ROWS, vt.shape[1]), BF16)], axis=0)

    def normalise(acc):
        return acc[0:NSA_DH] / acc[NSA_DH:NSA_DH + 1]

    s = _dot_nt(kc_ref[...], qs)
    cmp_end = lax.broadcasted_iota(jnp.int32, (N_CMP_PAD, tq), 0) * CMP_STRIDE + (CMP_BLOCK - 1)
    s = s + _tile_heads(jnp.where(cmp_end <= pos_q, 0.0, NEG))
    m = jnp.max(s, 0, keepdims=True)
    e = jnp.where(s > 0.5 * NEG, jnp.exp2(s - m), 0.0)
    p = e / jnp.maximum(jnp.sum(e, 0, keepdims=True), 1e-30)
    o_cmp = _dot(vc_ref[...], p.astype(BF16))
    p_sum = p[:, 0:tq] + p[:, tq:2 * tq] + p[:, 2 * tq:3 * tq] + p[:, 3 * tq:4 * tq]
    ps_hi, ps_lo = _split_bf16(p_sum)
    ovt = ovt_ref[...]
    p_blk_t = _dot(ovt, ps_hi) + _dot(ovt, ps_lo)

    jj = lax.broadcasted_iota(jnp.int32, (N_BLK, tq), 0)
    cur = (q0 + lax.broadcasted_iota(jnp.int32, (N_BLK, tq), 1)) // SEL_BLOCK
    forced = (jj == 0) | (jj == cur) | (jj == cur - 1)
    allowed = jj <= cur
    score = jnp.where(forced, 3.0e38, jnp.where(allowed, p_blk_t, -1.0))
    score_sc[...] = score

    def rank_body(i, rank):
        row = score_sc[pl.ds(i, 1), :]
        first = jnp.where(jj > i, 1.0, 0.0)
        return rank + jnp.where(row > score, 1.0, jnp.where(row == score, first, 0.0))

    n_live = (q0 + tq - 1) // SEL_BLOCK + 1
    rank = lax.fori_loop(0, n_live, rank_body, jnp.zeros((N_BLK, tq), F32))
    sel_t = jnp.where(allowed, jnp.where(rank < SEL_TOPK, 1.0, 0.0), 0.0)
    sel = jnp.concatenate([sel_t, jnp.zeros((LANE - N_BLK, tq), F32)], axis=0).astype(BF16)

    m_sc[...] = jnp.full(m_sc.shape, NEG, F32)
    acc_sc[...] = jnp.zeros(acc_sc.shape, F32)

    def sel_body(kt, carry):
        k0 = pl.multiple_of(kt * TK_SEL, TK_SEL)
        s = _dot_nt(ks_ref[pl.ds(k0, TK_SEL), :], qs)
        sel_x = _dot(expand_ref[pl.ds(k0, TK_SEL), :], sel)
        kpos = k0 + lax.broadcasted_iota(jnp.int32, (TK_SEL, tq), 0)
        bias = jnp.where(kpos <= pos_q, jnp.where(sel_x > 0.5, 0.0, NEG), NEG)
        s = s + _tile_heads(bias)
        m_prev = m_sc[...]
        m_new = jnp.maximum(m_prev, jnp.max(s, 0, keepdims=True))
        alpha = jnp.exp2(m_prev - m_new)
        p = jnp.exp2(s - m_new).astype(BF16)
        acc_sc[...] = alpha * acc_sc[...] + _dot(with_ones(vs_ref[:, pl.ds(k0, TK_SEL)]), p)
        m_sc[...] = m_new
        return carry

    lax.fori_loop(0, (q0 + tq - 1) // TK_SEL + 1, sel_body, 0)
    o_slc = normalise(acc_sc[...])

    start = pl.multiple_of(jnp.maximum(q0 - WINDOW, 0), LANE)
    s = _dot_nt(kw_ref[pl.ds(start, WIN_SPAN), :], qs)
    dist = pos_q - (start + lax.broadcasted_iota(jnp.int32, (WIN_SPAN, tq), 0))
    bias = jnp.where(dist >= 0, jnp.where(dist < WINDOW, 0.0, NEG), NEG)
    s = s + _tile_heads(bias)
    p = jnp.exp2(s - jnp.max(s, 0, keepdims=True)).astype(BF16)
    o_win = normalise(_dot(with_ones(vw_ref[:, pl.ds(start, WIN_SPAN)]), p))

    gates = jax.nn.sigmoid(gate_ref[...]).T

    def gate(hh, branch):
        lane0 = GATE_LANE0 + hh * 3 + branch
        lane1 = lane0 + NSA_HPG * 3
        return jnp.where(g == 0, gates[lane0:lane0 + 1, :], gates[lane1:lane1 + 1, :])

    for hh in range(NSA_HPG):
        c = slice(hh * tq, (hh + 1) * tq)
        o = gate(hh, 0) * o_cmp[:, c] + gate(hh, 1) * o_slc[:, c] + gate(hh, 2) * o_win[:, c]
        o_ref[:, hh * NSA_DH:(hh + 1) * NSA_DH] = o.T.astype(BF16)


def _nsa(q_r, kc, vc, ks, vs, kw, vw, proj):
    nq = SEQ // TQ_NSA
    gw = NSA_HPG * NSA_DH
    overlap_t, expand = _nsa_constants()
    cols = NSA_HPG * TQ_NSA
    per_group = lambda d0, d1: pl.BlockSpec((None, None, d0, d1), lambda b, g, i: (b, g, 0, 0))
    return pl.pallas_call(
        _nsa_kernel,
        grid=(BATCH, NSA_GROUPS, nq),
        in_specs=[pl.BlockSpec((TQ_NSA, gw), lambda b, g, i: (b * nq + i, g)),
                  per_group(N_CMP_PAD, NSA_DH), per_group(NSA_DH, N_CMP_PAD),
                  per_group(SEQ, NSA_DH), per_group(NSA_DH, SEQ),
                  per_group(SEQ, NSA_DH), per_group(NSA_DH, SEQ),
                  pl.BlockSpec((TQ_NSA, LANE), lambda b, g, i: (b * nq + i, COL_TAIL // LANE)),
                  pl.BlockSpec(overlap_t.shape, lambda b, g, i: (0, 0)),
                  pl.BlockSpec(expand.shape, lambda b, g, i: (0, 0))],
        out_specs=pl.BlockSpec((TQ_NSA, gw), lambda b, g, i: (b * nq + i, g)),
        out_shape=jax.ShapeDtypeStruct((N_TOK, NSA_GROUPS * gw), BF16),
        scratch_shapes=[pltpu.VMEM((N_BLK, TQ_NSA), F32),
                        pltpu.VMEM((1, cols), F32),
                        pltpu.VMEM((NSA_DH + ONES_ROWS, cols), F32)],
        compiler_params=_params("parallel", "parallel", "arbitrary"),
        name="nsa_attn",
    )(q_r, kc, vc, ks, vs, kw, vw, proj, jnp.asarray(overlap_t, BF16), jnp.asarray(expand, BF16))


def _outproj_kernel(yg_ref, yn_ref, x_ref, wo_ref, g1_ref, sc2_ref, sh2_ref, ln1g_ref, ln1b_ref,
                    wr_ref, br_ref, ltri_ref, x1_ref, h2_ref, idx_ref, wt_ref, rank_ref,
                    cnt_ref, base_sc):
    half = D_MODEL // 2

    @pl.when(pl.program_id(0) == 0)
    def _():
        base_sc[...] = jnp.zeros_like(base_sc)

    mix_all = _dot(yg_ref[...], wo_ref[0:half, :]) + _dot(yn_ref[...], wo_ref[half:D_MODEL, :])
    wr = wr_ref[...]
    sub = TM_OUT // OUT_CHAINS
    lane = lax.broadcasted_iota(jnp.int32, (TM_OUT, LANE), 1)
    lane_sub = lax.broadcasted_iota(jnp.int32, (sub, LANE), 1)
    lane_f = lane_sub.astype(F32)
    onehot_parts, h2_parts = [], []
    for ci in range(OUT_CHAINS):
        r = slice(ci * sub, (ci + 1) * sub)
        x1 = _ln(DN_ALPHA * x_ref[r, :] + (1.0 + g1_ref[...]) * mix_all[r]) * ln1g_ref[...] + ln1b_ref[...]
        x1_ref[r, :] = x1
        h2 = _ln(x1) * (1.0 + sc2_ref[...]) + sh2_ref[...]
        h_hi, h_lo = _split_bf16(h2)
        h2_parts.append(h2)
        lg = _dot(h_hi, wr) + _dot(h_lo, wr)
        logits = lg[:, 0:LANE] + lg[:, LANE:2 * LANE] + br_ref[...]

        vals = logits
        idx_out = jnp.zeros(logits.shape, jnp.int32)
        exp_out = jnp.zeros(logits.shape, F32)
        denom = jnp.zeros((sub, 1), F32)
        m0 = None
        onehots = []
        for k in range(TOP_K):
            mk = jnp.max(vals, -1, keepdims=True)
            ik = jnp.min(jnp.where(vals == mk, lane_f, float(LANE)), -1, keepdims=True)
            if k == 0:
                m0 = mk
            ek = jnp.exp(mk - m0)
            denom = denom + ek
            idx_out = jnp.where(lane_sub == k, ik.astype(jnp.int32), idx_out)
            exp_out = jnp.where(lane_sub == k, ek, exp_out)
            hit = lane_f == ik
            onehots.append(jnp.where(hit, 1.0, 0.0))
            vals = jnp.where(hit, -3.0e38, vals)
        idx_ref[r, :] = idx_out
        wt_ref[r, :] = exp_out / denom
        onehot_parts.append(onehots)
    onehots = [jnp.concatenate([part[k] for part in onehot_parts], axis=0) for k in range(TOP_K)]
    h2_all = jnp.concatenate(h2_parts, axis=0)
    for s in range(ROW_CHUNKS):
        h2_ref[pl.ds(s, TM_OUT, stride=ROW_CHUNKS), :] = h2_all[:, s * LANE:(s + 1) * LANE]

    cnt = onehots[0] + onehots[1] + onehots[2] + onehots[3]
    base = base_sc[0:1, :]
    before = _dot(ltri_ref[...], cnt.astype(BF16)) + base
    rank_out = jnp.zeros((TM_OUT, LANE), jnp.int32)
    for k in range(TOP_K):
        rk = jnp.sum(onehots[k] * before, -1, keepdims=True)
        rank_out = jnp.where(lane == k, rk.astype(jnp.int32), rank_out)
    rank_ref[...] = rank_out
    base_sc[...] = jnp.broadcast_to(base + jnp.sum(cnt, 0, keepdims=True), base_sc.shape)
    cnt_ref[...] = base_sc[...]


def _outproj(y_gla, y_nsa, xf, w_o, mod3, ln1_g, ln1_b, wr, br):
    tiles_per_batch = SEQ // TM_OUT
    half = D_MODEL // 2
    mod_spec = lambda chunk: pl.BlockSpec((None, 1, D_MODEL), lambda i: (i // tiles_per_batch, 0, chunk))
    full = lambda shape: pl.BlockSpec(shape, lambda i: tuple(0 for _ in shape))
    row = lambda width: pl.BlockSpec((TM_OUT, width), lambda i: (i, 0))
    ltri = jnp.asarray(np.tril(np.ones((TM_OUT, TM_OUT), np.float32), -1), BF16)
    return pl.pallas_call(
        _outproj_kernel,
        grid=(N_TOK // TM_OUT,),
        in_specs=[row(half), row(half), row(D_MODEL), full(w_o.shape),
                  mod_spec(2), mod_spec(4), mod_spec(3),
                  full(ln1_g.shape), full(ln1_b.shape), full(wr.shape), full(br.shape),
                  full(ltri.shape)],
        out_specs=[row(D_MODEL), pl.BlockSpec((TM_OUT * ROW_CHUNKS, LANE), lambda i: (i, 0)),
                   row(LANE), row(LANE), row(LANE), full((8, LANE))],
        out_shape=[jax.ShapeDtypeStruct((N_TOK, D_MODEL), F32),
                   jax.ShapeDtypeStruct((N_TOK * ROW_CHUNKS, LANE), F32),
                   jax.ShapeDtypeStruct((N_TOK, LANE), jnp.int32),
                   jax.ShapeDtypeStruct((N_TOK, LANE), F32),
                   jax.ShapeDtypeStruct((N_TOK, LANE), jnp.int32),
                   jax.ShapeDtypeStruct((8, LANE), F32)],
        scratch_shapes=[pltpu.VMEM((8, LANE), F32)],
        compiler_params=_params("arbitrary"),
        name="outproj_router",
    )(y_gla, y_nsa, xf, w_o, mod3, mod3, mod3, ln1_g, ln1_b, wr, br, ltri)


def _invert_kernel(dest_ref, src_sm):
    def fill(r, carry):
        src_sm[r] = 0
        return carry

    lax.fori_loop(0, P_ROWS, fill, 0, unroll=8)

    def scatter(tok2, carry):
        a0 = tok2 * (2 * TOP_K)
        rows = [dest_ref[a0 + i] for i in range(2 * TOP_K)]
        for i, row in enumerate(rows):
            src_sm[row] = tok2 * 2 + i // TOP_K
        return carry

    lax.fori_loop(0, N_TOK // 2, scatter, 0)


def _invert(dest_flat):
    smem = pl.BlockSpec(memory_space=pltpu.SMEM)
    return pl.pallas_call(
        _invert_kernel,
        in_specs=[smem],
        out_specs=smem,
        out_shape=jax.ShapeDtypeStruct((P_ROWS,), jnp.int32),
        name="moe_invert",
    )(dest_flat)


def _dispatch_kernel(src_sm, nused_ref, nbatch_ref, h2_hbm, xs_ref, buf, sems):
    t = pl.program_id(0)
    n_used = nused_ref[0]

    def start_gather(tile, slot):
        base = tile * TM_MOE

        def body(r8, carry):
            r0 = r8 * DMA_BATCH
            toks = [src_sm[base + r0 + i] for i in range(DMA_BATCH)]
            for i, tok in enumerate(toks):
                pltpu.make_async_copy(h2_hbm.at[pl.ds(pl.multiple_of(tok * ROW_CHUNKS, ROW_CHUNKS), ROW_CHUNKS), :],
                                      buf.at[slot, pl.ds(pl.multiple_of((r0 + i) * BUF_PITCH, 8), ROW_CHUNKS), :],
                                      sems.at[slot]).start(priority=i % 2)
            return carry

        lax.fori_loop(0, nbatch_ref[tile], body, 0)

    @pl.when(t == 0)
    def _():
        buf[...] = jnp.zeros(buf.shape, F32)
        start_gather(0, 0)

    @pl.when(t + 1 < n_used)
    def _():
        start_gather(t + 1, (t + 1) % 2)

    @pl.when(t < n_used)
    def _():
        slot = t % 2
        batch = buf.at[slot, pl.ds(0, DMA_BATCH * ROW_CHUNKS), :]

        def wait_batch(b, carry):
            pltpu.make_async_copy(batch, batch, sems.at[slot]).wait()
            return carry

        lax.fori_loop(0, nbatch_ref[t], wait_batch, 0)
        for s in range(ROW_CHUNKS):
            xs_ref[:, s * LANE:(s + 1) * LANE] = buf[slot, pl.ds(s, TM_MOE, stride=BUF_PITCH), :].astype(BF16)

    @pl.when(t >= n_used)
    def _():
        xs_ref[...] = jnp.zeros(xs_ref.shape, BF16)


def _dispatch(src_tok, n_used, n_batches, h2):
    return pl.pallas_call(
        _dispatch_kernel,
        grid_spec=pltpu.PrefetchScalarGridSpec(
            num_scalar_prefetch=3,
            grid=(N_MTILES,),
            in_specs=[pl.BlockSpec(memory_space=pl.ANY)],
            out_specs=pl.BlockSpec((TM_MOE, D_MODEL), lambda t, src, nu, nb: (t, 0)),
            scratch_shapes=[pltpu.VMEM((2, TM_MOE * BUF_PITCH, LANE), F32),
                            pltpu.SemaphoreType.DMA((2,))]),
        out_shape=jax.ShapeDtypeStruct((P_ROWS, D_MODEL), BF16),
        compiler_params=_params("arbitrary"),
        name="moe_dispatch",
    )(src_tok, n_used, n_batches, h2)


class _ExpertWeights:
    def __init__(self, eid_ref, n_used, w_hbms, bufs, sems, cnt_sm):
        self.eid_ref, self.n_used = eid_ref, n_used
        self.w_hbms, self.bufs, self.sems, self.cnt_sm = w_hbms, bufs, sems, cnt_sm

    def _copies(self, expert, j, slot):
        tn = self.bufs[0].shape[-1]
        cols = pl.ds(pl.multiple_of(j * tn, tn), tn)
        return [pltpu.make_async_copy(w.at[expert, :, cols], buf.at[slot], self.sems.at[i, slot])
                for i, (w, buf) in enumerate(zip(self.w_hbms, self.bufs))]

    def prologue(self):
        self.cnt_sm[0] = 0
        for cp in self._copies(self.eid_ref[0], 0, 0):
            cp.start()

    def acquire(self, j, t, n_passes):
        slot = self.cnt_sm[0] % 2
        for cp in self._copies(self.eid_ref[t], j, slot):
            cp.wait()
        expert = self.eid_ref[t]
        nxt = lax.while_loop(
            lambda u: (u < self.n_used) & (self.eid_ref[jnp.minimum(u, N_MTILES - 1)] == expert),
            lambda u: u + 1, t + 1)
        in_pass = nxt < self.n_used

        @pl.when(in_pass | (j + 1 < n_passes))
        def _():
            for cp in self._copies(self.eid_ref[jnp.where(in_pass, nxt, 0)], jnp.where(in_pass, j, j + 1),
                                   1 - slot):
                cp.start()

        self.cnt_sm[0] = self.cnt_sm[0] + 1
        return slot


def _on_live_rows(valid, out_ref, compute):
    lo = 0
    for rows in ROW_STEPS:
        @pl.when((valid > lo) & (valid <= rows) if lo else (valid <= rows))
        def _():
            out_ref[0:rows, :] = compute(rows).astype(out_ref.dtype)
            if rows < TM_MOE:
                out_ref[rows:TM_MOE, :] = jnp.zeros((TM_MOE - rows, out_ref.shape[1]), out_ref.dtype)

        lo = rows


def _moe_up_kernel(eid_ref, nused_ref, valid_ref, x_ref, wg_hbm, wu_hbm, bg_ref, bu_ref, h_ref,
                   wg_buf, wu_buf, wg_sc, wu_sc, sems, cnt_sm):
    j = pl.program_id(0)
    t = pl.program_id(1)
    n_used = nused_ref[0]
    weights = _ExpertWeights(eid_ref, n_used, (wg_hbm, wu_hbm), (wg_buf, wu_buf), sems, cnt_sm)

    @pl.when((j == 0) & (t == 0))
    def _():
        weights.prologue()

    @pl.when(t < n_used)
    def _():
        @pl.when((t == 0) | (eid_ref[t] != eid_ref[jnp.maximum(t - 1, 0)]))
        def _():
            slot = weights.acquire(j, t, pl.num_programs(0))
            wg_sc[...] = wg_buf[slot].astype(BF16)
            wu_sc[...] = wu_buf[slot].astype(BF16)

        def swiglu(rows):
            x = x_ref[0:rows, :]
            gate = jnp.minimum(_dot(x, wg_sc[...]) + bg_ref[...], SWIGLU_LIMIT)
            up = jnp.clip(_dot(x, wu_sc[...]) + bu_ref[...], -SWIGLU_LIMIT, SWIGLU_LIMIT)
            return gate * jax.nn.sigmoid(SWIGLU_ALPHA * gate) * (up + 1.0)

        _on_live_rows(valid_ref[t], h_ref, swiglu)

    @pl.when(t >= n_used)
    def _():
        h_ref[...] = jnp.zeros(h_ref.shape, BF16)


def _moe_down_kernel(eid_ref, nused_ref, valid_ref, h_ref, wd_hbm, bd_ref, y_ref, wd_buf, wd_sc, sems, cnt_sm):
    j = pl.program_id(0)
    t = pl.program_id(1)
    n_used = nused_ref[0]
    weights = _ExpertWeights(eid_ref, n_used, (wd_hbm,), (wd_buf,), sems, cnt_sm)

    @pl.when((j == 0) & (t == 0))
    def _():
        weights.prologue()

    @pl.when(t < n_used)
    def _():
        @pl.when((t == 0) | (eid_ref[t] != eid_ref[jnp.maximum(t - 1, 0)]))
        def _():
            slot = weights.acquire(j, t, pl.num_programs(0))
            wd_sc[...] = wd_buf[slot].astype(BF16)

        _on_live_rows(valid_ref[t], y_ref, lambda rows: _dot(h_ref[0:rows, :], wd_sc[...]) + bd_ref[...])

    @pl.when(t >= n_used)
    def _():
        y_ref[...] = jnp.zeros(y_ref.shape, F32)


def _row_tile(t, nused_ref):
    return jnp.minimum(t, jnp.maximum(nused_ref[0] - 1, 0))


def _moe_call(body, name, n_weights, k_dim, out_dim, tn, out_dtype, tile_eid, n_used, tile_valid, rows, weights,
              biases):
    any_spec = pl.BlockSpec(memory_space=pl.ANY)
    b_spec = pl.BlockSpec((None, 1, tn), lambda j, t, eid, nu, tv: (eid[t], 0, j))
    return pl.pallas_call(
        body,
        grid_spec=pltpu.PrefetchScalarGridSpec(
            num_scalar_prefetch=3,
            grid=(out_dim // tn, N_MTILES),
            in_specs=[pl.BlockSpec((TM_MOE, k_dim), lambda j, t, eid, nu, tv: (_row_tile(t, nu), 0))]
            + [any_spec] * n_weights + [b_spec] * n_weights,
            out_specs=pl.BlockSpec((TM_MOE, tn), lambda j, t, eid, nu, tv: (t, j)),
            scratch_shapes=[pltpu.VMEM((2, k_dim, tn), F32)] * n_weights
            + [pltpu.VMEM((k_dim, tn), BF16)] * n_weights
            + [pltpu.SemaphoreType.DMA((n_weights, 2)), pltpu.SMEM((1,), jnp.int32)]),
        out_shape=jax.ShapeDtypeStruct((P_ROWS, out_dim), out_dtype),
        compiler_params=_params("arbitrary", "arbitrary"),
        name=name,
    )(tile_eid, n_used, tile_valid, rows, *weights, *biases)


def _moe_up(tile_eid, n_used, tile_valid, xs, w_gate, w_up, b_gate, b_up):
    return _moe_call(_moe_up_kernel, "moe_up", 2, D_MODEL, D_FF, TN_UP, BF16, tile_eid, n_used, tile_valid, xs,
                     (w_gate, w_up), (b_gate, b_up))


def _moe_down(tile_eid, n_used, tile_valid, h, w_down, b_down):
    return _moe_call(_moe_down_kernel, "moe_down", 1, D_FF, D_MODEL, TN_DOWN, F32, tile_eid, n_used, tile_valid, h,
                     (w_down,), (b_down,))


def _final_kernel(dest_ref, y_hbm, x1_ref, wt_ref, g2_ref, ln2g_ref, ln2b_ref, o_ref, ybuf, sems):
    i = pl.program_id(0)
    n_tiles = pl.num_programs(0)

    def start_gather(tile, slot):
        base = tile * (TM_FIN * TOP_K)

        def body(r2, carry):
            a0 = base + r2 * DMA_BATCH
            rows = [dest_ref[a0 + i] for i in range(DMA_BATCH)]
            for i, d in enumerate(rows):
                r = r2 * (DMA_BATCH // TOP_K) + i // TOP_K
                pltpu.make_async_copy(y_hbm.at[pl.ds(d, 1), :], ybuf.at[slot, i % TOP_K, pl.ds(r, 1), :],
                                      sems.at[slot]).start(priority=i % 2)
            return carry

        lax.fori_loop(0, TM_FIN * TOP_K // DMA_BATCH, body, 0)

    @pl.when(i == 0)
    def _():
        start_gather(0, 0)

    @pl.when(i + 1 < n_tiles)
    def _():
        start_gather(i + 1, (i + 1) % 2)

    slot = i % 2
    pltpu.make_async_copy(ybuf.at[slot], ybuf.at[slot], sems.at[slot]).wait()
    wt = wt_ref[...]
    y = wt[:, 0:1] * ybuf[slot, 0]
    for k in range(1, TOP_K):
        y = y + wt[:, k:k + 1] * ybuf[slot, k]
    o_ref[...] = _ln(DN_ALPHA * x1_ref[...] + (1.0 + g2_ref[...]) * y) * ln2g_ref[...] + ln2b_ref[...]


def _final(dest_flat, y, x1, wt, mod3, ln2_g, ln2_b):
    tiles_per_batch = SEQ // TM_FIN
    full = lambda shape: pl.BlockSpec(shape, lambda i, dest: tuple(0 for _ in shape))
    row = lambda width: pl.BlockSpec((TM_FIN, width), lambda i, dest: (i, 0))
    return pl.pallas_call(
        _final_kernel,
        grid_spec=pltpu.PrefetchScalarGridSpec(
            num_scalar_prefetch=1,
            grid=(N_TOK // TM_FIN,),
            in_specs=[pl.BlockSpec(memory_space=pl.ANY), row(D_MODEL), row(LANE),
                      pl.BlockSpec((None, 1, D_MODEL), lambda i, dest: (i // tiles_per_batch, 0, 5)),
                      full(ln2_g.shape), full(ln2_b.shape)],
            out_specs=row(D_MODEL),
            scratch_shapes=[pltpu.VMEM((2, TOP_K, TM_FIN, D_MODEL), F32),
                            pltpu.SemaphoreType.DMA((2,))]),
        out_shape=jax.ShapeDtypeStruct((N_TOK, D_MODEL), F32),
        compiler_params=_params("arbitrary"),
        name="combine_ln2",
    )(dest_flat, y, x1, wt, mod3, ln2_g, ln2_b)


def _route(idx, rank, counts):
    experts = jnp.arange(N_EXPERTS, dtype=jnp.int32)
    padded = ((counts + TM_MOE - 1) // TM_MOE) * TM_MOE
    ends = jnp.sum(jnp.where(experts[None, :] <= experts[:, None], padded[None, :], 0), axis=1)
    starts = ends - padded
    dest = rank
    for e in range(N_EXPERTS):
        dest = dest + jnp.where(idx == e, starts[e], 0)
    dest = dest[:, :TOP_K]
    tile_start = jnp.arange(N_MTILES, dtype=jnp.int32) * TM_MOE
    tile_eid = jnp.minimum(jnp.sum((ends[None, :] <= tile_start[:, None]).astype(jnp.int32), axis=1),
                           N_EXPERTS - 1)
    n_used = (ends[-1] // TM_MOE).astype(jnp.int32).reshape(1)
    group_fill = jnp.sum(jnp.where(tile_eid[:, None] == experts[None, :], (starts + counts)[None, :], 0), axis=1)
    tile_valid = jnp.clip(group_fill - tile_start, 0, TM_MOE)
    n_batches = (tile_valid + DMA_BATCH - 1) // DMA_BATCH
    return dest.reshape(-1), tile_eid, n_used, tile_valid, n_batches


def kernel(x, c, w_ada, b_ada, w_in, w_gk, b_gk, gla_norm_g, pe_k, pe_v, w_ck1, b_ck1, w_ck2, b_ck2,
           w_cv1, b_cv1, w_cv2, b_cv2, w_o, ln1_g, ln1_b, w_router, b_router, w_gate, b_gate, w_up, b_up,
           w_down, b_down, ln2_g, ln2_b):
    l = 0
    xf = x.reshape(N_TOK, D_MODEL)
    row2 = lambda a: a.reshape(1, -1)

    c8 = jnp.pad(c, ((0, 8 - BATCH), (0, 0)))
    mod3 = _adaln(c8, w_ada[l], row2(b_ada[l]))[:BATCH].reshape(BATCH, 1, 6 * D_MODEL)

    wt = w_in[l].T
    glr0 = 3072
    nsa0 = glr0 + GLA_RANK
    ngt0 = nsa0 + 1024 + 6 * 256
    w_in_t = jnp.concatenate(
        [wt[:glr0], wt[nsa0:ngt0], wt[glr0:nsa0], wt[ngt0:],
         jnp.zeros((D_IN_PAD - wt.shape[0], D_MODEL), F32)], axis=0).astype(BF16)
    proj = _inproj(xf, mod3, w_in_t)

    w_gk_pad = jnp.pad(w_gk[l], ((0, LANE - GLA_RANK), (0, 0)))
    y_gla = _gla(proj, w_gk_pad, row2(b_gk[l]), row2(gla_norm_g[l]))

    cos_t, sin_t = _rope_tables(jnp.arange(SEQ))
    cmp_end = jnp.arange(N_CMP_PAD) * CMP_STRIDE + (CMP_BLOCK - 1)
    cos_c, sin_c = _rope_tables(cmp_end)
    q_r, ks, vs, kw, vw = _prep(proj, cos_t, sin_t)
    kc = _compress(proj, 0, pe_k[l], w_ck1[l], row2(b_ck1[l]), w_ck2[l], row2(b_ck2[l]), cos_c, sin_c, True)
    vc = _compress(proj, 1, pe_v[l], w_cv1[l], row2(b_cv1[l]), w_cv2[l], row2(b_cv2[l]), cos_c, sin_c, False)
    y_nsa = _nsa(q_r, kc, vc, ks, vs, kw, vw, proj)

    wr = jnp.pad(w_router[l], ((0, 0), (0, LANE - N_EXPERTS)))
    wr_hi, wr_lo = _split_bf16(wr)
    br = jnp.concatenate([b_router[l], jnp.full((LANE - N_EXPERTS,), NEG, F32)]).reshape(1, LANE)
    x1, h2, idx128, wt128, rank128, cnt8 = _outproj(y_gla, y_nsa, xf, w_o[l].astype(BF16), mod3,
                                                    row2(ln1_g[l]), row2(ln1_b[l]),
                                                    jnp.concatenate([wr_hi, wr_lo], axis=1), br)

    counts = cnt8[0, :N_EXPERTS].astype(jnp.int32)
    dest, tile_eid, n_used, tile_valid, n_batches = _route(idx128, rank128, counts)
    xs = _dispatch(_invert(dest), n_used, n_batches, h2)
    h = _moe_up(tile_eid, n_used, tile_valid, xs, w_gate[l], w_up[l],
                b_gate[l].reshape(N_EXPERTS, 1, D_FF), b_up[l].reshape(N_EXPERTS, 1, D_FF))
    y = _moe_down(tile_eid, n_used, tile_valid, h, w_down[l], b_down[l].reshape(N_EXPERTS, 1, D_MODEL))

    out = _final(dest, y, x1, wt128, mod3, row2(ln2_g[l]), row2(ln2_b[l]))
    return out.reshape(BATCH, SEQ, D_MODEL)
```

```python
import functools

import numpy as np
import jax
import jax.numpy as jnp
from jax import lax
from jax.experimental import pallas as pl
from jax.experimental.pallas import tpu as pltpu

F32 = jnp.float32
BF16 = jnp.bfloat16

D_MODEL = 2048
BATCH = 2
SEQ = 4096
N_TOK = BATCH * SEQ

GLA_HEADS = 4
GLA_DK = 128
GLA_DV = 256
GLA_RANK = 16
GLA_TAU = 16.0
GLA_CHUNK = 64

NSA_DH = 128
NSA_HEADS = 8
NSA_GROUPS = 2
NSA_HPG = 4
CMP_STRIDE = 16
CMP_BLOCK = 32
CMP_HIDDEN = 256
N_CMP_PAD = SEQ // CMP_STRIDE
SEL_BLOCK = 64
N_BLK = SEQ // SEL_BLOCK
SEL_TOPK = 16
WINDOW = 512
ROPE_DIM = 32
ROPE_THETA = 500000.0
Q_SCALE_LOG2 = NSA_DH ** -0.5 * 1.4426950408889634

N_EXPERTS = 32
TOP_K = 4
D_FF = D_MODEL
SWIGLU_LIMIT = 7.0
SWIGLU_ALPHA = 1.702
DN_ALPHA = 2.0 ** 0.25
LN_EPS = 1e-5

COL_GQ, COL_GK, COL_GV, COL_GR, COL_NQ = 0, 512, 1024, 2048, 3072
COL_KV = 4096
COL_TAIL = 5632
D_IN_PAD = 6144
GATE_LANE0 = GLA_RANK

LANE = 128
ROW_CHUNKS = D_MODEL // LANE
BUF_PITCH = ROW_CHUNKS + 8
NEG = -1e30
VMEM_LIMIT = 56 * 1024 * 1024

TM_IN, TN_IN = 1024, 1024
TT_GLA = 512
TR_PREP = 512
TQ_NSA = 256
TK_SEL = 512
WIN_SPAN = WINDOW + TQ_NSA
ONES_ROWS = 16
TM_OUT = 256
OUT_CHAINS = 2
TM_MOE = 256
ROW_STEPS = (64, 128, 192, 256)
TN_UP, TN_DOWN = 1024, 2048
P_ROWS = N_TOK * TOP_K + N_EXPERTS * TM_MOE
N_MTILES = P_ROWS // TM_MOE
TM_FIN = 256
DMA_BATCH = 8


def _dot(a, b):
    return jnp.dot(a, b, preferred_element_type=F32)


def _dot_nt(a, b):
    return lax.dot_general(a, b, (((1,), (1,)), ((), ())), preferred_element_type=F32)


def _dot_tn(a, b):
    return lax.dot_general(a, b, (((0,), (0,)), ((), ())), preferred_element_type=F32)


def _ln(x):
    xc = x - jnp.mean(x, -1, keepdims=True)
    return xc * lax.rsqrt(jnp.mean(xc * xc, -1, keepdims=True) + LN_EPS)


def _split_bf16(x):
    hi = x.astype(BF16)
    lo = (x - hi.astype(F32)).astype(BF16)
    return hi, lo


def _params(*sem):
    return pltpu.CompilerParams(dimension_semantics=sem, vmem_limit_bytes=VMEM_LIMIT)


def _adaln_kernel(c_ref, w_ref, b_ref, o_ref):
    c = c_ref[...]
    a = (c * jax.nn.sigmoid(c)).astype(BF16)
    o_ref[...] = _dot(a, w_ref[...].astype(BF16)) + b_ref[...]


def _adaln(c8, w, b):
    n = w.shape[1]
    tn = 1024
    return pl.pallas_call(
        _adaln_kernel,
        grid=(n // tn,),
        in_specs=[pl.BlockSpec((8, D_MODEL), lambda j: (0, 0)),
                  pl.BlockSpec((D_MODEL, tn), lambda j: (0, j)),
                  pl.BlockSpec((1, tn), lambda j: (0, j))],
        out_specs=pl.BlockSpec((8, tn), lambda j: (0, j)),
        out_shape=jax.ShapeDtypeStruct((8, n), F32),
        compiler_params=_params("arbitrary"),
        name="adaln",
    )(c8, w, b)


def _inproj_kernel(x_ref, sh_ref, sc_ref, w_ref, o_ref, h_sc):
    @pl.when(pl.program_id(1) == 0)
    def _():
        h = _ln(x_ref[...]) * (1.0 + sc_ref[...]) + sh_ref[...]
        h_sc[...] = h.astype(BF16)

    o_ref[...] = _dot_nt(h_sc[...], w_ref[...])


def _inproj(xf, mod3, w_in_t):
    tiles_per_batch = SEQ // TM_IN
    return pl.pallas_call(
        _inproj_kernel,
        grid=(N_TOK // TM_IN, D_IN_PAD // TN_IN),
        in_specs=[pl.BlockSpec((TM_IN, D_MODEL), lambda i, j: (i, 0)),
                  pl.BlockSpec((None, 1, D_MODEL), lambda i, j: (i // tiles_per_batch, 0, 0)),
                  pl.BlockSpec((None, 1, D_MODEL), lambda i, j: (i // tiles_per_batch, 0, 1)),
                  pl.BlockSpec((TN_IN, D_MODEL), lambda i, j: (j, 0))],
        out_specs=pl.BlockSpec((TM_IN, TN_IN), lambda i, j: (i, j)),
        out_shape=jax.ShapeDtypeStruct((N_TOK, D_IN_PAD), F32),
        scratch_shapes=[pltpu.VMEM((TM_IN, D_MODEL), BF16)],
        compiler_params=_params("parallel", "arbitrary"),
        name="inproj",
    )(xf, mod3, mod3, w_in_t)


GLA_HALVES = (32, 16, 8, 4, 2, 1)
N_LEVELS = len(GLA_HALVES)
ROW_EB = 2 * N_LEVELS
ROW_EL = 2 * N_LEVELS + 1
N_EVIEWS = 2 * N_LEVELS + 2


def _gla_constants():
    c = GLA_CHUNK
    t = np.arange(c)[:, None]
    r = np.arange(c)[None, :]
    mall = np.zeros((N_EVIEWS, c, c), np.float32)
    valid = np.zeros((N_EVIEWS, c, LANE), np.float32)
    masks = np.zeros((N_LEVELS + 1, c, c), np.float32)
    for li, n in enumerate(GLA_HALVES):
        same = (t // (2 * n)) == (r // (2 * n))
        t_up = (t % (2 * n)) >= n
        r_up = (r % (2 * n)) >= n
        mall[2 * li] = same & t_up & r_up & (r <= t)
        mall[2 * li + 1] = same & ~t_up & ~r_up & (r > t)
        valid[2 * li] = np.broadcast_to(t_up, (c, LANE))
        valid[2 * li + 1] = np.broadcast_to(~t_up, (c, LANE))
        masks[li] = same & t_up & ~r_up
    mall[ROW_EB] = r <= t
    mall[ROW_EL] = r > t
    valid[ROW_EB] = 1.0
    valid[ROW_EL] = 1.0
    masks[N_LEVELS] = np.eye(c)
    return (mall.reshape(N_EVIEWS * c, c), valid.reshape(N_EVIEWS * c, LANE), masks)


def _gla_kernel(q_ref, k_ref, v_ref, r_ref, glr_ref, wgk_ref, bgk_ref, g_ref, mall_ref, valid_ref,
                masks_ref, o_ref, st_sc):
    c = GLA_CHUNK

    @pl.when(pl.program_id(1) == 0)
    def _():
        st_sc[...] = jnp.zeros_like(st_sc)

    z = _dot(glr_ref[...].astype(BF16), wgk_ref[...].astype(BF16)) + bgk_ref[...]
    log_a = (jnp.minimum(z, 0.0) - jnp.log1p(jnp.exp(-jnp.abs(z)))) * (1.0 / GLA_TAU)
    mall = mall_ref[...]
    valid = valid_ref[...]
    for ci in range(TT_GLA // c):
        rows = slice(ci * c, (ci + 1) * c)
        la_hi, la_lo = _split_bf16(log_a[rows])
        e_all = jnp.exp(_dot(mall, la_hi) + _dot(mall, la_lo))
        for h in range(GLA_HEADS):
            kcols = slice(h * GLA_DK, (h + 1) * GLA_DK)
            vcols = slice(h * GLA_DV, (h + 1) * GLA_DV)
            e = e_all[:, kcols] * valid
            q = q_ref[rows, kcols] * (GLA_DK ** -0.5)
            k = k_ref[rows, kcols]
            vb = v_ref[rows, vcols].astype(BF16)
            att = masks_ref[N_LEVELS] * _dot_nt(q.astype(BF16), k.astype(BF16))
            for li in range(N_LEVELS):
                eq = e[(2 * li) * c:(2 * li + 1) * c]
                ek = e[(2 * li + 1) * c:(2 * li + 2) * c]
                att = att + masks_ref[li] * _dot_nt((q * eq).astype(BF16), (k * ek).astype(BF16))
            eb = e[ROW_EB * c:(ROW_EB + 1) * c]
            el = e[ROW_EL * c:(ROW_EL + 1) * c]
            st = st_sc[h]
            o = _dot_nt((q * eb).astype(BF16), st.astype(BF16)) + _dot(att.astype(BF16), vb)
            st_sc[h] = st * eb[c - 1:c, :] + _dot_tn(vb, (k * el).astype(BF16))
            o = o * lax.rsqrt(jnp.mean(o * o, -1, keepdims=True) + LN_EPS) * g_ref[...]
            r = r_ref[rows, vcols]
            o_ref[rows, vcols] = (o * (r * jax.nn.sigmoid(r))).astype(BF16)


def _gla(proj, w_gk_pad, b_gk, norm_g):
    nt = SEQ // TT_GLA
    mall, valid, masks = _gla_constants()
    kw = GLA_HEADS * GLA_DK
    vw = GLA_HEADS * GLA_DV
    full = lambda shape: pl.BlockSpec(shape, lambda b, i: tuple(0 for _ in shape))
    cols = lambda width, col0: pl.BlockSpec((TT_GLA, width), lambda b, i: (b * nt + i, col0 // width))
    return pl.pallas_call(
        _gla_kernel,
        grid=(BATCH, nt),
        in_specs=[cols(kw, COL_GQ), cols(kw, COL_GK), cols(vw, COL_GV), cols(vw, COL_GR),
                  cols(LANE, COL_TAIL), full(w_gk_pad.shape), full(b_gk.shape), full(norm_g.shape),
                  full(mall.shape), full(valid.shape), full(masks.shape)],
        out_specs=cols(vw, 0),
        out_shape=jax.ShapeDtypeStruct((N_TOK, vw), BF16),
        scratch_shapes=[pltpu.VMEM((GLA_HEADS, GLA_DV, GLA_DK), F32)],
        compiler_params=_params("parallel", "arbitrary"),
        name="gla",
    )(proj, proj, proj, proj, proj, w_gk_pad, b_gk, norm_g,
      jnp.asarray(mall, BF16), jnp.asarray(valid), jnp.asarray(masks))


def _rope_tables(pos):
    half = ROPE_DIM // 2
    inv_freq = ROPE_THETA ** (-jnp.arange(half, dtype=F32) * (2.0 / ROPE_DIM))
    ang = pos.astype(F32)[:, None] * inv_freq
    cos, sin = jnp.cos(ang), jnp.sin(ang)
    n = pos.shape[0]
    cosf = jnp.concatenate([cos, cos, jnp.ones((n, LANE - ROPE_DIM), F32)], -1)
    sinf = jnp.concatenate([-sin, sin, jnp.zeros((n, LANE - ROPE_DIM), F32)], -1)
    return cosf, sinf


def _rope(x, cosf, sinf):
    lane = lax.broadcasted_iota(jnp.int32, x.shape, 1)
    half = ROPE_DIM // 2
    swapped = jnp.where(lane < half, pltpu.roll(x, LANE - half, 1), pltpu.roll(x, half, 1))
    return x * cosf + swapped * sinf


def _prep_kernel(q_ref, ks_ref, vs_ref, kw_ref, vw_ref, cos_ref, sin_ref,
                 qo_ref, kso_ref, vso_ref, kwo_ref, vwo_ref):
    cosf = cos_ref[...]
    sinf = sin_ref[...]
    for hh in range(NSA_HEADS):
        cols = slice(hh * NSA_DH, (hh + 1) * NSA_DH)
        qo_ref[:, cols] = (_rope(q_ref[:, cols], cosf, sinf) * Q_SCALE_LOG2).astype(BF16)
    for g in range(NSA_GROUPS):
        cols = slice(g * NSA_DH, (g + 1) * NSA_DH)
        kso_ref[g] = _rope(ks_ref[:, cols], cosf, sinf).astype(BF16)
        kwo_ref[g] = _rope(kw_ref[:, cols], cosf, sinf).astype(BF16)
        vso_ref[g] = vs_ref[:, cols].T.astype(BF16)
        vwo_ref[g] = vw_ref[:, cols].T.astype(BF16)


def _prep(proj, cosf, sinf):
    nt = SEQ // TR_PREP
    kvw = NSA_GROUPS * NSA_DH
    kv_in = lambda which: pl.BlockSpec((TR_PREP, kvw), lambda b, i: (b * nt + i, COL_KV // kvw + which))
    k_out = pl.BlockSpec((None, NSA_GROUPS, TR_PREP, NSA_DH), lambda b, i: (b, 0, i, 0))
    k_shape = jax.ShapeDtypeStruct((BATCH, NSA_GROUPS, SEQ, NSA_DH), BF16)
    vt_out = pl.BlockSpec((None, NSA_GROUPS, NSA_DH, TR_PREP), lambda b, i: (b, 0, 0, i))
    vt_shape = jax.ShapeDtypeStruct((BATCH, NSA_GROUPS, NSA_DH, SEQ), BF16)
    qw = NSA_HEADS * NSA_DH
    return pl.pallas_call(
        _prep_kernel,
        grid=(BATCH, nt),
        in_specs=[pl.BlockSpec((TR_PREP, qw), lambda b, i: (b * nt + i, COL_NQ // qw)),
                  kv_in(2), kv_in(3), kv_in(4), kv_in(5),
                  pl.BlockSpec((TR_PREP, LANE), lambda b, i: (i, 0)),
                  pl.BlockSpec((TR_PREP, LANE), lambda b, i: (i, 0))],
        out_specs=[pl.BlockSpec((TR_PREP, qw), lambda b, i: (b * nt + i, 0)),
                   k_out, vt_out, k_out, vt_out],
        out_shape=[jax.ShapeDtypeStruct((N_TOK, qw), BF16), k_shape, vt_shape, k_shape, vt_shape],
        compiler_params=_params("parallel", "parallel"),
        name="nsa_prep",
    )(proj, proj, proj, proj, proj, cosf, sinf)


def _compress_kernel(a_ref, pe_ref, w1_ref, b1_ref, w2_ref, b2_ref, cos_ref, sin_ref, o_ref, bot_sc,
                     *, rope):
    n = N_CMP_PAD
    top = jnp.zeros((n, CMP_HIDDEN), F32)
    bot = jnp.zeros((n, CMP_HIDDEN), F32)
    for p in range(CMP_STRIDE):
        ap = a_ref[pl.ds(p, n, stride=CMP_STRIDE), :]
        w_top = w1_ref[p * NSA_DH:(p + 1) * NSA_DH, :].astype(BF16)
        w_bot = w1_ref[(CMP_STRIDE + p) * NSA_DH:(CMP_STRIDE + p + 1) * NSA_DH, :].astype(BF16)
        top = top + _dot((ap + pe_ref[p:p + 1, :]).astype(BF16), w_top)
        bot = bot + _dot((ap + pe_ref[CMP_STRIDE + p:CMP_STRIDE + p + 1, :]).astype(BF16), w_bot)
    bot_sc[0:n, :] = bot
    bot_sc[n:n + 8, :] = jnp.zeros((8, CMP_HIDDEN), F32)
    h = top + bot_sc[1:n + 1, :] + b1_ref[...]
    h = h * jax.nn.sigmoid(h)
    out = _dot(h.astype(BF16), w2_ref[...].astype(BF16)) + b2_ref[...]
    if rope:
        out = _rope(out, cos_ref[...], sin_ref[...])
    row = lax.broadcasted_iota(jnp.int32, out.shape, 0)
    out = jnp.where(row < n - 1, out, 0.0)
    o_ref[...] = (out if rope else out.T).astype(BF16)


def _compress(proj, which, pe, w1, b1, w2, b2, cosf, sinf, rope):
    col0 = (COL_KV + which * NSA_GROUPS * NSA_DH) // NSA_DH
    out_dims = (N_CMP_PAD, NSA_DH) if rope else (NSA_DH, N_CMP_PAD)
    full = lambda shape: pl.BlockSpec(shape, lambda b, g: tuple(0 for _ in shape))
    return pl.pallas_call(
        functools.partial(_compress_kernel, rope=rope),
        grid=(BATCH, NSA_GROUPS),
        in_specs=[pl.BlockSpec((SEQ, NSA_DH), lambda b, g: (b, col0 + g)),
                  full(pe.shape), full(w1.shape), full(b1.shape), full(w2.shape), full(b2.shape),
                  full(cosf.shape), full(sinf.shape)],
        out_specs=pl.BlockSpec((None, None) + out_dims, lambda b, g: (b, g, 0, 0)),
        out_shape=jax.ShapeDtypeStruct((BATCH, NSA_GROUPS) + out_dims, BF16),
        scratch_shapes=[pltpu.VMEM((N_CMP_PAD + 8, CMP_HIDDEN), F32)],
        compiler_params=_params("parallel", "parallel"),
        name="nsa_compress",
    )(proj, pe, w1, b1, w2, b2, cosf, sinf)


def _nsa_constants():
    c_start = np.arange(N_CMP_PAD) * CMP_STRIDE
    b_start = np.arange(N_BLK) * SEL_BLOCK
    overlap_t = ((c_start[None, :] < b_start[:, None] + SEL_BLOCK)
                 & (c_start[None, :] + CMP_BLOCK > b_start[:, None])).astype(np.float32)
    overlap_t[:, N_CMP_PAD - 1] = 0.0
    expand_t = (np.arange(SEQ)[:, None] // SEL_BLOCK == np.arange(LANE)[None, :]).astype(np.float32)
    return overlap_t, expand_t


def _tile_heads(a):
    return jnp.concatenate([a] * NSA_HPG, axis=1)


def _nsa_kernel(q_ref, kc_ref, vc_ref, ks_ref, vs_ref, kw_ref, vw_ref, gate_ref, ovt_ref, expand_ref,
                o_ref, score_sc, m_sc, acc_sc):
    tq = TQ_NSA
    g = pl.program_id(1)
    q0 = pl.program_id(2) * tq
    qs = jnp.concatenate([q_ref[:, hh * NSA_DH:(hh + 1) * NSA_DH] for hh in range(NSA_HPG)], axis=0)
    pos_q = q0 + lax.broadcasted_iota(jnp.int32, (1, tq), 1)

    def with_ones(vt):
        return jnp.concatenate([vt, jnp.ones((ONES_ROWS, vt.shape[1]), BF16)], axis=0)

    def normalise(acc):
        return acc[0:NSA_DH] / acc[NSA_DH:NSA_DH + 1]

    s = _dot_nt(kc_ref[...], qs)
    cmp_end = lax.broadcasted_iota(jnp.int32, (N_CMP_PAD, tq), 0) * CMP_STRIDE + (CMP_BLOCK - 1)
    s = s + _tile_heads(jnp.where(cmp_end <= pos_q, 0.0, NEG))
    m = jnp.max(s, 0, keepdims=True)
    e = jnp.where(s > 0.5 * NEG, jnp.exp2(s - m), 0.0)
    p = e / jnp.maximum(jnp.sum(e, 0, keepdims=True), 1e-30)
    o_cmp = _dot(vc_ref[...], p.astype(BF16))
    p_sum = p[:, 0:tq] + p[:, tq:2 * tq] + p[:, 2 * tq:3 * tq] + p[:, 3 * tq:4 * tq]
    ps_hi, ps_lo = _split_bf16(p_sum)
    ovt = ovt_ref[...]
    p_blk_t = _dot(ovt, ps_hi) + _dot(ovt, ps_lo)

    jj = lax.broadcasted_iota(jnp.int32, (N_BLK, tq), 0)
    cur = (q0 + lax.broadcasted_iota(jnp.int32, (N_BLK, tq), 1)) // SEL_BLOCK
    forced = (jj == 0) | (jj == cur) | (jj == cur - 1)
    allowed = jj <= cur
    score = jnp.where(forced, 3.0e38, jnp.where(allowed, p_blk_t, -1.0))
    score_sc[...] = score

    def rank_body(i, rank):
        row = score_sc[pl.ds(i, 1), :]
        first = jnp.where(jj > i, 1.0, 0.0)
        return rank + jnp.where(row > score, 1.0, jnp.where(row == score, first, 0.0))

    n_live = (q0 + tq - 1) // SEL_BLOCK + 1
    rank = lax.fori_loop(0, n_live, rank_body, jnp.zeros((N_BLK, tq), F32))
    sel_t = jnp.where(allowed, jnp.where(rank < SEL_TOPK, 1.0, 0.0), 0.0)
    sel = jnp.concatenate([sel_t, jnp.zeros((LANE - N_BLK, tq), F32)], axis=0).astype(BF16)

    m_sc[...] = jnp.full(m_sc.shape, NEG, F32)
    acc_sc[...] = jnp.zeros(acc_sc.shape, F32)

    def sel_body(kt, carry):
        k0 = pl.multiple_of(kt * TK_SEL, TK_SEL)
        s = _dot_nt(ks_ref[pl.ds(k0, TK_SEL), :], qs)
        sel_x = _dot(expand_ref[pl.ds(k0, TK_SEL), :], sel)
        kpos = k0 + lax.broadcasted_iota(jnp.int32, (TK_SEL, tq), 0)
        bias = jnp.where(kpos <= pos_q, jnp.where(sel_x > 0.5, 0.0, NEG), NEG)
        s = s + _tile_heads(bias)
        m_prev = m_sc[...]
        m_new = jnp.maximum(m_prev, jnp.max(s, 0, keepdims=True))
        alpha = jnp.exp2(m_prev - m_new)
        p = jnp.exp2(s - m_new).astype(BF16)
        acc_sc[...] = alpha * acc_sc[...] + _dot(with_ones(vs_ref[:, pl.ds(k0, TK_SEL)]), p)
        m_sc[...] = m_new
        return carry

    lax.fori_loop(0, (q0 + tq - 1) // TK_SEL + 1, sel_body, 0)
    o_slc = normalise(acc_sc[...])

    start = pl.multiple_of(jnp.maximum(q0 - WINDOW, 0), LANE)
    s = _dot_nt(kw_ref[pl.ds(start, WIN_SPAN), :], qs)
    dist = pos_q - (start + lax.broadcasted_iota(jnp.int32, (WIN_SPAN, tq), 0))
    bias = jnp.where(dist >= 0, jnp.where(dist < WINDOW, 0.0, NEG), NEG)
    s = s + _tile_heads(bias)
    p = jnp.exp2(s - jnp.max(s, 0, keepdims=True)).astype(BF16)
    o_win = normalise(_dot(with_ones(vw_ref[:, pl.ds(start, WIN_SPAN)]), p))

    gates = jax.nn.sigmoid(gate_ref[...]).T

    def gate(hh, branch):
        lane0 = GATE_LANE0 + hh * 3 + branch
        lane1 = lane0 + NSA_HPG * 3
        return jnp.where(g == 0, gates[lane0:lane0 + 1, :], gates[lane1:lane1 + 1, :])

    for hh in range(NSA_HPG):
        c = slice(hh * tq, (hh + 1) * tq)
        o = gate(hh, 0) * o_cmp[:, c] + gate(hh, 1) * o_slc[:, c] + gate(hh, 2) * o_win[:, c]
        o_ref[:, hh * NSA_DH:(hh + 1) * NSA_DH] = o.T.astype(BF16)


def _nsa(q_r, kc, vc, ks, vs, kw, vw, proj):
    nq = SEQ // TQ_NSA
    gw = NSA_HPG * NSA_DH
    overlap_t, expand = _nsa_constants()
    cols = NSA_HPG * TQ_NSA
    per_group = lambda d0, d1: pl.BlockSpec((None, None, d0, d1), lambda b, g, i: (b, g, 0, 0))
    return pl.pallas_call(
        _nsa_kernel,
        grid=(BATCH, NSA_GROUPS, nq),
        in_specs=[pl.BlockSpec((TQ_NSA, gw), lambda b, g, i: (b * nq + i, g)),
                  per_group(N_CMP_PAD, NSA_DH), per_group(NSA_DH, N_CMP_PAD),
                  per_group(SEQ, NSA_DH), per_group(NSA_DH, SEQ),
                  per_group(SEQ, NSA_DH), per_group(NSA_DH, SEQ),
                  pl.BlockSpec((TQ_NSA, LANE), lambda b, g, i: (b * nq + i, COL_TAIL // LANE)),
                  pl.BlockSpec(overlap_t.shape, lambda b, g, i: (0, 0)),
                  pl.BlockSpec(expand.shape, lambda b, g, i: (0, 0))],
        out_specs=pl.BlockSpec((TQ_NSA, gw), lambda b, g, i: (b * nq + i, g)),
        out_shape=jax.ShapeDtypeStruct((N_TOK, NSA_GROUPS * gw), BF16),
        scratch_shapes=[pltpu.VMEM((N_BLK, TQ_NSA), F32),
                        pltpu.VMEM((1, cols), F32),
                        pltpu.VMEM((NSA_DH + ONES_ROWS, cols), F32)],
        compiler_params=_params("parallel", "parallel", "arbitrary"),
        name="nsa_attn",
    )(q_r, kc, vc, ks, vs, kw, vw, proj, jnp.asarray(overlap_t, BF16), jnp.asarray(expand, BF16))


def _outproj_kernel(yg_ref, yn_ref, x_ref, wo_ref, g1_ref, sc2_ref, sh2_ref, ln1g_ref, ln1b_ref,
                    wr_ref, br_ref, ltri_ref, x1_ref, h2_ref, idx_ref, wt_ref, rank_ref,
                    cnt_ref, base_sc):
    half = D_MODEL // 2

    @pl.when(pl.program_id(0) == 0)
    def _():
        base_sc[...] = jnp.zeros_like(base_sc)

    mix_all = _dot(yg_ref[...], wo_ref[0:half, :]) + _dot(yn_ref[...], wo_ref[half:D_MODEL, :])
    wr = wr_ref[...]
    sub = TM_OUT // OUT_CHAINS
    lane = lax.broadcasted_iota(jnp.int32, (TM_OUT, LANE), 1)
    lane_sub = lax.broadcasted_iota(jnp.int32, (sub, LANE), 1)
    lane_f = lane_sub.astype(F32)
    onehot_parts, h2_parts = [], []
    for ci in range(OUT_CHAINS):
        r = slice(ci * sub, (ci + 1) * sub)
        x1 = _ln(DN_ALPHA * x_ref[r, :] + (1.0 + g1_ref[...]) * mix_all[r]) * ln1g_ref[...] + ln1b_ref[...]
        x1_ref[r, :] = x1
        h2 = _ln(x1) * (1.0 + sc2_ref[...]) + sh2_ref[...]
        h_hi, h_lo = _split_bf16(h2)
        h2_parts.append(h2)
        lg = _dot(h_hi, wr) + _dot(h_lo, wr)
        logits = lg[:, 0:LANE] + lg[:, LANE:2 * LANE] + br_ref[...]

        vals = logits
        idx_out = jnp.zeros(logits.shape, jnp.int32)
        exp_out = jnp.zeros(logits.shape, F32)
        denom = jnp.zeros((sub, 1), F32)
        m0 = None
        onehots = []
        for k in range(TOP_K):
            mk = jnp.max(vals, -1, keepdims=True)
            ik = jnp.min(jnp.where(vals == mk, lane_f, float(LANE)), -1, keepdims=True)
            if k == 0:
                m0 = mk
            ek = jnp.exp(mk - m0)
            denom = denom + ek
            idx_out = jnp.where(lane_sub == k, ik.astype(jnp.int32), idx_out)
            exp_out = jnp.where(lane_sub == k, ek, exp_out)
            hit = lane_f == ik
            onehots.append(jnp.where(hit, 1.0, 0.0))
            vals = jnp.where(hit, -3.0e38, vals)
        idx_ref[r, :] = idx_out
        wt_ref[r, :] = exp_out / denom
        onehot_parts.append(onehots)
    onehots = [jnp.concatenate([part[k] for part in onehot_parts], axis=0) for k in range(TOP_K)]
    h2_all = jnp.concatenate(h2_parts, axis=0)
    for s in range(ROW_CHUNKS):
        h2_ref[pl.ds(s, TM_OUT, stride=ROW_CHUNKS), :] = h2_all[:, s * LANE:(s + 1) * LANE]

    cnt = onehots[0] + onehots[1] + onehots[2] + onehots[3]
    base = base_sc[0:1, :]
    before = _dot(ltri_ref[...], cnt.astype(BF16)) + base
    rank_out = jnp.zeros((TM_OUT, LANE), jnp.int32)
    for k in range(TOP_K):
        rk = jnp.sum(onehots[k] * before, -1, keepdims=True)
        rank_out = jnp.where(lane == k, rk.astype(jnp.int32), rank_out)
    rank_ref[...] = rank_out
    base_sc[...] = jnp.broadcast_to(base + jnp.sum(cnt, 0, keepdims=True), base_sc.shape)
    cnt_ref[...] = base_sc[...]


def _outproj(y_gla, y_nsa, xf, w_o, mod3, ln1_g, ln1_b, wr, br):
    tiles_per_batch = SEQ // TM_OUT
    half = D_MODEL // 2
    mod_spec = lambda chunk: pl.BlockSpec((None, 1, D_MODEL), lambda i: (i // tiles_per_batch, 0, chunk))
    full = lambda shape: pl.BlockSpec(shape, lambda i: tuple(0 for _ in shape))
    row = lambda width: pl.BlockSpec((TM_OUT, width), lambda i: (i, 0))
    ltri = jnp.asarray(np.tril(np.ones((TM_OUT, TM_OUT), np.float32), -1), BF16)
    return pl.pallas_call(
        _outproj_kernel,
        grid=(N_TOK // TM_OUT,),
        in_specs=[row(half), row(half), row(D_MODEL), full(w_o.shape),
                  mod_spec(2), mod_spec(4), mod_spec(3),
                  full(ln1_g.shape), full(ln1_b.shape), full(wr.shape), full(br.shape),
                  full(ltri.shape)],
        out_specs=[row(D_MODEL), pl.BlockSpec((TM_OUT * ROW_CHUNKS, LANE), lambda i: (i, 0)),
                   row(LANE), row(LANE), row(LANE), full((8, LANE))],
        out_shape=[jax.ShapeDtypeStruct((N_TOK, D_MODEL), F32),
                   jax.ShapeDtypeStruct((N_TOK * ROW_CHUNKS, LANE), F32),
                   jax.ShapeDtypeStruct((N_TOK, LANE), jnp.int32),
                   jax.ShapeDtypeStruct((N_TOK, LANE), F32),
                   jax.ShapeDtypeStruct((N_TOK, LANE), jnp.int32),
                   jax.ShapeDtypeStruct((8, LANE), F32)],
        scratch_shapes=[pltpu.VMEM((8, LANE), F32)],
        compiler_params=_params("arbitrary"),
        name="outproj_router",
    )(y_gla, y_nsa, xf, w_o, mod3, mod3, mod3, ln1_g, ln1_b, wr, br, ltri)


def _dispatch_kernel(dest_ref, fill_ref, nused_ref, nbatch_ref, h2_hbm, xs_ref, src_sm, buf, sems):
    t = pl.program_id(0)
    n_used = nused_ref[0]

    @pl.when(t == 0)
    def _():
        def pad(e, carry):
            for i in range(DMA_BATCH):
                src_sm[jnp.minimum(fill_ref[e] + i, P_ROWS - 1)] = 0
            return carry

        lax.fori_loop(0, N_EXPERTS, pad, 0)

        def scatter(tok2, carry):
            a0 = tok2 * (2 * TOP_K)
            rows = [dest_ref[a0 + i] for i in range(2 * TOP_K)]
            for i, row in enumerate(rows):
                src_sm[row] = tok2 * 2 + i // TOP_K
            return carry

        lax.fori_loop(0, N_TOK // 2, scatter, 0)

    def start_gather(tile, slot):
        base = tile * TM_MOE

        def body(r8, carry):
            r0 = r8 * DMA_BATCH
            toks = [src_sm[base + r0 + i] for i in range(DMA_BATCH)]
            for i, tok in enumerate(toks):
                pltpu.make_async_copy(h2_hbm.at[pl.ds(pl.multiple_of(tok * ROW_CHUNKS, ROW_CHUNKS), ROW_CHUNKS), :],
                                      buf.at[slot, pl.ds(pl.multiple_of((r0 + i) * BUF_PITCH, 8), ROW_CHUNKS), :],
                                      sems.at[slot]).start(priority=i % 2)
            return carry

        lax.fori_loop(0, nbatch_ref[tile], body, 0)

    @pl.when(t == 0)
    def _():
        buf[...] = jnp.zeros(buf.shape, F32)
        start_gather(0, 0)

    @pl.when(t + 1 < n_used)
    def _():
        start_gather(t + 1, (t + 1) % 2)

    @pl.when(t < n_used)
    def _():
        slot = t % 2
        batch = buf.at[slot, pl.ds(0, DMA_BATCH * ROW_CHUNKS), :]

        def wait_batch(b, carry):
            pltpu.make_async_copy(batch, batch, sems.at[slot]).wait()
            return carry

        lax.fori_loop(0, nbatch_ref[t], wait_batch, 0)
        for s in range(ROW_CHUNKS):
            xs_ref[:, s * LANE:(s + 1) * LANE] = buf[slot, pl.ds(s, TM_MOE, stride=BUF_PITCH), :].astype(BF16)

    @pl.when(t >= n_used)
    def _():
        xs_ref[...] = jnp.zeros(xs_ref.shape, BF16)


def _dispatch(dest_flat, group_fill, n_used, n_batches, h2):
    return pl.pallas_call(
        _dispatch_kernel,
        grid_spec=pltpu.PrefetchScalarGridSpec(
            num_scalar_prefetch=4,
            grid=(N_MTILES,),
            in_specs=[pl.BlockSpec(memory_space=pl.ANY)],
            out_specs=pl.BlockSpec((TM_MOE, D_MODEL), lambda t, dest, fill, nu, nb: (t, 0)),
            scratch_shapes=[pltpu.SMEM((P_ROWS,), jnp.int32),
                            pltpu.VMEM((2, TM_MOE * BUF_PITCH, LANE), F32),
                            pltpu.SemaphoreType.DMA((2,))]),
        out_shape=jax.ShapeDtypeStruct((P_ROWS, D_MODEL), BF16),
        compiler_params=_params("arbitrary"),
        name="moe_dispatch",
    )(dest_flat, group_fill, n_used, n_batches, h2)


class _ExpertWeights:
    def __init__(self, eid_ref, n_used, w_hbms, bufs, sems, cnt_sm):
        self.eid_ref, self.n_used = eid_ref, n_used
        self.w_hbms, self.bufs, self.sems, self.cnt_sm = w_hbms, bufs, sems, cnt_sm

    def _copies(self, expert, j, slot):
        tn = self.bufs[0].shape[-1]
        cols = pl.ds(pl.multiple_of(j * tn, tn), tn)
        return [pltpu.make_async_copy(w.at[expert, :, cols], buf.at[slot], self.sems.at[i, slot])
                for i, (w, buf) in enumerate(zip(self.w_hbms, self.bufs))]

    def prologue(self):
        self.cnt_sm[0] = 0
        for cp in self._copies(self.eid_ref[0], 0, 0):
            cp.start()

    def acquire(self, j, t, n_passes):
        slot = self.cnt_sm[0] % 2
        for cp in self._copies(self.eid_ref[t], j, slot):
            cp.wait()
        expert = self.eid_ref[t]
        nxt = lax.while_loop(
            lambda u: (u < self.n_used) & (self.eid_ref[jnp.minimum(u, N_MTILES - 1)] == expert),
            lambda u: u + 1, t + 1)
        in_pass = nxt < self.n_used

        @pl.when(in_pass | (j + 1 < n_passes))
        def _():
            for cp in self._copies(self.eid_ref[jnp.where(in_pass, nxt, 0)], jnp.where(in_pass, j, j + 1),
                                   1 - slot):
                cp.start()

        self.cnt_sm[0] = self.cnt_sm[0] + 1
        return slot


def _on_live_rows(valid, out_ref, compute):
    lo = 0
    for rows in ROW_STEPS:
        @pl.when((valid > lo) & (valid <= rows) if lo else (valid <= rows))
        def _():
            out_ref[0:rows, :] = compute(rows).astype(out_ref.dtype)
            if rows < TM_MOE:
                out_ref[rows:TM_MOE, :] = jnp.zeros((TM_MOE - rows, out_ref.shape[1]), out_ref.dtype)

        lo = rows


def _moe_up_kernel(eid_ref, nused_ref, valid_ref, x_ref, wg_hbm, wu_hbm, bg_ref, bu_ref, h_ref,
                   wg_buf, wu_buf, wg_sc, wu_sc, sems, cnt_sm):
    j = pl.program_id(0)
    t = pl.program_id(1)
    n_used = nused_ref[0]
    weights = _ExpertWeights(eid_ref, n_used, (wg_hbm, wu_hbm), (wg_buf, wu_buf), sems, cnt_sm)

    @pl.when((j == 0) & (t == 0))
    def _():
        weights.prologue()

    @pl.when(t < n_used)
    def _():
        @pl.when((t == 0) | (eid_ref[t] != eid_ref[jnp.maximum(t - 1, 0)]))
        def _():
            slot = weights.acquire(j, t, pl.num_programs(0))
            wg_sc[...] = wg_buf[slot].astype(BF16)
            wu_sc[...] = wu_buf[slot].astype(BF16)

        def swiglu(rows):
            x = x_ref[0:rows, :]
            gate = jnp.minimum(_dot(x, wg_sc[...]) + bg_ref[...], SWIGLU_LIMIT)
            up = jnp.clip(_dot(x, wu_sc[...]) + bu_ref[...], -SWIGLU_LIMIT, SWIGLU_LIMIT)
            return gate * jax.nn.sigmoid(SWIGLU_ALPHA * gate) * (up + 1.0)

        _on_live_rows(valid_ref[t], h_ref, swiglu)

    @pl.when(t >= n_used)
    def _():
        h_ref[...] = jnp.zeros(h_ref.shape, BF16)


def _moe_down_kernel(eid_ref, nused_ref, valid_ref, h_ref, wd_hbm, bd_ref, y_ref, wd_buf, wd_sc, sems, cnt_sm):
    j = pl.program_id(0)
    t = pl.program_id(1)
    n_used = nused_ref[0]
    weights = _ExpertWeights(eid_ref, n_used, (wd_hbm,), (wd_buf,), sems, cnt_sm)

    @pl.when((j == 0) & (t == 0))
    def _():
        weights.prologue()

    @pl.when(t < n_used)
    def _():
        @pl.when((t == 0) | (eid_ref[t] != eid_ref[jnp.maximum(t - 1, 0)]))
        def _():
            slot = weights.acquire(j, t, pl.num_programs(0))
            wd_sc[...] = wd_buf[slot].astype(BF16)

        _on_live_rows(valid_ref[t], y_ref, lambda rows: _dot(h_ref[0:rows, :], wd_sc[...]) + bd_ref[...])

    @pl.when(t >= n_used)
    def _():
        y_ref[...] = jnp.zeros(y_ref.shape, F32)


def _row_tile(t, nused_ref):
    return jnp.minimum(t, jnp.maximum(nused_ref[0] - 1, 0))


def _moe_call(body, name, n_weights, k_dim, out_dim, tn, out_dtype, tile_eid, n_used, tile_valid, rows, weights,
              biases):
    any_spec = pl.BlockSpec(memory_space=pl.ANY)
    b_spec = pl.BlockSpec((None, 1, tn), lambda j, t, eid, nu, tv: (eid[t], 0, j))
    return pl.pallas_call(
        body,
        grid_spec=pltpu.PrefetchScalarGridSpec(
            num_scalar_prefetch=3,
            grid=(out_dim // tn, N_MTILES),
            in_specs=[pl.BlockSpec((TM_MOE, k_dim), lambda j, t, eid, nu, tv: (_row_tile(t, nu), 0))]
            + [any_spec] * n_weights + [b_spec] * n_weights,
            out_specs=pl.BlockSpec((TM_MOE, tn), lambda j, t, eid, nu, tv: (t, j)),
            scratch_shapes=[pltpu.VMEM((2, k_dim, tn), F32)] * n_weights
            + [pltpu.VMEM((k_dim, tn), BF16)] * n_weights
            + [pltpu.SemaphoreType.DMA((n_weights, 2)), pltpu.SMEM((1,), jnp.int32)]),
        out_shape=jax.ShapeDtypeStruct((P_ROWS, out_dim), out_dtype),
        compiler_params=_params("arbitrary", "arbitrary"),
        name=name,
    )(tile_eid, n_used, tile_valid, rows, *weights, *biases)


def _moe_up(tile_eid, n_used, tile_valid, xs, w_gate, w_up, b_gate, b_up):
    return _moe_call(_moe_up_kernel, "moe_up", 2, D_MODEL, D_FF, TN_UP, BF16, tile_eid, n_used, tile_valid, xs,
                     (w_gate, w_up), (b_gate, b_up))


def _moe_down(tile_eid, n_used, tile_valid, h, w_down, b_down):
    return _moe_call(_moe_down_kernel, "moe_down", 1, D_FF, D_MODEL, TN_DOWN, F32, tile_eid, n_used, tile_valid, h,
                     (w_down,), (b_down,))


def _final_kernel(dest_ref, y_hbm, x1_ref, wt_ref, g2_ref, ln2g_ref, ln2b_ref, o_ref, ybuf, sems):
    i = pl.program_id(0)
    n_tiles = pl.num_programs(0)

    def start_gather(tile, slot):
        base = tile * (TM_FIN * TOP_K)

        def body(r2, carry):
            a0 = base + r2 * DMA_BATCH
            rows = [dest_ref[a0 + i] for i in range(DMA_BATCH)]
            for i, d in enumerate(rows):
                r = r2 * (DMA_BATCH // TOP_K) + i // TOP_K
                pltpu.make_async_copy(y_hbm.at[pl.ds(d, 1), :], ybuf.at[slot, i % TOP_K, pl.ds(r, 1), :],
                                      sems.at[slot]).start(priority=i % 2)
            return carry

        lax.fori_loop(0, TM_FIN * TOP_K // DMA_BATCH, body, 0)

    @pl.when(i == 0)
    def _():
        start_gather(0, 0)

    @pl.when(i + 1 < n_tiles)
    def _():
        start_gather(i + 1, (i + 1) % 2)

    slot = i % 2
    pltpu.make_async_copy(ybuf.at[slot], ybuf.at[slot], sems.at[slot]).wait()
    wt = wt_ref[...]
    y = wt[:, 0:1] * ybuf[slot, 0]
    for k in range(1, TOP_K):
        y = y + wt[:, k:k + 1] * ybuf[slot, k]
    o_ref[...] = _ln(DN_ALPHA * x1_ref[...] + (1.0 + g2_ref[...]) * y) * ln2g_ref[...] + ln2b_ref[...]


def _final(dest_flat, y, x1, wt, mod3, ln2_g, ln2_b):
    tiles_per_batch = SEQ // TM_FIN
    full = lambda shape: pl.BlockSpec(shape, lambda i, dest: tuple(0 for _ in shape))
    row = lambda width: pl.BlockSpec((TM_FIN, width), lambda i, dest: (i, 0))
    return pl.pallas_call(
        _final_kernel,
        grid_spec=pltpu.PrefetchScalarGridSpec(
            num_scalar_prefetch=1,
            grid=(N_TOK // TM_FIN,),
            in_specs=[pl.BlockSpec(memory_space=pl.ANY), row(D_MODEL), row(LANE),
                      pl.BlockSpec((None, 1, D_MODEL), lambda i, dest: (i // tiles_per_batch, 0, 5)),
                      full(ln2_g.shape), full(ln2_b.shape)],
            out_specs=row(D_MODEL),
            scratch_shapes=[pltpu.VMEM((2, TOP_K, TM_FIN, D_MODEL), F32),
                            pltpu.SemaphoreType.DMA((2,))]),
        out_shape=jax.ShapeDtypeStruct((N_TOK, D_MODEL), F32),
        compiler_params=_params("arbitrary"),
        name="combine_ln2",
    )(dest_flat, y, x1, wt, mod3, ln2_g, ln2_b)


def _route(idx, rank, counts):
    experts = jnp.arange(N_EXPERTS, dtype=jnp.int32)
    padded = ((counts + TM_MOE - 1) // TM_MOE) * TM_MOE
    ends = jnp.sum(jnp.where(experts[None, :] <= experts[:, None], padded[None, :], 0), axis=1)
    starts = ends - padded
    dest = rank
    for e in range(N_EXPERTS):
        dest = dest + jnp.where(idx == e, starts[e], 0)
    dest = dest[:, :TOP_K]
    tile_start = jnp.arange(N_MTILES, dtype=jnp.int32) * TM_MOE
    tile_eid = jnp.minimum(jnp.sum((ends[None, :] <= tile_start[:, None]).astype(jnp.int32), axis=1),
                           N_EXPERTS - 1)
    n_used = (ends[-1] // TM_MOE).astype(jnp.int32).reshape(1)
    group_fill = starts + counts
    tile_fill = jnp.sum(jnp.where(tile_eid[:, None] == experts[None, :], group_fill[None, :], 0), axis=1)
    tile_valid = jnp.clip(tile_fill - tile_start, 0, TM_MOE)
    n_batches = (tile_valid + DMA_BATCH - 1) // DMA_BATCH
    return dest.reshape(-1), tile_eid, n_used, tile_valid, n_batches, group_fill


def kernel(x, c, w_ada, b_ada, w_in, w_gk, b_gk, gla_norm_g, pe_k, pe_v, w_ck1, b_ck1, w_ck2, b_ck2,
           w_cv1, b_cv1, w_cv2, b_cv2, w_o, ln1_g, ln1_b, w_router, b_router, w_gate, b_gate, w_up, b_up,
           w_down, b_down, ln2_g, ln2_b):
    l = 0
    xf = x.reshape(N_TOK, D_MODEL)
    row2 = lambda a: a.reshape(1, -1)

    c8 = jnp.pad(c, ((0, 8 - BATCH), (0, 0)))
    mod3 = _adaln(c8, w_ada[l], row2(b_ada[l]))[:BATCH].reshape(BATCH, 1, 6 * D_MODEL)

    wt = w_in[l].T
    glr0 = 3072
    nsa0 = glr0 + GLA_RANK
    ngt0 = nsa0 + 1024 + 6 * 256
    w_in_t = jnp.concatenate(
        [wt[:glr0], wt[nsa0:ngt0], wt[glr0:nsa0], wt[ngt0:],
         jnp.zeros((D_IN_PAD - wt.shape[0], D_MODEL), F32)], axis=0).astype(BF16)
    proj = _inproj(xf, mod3, w_in_t)

    w_gk_pad = jnp.pad(w_gk[l], ((0, LANE - GLA_RANK), (0, 0)))
    y_gla = _gla(proj, w_gk_pad, row2(b_gk[l]), row2(gla_norm_g[l]))

    cos_t, sin_t = _rope_tables(jnp.arange(SEQ))
    cmp_end = jnp.arange(N_CMP_PAD) * CMP_STRIDE + (CMP_BLOCK - 1)
    cos_c, sin_c = _rope_tables(cmp_end)
    q_r, ks, vs, kw, vw = _prep(proj, cos_t, sin_t)
    kc = _compress(proj, 0, pe_k[l], w_ck1[l], row2(b_ck1[l]), w_ck2[l], row2(b_ck2[l]), cos_c, sin_c, True)
    vc = _compress(proj, 1, pe_v[l], w_cv1[l], row2(b_cv1[l]), w_cv2[l], row2(b_cv2[l]), cos_c, sin_c, False)
    y_nsa = _nsa(q_r, kc, vc, ks, vs, kw, vw, proj)

    wr = jnp.pad(w_router[l], ((0, 0), (0, LANE - N_EXPERTS)))
    wr_hi, wr_lo = _split_bf16(wr)
    br = jnp.concatenate([b_router[l], jnp.full((LANE - N_EXPERTS,), NEG, F32)]).reshape(1, LANE)
    x1, h2, idx128, wt128, rank128, cnt8 = _outproj(y_gla, y_nsa, xf, w_o[l].astype(BF16), mod3,
                                                    row2(ln1_g[l]), row2(ln1_b[l]),
                                                    jnp.concatenate([wr_hi, wr_lo], axis=1), br)

    counts = cnt8[0, :N_EXPERTS].astype(jnp.int32)
    dest, tile_eid, n_used, tile_valid, n_batches, group_fill = _route(idx128, rank128, counts)
    xs = _dispatch(dest, group_fill, n_used, n_batches, h2)
    h = _moe_up(tile_eid, n_used, tile_valid, xs, w_gate[l], w_up[l],
                b_gate[l].reshape(N_EXPERTS, 1, D_FF), b_up[l].reshape(N_EXPERTS, 1, D_FF))
    y = _moe_down(tile_eid, n_used, tile_valid, h, w_down[l], b_down[l].reshape(N_EXPERTS, 1, D_MODEL))

    out = _final(dest, y, x1, wt128, mod3, row2(ln2_g[l]), row2(ln2_b[l]))
    return out.reshape(BATCH, SEQ, D_MODEL)
```

```python
import functools

import numpy as np
import jax
import jax.numpy as jnp
from jax import lax
from jax.experimental import pallas as pl
from jax.experimental.pallas import tpu as pltpu

F32 = jnp.float32
BF16 = jnp.bfloat16

D_MODEL = 2048
BATCH = 2
SEQ = 4096
N_TOK = BATCH * SEQ

GLA_HEADS = 4
GLA_DK = 128
GLA_DV = 256
GLA_RANK = 16
GLA_TAU = 16.0
GLA_CHUNK = 64

NSA_DH = 128
NSA_HEADS = 8
NSA_GROUPS = 2
NSA_HPG = 4
CMP_STRIDE = 16
CMP_BLOCK = 32
CMP_HIDDEN = 256
N_CMP_PAD = SEQ // CMP_STRIDE
SEL_BLOCK = 64
N_BLK = SEQ // SEL_BLOCK
SEL_TOPK = 16
WINDOW = 512
ROPE_DIM = 32
ROPE_THETA = 500000.0
Q_SCALE_LOG2 = NSA_DH ** -0.5 * 1.4426950408889634

N_EXPERTS = 32
TOP_K = 4
D_FF = D_MODEL
SWIGLU_LIMIT = 7.0
SWIGLU_ALPHA = 1.702
DN_ALPHA = 2.0 ** 0.25
LN_EPS = 1e-5

COL_GQ, COL_GK, COL_GV, COL_GR, COL_NQ = 0, 512, 1024, 2048, 3072
COL_KV = 4096
COL_TAIL = 5632
D_IN_PAD = 6144
GATE_LANE0 = GLA_RANK

LANE = 128
ROW_CHUNKS = D_MODEL // LANE
BUF_PITCH = ROW_CHUNKS + 8
NEG = -1e30
VMEM_LIMIT = 56 * 1024 * 1024

TM_IN, TN_IN = 1024, 1024
TT_GLA = 512
TR_PREP = 512
TQ_NSA = 256
TK_SEL = 512
WIN_SPAN = WINDOW + TQ_NSA
ONES_ROWS = 16
TM_OUT = 256
OUT_CHAINS = 2
TM_MOE = 512
ROW_STEPS = tuple(range(64, TM_MOE + 1, 64))
TN_UP, TN_DOWN = 1024, 2048
P_ROWS = N_TOK * TOP_K + N_EXPERTS * TM_MOE
N_MTILES = P_ROWS // TM_MOE
TM_FIN = 256
DMA_BATCH = 8


def _dot(a, b):
    return jnp.dot(a, b, preferred_element_type=F32)


def _dot_nt(a, b):
    return lax.dot_general(a, b, (((1,), (1,)), ((), ())), preferred_element_type=F32)


def _dot_tn(a, b):
    return lax.dot_general(a, b, (((0,), (0,)), ((), ())), preferred_element_type=F32)


def _ln(x):
    xc = x - jnp.mean(x, -1, keepdims=True)
    return xc * lax.rsqrt(jnp.mean(xc * xc, -1, keepdims=True) + LN_EPS)


def _split_bf16(x):
    hi = x.astype(BF16)
    lo = (x - hi.astype(F32)).astype(BF16)
    return hi, lo


def _params(*sem):
    return pltpu.CompilerParams(dimension_semantics=sem, vmem_limit_bytes=VMEM_LIMIT)


def _adaln_kernel(c_ref, w_ref, b_ref, o_ref):
    c = c_ref[...]
    a = (c * jax.nn.sigmoid(c)).astype(BF16)
    o_ref[...] = _dot(a, w_ref[...].astype(BF16)) + b_ref[...]


def _adaln(c8, w, b):
    n = w.shape[1]
    tn = 1024
    return pl.pallas_call(
        _adaln_kernel,
        grid=(n // tn,),
        in_specs=[pl.BlockSpec((8, D_MODEL), lambda j: (0, 0)),
                  pl.BlockSpec((D_MODEL, tn), lambda j: (0, j)),
                  pl.BlockSpec((1, tn), lambda j: (0, j))],
        out_specs=pl.BlockSpec((8, tn), lambda j: (0, j)),
        out_shape=jax.ShapeDtypeStruct((8, n), F32),
        compiler_params=_params("arbitrary"),
        name="adaln",
    )(c8, w, b)


def _inproj_kernel(x_ref, sh_ref, sc_ref, w_ref, o_ref, h_sc):
    @pl.when(pl.program_id(1) == 0)
    def _():
        h = _ln(x_ref[...]) * (1.0 + sc_ref[...]) + sh_ref[...]
        h_sc[...] = h.astype(BF16)

    o_ref[...] = _dot_nt(h_sc[...], w_ref[...])


def _inproj(xf, mod3, w_in_t):
    tiles_per_batch = SEQ // TM_IN
    return pl.pallas_call(
        _inproj_kernel,
        grid=(N_TOK // TM_IN, D_IN_PAD // TN_IN),
        in_specs=[pl.BlockSpec((TM_IN, D_MODEL), lambda i, j: (i, 0)),
                  pl.BlockSpec((None, 1, D_MODEL), lambda i, j: (i // tiles_per_batch, 0, 0)),
                  pl.BlockSpec((None, 1, D_MODEL), lambda i, j: (i // tiles_per_batch, 0, 1)),
                  pl.BlockSpec((TN_IN, D_MODEL), lambda i, j: (j, 0))],
        out_specs=pl.BlockSpec((TM_IN, TN_IN), lambda i, j: (i, j)),
        out_shape=jax.ShapeDtypeStruct((N_TOK, D_IN_PAD), F32),
        scratch_shapes=[pltpu.VMEM((TM_IN, D_MODEL), BF16)],
        compiler_params=_params("parallel", "arbitrary"),
        name="inproj",
    )(xf, mod3, mod3, w_in_t)


GLA_HALVES = (32, 16, 8, 4, 2, 1)
N_LEVELS = len(GLA_HALVES)
ROW_EB = 2 * N_LEVELS
ROW_EL = 2 * N_LEVELS + 1
N_EVIEWS = 2 * N_LEVELS + 2


def _gla_constants():
    c = GLA_CHUNK
    t = np.arange(c)[:, None]
    r = np.arange(c)[None, :]
    mall = np.zeros((N_EVIEWS, c, c), np.float32)
    valid = np.zeros((N_EVIEWS, c, LANE), np.float32)
    masks = np.zeros((N_LEVELS + 1, c, c), np.float32)
    for li, n in enumerate(GLA_HALVES):
        same = (t // (2 * n)) == (r // (2 * n))
        t_up = (t % (2 * n)) >= n
        r_up = (r % (2 * n)) >= n
        mall[2 * li] = same & t_up & r_up & (r <= t)
        mall[2 * li + 1] = same & ~t_up & ~r_up & (r > t)
        valid[2 * li] = np.broadcast_to(t_up, (c, LANE))
        valid[2 * li + 1] = np.broadcast_to(~t_up, (c, LANE))
        masks[li] = same & t_up & ~r_up
    mall[ROW_EB] = r <= t
    mall[ROW_EL] = r > t
    valid[ROW_EB] = 1.0
    valid[ROW_EL] = 1.0
    masks[N_LEVELS] = np.eye(c)
    return (mall.reshape(N_EVIEWS * c, c), valid.reshape(N_EVIEWS * c, LANE), masks)


def _gla_kernel(q_ref, k_ref, v_ref, r_ref, glr_ref, wgk_ref, bgk_ref, g_ref, mall_ref, valid_ref,
                masks_ref, o_ref, st_sc):
    c = GLA_CHUNK

    @pl.when(pl.program_id(1) == 0)
    def _():
        st_sc[...] = jnp.zeros_like(st_sc)

    z = _dot(glr_ref[...].astype(BF16), wgk_ref[...].astype(BF16)) + bgk_ref[...]
    log_a = (jnp.minimum(z, 0.0) - jnp.log1p(jnp.exp(-jnp.abs(z)))) * (1.0 / GLA_TAU)
    mall = mall_ref[...]
    valid = valid_ref[...]
    for ci in range(TT_GLA // c):
        rows = slice(ci * c, (ci + 1) * c)
        la_hi, la_lo = _split_bf16(log_a[rows])
        e_all = jnp.exp(_dot(mall, la_hi) + _dot(mall, la_lo))
        for h in range(GLA_HEADS):
            kcols = slice(h * GLA_DK, (h + 1) * GLA_DK)
            vcols = slice(h * GLA_DV, (h + 1) * GLA_DV)
            e = e_all[:, kcols] * valid
            q = q_ref[rows, kcols] * (GLA_DK ** -0.5)
            k = k_ref[rows, kcols]
            vb = v_ref[rows, vcols].astype(BF16)
            att = masks_ref[N_LEVELS] * _dot_nt(q.astype(BF16), k.astype(BF16))
            for li in range(N_LEVELS):
                eq = e[(2 * li) * c:(2 * li + 1) * c]
                ek = e[(2 * li + 1) * c:(2 * li + 2) * c]
                att = att + masks_ref[li] * _dot_nt((q * eq).astype(BF16), (k * ek).astype(BF16))
            eb = e[ROW_EB * c:(ROW_EB + 1) * c]
            el = e[ROW_EL * c:(ROW_EL + 1) * c]
            st = st_sc[h]
            o = _dot_nt((q * eb).astype(BF16), st.astype(BF16)) + _dot(att.astype(BF16), vb)
            st_sc[h] = st * eb[c - 1:c, :] + _dot_tn(vb, (k * el).astype(BF16))
            o = o * lax.rsqrt(jnp.mean(o * o, -1, keepdims=True) + LN_EPS) * g_ref[...]
            r = r_ref[rows, vcols]
            o_ref[rows, vcols] = (o * (r * jax.nn.sigmoid(r))).astype(BF16)


def _gla(proj, w_gk_pad, b_gk, norm_g):
    nt = SEQ // TT_GLA
    mall, valid, masks = _gla_constants()
    kw = GLA_HEADS * GLA_DK
    vw = GLA_HEADS * GLA_DV
    full = lambda shape: pl.BlockSpec(shape, lambda b, i: tuple(0 for _ in shape))
    cols = lambda width, col0: pl.BlockSpec((TT_GLA, width), lambda b, i: (b * nt + i, col0 // width))
    return pl.pallas_call(
        _gla_kernel,
        grid=(BATCH, nt),
        in_specs=[cols(kw, COL_GQ), cols(kw, COL_GK), cols(vw, COL_GV), cols(vw, COL_GR),
                  cols(LANE, COL_TAIL), full(w_gk_pad.shape), full(b_gk.shape), full(norm_g.shape),
                  full(mall.shape), full(valid.shape), full(masks.shape)],
        out_specs=cols(vw, 0),
        out_shape=jax.ShapeDtypeStruct((N_TOK, vw), BF16),
        scratch_shapes=[pltpu.VMEM((GLA_HEADS, GLA_DV, GLA_DK), F32)],
        compiler_params=_params("parallel", "arbitrary"),
        name="gla",
    )(proj, proj, proj, proj, proj, w_gk_pad, b_gk, norm_g,
      jnp.asarray(mall, BF16), jnp.asarray(valid), jnp.asarray(masks))


def _rope_tables(pos):
    half = ROPE_DIM // 2
    inv_freq = ROPE_THETA ** (-jnp.arange(half, dtype=F32) * (2.0 / ROPE_DIM))
    ang = pos.astype(F32)[:, None] * inv_freq
    cos, sin = jnp.cos(ang), jnp.sin(ang)
    n = pos.shape[0]
    cosf = jnp.concatenate([cos, cos, jnp.ones((n, LANE - ROPE_DIM), F32)], -1)
    sinf = jnp.concatenate([-sin, sin, jnp.zeros((n, LANE - ROPE_DIM), F32)], -1)
    return cosf, sinf


def _rope(x, cosf, sinf):
    lane = lax.broadcasted_iota(jnp.int32, x.shape, 1)
    half = ROPE_DIM // 2
    swapped = jnp.where(lane < half, pltpu.roll(x, LANE - half, 1), pltpu.roll(x, half, 1))
    return x * cosf + swapped * sinf


def _prep_kernel(q_ref, ks_ref, vs_ref, kw_ref, vw_ref, cos_ref, sin_ref,
                 qo_ref, kso_ref, vso_ref, kwo_ref, vwo_ref):
    cosf = cos_ref[...]
    sinf = sin_ref[...]
    for hh in range(NSA_HEADS):
        cols = slice(hh * NSA_DH, (hh + 1) * NSA_DH)
        qo_ref[:, cols] = (_rope(q_ref[:, cols], cosf, sinf) * Q_SCALE_LOG2).astype(BF16)
    for g in range(NSA_GROUPS):
        cols = slice(g * NSA_DH, (g + 1) * NSA_DH)
        kso_ref[g] = _rope(ks_ref[:, cols], cosf, sinf).astype(BF16)
        kwo_ref[g] = _rope(kw_ref[:, cols], cosf, sinf).astype(BF16)
        vso_ref[g] = vs_ref[:, cols].T.astype(BF16)
        vwo_ref[g] = vw_ref[:, cols].T.astype(BF16)


def _prep(proj, cosf, sinf):
    nt = SEQ // TR_PREP
    kvw = NSA_GROUPS * NSA_DH
    kv_in = lambda which: pl.BlockSpec((TR_PREP, kvw), lambda b, i: (b * nt + i, COL_KV // kvw + which))
    k_out = pl.BlockSpec((None, NSA_GROUPS, TR_PREP, NSA_DH), lambda b, i: (b, 0, i, 0))
    k_shape = jax.ShapeDtypeStruct((BATCH, NSA_GROUPS, SEQ, NSA_DH), BF16)
    vt_out = pl.BlockSpec((None, NSA_GROUPS, NSA_DH, TR_PREP), lambda b, i: (b, 0, 0, i))
    vt_shape = jax.ShapeDtypeStruct((BATCH, NSA_GROUPS, NSA_DH, SEQ), BF16)
    qw = NSA_HEADS * NSA_DH
    return pl.pallas_call(
        _prep_kernel,
        grid=(BATCH, nt),
        in_specs=[pl.BlockSpec((TR_PREP, qw), lambda b, i: (b * nt + i, COL_NQ // qw)),
                  kv_in(2), kv_in(3), kv_in(4), kv_in(5),
                  pl.BlockSpec((TR_PREP, LANE), lambda b, i: (i, 0)),
                  pl.BlockSpec((TR_PREP, LANE), lambda b, i: (i, 0))],
        out_specs=[pl.BlockSpec((TR_PREP, qw), lambda b, i: (b * nt + i, 0)),
                   k_out, vt_out, k_out, vt_out],
        out_shape=[jax.ShapeDtypeStruct((N_TOK, qw), BF16), k_shape, vt_shape, k_shape, vt_shape],
        compiler_params=_params("parallel", "parallel"),
        name="nsa_prep",
    )(proj, proj, proj, proj, proj, cosf, sinf)


def _compress_kernel(a_ref, pe_ref, w1_ref, b1_ref, w2_ref, b2_ref, cos_ref, sin_ref, o_ref, bot_sc,
                     *, rope):
    n = N_CMP_PAD
    top = jnp.zeros((n, CMP_HIDDEN), F32)
    bot = jnp.zeros((n, CMP_HIDDEN), F32)
    for p in range(CMP_STRIDE):
        ap = a_ref[pl.ds(p, n, stride=CMP_STRIDE), :]
        w_top = w1_ref[p * NSA_DH:(p + 1) * NSA_DH, :].astype(BF16)
        w_bot = w1_ref[(CMP_STRIDE + p) * NSA_DH:(CMP_STRIDE + p + 1) * NSA_DH, :].astype(BF16)
        top = top + _dot((ap + pe_ref[p:p + 1, :]).astype(BF16), w_top)
        bot = bot + _dot((ap + pe_ref[CMP_STRIDE + p:CMP_STRIDE + p + 1, :]).astype(BF16), w_bot)
    bot_sc[0:n, :] = bot
    bot_sc[n:n + 8, :] = jnp.zeros((8, CMP_HIDDEN), F32)
    h = top + bot_sc[1:n + 1, :] + b1_ref[...]
    h = h * jax.nn.sigmoid(h)
    out = _dot(h.astype(BF16), w2_ref[...].astype(BF16)) + b2_ref[...]
    if rope:
        out = _rope(out, cos_ref[...], sin_ref[...])
    row = lax.broadcasted_iota(jnp.int32, out.shape, 0)
    out = jnp.where(row < n - 1, out, 0.0)
    o_ref[...] = (out if rope else out.T).astype(BF16)


def _compress(proj, which, pe, w1, b1, w2, b2, cosf, sinf, rope):
    col0 = (COL_KV + which * NSA_GROUPS * NSA_DH) // NSA_DH
    out_dims = (N_CMP_PAD, NSA_DH) if rope else (NSA_DH, N_CMP_PAD)
    full = lambda shape: pl.BlockSpec(shape, lambda b, g: tuple(0 for _ in shape))
    return pl.pallas_call(
        functools.partial(_compress_kernel, rope=rope),
        grid=(BATCH, NSA_GROUPS),
        in_specs=[pl.BlockSpec((SEQ, NSA_DH), lambda b, g: (b, col0 + g)),
                  full(pe.shape), full(w1.shape), full(b1.shape), full(w2.shape), full(b2.shape),
                  full(cosf.shape), full(sinf.shape)],
        out_specs=pl.BlockSpec((None, None) + out_dims, lambda b, g: (b, g, 0, 0)),
        out_shape=jax.ShapeDtypeStruct((BATCH, NSA_GROUPS) + out_dims, BF16),
        scratch_shapes=[pltpu.VMEM((N_CMP_PAD + 8, CMP_HIDDEN), F32)],
        compiler_params=_params("parallel", "parallel"),
        name="nsa_compress",
    )(proj, pe, w1, b1, w2, b2, cosf, sinf)


def _nsa_constants():
    c_start = np.arange(N_CMP_PAD) * CMP_STRIDE
    b_start = np.arange(N_BLK) * SEL_BLOCK
    overlap_t = ((c_start[None, :] < b_start[:, None] + SEL_BLOCK)
                 & (c_start[None, :] + CMP_BLOCK > b_start[:, None])).astype(np.float32)
    overlap_t[:, N_CMP_PAD - 1] = 0.0
    expand_t = (np.arange(SEQ)[:, None] // SEL_BLOCK == np.arange(LANE)[None, :]).astype(np.float32)
    return overlap_t, expand_t


def _tile_heads(a):
    return jnp.concatenate([a] * NSA_HPG, axis=1)


def _nsa_kernel(q_ref, kc_ref, vc_ref, ks_ref, vs_ref, kw_ref, vw_ref, gate_ref, ovt_ref, expand_ref,
                o_ref, score_sc, m_sc, acc_sc):
    tq = TQ_NSA
    g = pl.program_id(1)
    q0 = pl.program_id(2) * tq
    qs = jnp.concatenate([q_ref[:, hh * NSA_DH:(hh + 1) * NSA_DH] for hh in range(NSA_HPG)], axis=0)
    pos_q = q0 + lax.broadcasted_iota(jnp.int32, (1, tq), 1)

    def with_ones(vt):
        return jnp.concatenate([vt, jnp.ones((ONES_ROWS, vt.shape[1]), BF16)], axis=0)

    def normalise(acc):
        return acc[0:NSA_DH] / acc[NSA_DH:NSA_DH + 1]

    s = _dot_nt(kc_ref[...], qs)
    cmp_end = lax.broadcasted_iota(jnp.int32, (N_CMP_PAD, tq), 0) * CMP_STRIDE + (CMP_BLOCK - 1)
    s = s + _tile_heads(jnp.where(cmp_end <= pos_q, 0.0, NEG))
    m = jnp.max(s, 0, keepdims=True)
    e = jnp.where(s > 0.5 * NEG, jnp.exp2(s - m), 0.0)
    p = e / jnp.maximum(jnp.sum(e, 0, keepdims=True), 1e-30)
    o_cmp = _dot(vc_ref[...], p.astype(BF16))
    p_sum = p[:, 0:tq] + p[:, tq:2 * tq] + p[:, 2 * tq:3 * tq] + p[:, 3 * tq:4 * tq]
    ps_hi, ps_lo = _split_bf16(p_sum)
    ovt = ovt_ref[...]
    p_blk_t = _dot(ovt, ps_hi) + _dot(ovt, ps_lo)

    jj = lax.broadcasted_iota(jnp.int32, (N_BLK, tq), 0)
    cur = (q0 + lax.broadcasted_iota(jnp.int32, (N_BLK, tq), 1)) // SEL_BLOCK
    forced = (jj == 0) | (jj == cur) | (jj == cur - 1)
    allowed = jj <= cur
    score = jnp.where(forced, 3.0e38, jnp.where(allowed, p_blk_t, -1.0))
    score_sc[...] = score

    def rank_body(i, rank):
        row = score_sc[pl.ds(i, 1), :]
        first = jnp.where(jj > i, 1.0, 0.0)
        return rank + jnp.where(row > score, 1.0, jnp.where(row == score, first, 0.0))

    n_live = (q0 + tq - 1) // SEL_BLOCK + 1
    rank = lax.fori_loop(0, n_live, rank_body, jnp.zeros((N_BLK, tq), F32))
    sel_t = jnp.where(allowed, jnp.where(rank < SEL_TOPK, 1.0, 0.0), 0.0)
    sel = jnp.concatenate([sel_t, jnp.zeros((LANE - N_BLK, tq), F32)], axis=0).astype(BF16)

    m_sc[...] = jnp.full(m_sc.shape, NEG, F32)
    acc_sc[...] = jnp.zeros(acc_sc.shape, F32)

    def sel_body(kt, carry):
        k0 = pl.multiple_of(kt * TK_SEL, TK_SEL)
        s = _dot_nt(ks_ref[pl.ds(k0, TK_SEL), :], qs)
        sel_x = _dot(expand_ref[pl.ds(k0, TK_SEL), :], sel)
        kpos = k0 + lax.broadcasted_iota(jnp.int32, (TK_SEL, tq), 0)
        bias = jnp.where(kpos <= pos_q, jnp.where(sel_x > 0.5, 0.0, NEG), NEG)
        s = s + _tile_heads(bias)
        m_prev = m_sc[...]
        m_new = jnp.maximum(m_prev, jnp.max(s, 0, keepdims=True))
        alpha = jnp.exp2(m_prev - m_new)
        p = jnp.exp2(s - m_new).astype(BF16)
        acc_sc[...] = alpha * acc_sc[...] + _dot(with_ones(vs_ref[:, pl.ds(k0, TK_SEL)]), p)
        m_sc[...] = m_new
        return carry

    lax.fori_loop(0, (q0 + tq - 1) // TK_SEL + 1, sel_body, 0)
    o_slc = normalise(acc_sc[...])

    start = pl.multiple_of(jnp.maximum(q0 - WINDOW, 0), LANE)
    s = _dot_nt(kw_ref[pl.ds(start, WIN_SPAN), :], qs)
    dist = pos_q - (start + lax.broadcasted_iota(jnp.int32, (WIN_SPAN, tq), 0))
    bias = jnp.where(dist >= 0, jnp.where(dist < WINDOW, 0.0, NEG), NEG)
    s = s + _tile_heads(bias)
    p = jnp.exp2(s - jnp.max(s, 0, keepdims=True)).astype(BF16)
    o_win = normalise(_dot(with_ones(vw_ref[:, pl.ds(start, WIN_SPAN)]), p))

    gates = jax.nn.sigmoid(gate_ref[...]).T

    def gate(hh, branch):
        lane0 = GATE_LANE0 + hh * 3 + branch
        lane1 = lane0 + NSA_HPG * 3
        return jnp.where(g == 0, gates[lane0:lane0 + 1, :], gates[lane1:lane1 + 1, :])

    for hh in range(NSA_HPG):
        c = slice(hh * tq, (hh + 1) * tq)
        o = gate(hh, 0) * o_cmp[:, c] + gate(hh, 1) * o_slc[:, c] + gate(hh, 2) * o_win[:, c]
        o_ref[:, hh * NSA_DH:(hh + 1) * NSA_DH] = o.T.astype(BF16)


def _nsa(q_r, kc, vc, ks, vs, kw, vw, proj):
    nq = SEQ // TQ_NSA
    gw = NSA_HPG * NSA_DH
    overlap_t, expand = _nsa_constants()
    cols = NSA_HPG * TQ_NSA
    per_group = lambda d0, d1: pl.BlockSpec((None, None, d0, d1), lambda b, g, i: (b, g, 0, 0))
    return pl.pallas_call(
        _nsa_kernel,
        grid=(BATCH, NSA_GROUPS, nq),
        in_specs=[pl.BlockSpec((TQ_NSA, gw), lambda b, g, i: (b * nq + i, g)),
                  per_group(N_CMP_PAD, NSA_DH), per_group(NSA_DH, N_CMP_PAD),
                  per_group(SEQ, NSA_DH), per_group(NSA_DH, SEQ),
                  per_group(SEQ, NSA_DH), per_group(NSA_DH, SEQ),
                  pl.BlockSpec((TQ_NSA, LANE), lambda b, g, i: (b * nq + i, COL_TAIL // LANE)),
                  pl.BlockSpec(overlap_t.shape, lambda b, g, i: (0, 0)),
                  pl.BlockSpec(expand.shape, lambda b, g, i: (0, 0))],
        out_specs=pl.BlockSpec((TQ_NSA, gw), lambda b, g, i: (b * nq + i, g)),
        out_shape=jax.ShapeDtypeStruct((N_TOK, NSA_GROUPS * gw), BF16),
        scratch_shapes=[pltpu.VMEM((N_BLK, TQ_NSA), F32),
                        pltpu.VMEM((1, cols), F32),
                        pltpu.VMEM((NSA_DH + ONES_ROWS, cols), F32)],
        compiler_params=_params("parallel", "parallel", "arbitrary"),
        name="nsa_attn",
    )(q_r, kc, vc, ks, vs, kw, vw, proj, jnp.asarray(overlap_t, BF16), jnp.asarray(expand, BF16))


def _outproj_kernel(yg_ref, yn_ref, x_ref, wo_ref, g1_ref, sc2_ref, sh2_ref, ln1g_ref, ln1b_ref,
                    wr_ref, br_ref, ltri_ref, x1_ref, h2_ref, idx_ref, wt_ref, rank_ref,
                    cnt_ref, base_sc):
    half = D_MODEL // 2

    @pl.when(pl.program_id(0) == 0)
    def _():
        base_sc[...] = jnp.zeros_like(base_sc)

    mix_all = _dot(yg_ref[...], wo_ref[0:half, :]) + _dot(yn_ref[...], wo_ref[half:D_MODEL, :])
    wr = wr_ref[...]
    sub = TM_OUT // OUT_CHAINS
    lane = lax.broadcasted_iota(jnp.int32, (TM_OUT, LANE), 1)
    lane_sub = lax.broadcasted_iota(jnp.int32, (sub, LANE), 1)
    lane_f = lane_sub.astype(F32)
    onehot_parts, h2_parts = [], []
    for ci in range(OUT_CHAINS):
        r = slice(ci * sub, (ci + 1) * sub)
        x1 = _ln(DN_ALPHA * x_ref[r, :] + (1.0 + g1_ref[...]) * mix_all[r]) * ln1g_ref[...] + ln1b_ref[...]
        x1_ref[r, :] = x1
        h2 = _ln(x1) * (1.0 + sc2_ref[...]) + sh2_ref[...]
        h_hi, h_lo = _split_bf16(h2)
        h2_parts.append(h2)
        lg = _dot(h_hi, wr) + _dot(h_lo, wr)
        logits = lg[:, 0:LANE] + lg[:, LANE:2 * LANE] + br_ref[...]

        vals = logits
        idx_out = jnp.zeros(logits.shape, jnp.int32)
        exp_out = jnp.zeros(logits.shape, F32)
        denom = jnp.zeros((sub, 1), F32)
        m0 = None
        onehots = []
        for k in range(TOP_K):
            mk = jnp.max(vals, -1, keepdims=True)
            ik = jnp.min(jnp.where(vals == mk, lane_f, float(LANE)), -1, keepdims=True)
            if k == 0:
                m0 = mk
            ek = jnp.exp(mk - m0)
            denom = denom + ek
            idx_out = jnp.where(lane_sub == k, ik.astype(jnp.int32), idx_out)
            exp_out = jnp.where(lane_sub == k, ek, exp_out)
            hit = lane_f == ik
            onehots.append(jnp.where(hit, 1.0, 0.0))
            vals = jnp.where(hit, -3.0e38, vals)
        idx_ref[r, :] = idx_out
        wt_ref[r, :] = exp_out / denom
        onehot_parts.append(onehots)
    onehots = [jnp.concatenate([part[k] for part in onehot_parts], axis=0) for k in range(TOP_K)]
    h2_all = jnp.concatenate(h2_parts, axis=0)
    for s in range(ROW_CHUNKS):
        h2_ref[pl.ds(s, TM_OUT, stride=ROW_CHUNKS), :] = h2_all[:, s * LANE:(s + 1) * LANE]

    cnt = onehots[0] + onehots[1] + onehots[2] + onehots[3]
    base = base_sc[0:1, :]
    before = _dot(ltri_ref[...], cnt.astype(BF16)) + base
    rank_out = jnp.zeros((TM_OUT, LANE), jnp.int32)
    for k in range(TOP_K):
        rk = jnp.sum(onehots[k] * before, -1, keepdims=True)
        rank_out = jnp.where(lane == k, rk.astype(jnp.int32), rank_out)
    rank_ref[...] = rank_out
    base_sc[...] = jnp.broadcast_to(base + jnp.sum(cnt, 0, keepdims=True), base_sc.shape)
    cnt_ref[...] = base_sc[...]


def _outproj(y_gla, y_nsa, xf, w_o, mod3, ln1_g, ln1_b, wr, br):
    tiles_per_batch = SEQ // TM_OUT
    half = D_MODEL // 2
    mod_spec = lambda chunk: pl.BlockSpec((None, 1, D_MODEL), lambda i: (i // tiles_per_batch, 0, chunk))
    full = lambda shape: pl.BlockSpec(shape, lambda i: tuple(0 for _ in shape))
    row = lambda width: pl.BlockSpec((TM_OUT, width), lambda i: (i, 0))
    ltri = jnp.asarray(np.tril(np.ones((TM_OUT, TM_OUT), np.float32), -1), BF16)
    return pl.pallas_call(
        _outproj_kernel,
        grid=(N_TOK // TM_OUT,),
        in_specs=[row(half), row(half), row(D_MODEL), full(w_o.shape),
                  mod_spec(2), mod_spec(4), mod_spec(3),
                  full(ln1_g.shape), full(ln1_b.shape), full(wr.shape), full(br.shape),
                  full(ltri.shape)],
        out_specs=[row(D_MODEL), pl.BlockSpec((TM_OUT * ROW_CHUNKS, LANE), lambda i: (i, 0)),
                   row(LANE), row(LANE), row(LANE), full((8, LANE))],
        out_shape=[jax.ShapeDtypeStruct((N_TOK, D_MODEL), F32),
                   jax.ShapeDtypeStruct((N_TOK * ROW_CHUNKS, LANE), F32),
                   jax.ShapeDtypeStruct((N_TOK, LANE), jnp.int32),
                   jax.ShapeDtypeStruct((N_TOK, LANE), F32),
                   jax.ShapeDtypeStruct((N_TOK, LANE), jnp.int32),
                   jax.ShapeDtypeStruct((8, LANE), F32)],
        scratch_shapes=[pltpu.VMEM((8, LANE), F32)],
        compiler_params=_params("arbitrary"),
        name="outproj_router",
    )(y_gla, y_nsa, xf, w_o, mod3, mod3, mod3, ln1_g, ln1_b, wr, br, ltri)


def _dispatch_kernel(dest_ref, fill_ref, nused_ref, nbatch_ref, h2_hbm, xs_ref, src_sm, buf, sems):
    t = pl.program_id(0)
    n_used = nused_ref[0]

    @pl.when(t == 0)
    def _():
        def pad(e, carry):
            for i in range(DMA_BATCH):
                src_sm[jnp.minimum(fill_ref[e] + i, P_ROWS - 1)] = 0
            return carry

        lax.fori_loop(0, N_EXPERTS, pad, 0)

        def scatter(tok2, carry):
            a0 = tok2 * (2 * TOP_K)
            rows = [dest_ref[a0 + i] for i in range(2 * TOP_K)]
            for i, row in enumerate(rows):
                src_sm[row] = tok2 * 2 + i // TOP_K
            return carry

        lax.fori_loop(0, N_TOK // 2, scatter, 0)

    def start_gather(tile, slot):
        base = tile * TM_MOE

        def body(r8, carry):
            r0 = r8 * DMA_BATCH
            toks = [src_sm[base + r0 + i] for i in range(DMA_BATCH)]
            for i, tok in enumerate(toks):
                pltpu.make_async_copy(h2_hbm.at[pl.ds(pl.multiple_of(tok * ROW_CHUNKS, ROW_CHUNKS), ROW_CHUNKS), :],
                                      buf.at[slot, pl.ds(pl.multiple_of((r0 + i) * BUF_PITCH, 8), ROW_CHUNKS), :],
                                      sems.at[slot]).start(priority=i % 2)
            return carry

        lax.fori_loop(0, nbatch_ref[tile], body, 0)

    @pl.when(t == 0)
    def _():
        buf[...] = jnp.zeros(buf.shape, F32)
        start_gather(0, 0)

    @pl.when(t + 1 < n_used)
    def _():
        start_gather(t + 1, (t + 1) % 2)

    @pl.when(t < n_used)
    def _():
        slot = t % 2
        batch = buf.at[slot, pl.ds(0, DMA_BATCH * ROW_CHUNKS), :]

        def wait_batch(b, carry):
            pltpu.make_async_copy(batch, batch, sems.at[slot]).wait()
            return carry

        lax.fori_loop(0, nbatch_ref[t], wait_batch, 0)
        for s in range(ROW_CHUNKS):
            xs_ref[:, s * LANE:(s + 1) * LANE] = buf[slot, pl.ds(s, TM_MOE, stride=BUF_PITCH), :].astype(BF16)

    @pl.when(t >= n_used)
    def _():
        xs_ref[...] = jnp.zeros(xs_ref.shape, BF16)


def _dispatch(dest_flat, group_fill, n_used, n_batches, h2):
    return pl.pallas_call(
        _dispatch_kernel,
        grid_spec=pltpu.PrefetchScalarGridSpec(
            num_scalar_prefetch=4,
            grid=(N_MTILES,),
            in_specs=[pl.BlockSpec(memory_space=pl.ANY)],
            out_specs=pl.BlockSpec((TM_MOE, D_MODEL), lambda t, dest, fill, nu, nb: (t, 0)),
            scratch_shapes=[pltpu.SMEM((P_ROWS,), jnp.int32),
                            pltpu.VMEM((2, TM_MOE * BUF_PITCH, LANE), F32),
                            pltpu.SemaphoreType.DMA((2,))]),
        out_shape=jax.ShapeDtypeStruct((P_ROWS, D_MODEL), BF16),
        compiler_params=_params("arbitrary"),
        name="moe_dispatch",
    )(dest_flat, group_fill, n_used, n_batches, h2)


class _ExpertWeights:
    def __init__(self, eid_ref, n_used, w_hbms, bufs, sems, cnt_sm):
        self.eid_ref, self.n_used = eid_ref, n_used
        self.w_hbms, self.bufs, self.sems, self.cnt_sm = w_hbms, bufs, sems, cnt_sm

    def _copies(self, expert, j, slot):
        tn = self.bufs[0].shape[-1]
        cols = pl.ds(pl.multiple_of(j * tn, tn), tn)
        return [pltpu.make_async_copy(w.at[expert, :, cols], buf.at[slot], self.sems.at[i, slot])
                for i, (w, buf) in enumerate(zip(self.w_hbms, self.bufs))]

    def prologue(self):
        self.cnt_sm[0] = 0
        for cp in self._copies(self.eid_ref[0], 0, 0):
            cp.start()

    def acquire(self, j, t, n_passes):
        slot = self.cnt_sm[0] % 2
        for cp in self._copies(self.eid_ref[t], j, slot):
            cp.wait()
        expert = self.eid_ref[t]
        nxt = lax.while_loop(
            lambda u: (u < self.n_used) & (self.eid_ref[jnp.minimum(u, N_MTILES - 1)] == expert),
            lambda u: u + 1, t + 1)
        in_pass = nxt < self.n_used

        @pl.when(in_pass | (j + 1 < n_passes))
        def _():
            for cp in self._copies(self.eid_ref[jnp.where(in_pass, nxt, 0)], jnp.where(in_pass, j, j + 1),
                                   1 - slot):
                cp.start()

        self.cnt_sm[0] = self.cnt_sm[0] + 1
        return slot


def _on_live_rows(valid, out_ref, compute):
    lo = 0
    for rows in ROW_STEPS:
        @pl.when((valid > lo) & (valid <= rows) if lo else (valid <= rows))
        def _():
            out_ref[0:rows, :] = compute(rows).astype(out_ref.dtype)
            if rows < TM_MOE:
                out_ref[rows:TM_MOE, :] = jnp.zeros((TM_MOE - rows, out_ref.shape[1]), out_ref.dtype)

        lo = rows


def _moe_up_kernel(eid_ref, nused_ref, valid_ref, x_ref, wg_hbm, wu_hbm, bg_ref, bu_ref, h_ref,
                   wg_buf, wu_buf, wg_sc, wu_sc, sems, cnt_sm):
    j = pl.program_id(0)
    t = pl.program_id(1)
    n_used = nused_ref[0]
    weights = _ExpertWeights(eid_ref, n_used, (wg_hbm, wu_hbm), (wg_buf, wu_buf), sems, cnt_sm)

    @pl.when((j == 0) & (t == 0))
    def _():
        weights.prologue()

    @pl.when(t < n_used)
    def _():
        @pl.when((t == 0) | (eid_ref[t] != eid_ref[jnp.maximum(t - 1, 0)]))
        def _():
            slot = weights.acquire(j, t, pl.num_programs(0))
            wg_sc[...] = wg_buf[slot].astype(BF16)
            wu_sc[...] = wu_buf[slot].astype(BF16)

        def swiglu(rows):
            x = x_ref[0:rows, :]
            gate = jnp.minimum(_dot(x, wg_sc[...]) + bg_ref[...], SWIGLU_LIMIT)
            up = jnp.clip(_dot(x, wu_sc[...]) + bu_ref[...], -SWIGLU_LIMIT, SWIGLU_LIMIT)
            return gate * jax.nn.sigmoid(SWIGLU_ALPHA * gate) * (up + 1.0)

        _on_live_rows(valid_ref[t], h_ref, swiglu)

    @pl.when(t >= n_used)
    def _():
        h_ref[...] = jnp.zeros(h_ref.shape, BF16)


def _moe_down_kernel(eid_ref, nused_ref, valid_ref, h_ref, wd_hbm, bd_ref, y_ref, wd_buf, wd_sc, sems, cnt_sm):
    j = pl.program_id(0)
    t = pl.program_id(1)
    n_used = nused_ref[0]
    weights = _ExpertWeights(eid_ref, n_used, (wd_hbm,), (wd_buf,), sems, cnt_sm)

    @pl.when((j == 0) & (t == 0))
    def _():
        weights.prologue()

    @pl.when(t < n_used)
    def _():
        @pl.when((t == 0) | (eid_ref[t] != eid_ref[jnp.maximum(t - 1, 0)]))
        def _():
            slot = weights.acquire(j, t, pl.num_programs(0))
            wd_sc[...] = wd_buf[slot].astype(BF16)

        _on_live_rows(valid_ref[t], y_ref, lambda rows: _dot(h_ref[0:rows, :], wd_sc[...]) + bd_ref[...])

    @pl.when(t >= n_used)
    def _():
        y_ref[...] = jnp.zeros(y_ref.shape, F32)


def _row_tile(t, nused_ref):
    return jnp.minimum(t, jnp.maximum(nused_ref[0] - 1, 0))


def _moe_call(body, name, n_weights, k_dim, out_dim, tn, out_dtype, tile_eid, n_used, tile_valid, rows, weights,
              biases):
    any_spec = pl.BlockSpec(memory_space=pl.ANY)
    b_spec = pl.BlockSpec((None, 1, tn), lambda j, t, eid, nu, tv: (eid[t], 0, j))
    return pl.pallas_call(
        body,
        grid_spec=pltpu.PrefetchScalarGridSpec(
            num_scalar_prefetch=3,
            grid=(out_dim // tn, N_MTILES),
            in_specs=[pl.BlockSpec((TM_MOE, k_dim), lambda j, t, eid, nu, tv: (_row_tile(t, nu), 0))]
            + [any_spec] * n_weights + [b_spec] * n_weights,
            out_specs=pl.BlockSpec((TM_MOE, tn), lambda j, t, eid, nu, tv: (t, j)),
            scratch_shapes=[pltpu.VMEM((2, k_dim, tn), F32)] * n_weights
            + [pltpu.VMEM((k_dim, tn), BF16)] * n_weights
            + [pltpu.SemaphoreType.DMA((n_weights, 2)), pltpu.SMEM((1,), jnp.int32)]),
        out_shape=jax.ShapeDtypeStruct((P_ROWS, out_dim), out_dtype),
        compiler_params=_params("arbitrary", "arbitrary"),
        name=name,
    )(tile_eid, n_used, tile_valid, rows, *weights, *biases)


def _moe_up(tile_eid, n_used, tile_valid, xs, w_gate, w_up, b_gate, b_up):
    return _moe_call(_moe_up_kernel, "moe_up", 2, D_MODEL, D_FF, TN_UP, BF16, tile_eid, n_used, tile_valid, xs,
                     (w_gate, w_up), (b_gate, b_up))


def _moe_down(tile_eid, n_used, tile_valid, h, w_down, b_down):
    return _moe_call(_moe_down_kernel, "moe_down", 1, D_FF, D_MODEL, TN_DOWN, F32, tile_eid, n_used, tile_valid, h,
                     (w_down,), (b_down,))


def _final_kernel(dest_ref, y_hbm, x1_ref, wt_ref, g2_ref, ln2g_ref, ln2b_ref, o_ref, ybuf, sems):
    i = pl.program_id(0)
    n_tiles = pl.num_programs(0)

    def start_gather(tile, slot):
        base = tile * (TM_FIN * TOP_K)

        def body(r2, carry):
            a0 = base + r2 * DMA_BATCH
            rows = [dest_ref[a0 + i] for i in range(DMA_BATCH)]
            for i, d in enumerate(rows):
                r = r2 * (DMA_BATCH // TOP_K) + i // TOP_K
                pltpu.make_async_copy(y_hbm.at[pl.ds(d, 1), :], ybuf.at[slot, i % TOP_K, pl.ds(r, 1), :],
                                      sems.at[slot]).start(priority=i % 2)
            return carry

        lax.fori_loop(0, TM_FIN * TOP_K // DMA_BATCH, body, 0)

    @pl.when(i == 0)
    def _():
        start_gather(0, 0)

    @pl.when(i + 1 < n_tiles)
    def _():
        start_gather(i + 1, (i + 1) % 2)

    slot = i % 2
    pltpu.make_async_copy(ybuf.at[slot], ybuf.at[slot], sems.at[slot]).wait()
    wt = wt_ref[...]
    y = wt[:, 0:1] * ybuf[slot, 0]
    for k in range(1, TOP_K):
        y = y + wt[:, k:k + 1] * ybuf[slot, k]
    o_ref[...] = _ln(DN_ALPHA * x1_ref[...] + (1.0 + g2_ref[...]) * y) * ln2g_ref[...] + ln2b_ref[...]


def _final(dest_flat, y, x1, wt, mod3, ln2_g, ln2_b):
    tiles_per_batch = SEQ // TM_FIN
    full = lambda shape: pl.BlockSpec(shape, lambda i, dest: tuple(0 for _ in shape))
    row = lambda width: pl.BlockSpec((TM_FIN, width), lambda i, dest: (i, 0))
    return pl.pallas_call(
        _final_kernel,
        grid_spec=pltpu.PrefetchScalarGridSpec(
            num_scalar_prefetch=1,
            grid=(N_TOK // TM_FIN,),
            in_specs=[pl.BlockSpec(memory_space=pl.ANY), row(D_MODEL), row(LANE),
                      pl.BlockSpec((None, 1, D_MODEL), lambda i, dest: (i // tiles_per_batch, 0, 5)),
                      full(ln2_g.shape), full(ln2_b.shape)],
            out_specs=row(D_MODEL),
            scratch_shapes=[pltpu.VMEM((2, TOP_K, TM_FIN, D_MODEL), F32),
                            pltpu.SemaphoreType.DMA((2,))]),
        out_shape=jax.ShapeDtypeStruct((N_TOK, D_MODEL), F32),
        compiler_params=_params("arbitrary"),
        name="combine_ln2",
    )(dest_flat, y, x1, wt, mod3, ln2_g, ln2_b)


def _route(idx, rank, counts):
    experts = jnp.arange(N_EXPERTS, dtype=jnp.int32)
    padded = ((counts + TM_MOE - 1) // TM_MOE) * TM_MOE
    ends = jnp.sum(jnp.where(experts[None, :] <= experts[:, None], padded[None, :], 0), axis=1)
    starts = ends - padded
    dest = rank
    for e in range(N_EXPERTS):
        dest = dest + jnp.where(idx == e, starts[e], 0)
    dest = dest[:, :TOP_K]
    tile_start = jnp.arange(N_MTILES, dtype=jnp.int32) * TM_MOE
    tile_eid = jnp.minimum(jnp.sum((ends[None, :] <= tile_start[:, None]).astype(jnp.int32), axis=1),
                           N_EXPERTS - 1)
    n_used = (ends[-1] // TM_MOE).astype(jnp.int32).reshape(1)
    group_fill = starts + counts
    tile_fill = jnp.sum(jnp.where(tile_eid[:, None] == experts[None, :], group_fill[None, :], 0), axis=1)
    tile_valid = jnp.clip(tile_fill - tile_start, 0, TM_MOE)
    n_batches = (tile_valid + DMA_BATCH - 1) // DMA_BATCH
    return dest.reshape(-1), tile_eid, n_used, tile_valid, n_batches, group_fill


def kernel(x, c, w_ada, b_ada, w_in, w_gk, b_gk, gla_norm_g, pe_k, pe_v, w_ck1, b_ck1, w_ck2, b_ck2,
           w_cv1, b_cv1, w_cv2, b_cv2, w_o, ln1_g, ln1_b, w_router, b_router, w_gate, b_gate, w_up, b_up,
           w_down, b_down, ln2_g, ln2_b):
    l = 0
    xf = x.reshape(N_TOK, D_MODEL)
    row2 = lambda a: a.reshape(1, -1)

    c8 = jnp.pad(c, ((0, 8 - BATCH), (0, 0)))
    mod3 = _adaln(c8, w_ada[l], row2(b_ada[l]))[:BATCH].reshape(BATCH, 1, 6 * D_MODEL)

    wt = w_in[l].T
    glr0 = 3072
    nsa0 = glr0 + GLA_RANK
    ngt0 = nsa0 + 1024 + 6 * 256
    w_in_t = jnp.concatenate(
        [wt[:glr0], wt[nsa0:ngt0], wt[glr0:nsa0], wt[ngt0:],
         jnp.zeros((D_IN_PAD - wt.shape[0], D_MODEL), F32)], axis=0).astype(BF16)
    proj = _inproj(xf, mod3, w_in_t)

    w_gk_pad = jnp.pad(w_gk[l], ((0, LANE - GLA_RANK), (0, 0)))
    y_gla = _gla(proj, w_gk_pad, row2(b_gk[l]), row2(gla_norm_g[l]))

    cos_t, sin_t = _rope_tables(jnp.arange(SEQ))
    cmp_end = jnp.arange(N_CMP_PAD) * CMP_STRIDE + (CMP_BLOCK - 1)
    cos_c, sin_c = _rope_tables(cmp_end)
    q_r, ks, vs, kw, vw = _prep(proj, cos_t, sin_t)
    kc = _compress(proj, 0, pe_k[l], w_ck1[l], row2(b_ck1[l]), w_ck2[l], row2(b_ck2[l]), cos_c, sin_c, True)
    vc = _compress(proj, 1, pe_v[l], w_cv1[l], row2(b_cv1[l]), w_cv2[l], row2(b_cv2[l]), cos_c, sin_c, False)
    y_nsa = _nsa(q_r, kc, vc, ks, vs, kw, vw, proj)

    wr = jnp.pad(w_router[l], ((0, 0), (0, LANE - N_EXPERTS)))
    wr_hi, wr_lo = _split_bf16(wr)
    br = jnp.concatenate([b_router[l], jnp.full((LANE - N_EXPERTS,), NEG, F32)]).reshape(1, LANE)
    x1, h2, idx128, wt128, rank128, cnt8 = _outproj(y_gla, y_nsa, xf, w_o[l].astype(BF16), mod3,
                                                    row2(ln1_g[l]), row2(ln1_b[l]),
                                                    jnp.concatenate([wr_hi, wr_lo], axis=1), br)

    counts = cnt8[0, :N_EXPERTS].astype(jnp.int32)
    dest, tile_eid, n_used, tile_valid, n_batches, group_fill = _route(idx128, rank128, counts)
    xs = _dispatch(dest, group_fill, n_used, n_batches, h2)
    h = _moe_up(tile_eid, n_used, tile_valid, xs, w_gate[l], w_up[l],
                b_gate[l].reshape(N_EXPERTS, 1, D_FF), b_up[l].reshape(N_EXPERTS, 1, D_FF))
    y = _moe_down(tile_eid, n_used, tile_valid, h, w_down[l], b_down[l].reshape(N_EXPERTS, 1, D_MODEL))

    out = _final(dest, y, x1, wt128, mod3, row2(ln2_g[l]), row2(ln2_b[l]))
    return out.reshape(BATCH, SEQ, D_MODEL)
```

```python
import functools

import numpy as np
import jax
import jax.numpy as jnp
from jax import lax
from jax.experimental import pallas as pl
from jax.experimental.pallas import tpu as pltpu

F32 = jnp.float32
BF16 = jnp.bfloat16

D_MODEL = 2048
BATCH = 2
SEQ = 4096
N_TOK = BATCH * SEQ

GLA_HEADS = 4
GLA_DK = 128
GLA_DV = 256
GLA_RANK = 16
GLA_TAU = 16.0
GLA_CHUNK = 64

NSA_DH = 128
NSA_HEADS = 8
NSA_GROUPS = 2
NSA_HPG = 4
CMP_STRIDE = 16
CMP_BLOCK = 32
CMP_HIDDEN = 256
N_CMP_PAD = SEQ // CMP_STRIDE
SEL_BLOCK = 64
N_BLK = SEQ // SEL_BLOCK
SEL_TOPK = 16
WINDOW = 512
ROPE_DIM = 32
ROPE_THETA = 500000.0
Q_SCALE_LOG2 = NSA_DH ** -0.5 * 1.4426950408889634

N_EXPERTS = 32
TOP_K = 4
D_FF = D_MODEL
SWIGLU_LIMIT = 7.0
SWIGLU_ALPHA = 1.702
DN_ALPHA = 2.0 ** 0.25
LN_EPS = 1e-5

COL_GQ, COL_GK, COL_GV, COL_GR, COL_NQ = 0, 512, 1024, 2048, 3072
COL_KV = 4096
COL_TAIL = 5632
D_IN_PAD = 6144
GATE_LANE0 = GLA_RANK

LANE = 128
ROW_CHUNKS = D_MODEL // LANE
BUF_PITCH = ROW_CHUNKS + 8
NEG = -1e30
VMEM_LIMIT = 56 * 1024 * 1024

TM_IN, TN_IN = 1024, 1024
TT_GLA = 512
TR_PREP = 512
TQ_NSA = 256
TK_SEL = 512
WIN_SPAN = WINDOW + TQ_NSA
ONES_ROWS = 16
TM_OUT = 256
OUT_CHAINS = 2
TM_MOE = 512
ROW_STEPS = tuple(range(64, TM_MOE + 1, 64))
TN_UP, TN_DOWN = 1024, 2048
P_ROWS = N_TOK * TOP_K + N_EXPERTS * TM_MOE
N_MTILES = P_ROWS // TM_MOE
TM_FIN = 256
DMA_BATCH = 8


def _dot(a, b):
    return jnp.dot(a, b, preferred_element_type=F32)


def _dot_nt(a, b):
    return lax.dot_general(a, b, (((1,), (1,)), ((), ())), preferred_element_type=F32)


def _dot_tn(a, b):
    return lax.dot_general(a, b, (((0,), (0,)), ((), ())), preferred_element_type=F32)


def _ln(x):
    xc = x - jnp.mean(x, -1, keepdims=True)
    return xc * lax.rsqrt(jnp.mean(xc * xc, -1, keepdims=True) + LN_EPS)


def _split_bf16(x):
    hi = x.astype(BF16)
    lo = (x - hi.astype(F32)).astype(BF16)
    return hi, lo


def _params(*sem):
    return pltpu.CompilerParams(dimension_semantics=sem, vmem_limit_bytes=VMEM_LIMIT)


def _adaln_kernel(c_ref, w_ref, b_ref, o_ref):
    c = c_ref[...]
    a = (c * jax.nn.sigmoid(c)).astype(BF16)
    o_ref[...] = _dot(a, w_ref[...].astype(BF16)) + b_ref[...]


def _adaln(c8, w, b):
    n = w.shape[1]
    tn = 1024
    return pl.pallas_call(
        _adaln_kernel,
        grid=(n // tn,),
        in_specs=[pl.BlockSpec((8, D_MODEL), lambda j: (0, 0)),
                  pl.BlockSpec((D_MODEL, tn), lambda j: (0, j)),
                  pl.BlockSpec((1, tn), lambda j: (0, j))],
        out_specs=pl.BlockSpec((8, tn), lambda j: (0, j)),
        out_shape=jax.ShapeDtypeStruct((8, n), F32),
        compiler_params=_params("arbitrary"),
        name="adaln",
    )(c8, w, b)


def _inproj_kernel(x_ref, sh_ref, sc_ref, w_ref, o_ref, h_sc):
    @pl.when(pl.program_id(1) == 0)
    def _():
        h = _ln(x_ref[...]) * (1.0 + sc_ref[...]) + sh_ref[...]
        h_sc[...] = h.astype(BF16)

    o_ref[...] = _dot_nt(h_sc[...], w_ref[...])


def _inproj(xf, mod3, w_in_t):
    tiles_per_batch = SEQ // TM_IN
    return pl.pallas_call(
        _inproj_kernel,
        grid=(N_TOK // TM_IN, D_IN_PAD // TN_IN),
        in_specs=[pl.BlockSpec((TM_IN, D_MODEL), lambda i, j: (i, 0)),
                  pl.BlockSpec((None, 1, D_MODEL), lambda i, j: (i // tiles_per_batch, 0, 0)),
                  pl.BlockSpec((None, 1, D_MODEL), lambda i, j: (i // tiles_per_batch, 0, 1)),
                  pl.BlockSpec((TN_IN, D_MODEL), lambda i, j: (j, 0))],
        out_specs=pl.BlockSpec((TM_IN, TN_IN), lambda i, j: (i, j)),
        out_shape=jax.ShapeDtypeStruct((N_TOK, D_IN_PAD), F32),
        scratch_shapes=[pltpu.VMEM((TM_IN, D_MODEL), BF16)],
        compiler_params=_params("parallel", "arbitrary"),
        name="inproj",
    )(xf, mod3, mod3, w_in_t)


GLA_HALVES = (32, 16, 8, 4, 2, 1)
N_LEVELS = len(GLA_HALVES)
ROW_EB = 2 * N_LEVELS
ROW_EL = 2 * N_LEVELS + 1
N_EVIEWS = 2 * N_LEVELS + 2


def _gla_constants():
    c = GLA_CHUNK
    t = np.arange(c)[:, None]
    r = np.arange(c)[None, :]
    mall = np.zeros((N_EVIEWS, c, c), np.float32)
    valid = np.zeros((N_EVIEWS, c, LANE), np.float32)
    masks = np.zeros((N_LEVELS + 1, c, c), np.float32)
    for li, n in enumerate(GLA_HALVES):
        same = (t // (2 * n)) == (r // (2 * n))
        t_up = (t % (2 * n)) >= n
        r_up = (r % (2 * n)) >= n
        mall[2 * li] = same & t_up & r_up & (r <= t)
        mall[2 * li + 1] = same & ~t_up & ~r_up & (r > t)
        valid[2 * li] = np.broadcast_to(t_up, (c, LANE))
        valid[2 * li + 1] = np.broadcast_to(~t_up, (c, LANE))
        masks[li] = same & t_up & ~r_up
    mall[ROW_EB] = r <= t
    mall[ROW_EL] = r > t
    valid[ROW_EB] = 1.0
    valid[ROW_EL] = 1.0
    masks[N_LEVELS] = np.eye(c)
    return (mall.reshape(N_EVIEWS * c, c), valid.reshape(N_EVIEWS * c, LANE), masks)


def _gla_kernel(q_ref, k_ref, v_ref, r_ref, glr_ref, wgk_ref, bgk_ref, g_ref, mall_ref, valid_ref,
                masks_ref, o_ref, st_sc):
    c = GLA_CHUNK

    @pl.when(pl.program_id(1) == 0)
    def _():
        st_sc[...] = jnp.zeros_like(st_sc)

    z = _dot(glr_ref[...].astype(BF16), wgk_ref[...].astype(BF16)) + bgk_ref[...]
    log_a = (jnp.minimum(z, 0.0) - jnp.log1p(jnp.exp(-jnp.abs(z)))) * (1.0 / GLA_TAU)
    mall = mall_ref[...]
    valid = valid_ref[...]
    for ci in range(TT_GLA // c):
        rows = slice(ci * c, (ci + 1) * c)
        la_hi, la_lo = _split_bf16(log_a[rows])
        e_all = jnp.exp(_dot(mall, la_hi) + _dot(mall, la_lo))
        for h in range(GLA_HEADS):
            kcols = slice(h * GLA_DK, (h + 1) * GLA_DK)
            vcols = slice(h * GLA_DV, (h + 1) * GLA_DV)
            e = e_all[:, kcols] * valid
            q = q_ref[rows, kcols] * (GLA_DK ** -0.5)
            k = k_ref[rows, kcols]
            vb = v_ref[rows, vcols].astype(BF16)
            att = masks_ref[N_LEVELS] * _dot_nt(q.astype(BF16), k.astype(BF16))
            for li in range(N_LEVELS):
                eq = e[(2 * li) * c:(2 * li + 1) * c]
                ek = e[(2 * li + 1) * c:(2 * li + 2) * c]
                att = att + masks_ref[li] * _dot_nt((q * eq).astype(BF16), (k * ek).astype(BF16))
            eb = e[ROW_EB * c:(ROW_EB + 1) * c]
            el = e[ROW_EL * c:(ROW_EL + 1) * c]
            st = st_sc[h]
            o = _dot_nt((q * eb).astype(BF16), st.astype(BF16)) + _dot(att.astype(BF16), vb)
            st_sc[h] = st * eb[c - 1:c, :] + _dot_tn(vb, (k * el).astype(BF16))
            o = o * lax.rsqrt(jnp.mean(o * o, -1, keepdims=True) + LN_EPS) * g_ref[...]
            r = r_ref[rows, vcols]
            o_ref[rows, vcols] = (o * (r * jax.nn.sigmoid(r))).astype(BF16)


def _gla(proj, w_gk_pad, b_gk, norm_g):
    nt = SEQ // TT_GLA
    mall, valid, masks = _gla_constants()
    kw = GLA_HEADS * GLA_DK
    vw = GLA_HEADS * GLA_DV
    full = lambda shape: pl.BlockSpec(shape, lambda b, i: tuple(0 for _ in shape))
    cols = lambda width, col0: pl.BlockSpec((TT_GLA, width), lambda b, i: (b * nt + i, col0 // width))
    return pl.pallas_call(
        _gla_kernel,
        grid=(BATCH, nt),
        in_specs=[cols(kw, COL_GQ), cols(kw, COL_GK), cols(vw, COL_GV), cols(vw, COL_GR),
                  cols(LANE, COL_TAIL), full(w_gk_pad.shape), full(b_gk.shape), full(norm_g.shape),
                  full(mall.shape), full(valid.shape), full(masks.shape)],
        out_specs=cols(vw, 0),
        out_shape=jax.ShapeDtypeStruct((N_TOK, vw), BF16),
        scratch_shapes=[pltpu.VMEM((GLA_HEADS, GLA_DV, GLA_DK), F32)],
        compiler_params=_params("parallel", "arbitrary"),
        name="gla",
    )(proj, proj, proj, proj, proj, w_gk_pad, b_gk, norm_g,
      jnp.asarray(mall, BF16), jnp.asarray(valid), jnp.asarray(masks))


def _rope_tables(pos):
    half = ROPE_DIM // 2
    inv_freq = ROPE_THETA ** (-jnp.arange(half, dtype=F32) * (2.0 / ROPE_DIM))
    ang = pos.astype(F32)[:, None] * inv_freq
    cos, sin = jnp.cos(ang), jnp.sin(ang)
    n = pos.shape[0]
    cosf = jnp.concatenate([cos, cos, jnp.ones((n, LANE - ROPE_DIM), F32)], -1)
    sinf = jnp.concatenate([-sin, sin, jnp.zeros((n, LANE - ROPE_DIM), F32)], -1)
    return cosf, sinf


def _rope(x, cosf, sinf):
    lane = lax.broadcasted_iota(jnp.int32, x.shape, 1)
    half = ROPE_DIM // 2
    swapped = jnp.where(lane < half, pltpu.roll(x, LANE - half, 1), pltpu.roll(x, half, 1))
    return x * cosf + swapped * sinf


def _prep_kernel(q_ref, ks_ref, vs_ref, kw_ref, vw_ref, cos_ref, sin_ref,
                 qo_ref, kso_ref, vso_ref, kwo_ref, vwo_ref):
    cosf = cos_ref[...]
    sinf = sin_ref[...]
    for hh in range(NSA_HEADS):
        cols = slice(hh * NSA_DH, (hh + 1) * NSA_DH)
        qo_ref[:, cols] = (_rope(q_ref[:, cols], cosf, sinf) * Q_SCALE_LOG2).astype(BF16)
    for g in range(NSA_GROUPS):
        cols = slice(g * NSA_DH, (g + 1) * NSA_DH)
        kso_ref[g] = _rope(ks_ref[:, cols], cosf, sinf).astype(BF16)
        kwo_ref[g] = _rope(kw_ref[:, cols], cosf, sinf).astype(BF16)
        vso_ref[g] = vs_ref[:, cols].T.astype(BF16)
        vwo_ref[g] = vw_ref[:, cols].T.astype(BF16)


def _prep(proj, cosf, sinf):
    nt = SEQ // TR_PREP
    kvw = NSA_GROUPS * NSA_DH
    kv_in = lambda which: pl.BlockSpec((TR_PREP, kvw), lambda b, i: (b * nt + i, COL_KV // kvw + which))
    k_out = pl.BlockSpec((None, NSA_GROUPS, TR_PREP, NSA_DH), lambda b, i: (b, 0, i, 0))
    k_shape = jax.ShapeDtypeStruct((BATCH, NSA_GROUPS, SEQ, NSA_DH), BF16)
    vt_out = pl.BlockSpec((None, NSA_GROUPS, NSA_DH, TR_PREP), lambda b, i: (b, 0, 0, i))
    vt_shape = jax.ShapeDtypeStruct((BATCH, NSA_GROUPS, NSA_DH, SEQ), BF16)
    qw = NSA_HEADS * NSA_DH
    return pl.pallas_call(
        _prep_kernel,
        grid=(BATCH, nt),
        in_specs=[pl.BlockSpec((TR_PREP, qw), lambda b, i: (b * nt + i, COL_NQ // qw)),
                  kv_in(2), kv_in(3), kv_in(4), kv_in(5),
                  pl.BlockSpec((TR_PREP, LANE), lambda b, i: (i, 0)),
                  pl.BlockSpec((TR_PREP, LANE), lambda b, i: (i, 0))],
        out_specs=[pl.BlockSpec((TR_PREP, qw), lambda b, i: (b * nt + i, 0)),
                   k_out, vt_out, k_out, vt_out],
        out_shape=[jax.ShapeDtypeStruct((N_TOK, qw), BF16), k_shape, vt_shape, k_shape, vt_shape],
        compiler_params=_params("parallel", "parallel"),
        name="nsa_prep",
    )(proj, proj, proj, proj, proj, cosf, sinf)


def _compress_kernel(a_ref, pe_ref, w1_ref, b1_ref, w2_ref, b2_ref, cos_ref, sin_ref, o_ref, bot_sc,
                     *, rope):
    n = N_CMP_PAD
    top = jnp.zeros((n, CMP_HIDDEN), F32)
    bot = jnp.zeros((n, CMP_HIDDEN), F32)
    for p in range(CMP_STRIDE):
        ap = a_ref[pl.ds(p, n, stride=CMP_STRIDE), :]
        w_top = w1_ref[p * NSA_DH:(p + 1) * NSA_DH, :].astype(BF16)
        w_bot = w1_ref[(CMP_STRIDE + p) * NSA_DH:(CMP_STRIDE + p + 1) * NSA_DH, :].astype(BF16)
        top = top + _dot((ap + pe_ref[p:p + 1, :]).astype(BF16), w_top)
        bot = bot + _dot((ap + pe_ref[CMP_STRIDE + p:CMP_STRIDE + p + 1, :]).astype(BF16), w_bot)
    bot_sc[0:n, :] = bot
    bot_sc[n:n + 8, :] = jnp.zeros((8, CMP_HIDDEN), F32)
    h = top + bot_sc[1:n + 1, :] + b1_ref[...]
    h = h * jax.nn.sigmoid(h)
    out = _dot(h.astype(BF16), w2_ref[...].astype(BF16)) + b2_ref[...]
    if rope:
        out = _rope(out, cos_ref[...], sin_ref[...])
    row = lax.broadcasted_iota(jnp.int32, out.shape, 0)
    out = jnp.where(row < n - 1, out, 0.0)
    o_ref[...] = (out if rope else out.T).astype(BF16)


def _compress(proj, which, pe, w1, b1, w2, b2, cosf, sinf, rope):
    col0 = (COL_KV + which * NSA_GROUPS * NSA_DH) // NSA_DH
    out_dims = (N_CMP_PAD, NSA_DH) if rope else (NSA_DH, N_CMP_PAD)
    full = lambda shape: pl.BlockSpec(shape, lambda b, g: tuple(0 for _ in shape))
    return pl.pallas_call(
        functools.partial(_compress_kernel, rope=rope),
        grid=(BATCH, NSA_GROUPS),
        in_specs=[pl.BlockSpec((SEQ, NSA_DH), lambda b, g: (b, col0 + g)),
                  full(pe.shape), full(w1.shape), full(b1.shape), full(w2.shape), full(b2.shape),
                  full(cosf.shape), full(sinf.shape)],
        out_specs=pl.BlockSpec((None, None) + out_dims, lambda b, g: (b, g, 0, 0)),
        out_shape=jax.ShapeDtypeStruct((BATCH, NSA_GROUPS) + out_dims, BF16),
        scratch_shapes=[pltpu.VMEM((N_CMP_PAD + 8, CMP_HIDDEN), F32)],
        compiler_params=_params("parallel", "parallel"),
        name="nsa_compress",
    )(proj, pe, w1, b1, w2, b2, cosf, sinf)


def _nsa_constants():
    c_start = np.arange(N_CMP_PAD) * CMP_STRIDE
    b_start = np.arange(N_BLK) * SEL_BLOCK
    overlap_t = ((c_start[None, :] < b_start[:, None] + SEL_BLOCK)
                 & (c_start[None, :] + CMP_BLOCK > b_start[:, None])).astype(np.float32)
    overlap_t[:, N_CMP_PAD - 1] = 0.0
    expand_t = (np.arange(SEQ)[:, None] // SEL_BLOCK == np.arange(LANE)[None, :]).astype(np.float32)
    return overlap_t, expand_t


def _tile_heads(a):
    return jnp.concatenate([a] * NSA_HPG, axis=1)


def _nsa_kernel(q_ref, kc_ref, vc_ref, ks_ref, vs_ref, kw_ref, vw_ref, gate_ref, ovt_ref, expand_ref,
                o_ref, score_sc, m_sc, acc_sc):
    tq = TQ_NSA
    g = pl.program_id(1)
    q0 = pl.program_id(2) * tq
    qs = jnp.concatenate([q_ref[:, hh * NSA_DH:(hh + 1) * NSA_DH] for hh in range(NSA_HPG)], axis=0)
    pos_q = q0 + lax.broadcasted_iota(jnp.int32, (1, tq), 1)

    def with_ones(vt):
        return jnp.concatenate([vt, jnp.ones((ONES_ROWS, vt.shape[1]), BF16)], axis=0)

    def normalise(acc):
        return acc[0:NSA_DH] / acc[NSA_DH:NSA_DH + 1]

    s = _dot_nt(kc_ref[...], qs)
    cmp_end = lax.broadcasted_iota(jnp.int32, (N_CMP_PAD, tq), 0) * CMP_STRIDE + (CMP_BLOCK - 1)
    s = s + _tile_heads(jnp.where(cmp_end <= pos_q, 0.0, NEG))
    m = jnp.max(s, 0, keepdims=True)
    e = jnp.where(s > 0.5 * NEG, jnp.exp2(s - m), 0.0)
    p = e / jnp.maximum(jnp.sum(e, 0, keepdims=True), 1e-30)
    o_cmp = _dot(vc_ref[...], p.astype(BF16))
    p_sum = p[:, 0:tq] + p[:, tq:2 * tq] + p[:, 2 * tq:3 * tq] + p[:, 3 * tq:4 * tq]
    ps_hi, ps_lo = _split_bf16(p_sum)
    ovt = ovt_ref[...]
    p_blk_t = _dot(ovt, ps_hi) + _dot(ovt, ps_lo)

    jj = lax.broadcasted_iota(jnp.int32, (N_BLK, tq), 0)
    cur = (q0 + lax.broadcasted_iota(jnp.int32, (N_BLK, tq), 1)) // SEL_BLOCK
    forced = (jj == 0) | (jj == cur) | (jj == cur - 1)
    allowed = jj <= cur
    score = jnp.where(forced, 3.0e38, jnp.where(allowed, p_blk_t, -1.0))
    score_sc[...] = score

    blocks_per_step = tq // SEL_BLOCK

    def rank_body(g, rank):
        for u in range(blocks_per_step):
            i = g * blocks_per_step + u
            row = score_sc[pl.ds(i, 1), :]
            first = jnp.where(jj > i, 1.0, 0.0)
            rank = rank + jnp.where(row > score, 1.0, jnp.where(row == score, first, 0.0))
        return rank

    n_groups = q0 // tq + 1
    rank = lax.fori_loop(0, n_groups, rank_body, jnp.zeros((N_BLK, tq), F32))
    sel_t = jnp.where(allowed, jnp.where(rank < SEL_TOPK, 1.0, 0.0), 0.0)
    sel = jnp.concatenate([sel_t, jnp.zeros((LANE - N_BLK, tq), F32)], axis=0).astype(BF16)

    m_sc[...] = jnp.full(m_sc.shape, NEG, F32)
    acc_sc[...] = jnp.zeros(acc_sc.shape, F32)

    def sel_body(kt, carry):
        k0 = pl.multiple_of(kt * TK_SEL, TK_SEL)
        s = _dot_nt(ks_ref[pl.ds(k0, TK_SEL), :], qs)
        sel_x = _dot(expand_ref[pl.ds(k0, TK_SEL), :], sel)
        kpos = k0 + lax.broadcasted_iota(jnp.int32, (TK_SEL, tq), 0)
        bias = jnp.where(kpos <= pos_q, jnp.where(sel_x > 0.5, 0.0, NEG), NEG)
        s = s + _tile_heads(bias)
        m_prev = m_sc[...]
        m_new = jnp.maximum(m_prev, jnp.max(s, 0, keepdims=True))
        alpha = jnp.exp2(m_prev - m_new)
        p = jnp.exp2(s - m_new).astype(BF16)
        acc_sc[...] = alpha * acc_sc[...] + _dot(with_ones(vs_ref[:, pl.ds(k0, TK_SEL)]), p)
        m_sc[...] = m_new
        return carry

    lax.fori_loop(0, (q0 + tq - 1) // TK_SEL + 1, sel_body, 0)
    o_slc = normalise(acc_sc[...])

    start = pl.multiple_of(jnp.maximum(q0 - WINDOW, 0), LANE)
    s = _dot_nt(kw_ref[pl.ds(start, WIN_SPAN), :], qs)
    dist = pos_q - (start + lax.broadcasted_iota(jnp.int32, (WIN_SPAN, tq), 0))
    bias = jnp.where(dist >= 0, jnp.where(dist < WINDOW, 0.0, NEG), NEG)
    s = s + _tile_heads(bias)
    p = jnp.exp2(s - jnp.max(s, 0, keepdims=True)).astype(BF16)
    o_win = normalise(_dot(with_ones(vw_ref[:, pl.ds(start, WIN_SPAN)]), p))

    gates = jax.nn.sigmoid(gate_ref[...]).T

    def gate(hh, branch):
        lane0 = GATE_LANE0 + hh * 3 + branch
        lane1 = lane0 + NSA_HPG * 3
        return jnp.where(g == 0, gates[lane0:lane0 + 1, :], gates[lane1:lane1 + 1, :])

    for hh in range(NSA_HPG):
        c = slice(hh * tq, (hh + 1) * tq)
        o = gate(hh, 0) * o_cmp[:, c] + gate(hh, 1) * o_slc[:, c] + gate(hh, 2) * o_win[:, c]
        o_ref[:, hh * NSA_DH:(hh + 1) * NSA_DH] = o.T.astype(BF16)


def _nsa(q_r, kc, vc, ks, vs, kw, vw, proj):
    nq = SEQ // TQ_NSA
    gw = NSA_HPG * NSA_DH
    overlap_t, expand = _nsa_constants()
    cols = NSA_HPG * TQ_NSA
    per_group = lambda d0, d1: pl.BlockSpec((None, None, d0, d1), lambda b, g, i: (b, g, 0, 0))
    return pl.pallas_call(
        _nsa_kernel,
        grid=(BATCH, NSA_GROUPS, nq),
        in_specs=[pl.BlockSpec((TQ_NSA, gw), lambda b, g, i: (b * nq + i, g)),
                  per_group(N_CMP_PAD, NSA_DH), per_group(NSA_DH, N_CMP_PAD),
                  per_group(SEQ, NSA_DH), per_group(NSA_DH, SEQ),
                  per_group(SEQ, NSA_DH), per_group(NSA_DH, SEQ),
                  pl.BlockSpec((TQ_NSA, LANE), lambda b, g, i: (b * nq + i, COL_TAIL // LANE)),
                  pl.BlockSpec(overlap_t.shape, lambda b, g, i: (0, 0)),
                  pl.BlockSpec(expand.shape, lambda b, g, i: (0, 0))],
        out_specs=pl.BlockSpec((TQ_NSA, gw), lambda b, g, i: (b * nq + i, g)),
        out_shape=jax.ShapeDtypeStruct((N_TOK, NSA_GROUPS * gw), BF16),
        scratch_shapes=[pltpu.VMEM((N_BLK, TQ_NSA), F32),
                        pltpu.VMEM((1, cols), F32),
                        pltpu.VMEM((NSA_DH + ONES_ROWS, cols), F32)],
        compiler_params=_params("parallel", "parallel", "arbitrary"),
        name="nsa_attn",
    )(q_r, kc, vc, ks, vs, kw, vw, proj, jnp.asarray(overlap_t, BF16), jnp.asarray(expand, BF16))


def _outproj_kernel(yg_ref, yn_ref, x_ref, wo_ref, g1_ref, sc2_ref, sh2_ref, ln1g_ref, ln1b_ref,
                    wr_ref, br_ref, ltri_ref, x1_ref, h2_ref, idx_ref, wt_ref, rank_ref,
                    cnt_ref, base_sc):
    half = D_MODEL // 2

    @pl.when(pl.program_id(0) == 0)
    def _():
        base_sc[...] = jnp.zeros_like(base_sc)

    mix_all = _dot(yg_ref[...], wo_ref[0:half, :]) + _dot(yn_ref[...], wo_ref[half:D_MODEL, :])
    wr = wr_ref[...]
    sub = TM_OUT // OUT_CHAINS
    lane = lax.broadcasted_iota(jnp.int32, (TM_OUT, LANE), 1)
    lane_sub = lax.broadcasted_iota(jnp.int32, (sub, LANE), 1)
    lane_f = lane_sub.astype(F32)
    onehot_parts, h2_parts = [], []
    for ci in range(OUT_CHAINS):
        r = slice(ci * sub, (ci + 1) * sub)
        x1 = _ln(DN_ALPHA * x_ref[r, :] + (1.0 + g1_ref[...]) * mix_all[r]) * ln1g_ref[...] + ln1b_ref[...]
        x1_ref[r, :] = x1
        h2 = _ln(x1) * (1.0 + sc2_ref[...]) + sh2_ref[...]
        h_hi, h_lo = _split_bf16(h2)
        h2_parts.append(h2)
        lg = _dot(h_hi, wr) + _dot(h_lo, wr)
        logits = lg[:, 0:LANE] + lg[:, LANE:2 * LANE] + br_ref[...]

        vals = logits
        idx_out = jnp.zeros(logits.shape, jnp.int32)
        exp_out = jnp.zeros(logits.shape, F32)
        denom = jnp.zeros((sub, 1), F32)
        m0 = None
        onehots = []
        for k in range(TOP_K):
            mk = jnp.max(vals, -1, keepdims=True)
            ik = jnp.min(jnp.where(vals == mk, lane_f, float(LANE)), -1, keepdims=True)
            if k == 0:
                m0 = mk
            ek = jnp.exp(mk - m0)
            denom = denom + ek
            idx_out = jnp.where(lane_sub == k, ik.astype(jnp.int32), idx_out)
            exp_out = jnp.where(lane_sub == k, ek, exp_out)
            hit = lane_f == ik
            onehots.append(jnp.where(hit, 1.0, 0.0))
            vals = jnp.where(hit, -3.0e38, vals)
        idx_ref[r, :] = idx_out
        wt_ref[r, :] = exp_out / denom
        onehot_parts.append(onehots)
    onehots = [jnp.concatenate([part[k] for part in onehot_parts], axis=0) for k in range(TOP_K)]
    h2_all = jnp.concatenate(h2_parts, axis=0)
    for s in range(ROW_CHUNKS):
        h2_ref[pl.ds(s, TM_OUT, stride=ROW_CHUNKS), :] = h2_all[:, s * LANE:(s + 1) * LANE]

    cnt = onehots[0] + onehots[1] + onehots[2] + onehots[3]
    base = base_sc[0:1, :]
    before = _dot(ltri_ref[...], cnt.astype(BF16)) + base
    rank_out = jnp.zeros((TM_OUT, LANE), jnp.int32)
    for k in range(TOP_K):
        rk = jnp.sum(onehots[k] * before, -1, keepdims=True)
        rank_out = jnp.where(lane == k, rk.astype(jnp.int32), rank_out)
    rank_ref[...] = rank_out
    base_sc[...] = jnp.broadcast_to(base + jnp.sum(cnt, 0, keepdims=True), base_sc.shape)
    cnt_ref[...] = base_sc[...]


def _outproj(y_gla, y_nsa, xf, w_o, mod3, ln1_g, ln1_b, wr, br):
    tiles_per_batch = SEQ // TM_OUT
    half = D_MODEL // 2
    mod_spec = lambda chunk: pl.BlockSpec((None, 1, D_MODEL), lambda i: (i // tiles_per_batch, 0, chunk))
    full = lambda shape: pl.BlockSpec(shape, lambda i: tuple(0 for _ in shape))
    row = lambda width: pl.BlockSpec((TM_OUT, width), lambda i: (i, 0))
    ltri = jnp.asarray(np.tril(np.ones((TM_OUT, TM_OUT), np.float32), -1), BF16)
    return pl.pallas_call(
        _outproj_kernel,
        grid=(N_TOK // TM_OUT,),
        in_specs=[row(half), row(half), row(D_MODEL), full(w_o.shape),
                  mod_spec(2), mod_spec(4), mod_spec(3),
                  full(ln1_g.shape), full(ln1_b.shape), full(wr.shape), full(br.shape),
                  full(ltri.shape)],
        out_specs=[row(D_MODEL), pl.BlockSpec((TM_OUT * ROW_CHUNKS, LANE), lambda i: (i, 0)),
                   row(LANE), row(LANE), row(LANE), full((8, LANE))],
        out_shape=[jax.ShapeDtypeStruct((N_TOK, D_MODEL), F32),
                   jax.ShapeDtypeStruct((N_TOK * ROW_CHUNKS, LANE), F32),
                   jax.ShapeDtypeStruct((N_TOK, LANE), jnp.int32),
                   jax.ShapeDtypeStruct((N_TOK, LANE), F32),
                   jax.ShapeDtypeStruct((N_TOK, LANE), jnp.int32),
                   jax.ShapeDtypeStruct((8, LANE), F32)],
        scratch_shapes=[pltpu.VMEM((8, LANE), F32)],
        compiler_params=_params("arbitrary"),
        name="outproj_router",
    )(y_gla, y_nsa, xf, w_o, mod3, mod3, mod3, ln1_g, ln1_b, wr, br, ltri)


def _dispatch_kernel(dest_ref, fill_ref, nused_ref, nbatch_ref, h2_hbm, xs_ref, src_sm, buf, sems):
    t = pl.program_id(0)
    n_used = nused_ref[0]

    @pl.when(t == 0)
    def _():
        def pad(e, carry):
            for i in range(DMA_BATCH):
                src_sm[jnp.minimum(fill_ref[e] + i, P_ROWS - 1)] = 0
            return carry

        lax.fori_loop(0, N_EXPERTS, pad, 0)

        def scatter(tok2, carry):
            a0 = tok2 * (2 * TOP_K)
            rows = [dest_ref[a0 + i] for i in range(2 * TOP_K)]
            for i, row in enumerate(rows):
                src_sm[row] = tok2 * 2 + i // TOP_K
            return carry

        lax.fori_loop(0, N_TOK // 2, scatter, 0)

    def start_gather(tile, slot):
        base = tile * TM_MOE

        def body(r8, carry):
            r0 = r8 * DMA_BATCH
            toks = [src_sm[base + r0 + i] for i in range(DMA_BATCH)]
            for i, tok in enumerate(toks):
                pltpu.make_async_copy(h2_hbm.at[pl.ds(pl.multiple_of(tok * ROW_CHUNKS, ROW_CHUNKS), ROW_CHUNKS), :],
                                      buf.at[slot, pl.ds(pl.multiple_of((r0 + i) * BUF_PITCH, 8), ROW_CHUNKS), :],
                                      sems.at[slot]).start(priority=i % 2)
            return carry

        lax.fori_loop(0, nbatch_ref[tile], body, 0)

    @pl.when(t == 0)
    def _():
        buf[...] = jnp.zeros(buf.shape, F32)
        start_gather(0, 0)

    @pl.when(t + 1 < n_used)
    def _():
        start_gather(t + 1, (t + 1) % 2)

    @pl.when(t < n_used)
    def _():
        slot = t % 2
        batch = buf.at[slot, pl.ds(0, DMA_BATCH * ROW_CHUNKS), :]

        def wait_batch(b, carry):
            pltpu.make_async_copy(batch, batch, sems.at[slot]).wait()
            return carry

        lax.fori_loop(0, nbatch_ref[t], wait_batch, 0)
        for s in range(ROW_CHUNKS):
            xs_ref[:, s * LANE:(s + 1) * LANE] = buf[slot, pl.ds(s, TM_MOE, stride=BUF_PITCH), :].astype(BF16)

    @pl.when(t >= n_used)
    def _():
        xs_ref[...] = jnp.zeros(xs_ref.shape, BF16)


def _dispatch(dest_flat, group_fill, n_used, n_batches, h2):
    return pl.pallas_call(
        _dispatch_kernel,
        grid_spec=pltpu.PrefetchScalarGridSpec(
            num_scalar_prefetch=4,
            grid=(N_MTILES,),
            in_specs=[pl.BlockSpec(memory_space=pl.ANY)],
            out_specs=pl.BlockSpec((TM_MOE, D_MODEL), lambda t, dest, fill, nu, nb: (t, 0)),
            scratch_shapes=[pltpu.SMEM((P_ROWS,), jnp.int32),
                            pltpu.VMEM((2, TM_MOE * BUF_PITCH, LANE), F32),
                            pltpu.SemaphoreType.DMA((2,))]),
        out_shape=jax.ShapeDtypeStruct((P_ROWS, D_MODEL), BF16),
        compiler_params=_params("arbitrary"),
        name="moe_dispatch",
    )(dest_flat, group_fill, n_used, n_batches, h2)


class _ExpertWeights:
    def __init__(self, eid_ref, n_used, w_hbms, bufs, sems, cnt_sm):
        self.eid_ref, self.n_used = eid_ref, n_used
        self.w_hbms, self.bufs, self.sems, self.cnt_sm = w_hbms, bufs, sems, cnt_sm

    def _copies(self, expert, j, slot):
        tn = self.bufs[0].shape[-1]
        cols = pl.ds(pl.multiple_of(j * tn, tn), tn)
        return [pltpu.make_async_copy(w.at[expert, :, cols], buf.at[slot], self.sems.at[i, slot])
                for i, (w, buf) in enumerate(zip(self.w_hbms, self.bufs))]

    def prologue(self):
        self.cnt_sm[0] = 0
        for cp in self._copies(self.eid_ref[0], 0, 0):
            cp.start()

    def acquire(self, j, t, n_passes):
        slot = self.cnt_sm[0] % 2
        for cp in self._copies(self.eid_ref[t], j, slot):
            cp.wait()
        expert = self.eid_ref[t]
        nxt = lax.while_loop(
            lambda u: (u < self.n_used) & (self.eid_ref[jnp.minimum(u, N_MTILES - 1)] == expert),
            lambda u: u + 1, t + 1)
        in_pass = nxt < self.n_used

        @pl.when(in_pass | (j + 1 < n_passes))
        def _():
            for cp in self._copies(self.eid_ref[jnp.where(in_pass, nxt, 0)], jnp.where(in_pass, j, j + 1),
                                   1 - slot):
                cp.start()

        self.cnt_sm[0] = self.cnt_sm[0] + 1
        return slot


def _on_live_rows(valid, out_ref, compute):
    lo = 0
    for rows in ROW_STEPS:
        @pl.when((valid > lo) & (valid <= rows) if lo else (valid <= rows))
        def _():
            out_ref[0:rows, :] = compute(rows).astype(out_ref.dtype)
            if rows < TM_MOE:
                out_ref[rows:TM_MOE, :] = jnp.zeros((TM_MOE - rows, out_ref.shape[1]), out_ref.dtype)

        lo = rows


def _moe_up_kernel(eid_ref, nused_ref, valid_ref, x_ref, wg_hbm, wu_hbm, bg_ref, bu_ref, h_ref,
                   wg_buf, wu_buf, wg_sc, wu_sc, sems, cnt_sm):
    j = pl.program_id(0)
    t = pl.program_id(1)
    n_used = nused_ref[0]
    weights = _ExpertWeights(eid_ref, n_used, (wg_hbm, wu_hbm), (wg_buf, wu_buf), sems, cnt_sm)

    @pl.when((j == 0) & (t == 0))
    def _():
        weights.prologue()

    @pl.when(t < n_used)
    def _():
        @pl.when((t == 0) | (eid_ref[t] != eid_ref[jnp.maximum(t - 1, 0)]))
        def _():
            slot = weights.acquire(j, t, pl.num_programs(0))
            wg_sc[...] = wg_buf[slot].astype(BF16)
            wu_sc[...] = wu_buf[slot].astype(BF16)

        def swiglu(rows):
            x = x_ref[0:rows, :]
            gate = jnp.minimum(_dot(x, wg_sc[...]) + bg_ref[...], SWIGLU_LIMIT)
            up = jnp.clip(_dot(x, wu_sc[...]) + bu_ref[...], -SWIGLU_LIMIT, SWIGLU_LIMIT)
            return gate * jax.nn.sigmoid(SWIGLU_ALPHA * gate) * (up + 1.0)

        _on_live_rows(valid_ref[t], h_ref, swiglu)

    @pl.when(t >= n_used)
    def _():
        h_ref[...] = jnp.zeros(h_ref.shape, BF16)


def _moe_down_kernel(eid_ref, nused_ref, valid_ref, h_ref, wd_hbm, bd_ref, y_ref, wd_buf, wd_sc, sems, cnt_sm):
    j = pl.program_id(0)
    t = pl.program_id(1)
    n_used = nused_ref[0]
    weights = _ExpertWeights(eid_ref, n_used, (wd_hbm,), (wd_buf,), sems, cnt_sm)

    @pl.when((j == 0) & (t == 0))
    def _():
        weights.prologue()

    @pl.when(t < n_used)
    def _():
        @pl.when((t == 0) | (eid_ref[t] != eid_ref[jnp.maximum(t - 1, 0)]))
        def _():
            slot = weights.acquire(j, t, pl.num_programs(0))
            wd_sc[...] = wd_buf[slot].astype(BF16)

        _on_live_rows(valid_ref[t], y_ref, lambda rows: _dot(h_ref[0:rows, :], wd_sc[...]) + bd_ref[...])

    @pl.when(t >= n_used)
    def _():
        y_ref[...] = jnp.zeros(y_ref.shape, F32)


def _row_tile(t, nused_ref):
    return jnp.minimum(t, jnp.maximum(nused_ref[0] - 1, 0))


def _moe_call(body, name, n_weights, k_dim, out_dim, tn, out_dtype, tile_eid, n_used, tile_valid, rows, weights,
              biases):
    any_spec = pl.BlockSpec(memory_space=pl.ANY)
    b_spec = pl.BlockSpec((None, 1, tn), lambda j, t, eid, nu, tv: (eid[t], 0, j))
    return pl.pallas_call(
        body,
        grid_spec=pltpu.PrefetchScalarGridSpec(
            num_scalar_prefetch=3,
            grid=(out_dim // tn, N_MTILES),
            in_specs=[pl.BlockSpec((TM_MOE, k_dim), lambda j, t, eid, nu, tv: (_row_tile(t, nu), 0))]
            + [any_spec] * n_weights + [b_spec] * n_weights,
            out_specs=pl.BlockSpec((TM_MOE, tn), lambda j, t, eid, nu, tv: (t, j)),
            scratch_shapes=[pltpu.VMEM((2, k_dim, tn), F32)] * n_weights
            + [pltpu.VMEM((k_dim, tn), BF16)] * n_weights
            + [pltpu.SemaphoreType.DMA((n_weights, 2)), pltpu.SMEM((1,), jnp.int32)]),
        out_shape=jax.ShapeDtypeStruct((P_ROWS, out_dim), out_dtype),
        compiler_params=_params("arbitrary", "arbitrary"),
        name=name,
    )(tile_eid, n_used, tile_valid, rows, *weights, *biases)


def _moe_up(tile_eid, n_used, tile_valid, xs, w_gate, w_up, b_gate, b_up):
    return _moe_call(_moe_up_kernel, "moe_up", 2, D_MODEL, D_FF, TN_UP, BF16, tile_eid, n_used, tile_valid, xs,
                     (w_gate, w_up), (b_gate, b_up))


def _moe_down(tile_eid, n_used, tile_valid, h, w_down, b_down):
    return _moe_call(_moe_down_kernel, "moe_down", 1, D_FF, D_MODEL, TN_DOWN, F32, tile_eid, n_used, tile_valid, h,
                     (w_down,), (b_down,))


def _final_kernel(dest_ref, y_hbm, x1_ref, wt_ref, g2_ref, ln2g_ref, ln2b_ref, o_ref, ybuf, sems):
    i = pl.program_id(0)
    n_tiles = pl.num_programs(0)

    def start_gather(tile, slot):
        base = tile * (TM_FIN * TOP_K)

        def body(r2, carry):
            a0 = base + r2 * DMA_BATCH
            rows = [dest_ref[a0 + i] for i in range(DMA_BATCH)]
            for i, d in enumerate(rows):
                r = r2 * (DMA_BATCH // TOP_K) + i // TOP_K
                pltpu.make_async_copy(y_hbm.at[pl.ds(d, 1), :], ybuf.at[slot, i % TOP_K, pl.ds(r, 1), :],
                                      sems.at[slot]).start(priority=i % 2)
            return carry

        lax.fori_loop(0, TM_FIN * TOP_K // DMA_BATCH, body, 0)

    @pl.when(i == 0)
    def _():
        start_gather(0, 0)

    @pl.when(i + 1 < n_tiles)
    def _():
        start_gather(i + 1, (i + 1) % 2)

    slot = i % 2
    pltpu.make_async_copy(ybuf.at[slot], ybuf.at[slot], sems.at[slot]).wait()
    wt = wt_ref[...]
    y = wt[:, 0:1] * ybuf[slot, 0]
    for k in range(1, TOP_K):
        y = y + wt[:, k:k + 1] * ybuf[slot, k]
    o_ref[...] = _ln(DN_ALPHA * x1_ref[...] + (1.0 + g2_ref[...]) * y) * ln2g_ref[...] + ln2b_ref[...]


def _final(dest_flat, y, x1, wt, mod3, ln2_g, ln2_b):
    tiles_per_batch = SEQ // TM_FIN
    full = lambda shape: pl.BlockSpec(shape, lambda i, dest: tuple(0 for _ in shape))
    row = lambda width: pl.BlockSpec((TM_FIN, width), lambda i, dest: (i, 0))
    return pl.pallas_call(
        _final_kernel,
        grid_spec=pltpu.PrefetchScalarGridSpec(
            num_scalar_prefetch=1,
            grid=(N_TOK // TM_FIN,),
            in_specs=[pl.BlockSpec(memory_space=pl.ANY), row(D_MODEL), row(LANE),
                      pl.BlockSpec((None, 1, D_MODEL), lambda i, dest: (i // tiles_per_batch, 0, 5)),
                      full(ln2_g.shape), full(ln2_b.shape)],
            out_specs=row(D_MODEL),
            scratch_shapes=[pltpu.VMEM((2, TOP_K, TM_FIN, D_MODEL), F32),
                            pltpu.SemaphoreType.DMA((2,))]),
        out_shape=jax.ShapeDtypeStruct((N_TOK, D_MODEL), F32),
        compiler_params=_params("arbitrary"),
        name="combine_ln2",
    )(dest_flat, y, x1, wt, mod3, ln2_g, ln2_b)


def _route(idx, rank, counts):
    experts = jnp.arange(N_EXPERTS, dtype=jnp.int32)
    padded = ((counts + TM_MOE - 1) // TM_MOE) * TM_MOE
    ends = jnp.sum(jnp.where(experts[None, :] <= experts[:, None], padded[None, :], 0), axis=1)
    starts = ends - padded
    dest = rank
    for e in range(N_EXPERTS):
        dest = dest + jnp.where(idx == e, starts[e], 0)
    dest = dest[:, :TOP_K]
    tile_start = jnp.arange(N_MTILES, dtype=jnp.int32) * TM_MOE
    tile_eid = jnp.minimum(jnp.sum((ends[None, :] <= tile_start[:, None]).astype(jnp.int32), axis=1),
                           N_EXPERTS - 1)
    n_used = (ends[-1] // TM_MOE).astype(jnp.int32).reshape(1)
    group_fill = starts + counts
    tile_fill = jnp.sum(jnp.where(tile_eid[:, None] == experts[None, :], group_fill[None, :], 0), axis=1)
    tile_valid = jnp.clip(tile_fill - tile_start, 0, TM_MOE)
    n_batches = (tile_valid + DMA_BATCH - 1) // DMA_BATCH
    return dest.reshape(-1), tile_eid, n_used, tile_valid, n_batches, group_fill


def kernel(x, c, w_ada, b_ada, w_in, w_gk, b_gk, gla_norm_g, pe_k, pe_v, w_ck1, b_ck1, w_ck2, b_ck2,
           w_cv1, b_cv1, w_cv2, b_cv2, w_o, ln1_g, ln1_b, w_router, b_router, w_gate, b_gate, w_up, b_up,
           w_down, b_down, ln2_g, ln2_b):
    l = 0
    xf = x.reshape(N_TOK, D_MODEL)
    row2 = lambda a: a.reshape(1, -1)

    c8 = jnp.pad(c, ((0, 8 - BATCH), (0, 0)))
    mod3 = _adaln(c8, w_ada[l], row2(b_ada[l]))[:BATCH].reshape(BATCH, 1, 6 * D_MODEL)

    wt = w_in[l].T
    glr0 = 3072
    nsa0 = glr0 + GLA_RANK
    ngt0 = nsa0 + 1024 + 6 * 256
    w_in_t = jnp.concatenate(
        [wt[:glr0], wt[nsa0:ngt0], wt[glr0:nsa0], wt[ngt0:],
         jnp.zeros((D_IN_PAD - wt.shape[0], D_MODEL), F32)], axis=0).astype(BF16)
    proj = _inproj(xf, mod3, w_in_t)

    w_gk_pad = jnp.pad(w_gk[l], ((0, LANE - GLA_RANK), (0, 0)))
    y_gla = _gla(proj, w_gk_pad, row2(b_gk[l]), row2(gla_norm_g[l]))

    cos_t, sin_t = _rope_tables(jnp.arange(SEQ))
    cmp_end = jnp.arange(N_CMP_PAD) * CMP_STRIDE + (CMP_BLOCK - 1)
    cos_c, sin_c = _rope_tables(cmp_end)
    q_r, ks, vs, kw, vw = _prep(proj, cos_t, sin_t)
    kc = _compress(proj, 0, pe_k[l], w_ck1[l], row2(b_ck1[l]), w_ck2[l], row2(b_ck2[l]), cos_c, sin_c, True)
    vc = _compress(proj, 1, pe_v[l], w_cv1[l], row2(b_cv1[l]), w_cv2[l], row2(b_cv2[l]), cos_c, sin_c, False)
    y_nsa = _nsa(q_r, kc, vc, ks, vs, kw, vw, proj)

    wr = jnp.pad(w_router[l], ((0, 0), (0, LANE - N_EXPERTS)))
    wr_hi, wr_lo = _split_bf16(wr)
    br = jnp.concatenate([b_router[l], jnp.full((LANE - N_EXPERTS,), NEG, F32)]).reshape(1, LANE)
    x1, h2, idx128, wt128, rank128, cnt8 = _outproj(y_gla, y_nsa, xf, w_o[l].astype(BF16), mod3,
                                                    row2(ln1_g[l]), row2(ln1_b[l]),
                                                    jnp.concatenate([wr_hi, wr_lo], axis=1), br)

    counts = cnt8[0, :N_EXPERTS].astype(jnp.int32)
    dest, tile_eid, n_used, tile_valid, n_batches, group_fill = _route(idx128, rank128, counts)
    xs = _dispatch(dest, group_fill, n_used, n_batches, h2)
    h = _moe_up(tile_eid, n_used, tile_valid, xs, w_gate[l], w_up[l],
                b_gate[l].reshape(N_EXPERTS, 1, D_FF), b_up[l].reshape(N_EXPERTS, 1, D_FF))
    y = _moe_down(tile_eid, n_used, tile_valid, h, w_down[l], b_down[l].reshape(N_EXPERTS, 1, D_MODEL))

    out = _final(dest, y, x1, wt128, mod3, row2(ln2_g[l]), row2(ln2_b[l]))
    return out.reshape(BATCH, SEQ, D_MODEL)
```

```python
import functools

import numpy as np
import jax
import jax.numpy as jnp
from jax import lax
from jax.experimental import pallas as pl
from jax.experimental.pallas import tpu as pltpu

F32 = jnp.float32
BF16 = jnp.bfloat16

D_MODEL = 2048
BATCH = 2
SEQ = 4096
N_TOK = BATCH * SEQ

GLA_HEADS = 4
GLA_DK = 128
GLA_DV = 256
GLA_RANK = 16
GLA_TAU = 16.0
GLA_CHUNK = 64

NSA_DH = 128
NSA_HEADS = 8
NSA_GROUPS = 2
NSA_HPG = 4
CMP_STRIDE = 16
CMP_BLOCK = 32
CMP_HIDDEN = 256
N_CMP_PAD = SEQ // CMP_STRIDE
SEL_BLOCK = 64
N_BLK = SEQ // SEL_BLOCK
SEL_TOPK = 16
WINDOW = 512
ROPE_DIM = 32
ROPE_THETA = 500000.0
Q_SCALE_LOG2 = NSA_DH ** -0.5 * 1.4426950408889634

N_EXPERTS = 32
TOP_K = 4
D_FF = D_MODEL
SWIGLU_LIMIT = 7.0
SWIGLU_ALPHA = 1.702
DN_ALPHA = 2.0 ** 0.25
LN_EPS = 1e-5

COL_GQ, COL_GK, COL_GV, COL_GR, COL_NQ = 0, 512, 1024, 2048, 3072
COL_KV = 4096
COL_TAIL = 5632
D_IN_PAD = 6144
GATE_LANE0 = GLA_RANK

LANE = 128
ROW_CHUNKS = D_MODEL // LANE
BUF_PITCH = ROW_CHUNKS + 8
NEG = -1e30
VMEM_LIMIT = 56 * 1024 * 1024

TM_IN, TN_IN = 1024, 1024
TT_GLA = 512
TR_PREP = 512
TQ_NSA = 256
TK_SEL = 512
WIN_SPAN = WINDOW + TQ_NSA
ONES_ROWS = 16
TM_OUT = 256
OUT_CHAINS = 2
TM_MOE = 512
ROW_STEPS = tuple(range(64, TM_MOE + 1, 64))
TN_UP, TN_DOWN = 1024, 2048
P_ROWS = N_TOK * TOP_K + N_EXPERTS * TM_MOE
N_MTILES = P_ROWS // TM_MOE
TM_FIN = 256
DMA_BATCH = 8


def _dot(a, b):
    return jnp.dot(a, b, preferred_element_type=F32)


def _dot_nt(a, b):
    return lax.dot_general(a, b, (((1,), (1,)), ((), ())), preferred_element_type=F32)


def _dot_tn(a, b):
    return lax.dot_general(a, b, (((0,), (0,)), ((), ())), preferred_element_type=F32)


def _ln(x):
    xc = x - jnp.mean(x, -1, keepdims=True)
    return xc * lax.rsqrt(jnp.mean(xc * xc, -1, keepdims=True) + LN_EPS)


def _split_bf16(x):
    hi = x.astype(BF16)
    lo = (x - hi.astype(F32)).astype(BF16)
    return hi, lo


def _params(*sem):
    return pltpu.CompilerParams(dimension_semantics=sem, vmem_limit_bytes=VMEM_LIMIT)


def _adaln_kernel(c_ref, w_ref, b_ref, o_ref):
    c = c_ref[...]
    a = (c * jax.nn.sigmoid(c)).astype(BF16)
    o_ref[...] = _dot(a, w_ref[...].astype(BF16)) + b_ref[...]


def _adaln(c8, w, b):
    n = w.shape[1]
    tn = 1024
    return pl.pallas_call(
        _adaln_kernel,
        grid=(n // tn,),
        in_specs=[pl.BlockSpec((8, D_MODEL), lambda j: (0, 0)),
                  pl.BlockSpec((D_MODEL, tn), lambda j: (0, j)),
                  pl.BlockSpec((1, tn), lambda j: (0, j))],
        out_specs=pl.BlockSpec((8, tn), lambda j: (0, j)),
        out_shape=jax.ShapeDtypeStruct((8, n), F32),
        compiler_params=_params("arbitrary"),
        name="adaln",
    )(c8, w, b)


def _inproj_kernel(x_ref, sh_ref, sc_ref, w_ref, o_ref, h_sc):
    @pl.when(pl.program_id(1) == 0)
    def _():
        h = _ln(x_ref[...]) * (1.0 + sc_ref[...]) + sh_ref[...]
        h_sc[...] = h.astype(BF16)

    o_ref[...] = _dot_nt(h_sc[...], w_ref[...])


def _inproj(xf, mod3, w_in_t):
    tiles_per_batch = SEQ // TM_IN
    return pl.pallas_call(
        _inproj_kernel,
        grid=(N_TOK // TM_IN, D_IN_PAD // TN_IN),
        in_specs=[pl.BlockSpec((TM_IN, D_MODEL), lambda i, j: (i, 0)),
                  pl.BlockSpec((None, 1, D_MODEL), lambda i, j: (i // tiles_per_batch, 0, 0)),
                  pl.BlockSpec((None, 1, D_MODEL), lambda i, j: (i // tiles_per_batch, 0, 1)),
                  pl.BlockSpec((TN_IN, D_MODEL), lambda i, j: (j, 0))],
        out_specs=pl.BlockSpec((TM_IN, TN_IN), lambda i, j: (i, j)),
        out_shape=jax.ShapeDtypeStruct((N_TOK, D_IN_PAD), F32),
        scratch_shapes=[pltpu.VMEM((TM_IN, D_MODEL), BF16)],
        compiler_params=_params("parallel", "arbitrary"),
        name="inproj",
    )(xf, mod3, mod3, w_in_t)


GLA_HALVES = (32, 16, 8, 4, 2, 1)
N_LEVELS = len(GLA_HALVES)
ROW_EB = 2 * N_LEVELS
ROW_EL = 2 * N_LEVELS + 1
N_EVIEWS = 2 * N_LEVELS + 2


def _gla_constants():
    c = GLA_CHUNK
    t = np.arange(c)[:, None]
    r = np.arange(c)[None, :]
    mall = np.zeros((N_EVIEWS, c, c), np.float32)
    valid = np.zeros((N_EVIEWS, c, LANE), np.float32)
    masks = np.zeros((N_LEVELS + 1, c, c), np.float32)
    for li, n in enumerate(GLA_HALVES):
        same = (t // (2 * n)) == (r // (2 * n))
        t_up = (t % (2 * n)) >= n
        r_up = (r % (2 * n)) >= n
        mall[2 * li] = same & t_up & r_up & (r <= t)
        mall[2 * li + 1] = same & ~t_up & ~r_up & (r > t)
        valid[2 * li] = np.broadcast_to(t_up, (c, LANE))
        valid[2 * li + 1] = np.broadcast_to(~t_up, (c, LANE))
        masks[li] = same & t_up & ~r_up
    mall[ROW_EB] = r <= t
    mall[ROW_EL] = r > t
    valid[ROW_EB] = 1.0
    valid[ROW_EL] = 1.0
    masks[N_LEVELS] = np.eye(c)
    return (mall.reshape(N_EVIEWS * c, c), valid.reshape(N_EVIEWS * c, LANE), masks)


def _gla_kernel(q_ref, k_ref, v_ref, r_ref, glr_ref, wgk_ref, bgk_ref, g_ref, mall_ref, valid_ref,
                masks_ref, o_ref, st_sc):
    c = GLA_CHUNK

    @pl.when(pl.program_id(1) == 0)
    def _():
        st_sc[...] = jnp.zeros_like(st_sc)

    z = _dot(glr_ref[...].astype(BF16), wgk_ref[...].astype(BF16)) + bgk_ref[...]
    log_a = (jnp.minimum(z, 0.0) - jnp.log1p(jnp.exp(-jnp.abs(z)))) * (1.0 / GLA_TAU)
    mall = mall_ref[...]
    valid = valid_ref[...]
    for ci in range(TT_GLA // c):
        rows = slice(ci * c, (ci + 1) * c)
        la_hi, la_lo = _split_bf16(log_a[rows])
        e_all = jnp.exp(_dot(mall, la_hi) + _dot(mall, la_lo))
        for h in range(GLA_HEADS):
            kcols = slice(h * GLA_DK, (h + 1) * GLA_DK)
            vcols = slice(h * GLA_DV, (h + 1) * GLA_DV)
            e = e_all[:, kcols] * valid
            q = q_ref[rows, kcols] * (GLA_DK ** -0.5)
            k = k_ref[rows, kcols]
            vb = v_ref[rows, vcols].astype(BF16)
            att = masks_ref[N_LEVELS] * _dot_nt(q.astype(BF16), k.astype(BF16))
            for li in range(N_LEVELS):
                eq = e[(2 * li) * c:(2 * li + 1) * c]
                ek = e[(2 * li + 1) * c:(2 * li + 2) * c]
                att = att + masks_ref[li] * _dot_nt((q * eq).astype(BF16), (k * ek).astype(BF16))
            eb = e[ROW_EB * c:(ROW_EB + 1) * c]
            el = e[ROW_EL * c:(ROW_EL + 1) * c]
            st = st_sc[h]
            o = _dot_nt((q * eb).astype(BF16), st.astype(BF16)) + _dot(att.astype(BF16), vb)
            st_sc[h] = st * eb[c - 1:c, :] + _dot_tn(vb, (k * el).astype(BF16))
            o = o * lax.rsqrt(jnp.mean(o * o, -1, keepdims=True) + LN_EPS) * g_ref[...]
            r = r_ref[rows, vcols]
            o_ref[rows, vcols] = (o * (r * jax.nn.sigmoid(r))).astype(BF16)


def _gla(proj, w_gk_pad, b_gk, norm_g):
    nt = SEQ // TT_GLA
    mall, valid, masks = _gla_constants()
    kw = GLA_HEADS * GLA_DK
    vw = GLA_HEADS * GLA_DV
    full = lambda shape: pl.BlockSpec(shape, lambda b, i: tuple(0 for _ in shape))
    cols = lambda width, col0: pl.BlockSpec((TT_GLA, width), lambda b, i: (b * nt + i, col0 // width))
    return pl.pallas_call(
        _gla_kernel,
        grid=(BATCH, nt),
        in_specs=[cols(kw, COL_GQ), cols(kw, COL_GK), cols(vw, COL_GV), cols(vw, COL_GR),
                  cols(LANE, COL_TAIL), full(w_gk_pad.shape), full(b_gk.shape), full(norm_g.shape),
                  full(mall.shape), full(valid.shape), full(masks.shape)],
        out_specs=cols(vw, 0),
        out_shape=jax.ShapeDtypeStruct((N_TOK, vw), BF16),
        scratch_shapes=[pltpu.VMEM((GLA_HEADS, GLA_DV, GLA_DK), F32)],
        compiler_params=_params("parallel", "arbitrary"),
        name="gla",
    )(proj, proj, proj, proj, proj, w_gk_pad, b_gk, norm_g,
      jnp.asarray(mall, BF16), jnp.asarray(valid), jnp.asarray(masks))


def _rope_tables(pos):
    half = ROPE_DIM // 2
    inv_freq = ROPE_THETA ** (-np.arange(half, dtype=np.float64) * (2.0 / ROPE_DIM))
    ang = pos.astype(np.float64)[:, None] * inv_freq
    cos, sin = np.cos(ang), np.sin(ang)
    n = pos.shape[0]
    cosf = np.concatenate([cos, cos, np.ones((n, LANE - ROPE_DIM))], -1)
    sinf = np.concatenate([-sin, sin, np.zeros((n, LANE - ROPE_DIM))], -1)
    return jnp.asarray(cosf, F32), jnp.asarray(sinf, F32)


def _rope(x, cosf, sinf):
    lane = lax.broadcasted_iota(jnp.int32, x.shape, 1)
    half = ROPE_DIM // 2
    swapped = jnp.where(lane < half, pltpu.roll(x, LANE - half, 1), pltpu.roll(x, half, 1))
    return x * cosf + swapped * sinf


def _prep_kernel(q_ref, ks_ref, vs_ref, kw_ref, vw_ref, cos_ref, sin_ref,
                 qo_ref, kso_ref, vso_ref, kwo_ref, vwo_ref):
    cosf = cos_ref[...]
    sinf = sin_ref[...]
    for hh in range(NSA_HEADS):
        cols = slice(hh * NSA_DH, (hh + 1) * NSA_DH)
        qo_ref[:, cols] = (_rope(q_ref[:, cols], cosf, sinf) * Q_SCALE_LOG2).astype(BF16)
    for g in range(NSA_GROUPS):
        cols = slice(g * NSA_DH, (g + 1) * NSA_DH)
        kso_ref[g] = _rope(ks_ref[:, cols], cosf, sinf).astype(BF16)
        kwo_ref[g] = _rope(kw_ref[:, cols], cosf, sinf).astype(BF16)
        vso_ref[g] = vs_ref[:, cols].T.astype(BF16)
        vwo_ref[g] = vw_ref[:, cols].T.astype(BF16)


def _prep(proj, cosf, sinf):
    nt = SEQ // TR_PREP
    kvw = NSA_GROUPS * NSA_DH
    kv_in = lambda which: pl.BlockSpec((TR_PREP, kvw), lambda b, i: (b * nt + i, COL_KV // kvw + which))
    k_out = pl.BlockSpec((None, NSA_GROUPS, TR_PREP, NSA_DH), lambda b, i: (b, 0, i, 0))
    k_shape = jax.ShapeDtypeStruct((BATCH, NSA_GROUPS, SEQ, NSA_DH), BF16)
    vt_out = pl.BlockSpec((None, NSA_GROUPS, NSA_DH, TR_PREP), lambda b, i: (b, 0, 0, i))
    vt_shape = jax.ShapeDtypeStruct((BATCH, NSA_GROUPS, NSA_DH, SEQ), BF16)
    qw = NSA_HEADS * NSA_DH
    return pl.pallas_call(
        _prep_kernel,
        grid=(BATCH, nt),
        in_specs=[pl.BlockSpec((TR_PREP, qw), lambda b, i: (b * nt + i, COL_NQ // qw)),
                  kv_in(2), kv_in(3), kv_in(4), kv_in(5),
                  pl.BlockSpec((TR_PREP, LANE), lambda b, i: (i, 0)),
                  pl.BlockSpec((TR_PREP, LANE), lambda b, i: (i, 0))],
        out_specs=[pl.BlockSpec((TR_PREP, qw), lambda b, i: (b * nt + i, 0)),
                   k_out, vt_out, k_out, vt_out],
        out_shape=[jax.ShapeDtypeStruct((N_TOK, qw), BF16), k_shape, vt_shape, k_shape, vt_shape],
        compiler_params=_params("parallel", "parallel"),
        name="nsa_prep",
    )(proj, proj, proj, proj, proj, cosf, sinf)


def _compress_kernel(a_ref, pe_ref, w1_ref, b1_ref, w2_ref, b2_ref, cos_ref, sin_ref, o_ref, bot_sc,
                     *, rope):
    n = N_CMP_PAD
    top = jnp.zeros((n, CMP_HIDDEN), F32)
    bot = jnp.zeros((n, CMP_HIDDEN), F32)
    for p in range(CMP_STRIDE):
        ap = a_ref[pl.ds(p, n, stride=CMP_STRIDE), :]
        w_top = w1_ref[p * NSA_DH:(p + 1) * NSA_DH, :].astype(BF16)
        w_bot = w1_ref[(CMP_STRIDE + p) * NSA_DH:(CMP_STRIDE + p + 1) * NSA_DH, :].astype(BF16)
        top = top + _dot((ap + pe_ref[p:p + 1, :]).astype(BF16), w_top)
        bot = bot + _dot((ap + pe_ref[CMP_STRIDE + p:CMP_STRIDE + p + 1, :]).astype(BF16), w_bot)
    bot_sc[0:n, :] = bot
    bot_sc[n:n + 8, :] = jnp.zeros((8, CMP_HIDDEN), F32)
    h = top + bot_sc[1:n + 1, :] + b1_ref[...]
    h = h * jax.nn.sigmoid(h)
    out = _dot(h.astype(BF16), w2_ref[...].astype(BF16)) + b2_ref[...]
    if rope:
        out = _rope(out, cos_ref[...], sin_ref[...])
    row = lax.broadcasted_iota(jnp.int32, out.shape, 0)
    out = jnp.where(row < n - 1, out, 0.0)
    o_ref[...] = (out if rope else out.T).astype(BF16)


def _compress(proj, which, pe, w1, b1, w2, b2, cosf, sinf, rope):
    col0 = (COL_KV + which * NSA_GROUPS * NSA_DH) // NSA_DH
    out_dims = (N_CMP_PAD, NSA_DH) if rope else (NSA_DH, N_CMP_PAD)
    full = lambda shape: pl.BlockSpec(shape, lambda b, g: tuple(0 for _ in shape))
    return pl.pallas_call(
        functools.partial(_compress_kernel, rope=rope),
        grid=(BATCH, NSA_GROUPS),
        in_specs=[pl.BlockSpec((SEQ, NSA_DH), lambda b, g: (b, col0 + g)),
                  full(pe.shape), full(w1.shape), full(b1.shape), full(w2.shape), full(b2.shape),
                  full(cosf.shape), full(sinf.shape)],
        out_specs=pl.BlockSpec((None, None) + out_dims, lambda b, g: (b, g, 0, 0)),
        out_shape=jax.ShapeDtypeStruct((BATCH, NSA_GROUPS) + out_dims, BF16),
        scratch_shapes=[pltpu.VMEM((N_CMP_PAD + 8, CMP_HIDDEN), F32)],
        compiler_params=_params("parallel", "parallel"),
        name="nsa_compress",
    )(proj, pe, w1, b1, w2, b2, cosf, sinf)


def _nsa_constants():
    c_start = np.arange(N_CMP_PAD) * CMP_STRIDE
    b_start = np.arange(N_BLK) * SEL_BLOCK
    overlap_t = ((c_start[None, :] < b_start[:, None] + SEL_BLOCK)
                 & (c_start[None, :] + CMP_BLOCK > b_start[:, None])).astype(np.float32)
    overlap_t[:, N_CMP_PAD - 1] = 0.0
    expand_t = (np.arange(SEQ)[:, None] // SEL_BLOCK == np.arange(LANE)[None, :]).astype(np.float32)
    return overlap_t, expand_t


def _tile_heads(a):
    return jnp.concatenate([a] * NSA_HPG, axis=1)


def _nsa_kernel(q_ref, kc_ref, vc_ref, ks_ref, vs_ref, kw_ref, vw_ref, gate_ref, ovt_ref, expand_ref,
                o_ref, score_sc, m_sc, acc_sc):
    tq = TQ_NSA
    g = pl.program_id(1)
    q0 = pl.program_id(2) * tq
    qs = jnp.concatenate([q_ref[:, hh * NSA_DH:(hh + 1) * NSA_DH] for hh in range(NSA_HPG)], axis=0)
    pos_q = q0 + lax.broadcasted_iota(jnp.int32, (1, tq), 1)

    def with_ones(vt):
        return jnp.concatenate([vt, jnp.ones((ONES_ROWS, vt.shape[1]), BF16)], axis=0)

    def normalise(acc):
        return acc[0:NSA_DH] / acc[NSA_DH:NSA_DH + 1]

    s = _dot_nt(kc_ref[...], qs)
    cmp_end = lax.broadcasted_iota(jnp.int32, (N_CMP_PAD, tq), 0) * CMP_STRIDE + (CMP_BLOCK - 1)
    s = s + _tile_heads(jnp.where(cmp_end <= pos_q, 0.0, NEG))
    m = jnp.max(s, 0, keepdims=True)
    e = jnp.where(s > 0.5 * NEG, jnp.exp2(s - m), 0.0)
    p = e / jnp.maximum(jnp.sum(e, 0, keepdims=True), 1e-30)
    o_cmp = _dot(vc_ref[...], p.astype(BF16))
    p_sum = p[:, 0:tq] + p[:, tq:2 * tq] + p[:, 2 * tq:3 * tq] + p[:, 3 * tq:4 * tq]
    ps_hi, ps_lo = _split_bf16(p_sum)
    ovt = ovt_ref[...]
    p_blk_t = _dot(ovt, ps_hi) + _dot(ovt, ps_lo)

    jj = lax.broadcasted_iota(jnp.int32, (N_BLK, tq), 0)
    cur = (q0 + lax.broadcasted_iota(jnp.int32, (N_BLK, tq), 1)) // SEL_BLOCK
    forced = (jj == 0) | (jj == cur) | (jj == cur - 1)
    allowed = jj <= cur
    score = jnp.where(forced, 3.0e38, jnp.where(allowed, p_blk_t, -1.0))
    score_sc[...] = score

    blocks_per_step = tq // SEL_BLOCK

    def rank_body(g, rank):
        for u in range(blocks_per_step):
            i = g * blocks_per_step + u
            row = score_sc[pl.ds(i, 1), :]
            first = jnp.where(jj > i, 1.0, 0.0)
            rank = rank + jnp.where(row > score, 1.0, jnp.where(row == score, first, 0.0))
        return rank

    n_groups = q0 // tq + 1
    rank = lax.fori_loop(0, n_groups, rank_body, jnp.zeros((N_BLK, tq), F32))
    sel_t = jnp.where(allowed, jnp.where(rank < SEL_TOPK, 1.0, 0.0), 0.0)
    sel = jnp.concatenate([sel_t, jnp.zeros((LANE - N_BLK, tq), F32)], axis=0).astype(BF16)

    m_sc[...] = jnp.full(m_sc.shape, NEG, F32)
    acc_sc[...] = jnp.zeros(acc_sc.shape, F32)

    def sel_body(kt, carry):
        k0 = pl.multiple_of(kt * TK_SEL, TK_SEL)
        s = _dot_nt(ks_ref[pl.ds(k0, TK_SEL), :], qs)
        sel_x = _dot(expand_ref[pl.ds(k0, TK_SEL), :], sel)
        kpos = k0 + lax.broadcasted_iota(jnp.int32, (TK_SEL, tq), 0)
        bias = jnp.where(kpos <= pos_q, jnp.where(sel_x > 0.5, 0.0, NEG), NEG)
        s = s + _tile_heads(bias)
        m_prev = m_sc[...]
        m_new = jnp.maximum(m_prev, jnp.max(s, 0, keepdims=True))
        alpha = jnp.exp2(m_prev - m_new)
        p = jnp.exp2(s - m_new).astype(BF16)
        acc_sc[...] = alpha * acc_sc[...] + _dot(with_ones(vs_ref[:, pl.ds(k0, TK_SEL)]), p)
        m_sc[...] = m_new
        return carry

    lax.fori_loop(0, (q0 + tq - 1) // TK_SEL + 1, sel_body, 0)
    o_slc = normalise(acc_sc[...])

    start = pl.multiple_of(jnp.maximum(q0 - WINDOW, 0), LANE)
    s = _dot_nt(kw_ref[pl.ds(start, WIN_SPAN), :], qs)
    dist = pos_q - (start + lax.broadcasted_iota(jnp.int32, (WIN_SPAN, tq), 0))
    bias = jnp.where(dist >= 0, jnp.where(dist < WINDOW, 0.0, NEG), NEG)
    s = s + _tile_heads(bias)
    p = jnp.exp2(s - jnp.max(s, 0, keepdims=True)).astype(BF16)
    o_win = normalise(_dot(with_ones(vw_ref[:, pl.ds(start, WIN_SPAN)]), p))

    gates = jax.nn.sigmoid(gate_ref[...]).T

    def gate(hh, branch):
        lane0 = GATE_LANE0 + hh * 3 + branch
        lane1 = lane0 + NSA_HPG * 3
        return jnp.where(g == 0, gates[lane0:lane0 + 1, :], gates[lane1:lane1 + 1, :])

    for hh in range(NSA_HPG):
        c = slice(hh * tq, (hh + 1) * tq)
        o = gate(hh, 0) * o_cmp[:, c] + gate(hh, 1) * o_slc[:, c] + gate(hh, 2) * o_win[:, c]
        o_ref[:, hh * NSA_DH:(hh + 1) * NSA_DH] = o.T.astype(BF16)


def _nsa(q_r, kc, vc, ks, vs, kw, vw, proj):
    nq = SEQ // TQ_NSA
    gw = NSA_HPG * NSA_DH
    overlap_t, expand = _nsa_constants()
    cols = NSA_HPG * TQ_NSA
    per_group = lambda d0, d1: pl.BlockSpec((None, None, d0, d1), lambda b, g, i: (b, g, 0, 0))
    return pl.pallas_call(
        _nsa_kernel,
        grid=(BATCH, NSA_GROUPS, nq),
        in_specs=[pl.BlockSpec((TQ_NSA, gw), lambda b, g, i: (b * nq + i, g)),
                  per_group(N_CMP_PAD, NSA_DH), per_group(NSA_DH, N_CMP_PAD),
                  per_group(SEQ, NSA_DH), per_group(NSA_DH, SEQ),
                  per_group(SEQ, NSA_DH), per_group(NSA_DH, SEQ),
                  pl.BlockSpec((TQ_NSA, LANE), lambda b, g, i: (b * nq + i, COL_TAIL // LANE)),
                  pl.BlockSpec(overlap_t.shape, lambda b, g, i: (0, 0)),
                  pl.BlockSpec(expand.shape, lambda b, g, i: (0, 0))],
        out_specs=pl.BlockSpec((TQ_NSA, gw), lambda b, g, i: (b * nq + i, g)),
        out_shape=jax.ShapeDtypeStruct((N_TOK, NSA_GROUPS * gw), BF16),
        scratch_shapes=[pltpu.VMEM((N_BLK, TQ_NSA), F32),
                        pltpu.VMEM((1, cols), F32),
                        pltpu.VMEM((NSA_DH + ONES_ROWS, cols), F32)],
        compiler_params=_params("parallel", "parallel", "arbitrary"),
        name="nsa_attn",
    )(q_r, kc, vc, ks, vs, kw, vw, proj, jnp.asarray(overlap_t, BF16), jnp.asarray(expand, BF16))


def _outproj_kernel(yg_ref, yn_ref, x_ref, wo_ref, g1_ref, sc2_ref, sh2_ref, ln1g_ref, ln1b_ref,
                    wr_ref, br_ref, ltri_ref, x1_ref, h2_ref, idx_ref, wt_ref, rank_ref,
                    cnt_ref, base_sc):
    half = D_MODEL // 2

    @pl.when(pl.program_id(0) == 0)
    def _():
        base_sc[...] = jnp.zeros_like(base_sc)

    mix_all = _dot(yg_ref[...], wo_ref[0:half, :]) + _dot(yn_ref[...], wo_ref[half:D_MODEL, :])
    wr = wr_ref[...]
    sub = TM_OUT // OUT_CHAINS
    lane = lax.broadcasted_iota(jnp.int32, (TM_OUT, LANE), 1)
    lane_sub = lax.broadcasted_iota(jnp.int32, (sub, LANE), 1)
    lane_f = lane_sub.astype(F32)
    onehot_parts, h2_parts = [], []
    for ci in range(OUT_CHAINS):
        r = slice(ci * sub, (ci + 1) * sub)
        x1 = _ln(DN_ALPHA * x_ref[r, :] + (1.0 + g1_ref[...]) * mix_all[r]) * ln1g_ref[...] + ln1b_ref[...]
        x1_ref[r, :] = x1
        h2 = _ln(x1) * (1.0 + sc2_ref[...]) + sh2_ref[...]
        h_hi, h_lo = _split_bf16(h2)
        h2_parts.append(h2)
        lg = _dot(h_hi, wr) + _dot(h_lo, wr)
        logits = lg[:, 0:LANE] + lg[:, LANE:2 * LANE] + br_ref[...]

        vals = logits
        idx_out = jnp.zeros(logits.shape, jnp.int32)
        exp_out = jnp.zeros(logits.shape, F32)
        denom = jnp.zeros((sub, 1), F32)
        m0 = None
        onehots = []
        for k in range(TOP_K):
            mk = jnp.max(vals, -1, keepdims=True)
            ik = jnp.min(jnp.where(vals == mk, lane_f, float(LANE)), -1, keepdims=True)
            if k == 0:
                m0 = mk
            ek = jnp.exp(mk - m0)
            denom = denom + ek
            idx_out = jnp.where(lane_sub == k, ik.astype(jnp.int32), idx_out)
            exp_out = jnp.where(lane_sub == k, ek, exp_out)
            hit = lane_f == ik
            onehots.append(jnp.where(hit, 1.0, 0.0))
            vals = jnp.where(hit, -3.0e38, vals)
        idx_ref[r, :] = idx_out
        wt_ref[r, :] = exp_out / denom
        onehot_parts.append(onehots)
    onehots = [jnp.concatenate([part[k] for part in onehot_parts], axis=0) for k in range(TOP_K)]
    h2_all = jnp.concatenate(h2_parts, axis=0)
    for s in range(ROW_CHUNKS):
        h2_ref[pl.ds(s, TM_OUT, stride=ROW_CHUNKS), :] = h2_all[:, s * LANE:(s + 1) * LANE]

    cnt = onehots[0] + onehots[1] + onehots[2] + onehots[3]
    base = base_sc[0:1, :]
    before = _dot(ltri_ref[...], cnt.astype(BF16)) + base
    rank_out = jnp.zeros((TM_OUT, LANE), jnp.int32)
    for k in range(TOP_K):
        rk = jnp.sum(onehots[k] * before, -1, keepdims=True)
        rank_out = jnp.where(lane == k, rk.astype(jnp.int32), rank_out)
    rank_ref[...] = rank_out
    base_sc[...] = jnp.broadcast_to(base + jnp.sum(cnt, 0, keepdims=True), base_sc.shape)
    cnt_ref[...] = base_sc[...]


def _outproj(y_gla, y_nsa, xf, w_o, mod3, ln1_g, ln1_b, wr, br):
    tiles_per_batch = SEQ // TM_OUT
    half = D_MODEL // 2
    mod_spec = lambda chunk: pl.BlockSpec((None, 1, D_MODEL), lambda i: (i // tiles_per_batch, 0, chunk))
    full = lambda shape: pl.BlockSpec(shape, lambda i: tuple(0 for _ in shape))
    row = lambda width: pl.BlockSpec((TM_OUT, width), lambda i: (i, 0))
    ltri = jnp.asarray(np.tril(np.ones((TM_OUT, TM_OUT), np.float32), -1), BF16)
    return pl.pallas_call(
        _outproj_kernel,
        grid=(N_TOK // TM_OUT,),
        in_specs=[row(half), row(half), row(D_MODEL), full(w_o.shape),
                  mod_spec(2), mod_spec(4), mod_spec(3),
                  full(ln1_g.shape), full(ln1_b.shape), full(wr.shape), full(br.shape),
                  full(ltri.shape)],
        out_specs=[row(D_MODEL), pl.BlockSpec((TM_OUT * ROW_CHUNKS, LANE), lambda i: (i, 0)),
                   row(LANE), row(LANE), row(LANE), full((8, LANE))],
        out_shape=[jax.ShapeDtypeStruct((N_TOK, D_MODEL), F32),
                   jax.ShapeDtypeStruct((N_TOK * ROW_CHUNKS, LANE), F32),
                   jax.ShapeDtypeStruct((N_TOK, LANE), jnp.int32),
                   jax.ShapeDtypeStruct((N_TOK, LANE), F32),
                   jax.ShapeDtypeStruct((N_TOK, LANE), jnp.int32),
                   jax.ShapeDtypeStruct((8, LANE), F32)],
        scratch_shapes=[pltpu.VMEM((8, LANE), F32)],
        compiler_params=_params("arbitrary"),
        name="outproj_router",
    )(y_gla, y_nsa, xf, w_o, mod3, mod3, mod3, ln1_g, ln1_b, wr, br, ltri)


def _dispatch_kernel(dest_ref, fill_ref, nused_ref, nbatch_ref, h2_hbm, xs_ref, src_sm, buf, sems):
    t = pl.program_id(0)
    n_used = nused_ref[0]

    @pl.when(t == 0)
    def _():
        def pad(e, carry):
            for i in range(DMA_BATCH):
                src_sm[jnp.minimum(fill_ref[e] + i, P_ROWS - 1)] = 0
            return carry

        lax.fori_loop(0, N_EXPERTS, pad, 0)

        def scatter(tok2, carry):
            a0 = tok2 * (2 * TOP_K)
            rows = [dest_ref[a0 + i] for i in range(2 * TOP_K)]
            for i, row in enumerate(rows):
                src_sm[row] = tok2 * 2 + i // TOP_K
            return carry

        lax.fori_loop(0, N_TOK // 2, scatter, 0)

    def start_gather(tile, slot):
        base = tile * TM_MOE

        def body(r8, carry):
            r0 = r8 * DMA_BATCH
            toks = [src_sm[base + r0 + i] for i in range(DMA_BATCH)]
            for i, tok in enumerate(toks):
                pltpu.make_async_copy(h2_hbm.at[pl.ds(pl.multiple_of(tok * ROW_CHUNKS, ROW_CHUNKS), ROW_CHUNKS), :],
                                      buf.at[slot, pl.ds(pl.multiple_of((r0 + i) * BUF_PITCH, 8), ROW_CHUNKS), :],
                                      sems.at[slot]).start(priority=i % 2)
            return carry

        lax.fori_loop(0, nbatch_ref[tile], body, 0)

    @pl.when(t == 0)
    def _():
        buf[...] = jnp.zeros(buf.shape, F32)
        start_gather(0, 0)

    @pl.when(t + 1 < n_used)
    def _():
        start_gather(t + 1, (t + 1) % 2)

    @pl.when(t < n_used)
    def _():
        slot = t % 2
        batch = buf.at[slot, pl.ds(0, DMA_BATCH * ROW_CHUNKS), :]

        def wait_batch(b, carry):
            pltpu.make_async_copy(batch, batch, sems.at[slot]).wait()
            return carry

        lax.fori_loop(0, nbatch_ref[t], wait_batch, 0)
        for s in range(ROW_CHUNKS):
            xs_ref[:, s * LANE:(s + 1) * LANE] = buf[slot, pl.ds(s, TM_MOE, stride=BUF_PITCH), :].astype(BF16)

    @pl.when(t >= n_used)
    def _():
        xs_ref[...] = jnp.zeros(xs_ref.shape, BF16)


def _dispatch(dest_flat, group_fill, n_used, n_batches, h2):
    return pl.pallas_call(
        _dispatch_kernel,
        grid_spec=pltpu.PrefetchScalarGridSpec(
            num_scalar_prefetch=4,
            grid=(N_MTILES,),
            in_specs=[pl.BlockSpec(memory_space=pl.ANY)],
            out_specs=pl.BlockSpec((TM_MOE, D_MODEL), lambda t, dest, fill, nu, nb: (t, 0)),
            scratch_shapes=[pltpu.SMEM((P_ROWS,), jnp.int32),
                            pltpu.VMEM((2, TM_MOE * BUF_PITCH, LANE), F32),
                            pltpu.SemaphoreType.DMA((2,))]),
        out_shape=jax.ShapeDtypeStruct((P_ROWS, D_MODEL), BF16),
        compiler_params=_params("arbitrary"),
        name="moe_dispatch",
    )(dest_flat, group_fill, n_used, n_batches, h2)


class _ExpertWeights:
    def __init__(self, eid_ref, n_used, w_hbms, bufs, sems, cnt_sm):
        self.eid_ref, self.n_used = eid_ref, n_used
        self.w_hbms, self.bufs, self.sems, self.cnt_sm = w_hbms, bufs, sems, cnt_sm

    def _copies(self, expert, j, slot):
        tn = self.bufs[0].shape[-1]
        cols = pl.ds(pl.multiple_of(j * tn, tn), tn)
        return [pltpu.make_async_copy(w.at[expert, :, cols], buf.at[slot], self.sems.at[i, slot])
                for i, (w, buf) in enumerate(zip(self.w_hbms, self.bufs))]

    def prologue(self):
        self.cnt_sm[0] = 0
        for cp in self._copies(self.eid_ref[0], 0, 0):
            cp.start()

    def acquire(self, j, t, n_passes):
        slot = self.cnt_sm[0] % 2
        for cp in self._copies(self.eid_ref[t], j, slot):
            cp.wait()
        expert = self.eid_ref[t]
        nxt = lax.while_loop(
            lambda u: (u < self.n_used) & (self.eid_ref[jnp.minimum(u, N_MTILES - 1)] == expert),
            lambda u: u + 1, t + 1)
        in_pass = nxt < self.n_used

        @pl.when(in_pass | (j + 1 < n_passes))
        def _():
            for cp in self._copies(self.eid_ref[jnp.where(in_pass, nxt, 0)], jnp.where(in_pass, j, j + 1),
                                   1 - slot):
                cp.start()

        self.cnt_sm[0] = self.cnt_sm[0] + 1
        return slot


def _on_live_rows(valid, out_ref, compute):
    lo = 0
    for rows in ROW_STEPS:
        @pl.when((valid > lo) & (valid <= rows) if lo else (valid <= rows))
        def _():
            out_ref[0:rows, :] = compute(rows).astype(out_ref.dtype)
            if rows < TM_MOE:
                out_ref[rows:TM_MOE, :] = jnp.zeros((TM_MOE - rows, out_ref.shape[1]), out_ref.dtype)

        lo = rows


def _moe_up_kernel(eid_ref, nused_ref, valid_ref, x_ref, wg_hbm, wu_hbm, bg_ref, bu_ref, h_ref,
                   wg_buf, wu_buf, wg_sc, wu_sc, sems, cnt_sm):
    j = pl.program_id(0)
    t = pl.program_id(1)
    n_used = nused_ref[0]
    weights = _ExpertWeights(eid_ref, n_used, (wg_hbm, wu_hbm), (wg_buf, wu_buf), sems, cnt_sm)

    @pl.when((j == 0) & (t == 0))
    def _():
        weights.prologue()

    @pl.when(t < n_used)
    def _():
        @pl.when((t == 0) | (eid_ref[t] != eid_ref[jnp.maximum(t - 1, 0)]))
        def _():
            slot = weights.acquire(j, t, pl.num_programs(0))
            wg_sc[...] = wg_buf[slot].astype(BF16)
            wu_sc[...] = wu_buf[slot].astype(BF16)

        def swiglu(rows):
            x = x_ref[0:rows, :]
            gate = jnp.minimum(_dot(x, wg_sc[...]) + bg_ref[...], SWIGLU_LIMIT)
            up = jnp.clip(_dot(x, wu_sc[...]) + bu_ref[...], -SWIGLU_LIMIT, SWIGLU_LIMIT)
            return gate * jax.nn.sigmoid(SWIGLU_ALPHA * gate) * (up + 1.0)

        _on_live_rows(valid_ref[t], h_ref, swiglu)

    @pl.when(t >= n_used)
    def _():
        h_ref[...] = jnp.zeros(h_ref.shape, BF16)


def _moe_down_kernel(eid_ref, nused_ref, valid_ref, h_ref, wd_hbm, bd_ref, y_ref, wd_buf, wd_sc, sems, cnt_sm):
    j = pl.program_id(0)
    t = pl.program_id(1)
    n_used = nused_ref[0]
    weights = _ExpertWeights(eid_ref, n_used, (wd_hbm,), (wd_buf,), sems, cnt_sm)

    @pl.when((j == 0) & (t == 0))
    def _():
        weights.prologue()

    @pl.when(t < n_used)
    def _():
        @pl.when((t == 0) | (eid_ref[t] != eid_ref[jnp.maximum(t - 1, 0)]))
        def _():
            slot = weights.acquire(j, t, pl.num_programs(0))
            wd_sc[...] = wd_buf[slot].astype(BF16)

        _on_live_rows(valid_ref[t], y_ref, lambda rows: _dot(h_ref[0:rows, :], wd_sc[...]) + bd_ref[...])

    @pl.when(t >= n_used)
    def _():
        y_ref[...] = jnp.zeros(y_ref.shape, F32)


def _row_tile(t, nused_ref):
    return jnp.minimum(t, jnp.maximum(nused_ref[0] - 1, 0))


def _moe_call(body, name, n_weights, k_dim, out_dim, tn, out_dtype, tile_eid, n_used, tile_valid, rows, weights,
              biases):
    any_spec = pl.BlockSpec(memory_space=pl.ANY)
    b_spec = pl.BlockSpec((None, 1, tn), lambda j, t, eid, nu, tv: (eid[t], 0, j))
    return pl.pallas_call(
        body,
        grid_spec=pltpu.PrefetchScalarGridSpec(
            num_scalar_prefetch=3,
            grid=(out_dim // tn, N_MTILES),
            in_specs=[pl.BlockSpec((TM_MOE, k_dim), lambda j, t, eid, nu, tv: (_row_tile(t, nu), 0))]
            + [any_spec] * n_weights + [b_spec] * n_weights,
            out_specs=pl.BlockSpec((TM_MOE, tn), lambda j, t, eid, nu, tv: (t, j)),
            scratch_shapes=[pltpu.VMEM((2, k_dim, tn), F32)] * n_weights
            + [pltpu.VMEM((k_dim, tn), BF16)] * n_weights
            + [pltpu.SemaphoreType.DMA((n_weights, 2)), pltpu.SMEM((1,), jnp.int32)]),
        out_shape=jax.ShapeDtypeStruct((P_ROWS, out_dim), out_dtype),
        compiler_params=_params("arbitrary", "arbitrary"),
        name=name,
    )(tile_eid, n_used, tile_valid, rows, *weights, *biases)


def _moe_up(tile_eid, n_used, tile_valid, xs, w_gate, w_up, b_gate, b_up):
    return _moe_call(_moe_up_kernel, "moe_up", 2, D_MODEL, D_FF, TN_UP, BF16, tile_eid, n_used, tile_valid, xs,
                     (w_gate, w_up), (b_gate, b_up))


def _moe_down(tile_eid, n_used, tile_valid, h, w_down, b_down):
    return _moe_call(_moe_down_kernel, "moe_down", 1, D_FF, D_MODEL, TN_DOWN, F32, tile_eid, n_used, tile_valid, h,
                     (w_down,), (b_down,))


def _final_kernel(dest_ref, y_hbm, x1_ref, wt_ref, g2_ref, ln2g_ref, ln2b_ref, o_ref, ybuf, sems):
    i = pl.program_id(0)
    n_tiles = pl.num_programs(0)

    def start_gather(tile, slot):
        base = tile * (TM_FIN * TOP_K)

        def body(r2, carry):
            a0 = base + r2 * DMA_BATCH
            rows = [dest_ref[a0 + i] for i in range(DMA_BATCH)]
            for i, d in enumerate(rows):
                r = r2 * (DMA_BATCH // TOP_K) + i // TOP_K
                pltpu.make_async_copy(y_hbm.at[pl.ds(d, 1), :], ybuf.at[slot, i % TOP_K, pl.ds(r, 1), :],
                                      sems.at[slot]).start(priority=i % 2)
            return carry

        lax.fori_loop(0, TM_FIN * TOP_K // DMA_BATCH, body, 0)

    @pl.when(i == 0)
    def _():
        start_gather(0, 0)

    @pl.when(i + 1 < n_tiles)
    def _():
        start_gather(i + 1, (i + 1) % 2)

    slot = i % 2
    pltpu.make_async_copy(ybuf.at[slot], ybuf.at[slot], sems.at[slot]).wait()
    wt = wt_ref[...]
    y = wt[:, 0:1] * ybuf[slot, 0]
    for k in range(1, TOP_K):
        y = y + wt[:, k:k + 1] * ybuf[slot, k]
    o_ref[...] = _ln(DN_ALPHA * x1_ref[...] + (1.0 + g2_ref[...]) * y) * ln2g_ref[...] + ln2b_ref[...]


def _final(dest_flat, y, x1, wt, mod3, ln2_g, ln2_b):
    tiles_per_batch = SEQ // TM_FIN
    full = lambda shape: pl.BlockSpec(shape, lambda i, dest: tuple(0 for _ in shape))
    row = lambda width: pl.BlockSpec((TM_FIN, width), lambda i, dest: (i, 0))
    return pl.pallas_call(
        _final_kernel,
        grid_spec=pltpu.PrefetchScalarGridSpec(
            num_scalar_prefetch=1,
            grid=(N_TOK // TM_FIN,),
            in_specs=[pl.BlockSpec(memory_space=pl.ANY), row(D_MODEL), row(LANE),
                      pl.BlockSpec((None, 1, D_MODEL), lambda i, dest: (i // tiles_per_batch, 0, 5)),
                      full(ln2_g.shape), full(ln2_b.shape)],
            out_specs=row(D_MODEL),
            scratch_shapes=[pltpu.VMEM((2, TOP_K, TM_FIN, D_MODEL), F32),
                            pltpu.SemaphoreType.DMA((2,))]),
        out_shape=jax.ShapeDtypeStruct((N_TOK, D_MODEL), F32),
        compiler_params=_params("arbitrary"),
        name="combine_ln2",
    )(dest_flat, y, x1, wt, mod3, ln2_g, ln2_b)


def _route(idx, rank, counts):
    experts = jnp.arange(N_EXPERTS, dtype=jnp.int32)
    padded = ((counts + TM_MOE - 1) // TM_MOE) * TM_MOE
    ends = jnp.sum(jnp.where(experts[None, :] <= experts[:, None], padded[None, :], 0), axis=1)
    starts = ends - padded
    idx_t = idx[:, :TOP_K].T
    dest_t = rank[:, :TOP_K].T
    for e in range(N_EXPERTS):
        dest_t = dest_t + jnp.where(idx_t == e, starts[e], 0)
    dest = dest_t.T
    tile_start = jnp.arange(N_MTILES, dtype=jnp.int32) * TM_MOE
    tile_eid = jnp.minimum(jnp.sum((ends[None, :] <= tile_start[:, None]).astype(jnp.int32), axis=1),
                           N_EXPERTS - 1)
    n_used = (ends[-1] // TM_MOE).astype(jnp.int32).reshape(1)
    group_fill = starts + counts
    tile_fill = jnp.sum(jnp.where(tile_eid[:, None] == experts[None, :], group_fill[None, :], 0), axis=1)
    tile_valid = jnp.clip(tile_fill - tile_start, 0, TM_MOE)
    n_batches = (tile_valid + DMA_BATCH - 1) // DMA_BATCH
    return dest.reshape(-1), tile_eid, n_used, tile_valid, n_batches, group_fill


def kernel(x, c, w_ada, b_ada, w_in, w_gk, b_gk, gla_norm_g, pe_k, pe_v, w_ck1, b_ck1, w_ck2, b_ck2,
           w_cv1, b_cv1, w_cv2, b_cv2, w_o, ln1_g, ln1_b, w_router, b_router, w_gate, b_gate, w_up, b_up,
           w_down, b_down, ln2_g, ln2_b):
    l = 0
    xf = x.reshape(N_TOK, D_MODEL)
    row2 = lambda a: a.reshape(1, -1)

    c8 = jnp.pad(c, ((0, 8 - BATCH), (0, 0)))
    mod3 = _adaln(c8, w_ada[l], row2(b_ada[l]))[:BATCH].reshape(BATCH, 1, 6 * D_MODEL)

    wt = w_in[l].T
    glr0 = 3072
    nsa0 = glr0 + GLA_RANK
    ngt0 = nsa0 + 1024 + 6 * 256
    w_in_t = jnp.concatenate(
        [wt[:glr0], wt[nsa0:ngt0], wt[glr0:nsa0], wt[ngt0:],
         jnp.zeros((D_IN_PAD - wt.shape[0], D_MODEL), F32)], axis=0).astype(BF16)
    proj = _inproj(xf, mod3, w_in_t)

    w_gk_pad = jnp.pad(w_gk[l], ((0, LANE - GLA_RANK), (0, 0)))
    y_gla = _gla(proj, w_gk_pad, row2(b_gk[l]), row2(gla_norm_g[l]))

    cos_t, sin_t = _rope_tables(np.arange(SEQ))
    cmp_end = np.arange(N_CMP_PAD) * CMP_STRIDE + (CMP_BLOCK - 1)
    cos_c, sin_c = _rope_tables(cmp_end)
    q_r, ks, vs, kw, vw = _prep(proj, cos_t, sin_t)
    kc = _compress(proj, 0, pe_k[l], w_ck1[l], row2(b_ck1[l]), w_ck2[l], row2(b_ck2[l]), cos_c, sin_c, True)
    vc = _compress(proj, 1, pe_v[l], w_cv1[l], row2(b_cv1[l]), w_cv2[l], row2(b_cv2[l]), cos_c, sin_c, False)
    y_nsa = _nsa(q_r, kc, vc, ks, vs, kw, vw, proj)

    wr = jnp.pad(w_router[l], ((0, 0), (0, LANE - N_EXPERTS)))
    wr_hi, wr_lo = _split_bf16(wr)
    br = jnp.concatenate([b_router[l], jnp.full((LANE - N_EXPERTS,), NEG, F32)]).reshape(1, LANE)
    x1, h2, idx128, wt128, rank128, cnt8 = _outproj(y_gla, y_nsa, xf, w_o[l].astype(BF16), mod3,
                                                    row2(ln1_g[l]), row2(ln1_b[l]),
                                                    jnp.concatenate([wr_hi, wr_lo], axis=1), br)

    counts = cnt8[0, :N_EXPERTS].astype(jnp.int32)
    dest, tile_eid, n_used, tile_valid, n_batches, group_fill = _route(idx128, rank128, counts)
    xs = _dispatch(dest, group_fill, n_used, n_batches, h2)
    h = _moe_up(tile_eid, n_used, tile_valid, xs, w_gate[l], w_up[l],
                b_gate[l].reshape(N_EXPERTS, 1, D_FF), b_up[l].reshape(N_EXPERTS, 1, D_FF))
    y = _moe_down(tile_eid, n_used, tile_valid, h, w_down[l], b_down[l].reshape(N_EXPERTS, 1, D_MODEL))

    out = _final(dest, y, x1, wt128, mod3, row2(ln2_g[l]), row2(ln2_b[l]))
    return out.reshape(BATCH, SEQ, D_MODEL)
```

```python
import functools

import numpy as np
import jax
import jax.numpy as jnp
from jax import lax
from jax.experimental import pallas as pl
from jax.experimental.pallas import tpu as pltpu

F32 = jnp.float32
BF16 = jnp.bfloat16

D_MODEL = 2048
BATCH = 2
SEQ = 4096
N_TOK = BATCH * SEQ

GLA_HEADS = 4
GLA_DK = 128
GLA_DV = 256
GLA_RANK = 16
GLA_TAU = 16.0
GLA_CHUNK = 64

NSA_DH = 128
NSA_HEADS = 8
NSA_GROUPS = 2
NSA_HPG = 4
CMP_STRIDE = 16
CMP_BLOCK = 32
CMP_HIDDEN = 256
N_CMP_PAD = SEQ // CMP_STRIDE
SEL_BLOCK = 64
N_BLK = SEQ // SEL_BLOCK
SEL_TOPK = 16
WINDOW = 512
ROPE_DIM = 32
ROPE_THETA = 500000.0
Q_SCALE_LOG2 = NSA_DH ** -0.5 * 1.4426950408889634

N_EXPERTS = 32
TOP_K = 4
D_FF = D_MODEL
SWIGLU_LIMIT = 7.0
SWIGLU_ALPHA = 1.702
DN_ALPHA = 2.0 ** 0.25
LN_EPS = 1e-5

COL_GQ, COL_GK, COL_GV, COL_GR, COL_NQ = 0, 512, 1024, 2048, 3072
COL_KV = 4096
COL_TAIL = 5632
D_IN_PAD = 6144
GATE_LANE0 = GLA_RANK

LANE = 128
ROW_CHUNKS = D_MODEL // LANE
BUF_PITCH = ROW_CHUNKS + 4
NEG = -1e30
VMEM_LIMIT = 56 * 1024 * 1024

TM_IN, TN_IN = 1024, 1024
TT_GLA = 512
TR_PREP = 512
TQ_NSA = 256
TK_SEL = 512
WIN_SPAN = WINDOW + TQ_NSA
ONES_ROWS = 16
TM_OUT = 256
OUT_CHAINS = 2
TM_MOE = 512
ROW_STEPS = tuple(range(64, TM_MOE + 1, 64))
TN_UP, TN_DOWN = 1024, 2048
P_ROWS = N_TOK * TOP_K + N_EXPERTS * TM_MOE
N_MTILES = P_ROWS // TM_MOE
TM_FIN = 256
DMA_BATCH = 8


def _dot(a, b):
    return jnp.dot(a, b, preferred_element_type=F32)


def _dot_nt(a, b):
    return lax.dot_general(a, b, (((1,), (1,)), ((), ())), preferred_element_type=F32)


def _dot_tn(a, b):
    return lax.dot_general(a, b, (((0,), (0,)), ((), ())), preferred_element_type=F32)


def _ln(x):
    xc = x - jnp.mean(x, -1, keepdims=True)
    return xc * lax.rsqrt(jnp.mean(xc * xc, -1, keepdims=True) + LN_EPS)


def _split_bf16(x):
    hi = x.astype(BF16)
    lo = (x - hi.astype(F32)).astype(BF16)
    return hi, lo


def _params(*sem):
    return pltpu.CompilerParams(dimension_semantics=sem, vmem_limit_bytes=VMEM_LIMIT)


def _adaln_kernel(c_ref, w_ref, b_ref, o_ref):
    c = c_ref[...]
    a = (c * jax.nn.sigmoid(c)).astype(BF16)
    o_ref[...] = _dot(a, w_ref[...].astype(BF16)) + b_ref[...]


def _adaln(c8, w, b):
    n = w.shape[1]
    tn = 1024
    return pl.pallas_call(
        _adaln_kernel,
        grid=(n // tn,),
        in_specs=[pl.BlockSpec((8, D_MODEL), lambda j: (0, 0)),
                  pl.BlockSpec((D_MODEL, tn), lambda j: (0, j)),
                  pl.BlockSpec((1, tn), lambda j: (0, j))],
        out_specs=pl.BlockSpec((8, tn), lambda j: (0, j)),
        out_shape=jax.ShapeDtypeStruct((8, n), F32),
        compiler_params=_params("arbitrary"),
        name="adaln",
    )(c8, w, b)


def _inproj_kernel(x_ref, sh_ref, sc_ref, w_ref, o_ref, h_sc):
    @pl.when(pl.program_id(1) == 0)
    def _():
        h = _ln(x_ref[...]) * (1.0 + sc_ref[...]) + sh_ref[...]
        h_sc[...] = h.astype(BF16)

    o_ref[...] = _dot_nt(h_sc[...], w_ref[...])


def _inproj(xf, mod3, w_in_t):
    tiles_per_batch = SEQ // TM_IN
    return pl.pallas_call(
        _inproj_kernel,
        grid=(N_TOK // TM_IN, D_IN_PAD // TN_IN),
        in_specs=[pl.BlockSpec((TM_IN, D_MODEL), lambda i, j: (i, 0)),
                  pl.BlockSpec((None, 1, D_MODEL), lambda i, j: (i // tiles_per_batch, 0, 0)),
                  pl.BlockSpec((None, 1, D_MODEL), lambda i, j: (i // tiles_per_batch, 0, 1)),
                  pl.BlockSpec((TN_IN, D_MODEL), lambda i, j: (j, 0))],
        out_specs=pl.BlockSpec((TM_IN, TN_IN), lambda i, j: (i, j)),
        out_shape=jax.ShapeDtypeStruct((N_TOK, D_IN_PAD), F32),
        scratch_shapes=[pltpu.VMEM((TM_IN, D_MODEL), BF16)],
        compiler_params=_params("parallel", "arbitrary"),
        name="inproj",
    )(xf, mod3, mod3, w_in_t)


GLA_HALVES = (32, 16, 8, 4, 2, 1)
N_LEVELS = len(GLA_HALVES)
ROW_EB = 2 * N_LEVELS
ROW_EL = 2 * N_LEVELS + 1
N_EVIEWS = 2 * N_LEVELS + 2


def _gla_constants():
    c = GLA_CHUNK
    t = np.arange(c)[:, None]
    r = np.arange(c)[None, :]
    mall = np.zeros((N_EVIEWS, c, c), np.float32)
    valid = np.zeros((N_EVIEWS, c, LANE), np.float32)
    masks = np.zeros((N_LEVELS + 1, c, c), np.float32)
    for li, n in enumerate(GLA_HALVES):
        same = (t // (2 * n)) == (r // (2 * n))
        t_up = (t % (2 * n)) >= n
        r_up = (r % (2 * n)) >= n
        mall[2 * li] = same & t_up & r_up & (r <= t)
        mall[2 * li + 1] = same & ~t_up & ~r_up & (r > t)
        valid[2 * li] = np.broadcast_to(t_up, (c, LANE))
        valid[2 * li + 1] = np.broadcast_to(~t_up, (c, LANE))
        masks[li] = same & t_up & ~r_up
    mall[ROW_EB] = r <= t
    mall[ROW_EL] = r > t
    valid[ROW_EB] = 1.0
    valid[ROW_EL] = 1.0
    masks[N_LEVELS] = np.eye(c)
    return (mall.reshape(N_EVIEWS * c, c), valid.reshape(N_EVIEWS * c, LANE), masks)


def _gla_kernel(q_ref, k_ref, v_ref, r_ref, glr_ref, wgk_ref, bgk_ref, g_ref, mall_ref, valid_ref,
                masks_ref, o_ref, st_sc):
    c = GLA_CHUNK

    @pl.when(pl.program_id(1) == 0)
    def _():
        st_sc[...] = jnp.zeros_like(st_sc)

    z = _dot(glr_ref[...].astype(BF16), wgk_ref[...].astype(BF16)) + bgk_ref[...]
    log_a = (jnp.minimum(z, 0.0) - jnp.log1p(jnp.exp(-jnp.abs(z)))) * (1.0 / GLA_TAU)
    mall = mall_ref[...]
    valid = valid_ref[...]
    for ci in range(TT_GLA // c):
        rows = slice(ci * c, (ci + 1) * c)
        la_hi, la_lo = _split_bf16(log_a[rows])
        e_all = jnp.exp(_dot(mall, la_hi) + _dot(mall, la_lo))
        for h in range(GLA_HEADS):
            kcols = slice(h * GLA_DK, (h + 1) * GLA_DK)
            vcols = slice(h * GLA_DV, (h + 1) * GLA_DV)
            e = e_all[:, kcols] * valid
            q = q_ref[rows, kcols] * (GLA_DK ** -0.5)
            k = k_ref[rows, kcols]
            vb = v_ref[rows, vcols].astype(BF16)
            att = masks_ref[N_LEVELS] * _dot_nt(q.astype(BF16), k.astype(BF16))
            for li in range(N_LEVELS):
                eq = e[(2 * li) * c:(2 * li + 1) * c]
                ek = e[(2 * li + 1) * c:(2 * li + 2) * c]
                att = att + masks_ref[li] * _dot_nt((q * eq).astype(BF16), (k * ek).astype(BF16))
            eb = e[ROW_EB * c:(ROW_EB + 1) * c]
            el = e[ROW_EL * c:(ROW_EL + 1) * c]
            st = st_sc[h]
            o = _dot_nt((q * eb).astype(BF16), st.astype(BF16)) + _dot(att.astype(BF16), vb)
            st_sc[h] = st * eb[c - 1:c, :] + _dot_tn(vb, (k * el).astype(BF16))
            o = o * lax.rsqrt(jnp.mean(o * o, -1, keepdims=True) + LN_EPS) * g_ref[...]
            r = r_ref[rows, vcols]
            o_ref[rows, vcols] = (o * (r * jax.nn.sigmoid(r))).astype(BF16)


def _gla(proj, w_gk_pad, b_gk, norm_g):
    nt = SEQ // TT_GLA
    mall, valid, masks = _gla_constants()
    kw = GLA_HEADS * GLA_DK
    vw = GLA_HEADS * GLA_DV
    full = lambda shape: pl.BlockSpec(shape, lambda b, i: tuple(0 for _ in shape))
    cols = lambda width, col0: pl.BlockSpec((TT_GLA, width), lambda b, i: (b * nt + i, col0 // width))
    return pl.pallas_call(
        _gla_kernel,
        grid=(BATCH, nt),
        in_specs=[cols(kw, COL_GQ), cols(kw, COL_GK), cols(vw, COL_GV), cols(vw, COL_GR),
                  cols(LANE, COL_TAIL), full(w_gk_pad.shape), full(b_gk.shape), full(norm_g.shape),
                  full(mall.shape), full(valid.shape), full(masks.shape)],
        out_specs=cols(vw, 0),
        out_shape=jax.ShapeDtypeStruct((N_TOK, vw), BF16),
        scratch_shapes=[pltpu.VMEM((GLA_HEADS, GLA_DV, GLA_DK), F32)],
        compiler_params=_params("parallel", "arbitrary"),
        name="gla",
    )(proj, proj, proj, proj, proj, w_gk_pad, b_gk, norm_g,
      jnp.asarray(mall, BF16), jnp.asarray(valid), jnp.asarray(masks))


def _rope_tables(pos):
    half = ROPE_DIM // 2
    inv_freq = ROPE_THETA ** (-np.arange(half, dtype=np.float64) * (2.0 / ROPE_DIM))
    ang = pos.astype(np.float64)[:, None] * inv_freq
    cos, sin = np.cos(ang), np.sin(ang)
    n = pos.shape[0]
    cosf = np.concatenate([cos, cos, np.ones((n, LANE - ROPE_DIM))], -1)
    sinf = np.concatenate([-sin, sin, np.zeros((n, LANE - ROPE_DIM))], -1)
    return jnp.asarray(cosf, F32), jnp.asarray(sinf, F32)


def _rope(x, cosf, sinf):
    lane = lax.broadcasted_iota(jnp.int32, x.shape, 1)
    half = ROPE_DIM // 2
    swapped = jnp.where(lane < half, pltpu.roll(x, LANE - half, 1), pltpu.roll(x, half, 1))
    return x * cosf + swapped * sinf


def _prep_kernel(q_ref, ks_ref, vs_ref, kw_ref, vw_ref, cos_ref, sin_ref,
                 qo_ref, kso_ref, vso_ref, kwo_ref, vwo_ref):
    cosf = cos_ref[...]
    sinf = sin_ref[...]
    for hh in range(NSA_HEADS):
        cols = slice(hh * NSA_DH, (hh + 1) * NSA_DH)
        qo_ref[:, cols] = (_rope(q_ref[:, cols], cosf, sinf) * Q_SCALE_LOG2).astype(BF16)
    for g in range(NSA_GROUPS):
        cols = slice(g * NSA_DH, (g + 1) * NSA_DH)
        kso_ref[g] = _rope(ks_ref[:, cols], cosf, sinf).astype(BF16)
        kwo_ref[g] = _rope(kw_ref[:, cols], cosf, sinf).astype(BF16)
        vso_ref[g] = vs_ref[:, cols].T.astype(BF16)
        vwo_ref[g] = vw_ref[:, cols].T.astype(BF16)


def _prep(proj, cosf, sinf):
    nt = SEQ // TR_PREP
    kvw = NSA_GROUPS * NSA_DH
    kv_in = lambda which: pl.BlockSpec((TR_PREP, kvw), lambda b, i: (b * nt + i, COL_KV // kvw + which))
    k_out = pl.BlockSpec((None, NSA_GROUPS, TR_PREP, NSA_DH), lambda b, i: (b, 0, i, 0))
    k_shape = jax.ShapeDtypeStruct((BATCH, NSA_GROUPS, SEQ, NSA_DH), BF16)
    vt_out = pl.BlockSpec((None, NSA_GROUPS, NSA_DH, TR_PREP), lambda b, i: (b, 0, 0, i))
    vt_shape = jax.ShapeDtypeStruct((BATCH, NSA_GROUPS, NSA_DH, SEQ), BF16)
    qw = NSA_HEADS * NSA_DH
    return pl.pallas_call(
        _prep_kernel,
        grid=(BATCH, nt),
        in_specs=[pl.BlockSpec((TR_PREP, qw), lambda b, i: (b * nt + i, COL_NQ // qw)),
                  kv_in(2), kv_in(3), kv_in(4), kv_in(5),
                  pl.BlockSpec((TR_PREP, LANE), lambda b, i: (i, 0)),
                  pl.BlockSpec((TR_PREP, LANE), lambda b, i: (i, 0))],
        out_specs=[pl.BlockSpec((TR_PREP, qw), lambda b, i: (b * nt + i, 0)),
                   k_out, vt_out, k_out, vt_out],
        out_shape=[jax.ShapeDtypeStruct((N_TOK, qw), BF16), k_shape, vt_shape, k_shape, vt_shape],
        compiler_params=_params("parallel", "parallel"),
        name="nsa_prep",
    )(proj, proj, proj, proj, proj, cosf, sinf)


def _compress_kernel(a_ref, pe_ref, w1_ref, b1_ref, w2_ref, b2_ref, cos_ref, sin_ref, o_ref, bot_sc,
                     *, rope):
    n = N_CMP_PAD
    top = jnp.zeros((n, CMP_HIDDEN), F32)
    bot = jnp.zeros((n, CMP_HIDDEN), F32)
    for p in range(CMP_STRIDE):
        ap = a_ref[pl.ds(p, n, stride=CMP_STRIDE), :]
        w_top = w1_ref[p * NSA_DH:(p + 1) * NSA_DH, :].astype(BF16)
        w_bot = w1_ref[(CMP_STRIDE + p) * NSA_DH:(CMP_STRIDE + p + 1) * NSA_DH, :].astype(BF16)
        top = top + _dot((ap + pe_ref[p:p + 1, :]).astype(BF16), w_top)
        bot = bot + _dot((ap + pe_ref[CMP_STRIDE + p:CMP_STRIDE + p + 1, :]).astype(BF16), w_bot)
    bot_sc[0:n, :] = bot
    bot_sc[n:n + 8, :] = jnp.zeros((8, CMP_HIDDEN), F32)
    h = top + bot_sc[1:n + 1, :] + b1_ref[...]
    h = h * jax.nn.sigmoid(h)
    out = _dot(h.astype(BF16), w2_ref[...].astype(BF16)) + b2_ref[...]
    if rope:
        out = _rope(out, cos_ref[...], sin_ref[...])
    row = lax.broadcasted_iota(jnp.int32, out.shape, 0)
    out = jnp.where(row < n - 1, out, 0.0)
    o_ref[...] = (out if rope else out.T).astype(BF16)


def _compress(proj, which, pe, w1, b1, w2, b2, cosf, sinf, rope):
    col0 = (COL_KV + which * NSA_GROUPS * NSA_DH) // NSA_DH
    out_dims = (N_CMP_PAD, NSA_DH) if rope else (NSA_DH, N_CMP_PAD)
    full = lambda shape: pl.BlockSpec(shape, lambda b, g: tuple(0 for _ in shape))
    return pl.pallas_call(
        functools.partial(_compress_kernel, rope=rope),
        grid=(BATCH, NSA_GROUPS),
        in_specs=[pl.BlockSpec((SEQ, NSA_DH), lambda b, g: (b, col0 + g)),
                  full(pe.shape), full(w1.shape), full(b1.shape), full(w2.shape), full(b2.shape),
                  full(cosf.shape), full(sinf.shape)],
        out_specs=pl.BlockSpec((None, None) + out_dims, lambda b, g: (b, g, 0, 0)),
        out_shape=jax.ShapeDtypeStruct((BATCH, NSA_GROUPS) + out_dims, BF16),
        scratch_shapes=[pltpu.VMEM((N_CMP_PAD + 8, CMP_HIDDEN), F32)],
        compiler_params=_params("parallel", "parallel"),
        name="nsa_compress",
    )(proj, pe, w1, b1, w2, b2, cosf, sinf)


def _nsa_constants():
    c_start = np.arange(N_CMP_PAD) * CMP_STRIDE
    b_start = np.arange(N_BLK) * SEL_BLOCK
    overlap_t = ((c_start[None, :] < b_start[:, None] + SEL_BLOCK)
                 & (c_start[None, :] + CMP_BLOCK > b_start[:, None])).astype(np.float32)
    overlap_t[:, N_CMP_PAD - 1] = 0.0
    expand_t = (np.arange(SEQ)[:, None] // SEL_BLOCK == np.arange(LANE)[None, :]).astype(np.float32)
    return overlap_t, expand_t


def _tile_heads(a):
    return jnp.concatenate([a] * NSA_HPG, axis=1)


def _nsa_kernel(q_ref, kc_ref, vc_ref, ks_ref, vs_ref, kw_ref, vw_ref, gate_ref, ovt_ref, expand_ref,
                o_ref, score_sc, m_sc, acc_sc):
    tq = TQ_NSA
    g = pl.program_id(1)
    q0 = pl.program_id(2) * tq
    qs = jnp.concatenate([q_ref[:, hh * NSA_DH:(hh + 1) * NSA_DH] for hh in range(NSA_HPG)], axis=0)
    pos_q = q0 + lax.broadcasted_iota(jnp.int32, (1, tq), 1)

    def with_ones(vt):
        return jnp.concatenate([vt, jnp.ones((ONES_ROWS, vt.shape[1]), BF16)], axis=0)

    def normalise(acc):
        return acc[0:NSA_DH] / acc[NSA_DH:NSA_DH + 1]

    s = _dot_nt(kc_ref[...], qs)
    cmp_end = lax.broadcasted_iota(jnp.int32, (N_CMP_PAD, tq), 0) * CMP_STRIDE + (CMP_BLOCK - 1)
    s = s + _tile_heads(jnp.where(cmp_end <= pos_q, 0.0, NEG))
    m = jnp.max(s, 0, keepdims=True)
    e = jnp.where(s > 0.5 * NEG, jnp.exp2(s - m), 0.0)
    p = e / jnp.maximum(jnp.sum(e, 0, keepdims=True), 1e-30)
    o_cmp = _dot(vc_ref[...], p.astype(BF16))
    p_sum = p[:, 0:tq] + p[:, tq:2 * tq] + p[:, 2 * tq:3 * tq] + p[:, 3 * tq:4 * tq]
    ps_hi, ps_lo = _split_bf16(p_sum)
    ovt = ovt_ref[...]
    p_blk_t = _dot(ovt, ps_hi) + _dot(ovt, ps_lo)

    jj = lax.broadcasted_iota(jnp.int32, (N_BLK, tq), 0)
    cur = (q0 + lax.broadcasted_iota(jnp.int32, (N_BLK, tq), 1)) // SEL_BLOCK
    forced = (jj == 0) | (jj == cur) | (jj == cur - 1)
    allowed = jj <= cur
    score = jnp.where(forced, 3.0e38, jnp.where(allowed, p_blk_t, -1.0))
    score_sc[...] = score

    blocks_per_step = tq // SEL_BLOCK

    def rank_body(g, rank):
        for u in range(blocks_per_step):
            i = g * blocks_per_step + u
            row = score_sc[pl.ds(i, 1), :]
            first = jnp.where(jj > i, 1.0, 0.0)
            rank = rank + jnp.where(row > score, 1.0, jnp.where(row == score, first, 0.0))
        return rank

    n_groups = q0 // tq + 1
    rank = lax.fori_loop(0, n_groups, rank_body, jnp.zeros((N_BLK, tq), F32))
    sel_t = jnp.where(allowed, jnp.where(rank < SEL_TOPK, 1.0, 0.0), 0.0)
    sel = jnp.concatenate([sel_t, jnp.zeros((LANE - N_BLK, tq), F32)], axis=0).astype(BF16)

    m_sc[...] = jnp.full(m_sc.shape, NEG, F32)
    acc_sc[...] = jnp.zeros(acc_sc.shape, F32)

    def sel_body(kt, carry):
        k0 = pl.multiple_of(kt * TK_SEL, TK_SEL)
        s = _dot_nt(ks_ref[pl.ds(k0, TK_SEL), :], qs)
        sel_x = _dot(expand_ref[pl.ds(k0, TK_SEL), :], sel)
        kpos = k0 + lax.broadcasted_iota(jnp.int32, (TK_SEL, tq), 0)
        bias = jnp.where(kpos <= pos_q, jnp.where(sel_x > 0.5, 0.0, NEG), NEG)
        s = s + _tile_heads(bias)
        m_prev = m_sc[...]
        m_new = jnp.maximum(m_prev, jnp.max(s, 0, keepdims=True))
        alpha = jnp.exp2(m_prev - m_new)
        p = jnp.exp2(s - m_new).astype(BF16)
        acc_sc[...] = alpha * acc_sc[...] + _dot(with_ones(vs_ref[:, pl.ds(k0, TK_SEL)]), p)
        m_sc[...] = m_new
        return carry

    lax.fori_loop(0, (q0 + tq - 1) // TK_SEL + 1, sel_body, 0)
    o_slc = normalise(acc_sc[...])

    start = pl.multiple_of(jnp.maximum(q0 - WINDOW, 0), LANE)
    s = _dot_nt(kw_ref[pl.ds(start, WIN_SPAN), :], qs)
    dist = pos_q - (start + lax.broadcasted_iota(jnp.int32, (WIN_SPAN, tq), 0))
    bias = jnp.where(dist >= 0, jnp.where(dist < WINDOW, 0.0, NEG), NEG)
    s = s + _tile_heads(bias)
    p = jnp.exp2(s - jnp.max(s, 0, keepdims=True)).astype(BF16)
    o_win = normalise(_dot(with_ones(vw_ref[:, pl.ds(start, WIN_SPAN)]), p))

    gates = jax.nn.sigmoid(gate_ref[...]).T

    def gate(hh, branch):
        lane0 = GATE_LANE0 + hh * 3 + branch
        lane1 = lane0 + NSA_HPG * 3
        return jnp.where(g == 0, gates[lane0:lane0 + 1, :], gates[lane1:lane1 + 1, :])

    for hh in range(NSA_HPG):
        c = slice(hh * tq, (hh + 1) * tq)
        o = gate(hh, 0) * o_cmp[:, c] + gate(hh, 1) * o_slc[:, c] + gate(hh, 2) * o_win[:, c]
        o_ref[:, hh * NSA_DH:(hh + 1) * NSA_DH] = o.T.astype(BF16)


def _nsa(q_r, kc, vc, ks, vs, kw, vw, proj):
    nq = SEQ // TQ_NSA
    gw = NSA_HPG * NSA_DH
    overlap_t, expand = _nsa_constants()
    cols = NSA_HPG * TQ_NSA
    per_group = lambda d0, d1: pl.BlockSpec((None, None, d0, d1), lambda b, g, i: (b, g, 0, 0))
    return pl.pallas_call(
        _nsa_kernel,
        grid=(BATCH, NSA_GROUPS, nq),
        in_specs=[pl.BlockSpec((TQ_NSA, gw), lambda b, g, i: (b * nq + i, g)),
                  per_group(N_CMP_PAD, NSA_DH), per_group(NSA_DH, N_CMP_PAD),
                  per_group(SEQ, NSA_DH), per_group(NSA_DH, SEQ),
                  per_group(SEQ, NSA_DH), per_group(NSA_DH, SEQ),
                  pl.BlockSpec((TQ_NSA, LANE), lambda b, g, i: (b * nq + i, COL_TAIL // LANE)),
                  pl.BlockSpec(overlap_t.shape, lambda b, g, i: (0, 0)),
                  pl.BlockSpec(expand.shape, lambda b, g, i: (0, 0))],
        out_specs=pl.BlockSpec((TQ_NSA, gw), lambda b, g, i: (b * nq + i, g)),
        out_shape=jax.ShapeDtypeStruct((N_TOK, NSA_GROUPS * gw), BF16),
        scratch_shapes=[pltpu.VMEM((N_BLK, TQ_NSA), F32),
                        pltpu.VMEM((1, cols), F32),
                        pltpu.VMEM((NSA_DH + ONES_ROWS, cols), F32)],
        compiler_params=_params("parallel", "parallel", "arbitrary"),
        name="nsa_attn",
    )(q_r, kc, vc, ks, vs, kw, vw, proj, jnp.asarray(overlap_t, BF16), jnp.asarray(expand, BF16))


def _outproj_kernel(yg_ref, yn_ref, x_ref, wo_ref, g1_ref, sc2_ref, sh2_ref, ln1g_ref, ln1b_ref,
                    wr_ref, br_ref, ltri_ref, x1_ref, h2_ref, idx_ref, wt_ref, rank_ref,
                    cnt_ref, base_sc):
    half = D_MODEL // 2

    @pl.when(pl.program_id(0) == 0)
    def _():
        base_sc[...] = jnp.zeros_like(base_sc)

    mix_all = _dot(yg_ref[...], wo_ref[0:half, :]) + _dot(yn_ref[...], wo_ref[half:D_MODEL, :])
    wr = wr_ref[...]
    sub = TM_OUT // OUT_CHAINS
    lane = lax.broadcasted_iota(jnp.int32, (TM_OUT, LANE), 1)
    lane_sub = lax.broadcasted_iota(jnp.int32, (sub, LANE), 1)
    lane_f = lane_sub.astype(F32)
    onehot_parts, h2_parts = [], []
    for ci in range(OUT_CHAINS):
        r = slice(ci * sub, (ci + 1) * sub)
        x1 = _ln(DN_ALPHA * x_ref[r, :] + (1.0 + g1_ref[...]) * mix_all[r]) * ln1g_ref[...] + ln1b_ref[...]
        x1_ref[r, :] = x1
        h2 = _ln(x1) * (1.0 + sc2_ref[...]) + sh2_ref[...]
        h_hi, h_lo = _split_bf16(h2)
        h2_parts.append(h2)
        lg = _dot(h_hi, wr) + _dot(h_lo, wr)
        logits = lg[:, 0:LANE] + lg[:, LANE:2 * LANE] + br_ref[...]

        vals = logits
        idx_out = jnp.zeros(logits.shape, jnp.int32)
        exp_out = jnp.zeros(logits.shape, F32)
        denom = jnp.zeros((sub, 1), F32)
        m0 = None
        onehots = []
        for k in range(TOP_K):
            mk = jnp.max(vals, -1, keepdims=True)
            ik = jnp.min(jnp.where(vals == mk, lane_f, float(LANE)), -1, keepdims=True)
            if k == 0:
                m0 = mk
            ek = jnp.exp(mk - m0)
            denom = denom + ek
            idx_out = jnp.where(lane_sub == k, ik.astype(jnp.int32), idx_out)
            exp_out = jnp.where(lane_sub == k, ek, exp_out)
            hit = lane_f == ik
            onehots.append(jnp.where(hit, 1.0, 0.0))
            vals = jnp.where(hit, -3.0e38, vals)
        idx_ref[r, :] = idx_out
        wt_ref[r, :] = exp_out / denom
        onehot_parts.append(onehots)
    onehots = [jnp.concatenate([part[k] for part in onehot_parts], axis=0) for k in range(TOP_K)]
    h2_all = jnp.concatenate(h2_parts, axis=0)
    for s in range(ROW_CHUNKS):
        h2_ref[pl.ds(s, TM_OUT, stride=ROW_CHUNKS), :] = h2_all[:, s * LANE:(s + 1) * LANE]

    cnt = onehots[0] + onehots[1] + onehots[2] + onehots[3]
    base = base_sc[0:1, :]
    before = _dot(ltri_ref[...], cnt.astype(BF16)) + base
    rank_out = jnp.zeros((TM_OUT, LANE), jnp.int32)
    for k in range(TOP_K):
        rk = jnp.sum(onehots[k] * before, -1, keepdims=True)
        rank_out = jnp.where(lane == k, rk.astype(jnp.int32), rank_out)
    rank_ref[...] = rank_out
    base_sc[...] = jnp.broadcast_to(base + jnp.sum(cnt, 0, keepdims=True), base_sc.shape)
    cnt_ref[...] = base_sc[...]


def _outproj(y_gla, y_nsa, xf, w_o, mod3, ln1_g, ln1_b, wr, br):
    tiles_per_batch = SEQ // TM_OUT
    half = D_MODEL // 2
    mod_spec = lambda chunk: pl.BlockSpec((None, 1, D_MODEL), lambda i: (i // tiles_per_batch, 0, chunk))
    full = lambda shape: pl.BlockSpec(shape, lambda i: tuple(0 for _ in shape))
    row = lambda width: pl.BlockSpec((TM_OUT, width), lambda i: (i, 0))
    ltri = jnp.asarray(np.tril(np.ones((TM_OUT, TM_OUT), np.float32), -1), BF16)
    return pl.pallas_call(
        _outproj_kernel,
        grid=(N_TOK // TM_OUT,),
        in_specs=[row(half), row(half), row(D_MODEL), full(w_o.shape),
                  mod_spec(2), mod_spec(4), mod_spec(3),
                  full(ln1_g.shape), full(ln1_b.shape), full(wr.shape), full(br.shape),
                  full(ltri.shape)],
        out_specs=[row(D_MODEL), pl.BlockSpec((TM_OUT * ROW_CHUNKS, LANE), lambda i: (i, 0)),
                   row(LANE), row(LANE), row(LANE), full((8, LANE))],
        out_shape=[jax.ShapeDtypeStruct((N_TOK, D_MODEL), F32),
                   jax.ShapeDtypeStruct((N_TOK * ROW_CHUNKS, LANE), F32),
                   jax.ShapeDtypeStruct((N_TOK, LANE), jnp.int32),
                   jax.ShapeDtypeStruct((N_TOK, LANE), F32),
                   jax.ShapeDtypeStruct((N_TOK, LANE), jnp.int32),
                   jax.ShapeDtypeStruct((8, LANE), F32)],
        scratch_shapes=[pltpu.VMEM((8, LANE), F32)],
        compiler_params=_params("arbitrary"),
        name="outproj_router",
    )(y_gla, y_nsa, xf, w_o, mod3, mod3, mod3, ln1_g, ln1_b, wr, br, ltri)


def _dispatch_kernel(dest_ref, fill_ref, nused_ref, nbatch_ref, h2_hbm, xs_ref, src_sm, buf, sems):
    t = pl.program_id(0)
    n_used = nused_ref[0]

    @pl.when(t == 0)
    def _():
        def pad(e, carry):
            for i in range(DMA_BATCH):
                src_sm[jnp.minimum(fill_ref[e] + i, P_ROWS - 1)] = 0
            return carry

        lax.fori_loop(0, N_EXPERTS, pad, 0)

        def scatter(tok2, carry):
            a0 = tok2 * (2 * TOP_K)
            rows = [dest_ref[a0 + i] for i in range(2 * TOP_K)]
            for i, row in enumerate(rows):
                src_sm[row] = tok2 * 2 + i // TOP_K
            return carry

        lax.fori_loop(0, N_TOK // 2, scatter, 0)

    def start_gather(tile, slot):
        base = tile * TM_MOE

        def body(r8, carry):
            r0 = r8 * DMA_BATCH
            toks = [src_sm[base + r0 + i] for i in range(DMA_BATCH)]
            for i, tok in enumerate(toks):
                pltpu.make_async_copy(h2_hbm.at[pl.ds(pl.multiple_of(tok * ROW_CHUNKS, ROW_CHUNKS), ROW_CHUNKS), :],
                                      buf.at[slot, pl.ds(pl.multiple_of((r0 + i) * BUF_PITCH, 4), ROW_CHUNKS), :],
                                      sems.at[slot]).start(priority=i % 2)
            return carry

        lax.fori_loop(0, nbatch_ref[tile], body, 0)

    @pl.when(t == 0)
    def _():
        buf[...] = jnp.zeros(buf.shape, F32)
        start_gather(0, 0)

    @pl.when(t + 1 < n_used)
    def _():
        start_gather(t + 1, (t + 1) % 2)

    @pl.when(t < n_used)
    def _():
        slot = t % 2
        batch = buf.at[slot, pl.ds(0, DMA_BATCH * ROW_CHUNKS), :]

        def wait_batch(b, carry):
            pltpu.make_async_copy(batch, batch, sems.at[slot]).wait()
            return carry

        lax.fori_loop(0, nbatch_ref[t], wait_batch, 0)
        for s in range(ROW_CHUNKS):
            xs_ref[:, s * LANE:(s + 1) * LANE] = buf[slot, pl.ds(s, TM_MOE, stride=BUF_PITCH), :].astype(BF16)

    @pl.when(t >= n_used)
    def _():
        xs_ref[...] = jnp.zeros(xs_ref.shape, BF16)


def _dispatch(dest_flat, group_fill, n_used, n_batches, h2):
    return pl.pallas_call(
        _dispatch_kernel,
        grid_spec=pltpu.PrefetchScalarGridSpec(
            num_scalar_prefetch=4,
            grid=(N_MTILES,),
            in_specs=[pl.BlockSpec(memory_space=pl.ANY)],
            out_specs=pl.BlockSpec((TM_MOE, D_MODEL), lambda t, dest, fill, nu, nb: (t, 0)),
            scratch_shapes=[pltpu.SMEM((P_ROWS,), jnp.int32),
                            pltpu.VMEM((2, TM_MOE * BUF_PITCH, LANE), F32),
                            pltpu.SemaphoreType.DMA((2,))]),
        out_shape=jax.ShapeDtypeStruct((P_ROWS, D_MODEL), BF16),
        compiler_params=_params("arbitrary"),
        name="moe_dispatch",
    )(dest_flat, group_fill, n_used, n_batches, h2)


class _ExpertWeights:
    def __init__(self, eid_ref, n_used, w_hbms, bufs, sems, cnt_sm):
        self.eid_ref, self.n_used = eid_ref, n_used
        self.w_hbms, self.bufs, self.sems, self.cnt_sm = w_hbms, bufs, sems, cnt_sm

    def _copies(self, expert, j, slot):
        tn = self.bufs[0].shape[-1]
        cols = pl.ds(pl.multiple_of(j * tn, tn), tn)
        return [pltpu.make_async_copy(w.at[expert, :, cols], buf.at[slot], self.sems.at[i, slot])
                for i, (w, buf) in enumerate(zip(self.w_hbms, self.bufs))]

    def prologue(self):
        self.cnt_sm[0] = 0
        for cp in self._copies(self.eid_ref[0], 0, 0):
            cp.start()

    def acquire(self, j, t, n_passes):
        slot = self.cnt_sm[0] % 2
        for cp in self._copies(self.eid_ref[t], j, slot):
            cp.wait()
        expert = self.eid_ref[t]
        nxt = lax.while_loop(
            lambda u: (u < self.n_used) & (self.eid_ref[jnp.minimum(u, N_MTILES - 1)] == expert),
            lambda u: u + 1, t + 1)
        in_pass = nxt < self.n_used

        @pl.when(in_pass | (j + 1 < n_passes))
        def _():
            for cp in self._copies(self.eid_ref[jnp.where(in_pass, nxt, 0)], jnp.where(in_pass, j, j + 1),
                                   1 - slot):
                cp.start()

        self.cnt_sm[0] = self.cnt_sm[0] + 1
        return slot


def _on_live_rows(valid, out_ref, compute):
    lo = 0
    for rows in ROW_STEPS:
        @pl.when((valid > lo) & (valid <= rows) if lo else (valid <= rows))
        def _():
            out_ref[0:rows, :] = compute(rows).astype(out_ref.dtype)
            if rows < TM_MOE:
                out_ref[rows:TM_MOE, :] = jnp.zeros((TM_MOE - rows, out_ref.shape[1]), out_ref.dtype)

        lo = rows


def _moe_up_kernel(eid_ref, nused_ref, valid_ref, x_ref, wg_hbm, wu_hbm, bg_ref, bu_ref, h_ref,
                   wg_buf, wu_buf, wg_sc, wu_sc, sems, cnt_sm):
    j = pl.program_id(0)
    t = pl.program_id(1)
    n_used = nused_ref[0]
    weights = _ExpertWeights(eid_ref, n_used, (wg_hbm, wu_hbm), (wg_buf, wu_buf), sems, cnt_sm)

    @pl.when((j == 0) & (t == 0))
    def _():
        weights.prologue()

    @pl.when(t < n_used)
    def _():
        @pl.when((t == 0) | (eid_ref[t] != eid_ref[jnp.maximum(t - 1, 0)]))
        def _():
            slot = weights.acquire(j, t, pl.num_programs(0))
            wg_sc[...] = wg_buf[slot].astype(BF16)
            wu_sc[...] = wu_buf[slot].astype(BF16)

        def swiglu(rows):
            x = x_ref[0:rows, :]
            gate = jnp.minimum(_dot(x, wg_sc[...]) + bg_ref[...], SWIGLU_LIMIT)
            up = jnp.clip(_dot(x, wu_sc[...]) + bu_ref[...], -SWIGLU_LIMIT, SWIGLU_LIMIT)
            return gate * jax.nn.sigmoid(SWIGLU_ALPHA * gate) * (up + 1.0)

        _on_live_rows(valid_ref[t], h_ref, swiglu)

    @pl.when(t >= n_used)
    def _():
        h_ref[...] = jnp.zeros(h_ref.shape, BF16)


def _moe_down_kernel(eid_ref, nused_ref, valid_ref, h_ref, wd_hbm, bd_ref, y_ref, wd_buf, wd_sc, sems, cnt_sm):
    j = pl.program_id(0)
    t = pl.program_id(1)
    n_used = nused_ref[0]
    weights = _ExpertWeights(eid_ref, n_used, (wd_hbm,), (wd_buf,), sems, cnt_sm)

    @pl.when((j == 0) & (t == 0))
    def _():
        weights.prologue()

    @pl.when(t < n_used)
    def _():
        @pl.when((t == 0) | (eid_ref[t] != eid_ref[jnp.maximum(t - 1, 0)]))
        def _():
            slot = weights.acquire(j, t, pl.num_programs(0))
            wd_sc[...] = wd_buf[slot].astype(BF16)

        _on_live_rows(valid_ref[t], y_ref, lambda rows: _dot(h_ref[0:rows, :], wd_sc[...]) + bd_ref[...])

    @pl.when(t >= n_used)
    def _():
        y_ref[...] = jnp.zeros(y_ref.shape, F32)


def _row_tile(t, nused_ref):
    return jnp.minimum(t, jnp.maximum(nused_ref[0] - 1, 0))


def _moe_call(body, name, n_weights, k_dim, out_dim, tn, out_dtype, tile_eid, n_used, tile_valid, rows, weights,
              biases):
    any_spec = pl.BlockSpec(memory_space=pl.ANY)
    b_spec = pl.BlockSpec((None, 1, tn), lambda j, t, eid, nu, tv: (eid[t], 0, j))
    return pl.pallas_call(
        body,
        grid_spec=pltpu.PrefetchScalarGridSpec(
            num_scalar_prefetch=3,
            grid=(out_dim // tn, N_MTILES),
            in_specs=[pl.BlockSpec((TM_MOE, k_dim), lambda j, t, eid, nu, tv: (_row_tile(t, nu), 0))]
            + [any_spec] * n_weights + [b_spec] * n_weights,
            out_specs=pl.BlockSpec((TM_MOE, tn), lambda j, t, eid, nu, tv: (t, j)),
            scratch_shapes=[pltpu.VMEM((2, k_dim, tn), F32)] * n_weights
            + [pltpu.VMEM((k_dim, tn), BF16)] * n_weights
            + [pltpu.SemaphoreType.DMA((n_weights, 2)), pltpu.SMEM((1,), jnp.int32)]),
        out_shape=jax.ShapeDtypeStruct((P_ROWS, out_dim), out_dtype),
        compiler_params=_params("arbitrary", "arbitrary"),
        name=name,
    )(tile_eid, n_used, tile_valid, rows, *weights, *biases)


def _moe_up(tile_eid, n_used, tile_valid, xs, w_gate, w_up, b_gate, b_up):
    return _moe_call(_moe_up_kernel, "moe_up", 2, D_MODEL, D_FF, TN_UP, BF16, tile_eid, n_used, tile_valid, xs,
                     (w_gate, w_up), (b_gate, b_up))


def _moe_down(tile_eid, n_used, tile_valid, h, w_down, b_down):
    return _moe_call(_moe_down_kernel, "moe_down", 1, D_FF, D_MODEL, TN_DOWN, F32, tile_eid, n_used, tile_valid, h,
                     (w_down,), (b_down,))


def _final_kernel(dest_ref, y_hbm, x1_ref, wt_ref, g2_ref, ln2g_ref, ln2b_ref, o_ref, ybuf, sems):
    i = pl.program_id(0)
    n_tiles = pl.num_programs(0)

    def start_gather(tile, slot):
        base = tile * (TM_FIN * TOP_K)

        def body(r2, carry):
            a0 = base + r2 * DMA_BATCH
            rows = [dest_ref[a0 + i] for i in range(DMA_BATCH)]
            for i, d in enumerate(rows):
                r = r2 * (DMA_BATCH // TOP_K) + i // TOP_K
                pltpu.make_async_copy(y_hbm.at[pl.ds(d, 1), :], ybuf.at[slot, i % TOP_K, pl.ds(r, 1), :],
                                      sems.at[slot]).start(priority=i % 2)
            return carry

        lax.fori_loop(0, TM_FIN * TOP_K // DMA_BATCH, body, 0)

    @pl.when(i == 0)
    def _():
        start_gather(0, 0)

    @pl.when(i + 1 < n_tiles)
    def _():
        start_gather(i + 1, (i + 1) % 2)

    slot = i % 2
    pltpu.make_async_copy(ybuf.at[slot], ybuf.at[slot], sems.at[slot]).wait()
    wt = wt_ref[...]
    y = wt[:, 0:1] * ybuf[slot, 0]
    for k in range(1, TOP_K):
        y = y + wt[:, k:k + 1] * ybuf[slot, k]
    o_ref[...] = _ln(DN_ALPHA * x1_ref[...] + (1.0 + g2_ref[...]) * y) * ln2g_ref[...] + ln2b_ref[...]


def _final(dest_flat, y, x1, wt, mod3, ln2_g, ln2_b):
    tiles_per_batch = SEQ // TM_FIN
    full = lambda shape: pl.BlockSpec(shape, lambda i, dest: tuple(0 for _ in shape))
    row = lambda width: pl.BlockSpec((TM_FIN, width), lambda i, dest: (i, 0))
    return pl.pallas_call(
        _final_kernel,
        grid_spec=pltpu.PrefetchScalarGridSpec(
            num_scalar_prefetch=1,
            grid=(N_TOK // TM_FIN,),
            in_specs=[pl.BlockSpec(memory_space=pl.ANY), row(D_MODEL), row(LANE),
                      pl.BlockSpec((None, 1, D_MODEL), lambda i, dest: (i // tiles_per_batch, 0, 5)),
                      full(ln2_g.shape), full(ln2_b.shape)],
            out_specs=row(D_MODEL),
            scratch_shapes=[pltpu.VMEM((2, TOP_K, TM_FIN, D_MODEL), F32),
                            pltpu.SemaphoreType.DMA((2,))]),
        out_shape=jax.ShapeDtypeStruct((N_TOK, D_MODEL), F32),
        compiler_params=_params("arbitrary"),
        name="combine_ln2",
    )(dest_flat, y, x1, wt, mod3, ln2_g, ln2_b)


def _route(idx, rank, counts):
    experts = jnp.arange(N_EXPERTS, dtype=jnp.int32)
    padded = ((counts + TM_MOE - 1) // TM_MOE) * TM_MOE
    ends = jnp.sum(jnp.where(experts[None, :] <= experts[:, None], padded[None, :], 0), axis=1)
    starts = ends - padded
    idx_t = idx[:, :TOP_K].T
    dest_t = rank[:, :TOP_K].T
    for e in range(N_EXPERTS):
        dest_t = dest_t + jnp.where(idx_t == e, starts[e], 0)
    dest = dest_t.T
    tile_start = jnp.arange(N_MTILES, dtype=jnp.int32) * TM_MOE
    tile_eid = jnp.minimum(jnp.sum((ends[None, :] <= tile_start[:, None]).astype(jnp.int32), axis=1),
                           N_EXPERTS - 1)
    n_used = (ends[-1] // TM_MOE).astype(jnp.int32).reshape(1)
    group_fill = starts + counts
    tile_fill = jnp.sum(jnp.where(tile_eid[:, None] == experts[None, :], group_fill[None, :], 0), axis=1)
    tile_valid = jnp.clip(tile_fill - tile_start, 0, TM_MOE)
    n_batches = (tile_valid + DMA_BATCH - 1) // DMA_BATCH
    return dest.reshape(-1), tile_eid, n_used, tile_valid, n_batches, group_fill


def kernel(x, c, w_ada, b_ada, w_in, w_gk, b_gk, gla_norm_g, pe_k, pe_v, w_ck1, b_ck1, w_ck2, b_ck2,
           w_cv1, b_cv1, w_cv2, b_cv2, w_o, ln1_g, ln1_b, w_router, b_router, w_gate, b_gate, w_up, b_up,
           w_down, b_down, ln2_g, ln2_b):
    l = 0
    xf = x.reshape(N_TOK, D_MODEL)
    row2 = lambda a: a.reshape(1, -1)

    c8 = jnp.pad(c, ((0, 8 - BATCH), (0, 0)))
    mod3 = _adaln(c8, w_ada[l], row2(b_ada[l]))[:BATCH].reshape(BATCH, 1, 6 * D_MODEL)

    wt = w_in[l].T
    glr0 = 3072
    nsa0 = glr0 + GLA_RANK
    ngt0 = nsa0 + 1024 + 6 * 256
    w_in_t = jnp.concatenate(
        [wt[:glr0], wt[nsa0:ngt0], wt[glr0:nsa0], wt[ngt0:],
         jnp.zeros((D_IN_PAD - wt.shape[0], D_MODEL), F32)], axis=0).astype(BF16)
    proj = _inproj(xf, mod3, w_in_t)

    w_gk_pad = jnp.pad(w_gk[l], ((0, LANE - GLA_RANK), (0, 0)))
    y_gla = _gla(proj, w_gk_pad, row2(b_gk[l]), row2(gla_norm_g[l]))

    cos_t, sin_t = _rope_tables(np.arange(SEQ))
    cmp_end = np.arange(N_CMP_PAD) * CMP_STRIDE + (CMP_BLOCK - 1)
    cos_c, sin_c = _rope_tables(cmp_end)
    q_r, ks, vs, kw, vw = _prep(proj, cos_t, sin_t)
    kc = _compress(proj, 0, pe_k[l], w_ck1[l], row2(b_ck1[l]), w_ck2[l], row2(b_ck2[l]), cos_c, sin_c, True)
    vc = _compress(proj, 1, pe_v[l], w_cv1[l], row2(b_cv1[l]), w_cv2[l], row2(b_cv2[l]), cos_c, sin_c, False)
    y_nsa = _nsa(q_r, kc, vc, ks, vs, kw, vw, proj)

    wr = jnp.pad(w_router[l], ((0, 0), (0, LANE - N_EXPERTS)))
    wr_hi, wr_lo = _split_bf16(wr)
    br = jnp.concatenate([b_router[l], jnp.full((LANE - N_EXPERTS,), NEG, F32)]).reshape(1, LANE)
    x1, h2, idx128, wt128, rank128, cnt8 = _outproj(y_gla, y_nsa, xf, w_o[l].astype(BF16), mod3,
                                                    row2(ln1_g[l]), row2(ln1_b[l]),
                                                    jnp.concatenate([wr_hi, wr_lo], axis=1), br)

    counts = cnt8[0, :N_EXPERTS].astype(jnp.int32)
    dest, tile_eid, n_used, tile_valid, n_batches, group_fill = _route(idx128, rank128, counts)
    xs = _dispatch(dest, group_fill, n_used, n_batches, h2)
    h = _moe_up(tile_eid, n_used, tile_valid, xs, w_gate[l], w_up[l],
                b_gate[l].reshape(N_EXPERTS, 1, D_FF), b_up[l].reshape(N_EXPERTS, 1, D_FF))
    y = _moe_down(tile_eid, n_used, tile_valid, h, w_down[l], b_down[l].reshape(N_EXPERTS, 1, D_MODEL))

    out = _final(dest, y, x1, wt128, mod3, row2(ln2_g[l]), row2(ln2_b[l]))
    return out.reshape(BATCH, SEQ, D_MODEL)
```
